```python
import functools
import jax, jax.numpy as jnp
from jax import lax
import numpy as np


D_MODEL = 1024
BATCH = 32
SEQ = 256
DEPTH = 2
DEC_BATCH = 2
DEC_SEQ = 1024
PAST_LEN = 256

GRID_W = 64
H_M = 4
DH_M = 128
H_R = 4
DH_R = 128
W_M = H_M * DH_M
W_R = H_R * DH_R
N_GATES = 4 * H_M
D_IN = 4 * W_M + N_GATES + 4 * W_R
SPLITS = [W_M, 2 * W_M, 3 * W_M, 4 * W_M, 4 * W_M + N_GATES,
          4 * W_M + N_GATES + W_R, 4 * W_M + N_GATES + 2 * W_R, 4 * W_M + N_GATES + 3 * W_R]
CHUNK = 128
D_FF = 2816
N_EXPERTS = 8
TOP_K = 2
N_DENSE = (DEPTH + 1) // 2
N_MOE = DEPTH // 2
ROPE_BASE = 10000.0
EPS = 1e-6

kernel_name = 'hybrid_mlstm_retention_diffusion_step'


def rmsnorm(x, g):
    xf = x.astype(jnp.float32)
    y = xf * lax.rsqrt(jnp.mean(xf * xf, -1, keepdims=True) + EPS)
    return (y * g.astype(jnp.float32)).astype(x.dtype)


def head_rmsnorm(h, g):
    H, d = h.shape[1], h.shape[3]
    y = h * lax.rsqrt(jnp.mean(h * h, -1, keepdims=True) + EPS)
    return y * g.astype(jnp.float32).reshape(H, 1, d)


def flip(a):
    return jnp.flip(a, axis=2)


def to_chunks(a):
    B, H, T = a.shape[:3]
    a = a.reshape((B, H, T // CHUNK, CHUNK) + a.shape[3:])
    return jnp.moveaxis(a, 2, 0)


def from_chunks(a):
    N, B, H, L = a.shape[:4]
    return jnp.moveaxis(a, 0, 2).reshape((B, H, N * L) + a.shape[4:])


def rope_1d(x, pos):
    half = x.shape[-1] // 2
    freqs = ROPE_BASE ** (-jnp.arange(half, dtype=jnp.float32) / half)
    ang = pos.astype(jnp.float32)[:, None] * freqs
    cos, sin = jnp.cos(ang), jnp.sin(ang)
    x1, x2 = x[..., :half], x[..., half:]
    return jnp.concatenate([x1 * cos - x2 * sin, x1 * sin + x2 * cos], -1)


def rope_2d(x, row, col):
    d = x.shape[-1] // 2
    return jnp.concatenate([rope_1d(x[..., :d], row), rope_1d(x[..., d:], col)], -1)


def mlstm_scan(q, k, v, i_pre, f_log, C0, n0, m0):
    causal = jnp.tril(jnp.ones((CHUNK, CHUNK), dtype=bool))

    def step(carry, xs):
        C, n, m = carry
        qc, kc, vc, ic, fc = xs
        b = jnp.cumsum(fc, axis=-1)
        logD = jnp.where(causal, b[..., :, None] - b[..., None, :] + ic[..., None, :], -jnp.inf)
        m_inter = b + m[..., None]
        m_row = jnp.maximum(m_inter, jnp.max(logD, -1))
        s = jnp.einsum('bhid,bhjd->bhij', qc, kc) * jnp.exp(logD - m_row[..., None])
        w_inter = jnp.exp(m_inter - m_row)
        num = jnp.einsum('bhij,bhjd->bhid', s, vc) + w_inter[..., None] * jnp.einsum('bhid,bhde->bhie', qc, C)
        den = jnp.sum(s, -1) + w_inter * jnp.einsum('bhid,bhd->bhi', qc, n)
        h = num / jnp.maximum(jnp.abs(den), jnp.exp(-m_row))[..., None]
        b_end = b[..., -1]
        log_w = b_end[..., None] - b + ic
        m_new = jnp.maximum(b_end + m, jnp.max(log_w, -1))
        decay = jnp.exp(b_end + m - m_new)
        wk = jnp.exp(log_w - m_new[..., None])[..., None] * kc
        C_new = decay[..., None, None] * C + jnp.einsum('bhjd,bhje->bhde', wk, vc)
        n_new = decay[..., None] * n + jnp.sum(wk, -2)
        return (C_new, n_new, m_new), h

    xs = tuple(to_chunks(a) for a in (q, k, v, i_pre, f_log))
    (C, n, m), h = lax.scan(step, (C0, n0, m0), xs)
    return from_chunks(h), C, n, m


def retention_scan(q, k, v, log_gamma, S0):
    L = CHUNK
    pos = jnp.arange(L, dtype=jnp.float32)
    lg = log_gamma[:, None]
    rel = pos[:, None] - pos[None, :]
    decay_mat = jnp.where(rel >= 0, jnp.exp(lg[..., None] * jnp.maximum(rel, 0.0)), 0.0)
    decay_q = jnp.exp(lg * (pos + 1.0))
    decay_k = jnp.exp(lg * (L - 1.0 - pos))
    decay_L = jnp.exp(log_gamma * L)

    def step(S, xs):
        qc, kc, vc = xs
        s = jnp.einsum('bhid,bhjd->bhij', qc, kc) * decay_mat
        o = jnp.einsum('bhij,bhjd->bhid', s, vc) + decay_q[..., None] * jnp.einsum('bhid,bhde->bhie', qc, S)
        S_new = decay_L[:, None, None] * S + jnp.einsum('bhjd,bhje->bhde', kc * decay_k[..., None], vc)
        return S_new, o

    S, o = lax.scan(step, S0, tuple(to_chunks(a) for a in (q, k, v)))
    return from_chunks(o), S


def mixer(h, st, pos, w_in, b_gates, decay_logit, norm_m, norm_r, w_out):
    B, T, _ = h.shape
    f32 = jnp.float32
    p = jnp.einsum('btd,de->bte', h, w_in).astype(f32)
    qm, km, vm, om, gts, qr, kr, vr, gr = jnp.split(p, SPLITS, axis=-1)

    def heads(a, n):
        return a.reshape(B, T, n, -1).transpose(0, 2, 1, 3)

    qm, km, vm, om = (heads(a, H_M) for a in (qm, km, vm, om))
    qr, kr, vr, gr = (heads(a, H_R) for a in (qr, kr, vr, gr))
    km = km * DH_M ** -0.5
    if pos is not None:
        qr = rope_2d(qr, pos[0], pos[1])
        kr = rope_2d(kr, pos[0], pos[1])
    kr = kr * DH_R ** -0.5
    g = (gts + b_gates.astype(f32)).reshape(B, T, 4, H_M).transpose(2, 0, 3, 1)
    i_fw, i_bw = g[0], g[1]
    f_fw, f_bw = jax.nn.log_sigmoid(g[2]), jax.nn.log_sigmoid(g[3])
    C0, n0, m0, S0 = (a.astype(f32) for a in st)

    hf, Cf, nf, mf = mlstm_scan(qm, km, vm, i_fw, f_fw, C0[:, 0], n0[:, 0], m0[:, 0])
    hb, Cb, nb, mb = mlstm_scan(flip(qm), flip(km), flip(vm), flip(i_bw), flip(f_bw),
                                C0[:, 1], n0[:, 1], m0[:, 1])
    y_m = jax.nn.sigmoid(om) * head_rmsnorm(hf + flip(hb), norm_m)

    log_gamma = jax.nn.log_sigmoid(decay_logit.astype(f32))
    of, Sf = retention_scan(qr, kr, vr, log_gamma[0], S0[:, 0])
    ob, Sb = retention_scan(flip(qr), flip(kr), flip(vr), log_gamma[1], S0[:, 1])
    y_r = jax.nn.silu(gr) * head_rmsnorm(of + flip(ob), norm_r)

    y = jnp.concatenate([y_m, y_r], axis=1).transpose(0, 2, 1, 3).reshape(B, T, W_M + W_R).astype(h.dtype)
    out = jnp.einsum('btd,de->bte', y, w_out)
    new_st = (jnp.stack([Cf, Cb], 1), jnp.stack([nf, nb], 1), jnp.stack([mf, mb], 1), jnp.stack([Sf, Sb], 1))
    return out, new_st


def swiglu(h, wg, wu, wd):
    a = jnp.einsum('btd,df->btf', h, wg)
    b = jnp.einsum('btd,df->btf', h, wu)
    return jnp.einsum('btf,fd->btd', jax.nn.silu(a) * b, wd)


def moe_swiglu(h, w_router, wg, wu, wd):
    logits = jnp.einsum('btd,de->bte', h, w_router).astype(jnp.float32)
    top_v, top_i = lax.top_k(logits, TOP_K)
    w = jax.nn.softmax(top_v, axis=-1)
    gate = jnp.sum(jax.nn.one_hot(top_i, N_EXPERTS, dtype=jnp.float32) * w[..., None], axis=-2)
    gate = gate.astype(h.dtype)
    y = jnp.zeros_like(h)
    for e in range(N_EXPERTS):
        y = y + gate[..., e:e + 1] * swiglu(h, wg[e], wu[e], wd[e])
    return y


def block(x, mod, st, pos, g1, g2, w_in, b_gates, decay_logit, norm_m, norm_r, w_out, ffn):
    shift1, scale1, gate1, shift2, scale2, gate2 = jnp.split(mod[:, None, :].astype(x.dtype), 6, axis=-1)
    h = rmsnorm(x, g1) * (1.0 + scale1) + shift1
    y, new_st = mixer(h, st, pos, w_in, b_gates, decay_logit, norm_m, norm_r, w_out)
    x = x + gate1 * y
    h = rmsnorm(x, g2) * (1.0 + scale2) + shift2
    x = x + gate2 * ffn(h)
    return x, new_st


def setup_inputs(seed: int = 0) -> dict:
    key = jax.random.key(seed)
    ks = jax.random.split(key, 32)
    nrm = jax.random.normal
    f32 = jnp.float32
    D = D_MODEL
    gam = 1.0 - np.exp(np.linspace(np.log(1.0 / 32), np.log(1.0 / 512), H_R))
    logit = np.log(gam / (1.0 - gam)).astype(np.float32)
    f_bias = jnp.asarray(np.linspace(3.0, 6.0, H_M), dtype=f32)
    b_gates = jnp.concatenate([
        0.1 * nrm(ks[10], (DEPTH, 2 * H_M), f32),
        jnp.tile(f_bias, 2)[None, :] + 0.1 * nrm(ks[11], (DEPTH, 2 * H_M), f32)], axis=-1)
    return {
        'x_prompt': nrm(ks[0], (BATCH, SEQ, D), f32),
        'x_sample': nrm(ks[1], (DEC_BATCH, DEC_SEQ, D), f32),
        'state_mlstm_C': 0.3 * nrm(ks[2], (DEC_BATCH, DEPTH, 2, H_M, DH_M, DH_M), f32),
        'state_mlstm_n': 0.3 * nrm(ks[3], (DEC_BATCH, DEPTH, 2, H_M, DH_M), f32),
        'state_mlstm_m': 0.5 * nrm(ks[4], (DEC_BATCH, DEPTH, 2, H_M), f32),
        'state_ret_S': nrm(ks[5], (DEC_BATCH, DEPTH, 2, H_R, DH_R, DH_R), f32),
        'c': nrm(ks[6], (DEC_BATCH, D), f32),
        'c_ctx': nrm(ks[7], (D,), f32),
        'norm1_g': 1.0 + 0.02 * nrm(ks[8], (DEPTH, D), f32),
        'norm2_g': 1.0 + 0.02 * nrm(ks[9], (DEPTH, D), f32),
        'norm_f_g': 1.0 + 0.02 * nrm(ks[12], (D,), f32),
        'w_ada': 0.5 * D ** -0.5 * nrm(ks[13], (DEPTH, D, 6 * D), f32),
        'b_ada': 0.02 * nrm(ks[14], (DEPTH, 6 * D), f32),
        'w_in': D ** -0.5 * nrm(ks[15], (DEPTH, D, D_IN), f32),
        'b_gates': b_gates,
        'ret_decay_logit': jnp.asarray(logit)[None, None, :] + 0.05 * nrm(ks[16], (DEPTH, 2, H_R), f32),
        'mlstm_norm_g': 1.0 + 0.02 * nrm(ks[17], (DEPTH, W_M), f32),
        'ret_norm_g': 1.0 + 0.02 * nrm(ks[18], (DEPTH, W_R), f32),
        'w_out': (W_M + W_R) ** -0.5 * nrm(ks[19], (DEPTH, W_M + W_R, D), f32),
        'ffn_w_gate': D ** -0.5 * nrm(ks[20], (N_DENSE, D, D_FF), f32),
        'ffn_w_up': D ** -0.5 * nrm(ks[21], (N_DENSE, D, D_FF), f32),
        'ffn_w_down': D_FF ** -0.5 * nrm(ks[22], (N_DENSE, D_FF, D), f32),
        'moe_w_router': D ** -0.5 * nrm(ks[23], (N_MOE, D, N_EXPERTS), f32),
        'moe_w_gate': D ** -0.5 * nrm(ks[24], (N_MOE, N_EXPERTS, D, D_FF), f32),
        'moe_w_up': D ** -0.5 * nrm(ks[25], (N_MOE, N_EXPERTS, D, D_FF), f32),
        'moe_w_down': D_FF ** -0.5 * nrm(ks[26], (N_MOE, N_EXPERTS, D_FF, D), f32),
    }


def reference(x_prompt, x_sample, state_mlstm_C, state_mlstm_n, state_mlstm_m, state_ret_S, c, c_ctx,
              norm1_g, norm2_g, norm_f_g, w_ada, b_ada, w_in, b_gates, ret_decay_logit,
              mlstm_norm_g, ret_norm_g, w_out, ffn_w_gate, ffn_w_up, ffn_w_down,
              moe_w_router, moe_w_gate, moe_w_up, moe_w_down):
    f32 = jnp.float32
    B_ctx = x_prompt.shape[0]
    T_lat = x_sample.shape[1]
    rows = T_lat // GRID_W
    row = jnp.repeat(jnp.arange(rows), GRID_W)
    col = jnp.broadcast_to(jnp.arange(GRID_W)[None, :], (rows, GRID_W)).reshape(-1)
    pos = (row, col)
    zero_st = (jnp.zeros((B_ctx, 2, H_M, DH_M, DH_M), f32), jnp.zeros((B_ctx, 2, H_M, DH_M), f32),
               jnp.zeros((B_ctx, 2, H_M), f32), jnp.zeros((B_ctx, 2, H_R, DH_R, DH_R), f32))
    xp, xs = x_prompt, x_sample
    new_C, new_n, new_m, new_S = [], [], [], []
    for l in range(DEPTH):
        j = l // 2
        if l % 2 == 0:
            ffn = functools.partial(swiglu, wg=ffn_w_gate[j], wu=ffn_w_up[j], wd=ffn_w_down[j])
        else:
            ffn = functools.partial(moe_swiglu, w_router=moe_w_router[j], wg=moe_w_gate[j],
                                    wu=moe_w_up[j], wd=moe_w_down[j])
        lw = (norm1_g[l], norm2_g[l], w_in[l], b_gates[l], ret_decay_logit[l],
              mlstm_norm_g[l], ret_norm_g[l], w_out[l])
        mod_ctx = jnp.einsum('d,de->e', jax.nn.silu(c_ctx), w_ada[l])[None, :] + b_ada[l][None, :]
        mod_lat = jnp.einsum('bd,de->be', jax.nn.silu(c), w_ada[l]) + b_ada[l][None, :]
        xp, st_ctx = block(xp, mod_ctx, zero_st, None, *lw, ffn)
        new_C.append(st_ctx[0]); new_n.append(st_ctx[1]); new_m.append(st_ctx[2]); new_S.append(st_ctx[3])
        cache_l = (state_mlstm_C[:, l], state_mlstm_n[:, l], state_mlstm_m[:, l], state_ret_S[:, l])
        xs, _ = block(xs, mod_lat, cache_l, pos, *lw, ffn)
    y_prompt = rmsnorm(xp, norm_f_g)
    y_sample = rmsnorm(xs, norm_f_g)
    new_mlstm_C = jnp.stack(new_C, axis=1)
    new_mlstm_n = jnp.stack(new_n, axis=1)
    new_mlstm_m = jnp.stack(new_m, axis=1)
    new_ret_S = jnp.stack(new_S, axis=1)
    return (y_prompt, y_sample, new_mlstm_C, new_mlstm_n, new_mlstm_m, new_ret_S)
```

```python
import functools

import numpy as np
import jax
import jax.numpy as jnp
from jax import lax
from jax.experimental import pallas as pl
from jax.experimental.pallas import tpu as pltpu

D_MODEL = 1024
BATCH = 32
SEQ = 256
DEPTH = 2
DEC_BATCH = 2
DEC_SEQ = 1024
GRID_W = 64
H_M = 4
DH = 128
H_R = 4
W_M = H_M * DH
W_R = H_R * DH
N_GATES = 4 * H_M
CHUNK = 128
D_FF = 2816
N_EXPERTS = 8
ROPE_BASE = 10000.0
EPS = 1e-6

N_CTX = BATCH * SEQ
N_LAT = DEC_BATCH * DEC_SEQ
NTOK = N_CTX + N_LAT
N_GROUPS = 8
K_SCALE = DH ** -0.5
P_COLS = 4 * W_M + 4 * W_R
LANES = 128
VMEM_LIMIT = 56 * 1024 * 1024

F32 = jnp.float32
BF16 = jnp.bfloat16
HIGHEST = lax.Precision.HIGHEST

TM = 1024
TN = 512
FC = 256


def _group_of_tile(i, tm):
    return jnp.maximum(i * tm // DEC_SEQ - (N_CTX // DEC_SEQ - 1), 0)


def _silu(x):
    return x * jax.nn.sigmoid(x)


def _log_sigmoid(x):
    return jnp.minimum(x, 0.0) - jnp.log(1.0 + jnp.exp(-jnp.abs(x)))


def _rmsnorm(x, g):
    return x * lax.rsqrt(jnp.mean(x * x, -1, keepdims=True) + EPS) * g


def _ada_kernel(cv_ref, w_ref, b_ref, o_ref):
    s = _silu(cv_ref[...]).astype(BF16)
    o_ref[0] = jnp.dot(s, w_ref[0].astype(BF16), preferred_element_type=F32) + b_ref[0]


def _ada(cvec, w_ada, b_ada):
    tn = 1536
    n = 6 * D_MODEL
    return pl.pallas_call(
        _ada_kernel,
        grid=(DEPTH, n // tn),
        in_specs=[
            pl.BlockSpec((N_GROUPS, D_MODEL), lambda l, j: (0, 0)),
            pl.BlockSpec((1, D_MODEL, tn), lambda l, j: (l, 0, j)),
            pl.BlockSpec((1, 1, tn), lambda l, j: (l, 0, j)),
        ],
        out_specs=pl.BlockSpec((1, N_GROUPS, tn), lambda l, j: (l, 0, j)),
        out_shape=jax.ShapeDtypeStruct((DEPTH, N_GROUPS, n), F32),
        compiler_params=pltpu.CompilerParams(
            dimension_semantics=("arbitrary", "arbitrary"), vmem_limit_bytes=VMEM_LIMIT),
        name="ada",
    )(cvec, w_ada, b_ada.reshape(DEPTH, 1, n))


def _rope_tables():
    half = DH // 4
    freqs = ROPE_BASE ** (-np.arange(half, dtype=np.float64) / half)
    t = np.arange(DEC_SEQ)
    pos = np.stack([t // GRID_W, t % GRID_W], 1).astype(np.float64)
    d = np.arange(DH)
    ang = pos[:, d // (DH // 2)] * freqs[d % half][None, :]
    sign = np.where((d % (DH // 2)) < half, -1.0, 1.0)[None, :]
    return np.cos(ang).astype(np.float32), (sign * np.sin(ang)).astype(np.float32)


def _rope(a, cos, sin):
    lane = lax.broadcasted_iota(jnp.int32, a.shape, 1)
    first = (lane % (DH // 2)) < (DH // 4)
    partner = jnp.where(first, pltpu.roll(a, DH - DH // 4, 1), pltpu.roll(a, DH // 4, 1))
    return a * cos + partner * sin


def _proj_kernel(x_ref, g_ref, mod_ref, w_ref, wg_ref, bg_ref, cos_ref, sin_ref,
                 p_ref, gate_ref, h_scr, *, n_ctx_tiles):
    i = pl.program_id(0)
    j = pl.program_id(1)

    @pl.when(j == 0)
    def _():
        h = _rmsnorm(x_ref[...], g_ref[0]) * (1.0 + mod_ref[0, 0, 1:2, :]) + mod_ref[0, 0, 0:1, :]
        h_scr[...] = h.astype(BF16)
        gate_ref[...] = jnp.dot(h, wg_ref[0], precision=HIGHEST,
                                preferred_element_type=F32) + bg_ref[0]

    acc = jnp.dot(h_scr[...], w_ref[0].astype(BF16), preferred_element_type=F32)
    j_km = 1
    j_qr = P_COLS // 2 // TN
    j_kr = j_qr + 1

    @pl.when(j == j_km)
    def _():
        p_ref[...] = (acc * K_SCALE).astype(BF16)

    @pl.when((j == j_qr) | (j == j_kr))
    def _():
        is_lat = i >= n_ctx_tiles
        scale = jnp.where(j == j_kr, K_SCALE, 1.0).astype(F32)
        cos = cos_ref[...]
        sin = sin_ref[...]
        for hd in range(TN // DH):
            a = acc[:, hd * DH:(hd + 1) * DH]
            r = jnp.where(is_lat, _rope(a, cos, sin), a)
            p_ref[:, hd * DH:(hd + 1) * DH] = (r * scale).astype(BF16)

    @pl.when((j != j_km) & (j != j_qr) & (j != j_kr))
    def _():
        p_ref[...] = acc.astype(BF16)


def _proj(x, g1, mods, w2, wg, bg, cos_t, sin_t, layer):
    n_ctx_tiles = N_CTX // TM
    tiles_per_seq = DEC_SEQ // TM
    return pl.pallas_call(
        functools.partial(_proj_kernel, n_ctx_tiles=n_ctx_tiles),
        grid=(NTOK // TM, P_COLS // TN),
        in_specs=[
            pl.BlockSpec((TM, D_MODEL), lambda i, j: (i, 0)),
            pl.BlockSpec((1, 1, D_MODEL), lambda i, j: (layer, 0, 0)),
            pl.BlockSpec((1, 1, 6, D_MODEL), lambda i, j: (layer, _group_of_tile(i, TM), 0, 0)),
            pl.BlockSpec((1, D_MODEL, TN), lambda i, j: (layer, 0, j)),
            pl.BlockSpec((1, D_MODEL, LANES), lambda i, j: (layer, 0, 0)),
            pl.BlockSpec((1, 1, LANES), lambda i, j: (layer, 0, 0)),
            pl.BlockSpec((TM, DH), lambda i, j: (i % tiles_per_seq, 0)),
            pl.BlockSpec((TM, DH), lambda i, j: (i % tiles_per_seq, 0)),
        ],
        out_specs=[
            pl.BlockSpec((TM, TN), lambda i, j: (i, j)),
            pl.BlockSpec((TM, LANES), lambda i, j: (i, 0)),
        ],
        out_shape=[
            jax.ShapeDtypeStruct((NTOK, P_COLS), BF16),
            jax.ShapeDtypeStruct((NTOK, LANES), F32),
        ],
        scratch_shapes=[pltpu.VMEM((TM, D_MODEL), BF16)],
        compiler_params=pltpu.CompilerParams(
            dimension_semantics=("arbitrary", "arbitrary"), vmem_limit_bytes=VMEM_LIMIT),
        name="proj",
    )(x, g1, mods, w2, wg, bg, cos_t, sin_t)


def _split3(x):
    hi = x.astype(BF16)
    r1 = x - hi.astype(F32)
    mid = r1.astype(BF16)
    lo = (r1 - mid.astype(F32)).astype(BF16)
    return hi, mid, lo


def _dot(a, b):
    return jnp.dot(a, b, preferred_element_type=F32)


def _dot_nt(a, b):
    return lax.dot_general(a, b, (((1,), (1,)), ((), ())), preferred_element_type=F32)


def _tri_dot_left(tri, x):
    hi, mid, lo = _split3(x)
    return _dot(tri, hi) + _dot(tri, mid) + _dot(tri, lo)


def _tri_dot_right(x, tri):
    hi, mid, lo = _split3(x)
    return _dot(hi, tri) + _dot(mid, tri) + _dot(lo, tri)


def _scan_kernel(*refs, T, has_state):
    if has_state:
        (p_ref, g_ref, dl_ref, nm_ref, nr_ref, C0_ref, n0_ref, m0_ref, S0_ref, _yprev_ref,
         y_ref, C_s, n_s, m_s, S_s, hf_s, hb_s, dm_s, dq_s, dk_s) = refs
    else:
        (p_ref, g_ref, dl_ref, nm_ref, nr_ref,
         y_ref, C_out, n_out, m_out, S_out,
         C_s, n_s, m_s, S_s, hf_s, hb_s, dm_s, dq_s, dk_s) = refs
    L = CHUNK
    n_chunks = T // L
    row_i = lax.broadcasted_iota(jnp.int32, (L, L), 0)
    col_j = lax.broadcasted_iota(jnp.int32, (L, L), 1)
    lower = col_j <= row_i
    upper = col_j >= row_i
    tril = lower.astype(BF16)
    triu = upper.astype(BF16)
    pos = row_i.astype(F32)

    for d in range(2):
        for h in range(H_M):
            k = d * H_M + h
            if has_state:
                C_s[k] = C0_ref[0, 0, d, h]
                n_s[k] = n0_ref[0, 0, d, h:h + 1, :]
                m_s[k] = m0_ref[0, 0, k:k + 1, :]
                S_s[k] = S0_ref[0, 0, d, h]
            else:
                C_s[k] = jnp.zeros((DH, DH), F32)
                n_s[k] = jnp.zeros((1, DH), F32)
                m_s[k] = jnp.zeros((1, LANES), F32)
                S_s[k] = jnp.zeros((DH, DH), F32)
            lg = jnp.broadcast_to(_log_sigmoid(dl_ref[0, k:k + 1, :]), (L, L))
            rel = (row_i - col_j if d == 0 else col_j - row_i).astype(F32)
            dm_s[k] = jnp.where(rel >= 0, jnp.exp(lg * jnp.maximum(rel, 0.0)), 0.0)
            dq_s[k] = jnp.exp(lg * (pos + 1.0 if d == 0 else L - pos))
            dk_s[k] = jnp.exp(lg * (L - 1.0 - pos if d == 0 else pos))

    def chunk_step(c, carry):
        for d in range(2):
            r0 = pl.multiple_of((c if d == 0 else n_chunks - 1 - c) * L, L)
            mask = lower if d == 0 else upper
            G = g_ref[pl.ds(r0, L), :]
            FL = _log_sigmoid(G)
            GT = G.T
            FLT = FL.T
            Bc = _tri_dot_left(tril if d == 0 else triu, FL)
            Br = _tri_dot_right(FLT, triu if d == 0 else tril)
            h_dst = hf_s if d == 0 else hb_s
            for h in range(H_M):
                k = d * H_M + h
                ci = d * H_M + h
                cf = 2 * H_M + d * H_M + h
                q = p_ref[pl.ds(r0, L), h * DH:(h + 1) * DH]
                kk = p_ref[pl.ds(r0, L), W_M + h * DH:W_M + (h + 1) * DH]
                v = p_ref[pl.ds(r0, L), 2 * W_M + h * DH:2 * W_M + (h + 1) * DH]
                b_col = jnp.broadcast_to(Bc[:, cf:cf + 1], (L, L))
                b_row = jnp.broadcast_to(Br[cf:cf + 1, :], (L, L))
                i_row = jnp.broadcast_to(GT[ci:ci + 1, :], (L, L))
                i_col = jnp.broadcast_to(G[:, ci:ci + 1], (L, L))
                m_prev = m_s[k]
                C = C_s[k]
                n_row = n_s[k]
                logD = jnp.where(mask, b_col - b_row + i_row, -jnp.inf)
                m_inter = b_col + m_prev
                m_row = jnp.maximum(m_inter, jnp.max(logD, -1, keepdims=True))
                s = _dot_nt(q, kk) * jnp.exp(logD - m_row)
                w_inter = jnp.exp(m_inter - m_row)
                qf = q.astype(F32)
                num = _dot(s.astype(BF16), v) + w_inter * _dot(q, C.astype(BF16))
                den = jnp.sum(s, -1, keepdims=True) + w_inter * jnp.sum(qf * n_row, -1, keepdims=True)
                hh = num / jnp.maximum(jnp.abs(den), jnp.exp(-m_row))
                h_dst[pl.ds(r0, L), h * DH:(h + 1) * DH] = hh
                b_end = b_col[L - 1:L, :] if d == 0 else b_col[0:1, :]
                log_w = b_end - b_col + i_col
                m_new = jnp.maximum(b_end + m_prev, jnp.max(log_w, 0, keepdims=True))
                decay = jnp.exp(b_end + m_prev - m_new)
                wk = jnp.exp(log_w - m_new) * kk.astype(F32)
                C_s[k] = decay * C + _dot(wk.T.astype(BF16), v)
                n_s[k] = decay * n_row + jnp.sum(wk, 0, keepdims=True)
                m_s[k] = m_new
                c0 = 4 * W_M
                qr = p_ref[pl.ds(r0, L), c0 + h * DH:c0 + (h + 1) * DH]
                kr = p_ref[pl.ds(r0, L), c0 + W_R + h * DH:c0 + W_R + (h + 1) * DH]
                vr = p_ref[pl.ds(r0, L), c0 + 2 * W_R + h * DH:c0 + 2 * W_R + (h + 1) * DH]
                S = S_s[k]
                sr = _dot_nt(qr, kr) * dm_s[k]
                o = _dot(sr.astype(BF16), vr) + dq_s[k] * _dot(qr, S.astype(BF16))
                (hf_s if d == 0 else hb_s)[pl.ds(r0, L), W_M + h * DH:W_M + (h + 1) * DH] = o
                kd = kr.astype(F32) * dk_s[k]
                d_L = jnp.exp(_log_sigmoid(dl_ref[0, k:k + 1, :]) * float(L))
                S_s[k] = d_L * S + _dot(kd.T.astype(BF16), vr)
        return carry

    lax.fori_loop(0, n_chunks, chunk_step, 0)

    for h in range(H_M):
        sl = slice(h * DH, (h + 1) * DH)
        hs = hf_s[:, sl] + hb_s[:, sl]
        yn = _rmsnorm(hs, nm_ref[0, :, sl])
        om = p_ref[:, 3 * W_M + h * DH:3 * W_M + (h + 1) * DH].astype(F32)
        y_ref[:, sl] = (jax.nn.sigmoid(om) * yn).astype(BF16)
        slr = slice(W_M + h * DH, W_M + (h + 1) * DH)
        hr = hf_s[:, slr] + hb_s[:, slr]
        ynr = _rmsnorm(hr, nr_ref[0, :, sl])
        gr = p_ref[:, 4 * W_M + 3 * W_R + h * DH:4 * W_M + 3 * W_R + (h + 1) * DH].astype(F32)
        y_ref[:, slr] = (_silu(gr) * ynr).astype(BF16)

    if not has_state:
        for d in range(2):
            for h in range(H_M):
                k = d * H_M + h
                C_out[0, d, h] = C_s[k]
                S_out[0, d, h] = S_s[k]
                n_out[0, d, h:h + 1, :] = n_s[k]
                m_out[0, k:k + 1, :] = m_s[k]


def _scan_scratch(T):
    return [
        pltpu.VMEM((2 * H_M, DH, DH), F32),
        pltpu.VMEM((2 * H_M, 1, DH), F32),
        pltpu.VMEM((2 * H_M, 1, LANES), F32),
        pltpu.VMEM((2 * H_R, DH, DH), F32),
        pltpu.VMEM((T, W_M + W_R), F32),
        pltpu.VMEM((T, W_M + W_R), F32),
        pltpu.VMEM((2 * H_R, CHUNK, CHUNK), F32),
        pltpu.VMEM((2 * H_R, CHUNK, CHUNK), F32),
        pltpu.VMEM((2 * H_R, CHUNK, CHUNK), F32),
    ]


def _scan_ctx(p, gates, dl, nm, nr, layer):
    T = SEQ
    common = [
        pl.BlockSpec((T, P_COLS), lambda b: (b, 0)),
        pl.BlockSpec((T, LANES), lambda b: (b, 0)),
        pl.BlockSpec((1, 2 * H_R, LANES), lambda b: (layer, 0, 0)),
        pl.BlockSpec((1, 1, W_M), lambda b: (layer, 0, 0)),
        pl.BlockSpec((1, 1, W_R), lambda b: (layer, 0, 0)),
    ]
    return pl.pallas_call(
        functools.partial(_scan_kernel, T=T, has_state=False),
        grid=(BATCH,),
        in_specs=common,
        out_specs=[
            pl.BlockSpec((T, D_MODEL), lambda b: (b, 0)),
            pl.BlockSpec((1, 2, H_M, DH, DH), lambda b: (b, 0, 0, 0, 0)),
            pl.BlockSpec((1, 2, H_M, DH), lambda b: (b, 0, 0, 0)),
            pl.BlockSpec((1, 2 * H_M, LANES), lambda b: (b, 0, 0)),
            pl.BlockSpec((1, 2, H_R, DH, DH), lambda b: (b, 0, 0, 0, 0)),
        ],
        out_shape=[
            jax.ShapeDtypeStruct((NTOK, D_MODEL), BF16),
            jax.ShapeDtypeStruct((BATCH, 2, H_M, DH, DH), F32),
            jax.ShapeDtypeStruct((BATCH, 2, H_M, DH), F32),
            jax.ShapeDtypeStruct((BATCH, 2 * H_M, LANES), F32),
            jax.ShapeDtypeStruct((BATCH, 2, H_R, DH, DH), F32),
        ],
        scratch_shapes=_scan_scratch(T),
        compiler_params=pltpu.CompilerParams(
            dimension_semantics=("arbitrary",), vmem_limit_bytes=VMEM_LIMIT),
        name="scan_ctx",
    )(p, gates, dl, nm, nr)


def _scan_lat(p, gates, dl, nm, nr, C0, n0, m0, S0, y_prev, layer):
    T = DEC_SEQ
    off = N_CTX // T
    in_specs = [
        pl.BlockSpec((T, P_COLS), lambda b: (off + b, 0)),
        pl.BlockSpec((T, LANES), lambda b: (off + b, 0)),
        pl.BlockSpec((1, 2 * H_R, LANES), lambda b: (layer, 0, 0)),
        pl.BlockSpec((1, 1, W_M), lambda b: (layer, 0, 0)),
        pl.BlockSpec((1, 1, W_R), lambda b: (layer, 0, 0)),
        pl.BlockSpec((1, 1, 2, H_M, DH, DH), lambda b: (b, layer, 0, 0, 0, 0)),
        pl.BlockSpec((1, 1, 2, H_M, DH), lambda b: (b, layer, 0, 0, 0)),
        pl.BlockSpec((1, 1, 2 * H_M, LANES), lambda b: (b, layer, 0, 0)),
        pl.BlockSpec((1, 1, 2, H_R, DH, DH), lambda b: (b, layer, 0, 0, 0, 0)),
        pl.BlockSpec(memory_space=pl.ANY),
    ]
    return pl.pallas_call(
        functools.partial(_scan_kernel, T=T, has_state=True),
        grid=(DEC_BATCH,),
        in_specs=in_specs,
        out_specs=pl.BlockSpec((T, D_MODEL), lambda b: (off + b, 0)),
        out_shape=jax.ShapeDtypeStruct((NTOK, D_MODEL), BF16),
        input_output_aliases={9: 0},
        scratch_shapes=_scan_scratch(T),
        compiler_params=pltpu.CompilerParams(
            dimension_semantics=("arbitrary",), vmem_limit_bytes=VMEM_LIMIT),
        name="scan_lat",
    )(p, gates, dl, nm, nr, C0, n0, m0, S0, y_prev)


def _top2_gates(logits):
    lane = lax.broadcasted_iota(jnp.int32, logits.shape, 1)
    v1 = jnp.max(logits, -1, keepdims=True)
    i1 = jnp.min(jnp.where(logits == v1, lane, LANES), -1, keepdims=True)
    rest = jnp.where(lane == i1, -jnp.inf, logits)
    v2 = jnp.max(rest, -1, keepdims=True)
    i2 = jnp.min(jnp.where(rest == v2, lane, LANES), -1, keepdims=True)
    e2 = jnp.exp(v2 - v1)
    w1 = 1.0 / (1.0 + e2)
    w2 = e2 / (1.0 + e2)
    return jnp.where(lane == i1, w1, 0.0) + jnp.where(lane == i2, w2, 0.0)


def _out_kernel(*refs, with_router):
    if with_router:
        y_ref, x_ref, w_ref, g_ref, mod_ref, wr_ref, x1_ref, h2_ref, gate_ref, w_scr = refs
    else:
        y_ref, x_ref, w_ref, g_ref, mod_ref, x1_ref, h2_ref, w_scr = refs

    @pl.when(pl.program_id(0) == 0)
    def _():
        w_scr[...] = w_ref[0].astype(BF16)

    o = jnp.dot(y_ref[...], w_scr[...], preferred_element_type=F32)
    x1 = x_ref[...] + mod_ref[0, 0, 2:3, :] * o
    x1_ref[...] = x1
    h2 = _rmsnorm(x1, g_ref[0]) * (1.0 + mod_ref[0, 0, 4:5, :]) + mod_ref[0, 0, 3:4, :]
    h2_ref[...] = h2.astype(BF16)
    if with_router:
        logits = jnp.dot(h2, wr_ref[0], precision=HIGHEST, preferred_element_type=F32)
        lane = lax.broadcasted_iota(jnp.int32, logits.shape, 1)
        gate_ref[...] = _top2_gates(jnp.where(lane < N_EXPERTS, logits, -jnp.inf))


def _out(y, x, w_out, g2, mods, layer, w_router_pad=None, router_idx=0):
    with_router = w_router_pad is not None
    in_specs = [
        pl.BlockSpec((TM, D_MODEL), lambda i: (i, 0)),
        pl.BlockSpec((TM, D_MODEL), lambda i: (i, 0)),
        pl.BlockSpec((1, D_MODEL, D_MODEL), lambda i: (layer, 0, 0)),
        pl.BlockSpec((1, 1, D_MODEL), lambda i: (layer, 0, 0)),
        pl.BlockSpec((1, 1, 6, D_MODEL), lambda i: (layer, _group_of_tile(i, TM), 0, 0)),
    ]
    out_specs = [
        pl.BlockSpec((TM, D_MODEL), lambda i: (i, 0)),
        pl.BlockSpec((TM, D_MODEL), lambda i: (i, 0)),
    ]
    out_shape = [
        jax.ShapeDtypeStruct((NTOK, D_MODEL), F32),
        jax.ShapeDtypeStruct((NTOK, D_MODEL), BF16),
    ]
    args = [y, x, w_out, g2, mods]
    if with_router:
        in_specs.append(pl.BlockSpec((1, D_MODEL, LANES), lambda i: (router_idx, 0, 0)))
        out_specs.append(pl.BlockSpec((TM, LANES), lambda i: (i, 0)))
        out_shape.append(jax.ShapeDtypeStruct((NTOK, LANES), F32))
        args.append(w_router_pad)
    return pl.pallas_call(
        functools.partial(_out_kernel, with_router=with_router),
        grid=(NTOK // TM,),
        in_specs=in_specs,
        out_specs=out_specs,
        out_shape=out_shape,
        scratch_shapes=[pltpu.VMEM((D_MODEL, D_MODEL), BF16)],
        compiler_params=pltpu.CompilerParams(
            dimension_semantics=("arbitrary",), vmem_limit_bytes=VMEM_LIMIT),
        name="out_router" if with_router else "out",
    )(*args)


def _ffn_kernel(*refs, expert):
    if expert is None:
        h_ref, res_ref, wg_ref, wu_ref, wd_ref, mod_ref, o_ref, acc = refs
    else:
        h_ref, res_ref, wg_ref, wu_ref, wd_ref, mod_ref, gate_ref, o_ref, acc = refs
    f = pl.program_id(1)
    h = h_ref[...]
    a = jnp.dot(h, wg_ref[0].astype(BF16), preferred_element_type=F32)
    b = jnp.dot(h, wu_ref[0].astype(BF16), preferred_element_type=F32)
    t = (_silu(a) * b).astype(BF16)
    contrib = jnp.dot(t, wd_ref[0].astype(BF16), preferred_element_type=F32)

    @pl.when(f == 0)
    def _():
        acc[...] = contrib

    @pl.when(f > 0)
    def _():
        acc[...] += contrib

    @pl.when(f == pl.num_programs(1) - 1)
    def _():
        y = acc[...]
        if expert is not None:
            y = gate_ref[:, expert:expert + 1] * y
        o_ref[...] = res_ref[...] + mod_ref[0, 0, 5:6, :] * y


def _ffn(h2, res, wg, wu, wd, mods, layer, w_idx, gates=None, expert=None):
    in_specs = [
        pl.BlockSpec((TM, D_MODEL), lambda i, f: (i, 0)),
        pl.BlockSpec((TM, D_MODEL), lambda i, f: (i, 0)),
        pl.BlockSpec((1, D_MODEL, FC), lambda i, f: (w_idx, 0, f)),
        pl.BlockSpec((1, D_MODEL, FC), lambda i, f: (w_idx, 0, f)),
        pl.BlockSpec((1, FC, D_MODEL), lambda i, f: (w_idx, f, 0)),
        pl.BlockSpec((1, 1, 6, D_MODEL), lambda i, f: (layer, _group_of_tile(i, TM), 0, 0)),
    ]
    args = [h2, res, wg, wu, wd, mods]
    if expert is not None:
        in_specs.append(pl.BlockSpec((TM, LANES), lambda i, f: (i, 0)))
        args.append(gates)
    return pl.pallas_call(
        functools.partial(_ffn_kernel, expert=expert),
        grid=(NTOK // TM, D_FF // FC),
        in_specs=in_specs,
        out_specs=pl.BlockSpec((TM, D_MODEL), lambda i, f: (i, 0)),
        out_shape=jax.ShapeDtypeStruct((NTOK, D_MODEL), F32),
        scratch_shapes=[pltpu.VMEM((TM, D_MODEL), F32)],
        compiler_params=pltpu.CompilerParams(
            dimension_semantics=("arbitrary", "arbitrary"), vmem_limit_bytes=VMEM_LIMIT),
        name="ffn" if expert is None else "ffn_expert",
    )(*args)


def _final_kernel(x_ref, g_ref, o_ref):
    o_ref[...] = _rmsnorm(x_ref[...], g_ref[...])


def _final(x, g, row_off, rows):
    off = row_off // TM
    return pl.pallas_call(
        _final_kernel,
        grid=(rows // TM,),
        in_specs=[
            pl.BlockSpec((TM, D_MODEL), lambda i: (off + i, 0)),
            pl.BlockSpec((1, D_MODEL), lambda i: (0, 0)),
        ],
        out_specs=pl.BlockSpec((TM, D_MODEL), lambda i: (i, 0)),
        out_shape=jax.ShapeDtypeStruct((rows, D_MODEL), F32),
        compiler_params=pltpu.CompilerParams(
            dimension_semantics=("arbitrary",), vmem_limit_bytes=VMEM_LIMIT),
        name="final_norm",
    )(x, g)


def kernel(x_prompt, x_sample, state_mlstm_C, state_mlstm_n, state_mlstm_m, state_ret_S, c, c_ctx,
           norm1_g, norm2_g, norm_f_g, w_ada, b_ada, w_in, b_gates, ret_decay_logit,
           mlstm_norm_g, ret_norm_g, w_out, ffn_w_gate, ffn_w_up, ffn_w_down,
           moe_w_router, moe_w_gate, moe_w_up, moe_w_down):
    x = jnp.concatenate([x_prompt.reshape(N_CTX, D_MODEL), x_sample.reshape(N_LAT, D_MODEL)], 0)
    cvec = jnp.concatenate(
        [c_ctx[None, :], c, jnp.zeros((N_GROUPS - 1 - DEC_BATCH, D_MODEL), F32)], 0)
    mods = _ada(cvec, w_ada, b_ada).reshape(DEPTH, N_GROUPS, 6, D_MODEL)

    n_m = 4 * W_M
    w2 = jnp.concatenate([w_in[:, :, :n_m], w_in[:, :, n_m + N_GATES:]], -1)
    wg = jnp.pad(w_in[:, :, n_m:n_m + N_GATES], ((0, 0), (0, 0), (0, LANES - N_GATES)))
    bg = jnp.pad(b_gates, ((0, 0), (0, LANES - N_GATES))).reshape(DEPTH, 1, LANES)
    cos_np, sin_np = _rope_tables()
    cos_t, sin_t = jnp.asarray(cos_np), jnp.asarray(sin_np)
    dl = jnp.broadcast_to(ret_decay_logit.reshape(DEPTH, 2 * H_R, 1), (DEPTH, 2 * H_R, LANES))
    m0 = jnp.broadcast_to(state_mlstm_m.reshape(DEC_BATCH, DEPTH, 2 * H_M, 1),
                          (DEC_BATCH, DEPTH, 2 * H_M, LANES))
    g1 = norm1_g.reshape(DEPTH, 1, D_MODEL)
    g2 = norm2_g.reshape(DEPTH, 1, D_MODEL)
    nm = mlstm_norm_g.reshape(DEPTH, 1, W_M)
    nr = ret_norm_g.reshape(DEPTH, 1, W_R)
    n_moe = moe_w_router.shape[0]
    wr_pad = jnp.pad(moe_w_router, ((0, 0), (0, 0), (0, LANES - N_EXPERTS)))
    moe_g = moe_w_gate.reshape(n_moe * N_EXPERTS, D_MODEL, D_FF)
    moe_u = moe_w_up.reshape(n_moe * N_EXPERTS, D_MODEL, D_FF)
    moe_d = moe_w_down.reshape(n_moe * N_EXPERTS, D_FF, D_MODEL)

    new_C, new_n, new_m, new_S = [], [], [], []
    for l in range(DEPTH):
        jl = l // 2
        p, gates = _proj(x, g1, mods, w2, wg, bg, cos_t, sin_t, l)
        y, C_l, n_l, m_l, S_l = _scan_ctx(p, gates, dl, nm, nr, l)
        y = _scan_lat(p, gates, dl, nm, nr, state_mlstm_C, state_mlstm_n, m0, state_ret_S, y, l)
        new_C.append(C_l)
        new_n.append(n_l)
        new_m.append(m_l[:, :, 0].reshape(BATCH, 2, H_M))
        new_S.append(S_l)
        if l % 2 == 0:
            x1, h2 = _out(y, x, w_out, g2, mods, l)
            x = _ffn(h2, x1, ffn_w_gate, ffn_w_up, ffn_w_down, mods, l, jl)
        else:
            x1, h2, route = _out(y, x, w_out, g2, mods, l, wr_pad, jl)
            x = x1
            for e in range(N_EXPERTS):
                x = _ffn(h2, x, moe_g, moe_u, moe_d, mods, l, jl * N_EXPERTS + e, route, e)

    y_prompt = _final(x, norm_f_g.reshape(1, D_MODEL), 0, N_CTX).reshape(BATCH, SEQ, D_MODEL)
    y_sample = _final(x, norm_f_g.reshape(1, D_MODEL), N_CTX, N_LAT).reshape(DEC_BATCH, DEC_SEQ, D_MODEL)
    return (y_prompt, y_sample, jnp.stack(new_C, 1), jnp.stack(new_n, 1),
            jnp.stack(new_m, 1), jnp.stack(new_S, 1))
```

```python
import functools

import numpy as np
import jax
import jax.numpy as jnp
from jax import lax
from jax.experimental import pallas as pl
from jax.experimental.pallas import tpu as pltpu

D_MODEL = 1024
BATCH = 32
SEQ = 256
DEPTH = 2
DEC_BATCH = 2
DEC_SEQ = 1024
GRID_W = 64
H_M = 4
DH = 128
H_R = 4
W_M = H_M * DH
W_R = H_R * DH
N_GATES = 4 * H_M
CHUNK = 128
D_FF = 2816
N_EXPERTS = 8
ROPE_BASE = 10000.0
EPS = 1e-6

N_CTX = BATCH * SEQ
N_LAT = DEC_BATCH * DEC_SEQ
NTOK = N_CTX + N_LAT
N_GROUPS = 8
K_SCALE = DH ** -0.5
P_COLS = 4 * W_M + 4 * W_R
LANES = 128
VMEM_LIMIT = 56 * 1024 * 1024

F32 = jnp.float32
BF16 = jnp.bfloat16
HIGHEST = lax.Precision.HIGHEST

TM = 1024
TN = 512
FC = 256
TOP_K = 2
TR = 896
REG_TILES = -(-NTOK // TR)
REG = REG_TILES * TR
MAX_TILES = -(-TOP_K * NTOK // TR) + N_EXPERTS
TD = 512


def _group_of_tile(i, tm):
    return jnp.maximum(i * tm // DEC_SEQ - (N_CTX // DEC_SEQ - 1), 0)


def _silu(x):
    return x * jax.nn.sigmoid(x)


def _log_sigmoid(x):
    return jnp.minimum(x, 0.0) - jnp.log(1.0 + jnp.exp(-jnp.abs(x)))


def _rmsnorm(x, g):
    return x * lax.rsqrt(jnp.mean(x * x, -1, keepdims=True) + EPS) * g


def _ada_kernel(cv_ref, w_ref, b_ref, o_ref):
    s = _silu(cv_ref[...]).astype(BF16)
    o_ref[0] = jnp.dot(s, w_ref[0].astype(BF16), preferred_element_type=F32) + b_ref[0]


def _ada(cvec, w_ada, b_ada):
    tn = 1536
    n = 6 * D_MODEL
    return pl.pallas_call(
        _ada_kernel,
        grid=(DEPTH, n // tn),
        in_specs=[
            pl.BlockSpec((N_GROUPS, D_MODEL), lambda l, j: (0, 0)),
            pl.BlockSpec((1, D_MODEL, tn), lambda l, j: (l, 0, j)),
            pl.BlockSpec((1, 1, tn), lambda l, j: (l, 0, j)),
        ],
        out_specs=pl.BlockSpec((1, N_GROUPS, tn), lambda l, j: (l, 0, j)),
        out_shape=jax.ShapeDtypeStruct((DEPTH, N_GROUPS, n), F32),
        compiler_params=pltpu.CompilerParams(
            dimension_semantics=("arbitrary", "arbitrary"), vmem_limit_bytes=VMEM_LIMIT),
        name="ada",
    )(cvec, w_ada, b_ada.reshape(DEPTH, 1, n))


def _rope_tables():
    half = DH // 4
    freqs = ROPE_BASE ** (-np.arange(half, dtype=np.float64) / half)
    t = np.arange(DEC_SEQ)
    pos = np.stack([t // GRID_W, t % GRID_W], 1).astype(np.float64)
    d = np.arange(DH)
    ang = pos[:, d // (DH // 2)] * freqs[d % half][None, :]
    sign = np.where((d % (DH // 2)) < half, -1.0, 1.0)[None, :]
    return np.cos(ang).astype(np.float32), (sign * np.sin(ang)).astype(np.float32)


def _rope(a, cos, sin):
    lane = lax.broadcasted_iota(jnp.int32, a.shape, 1)
    first = (lane % (DH // 2)) < (DH // 4)
    partner = jnp.where(first, pltpu.roll(a, DH - DH // 4, 1), pltpu.roll(a, DH // 4, 1))
    return a * cos + partner * sin


def _proj_kernel(x_ref, g_ref, mod_ref, w_ref, wg_ref, bg_ref, cos_ref, sin_ref,
                 p_ref, gate_ref, h_scr, *, n_ctx_tiles):
    i = pl.program_id(0)
    j = pl.program_id(1)

    @pl.when(j == 0)
    def _():
        h = _rmsnorm(x_ref[...], g_ref[0]) * (1.0 + mod_ref[0, 0, 1:2, :]) + mod_ref[0, 0, 0:1, :]
        h_scr[...] = h.astype(BF16)
        gate_ref[...] = _dot_f32x3(h, wg_ref[0]) + bg_ref[0]

    acc = jnp.dot(h_scr[...], w_ref[0].astype(BF16), preferred_element_type=F32)
    j_km = 1
    j_qr = P_COLS // 2 // TN
    j_kr = j_qr + 1

    @pl.when(j == j_km)
    def _():
        p_ref[...] = (acc * K_SCALE).astype(BF16)

    @pl.when((j == j_qr) | (j == j_kr))
    def _():
        is_lat = i >= n_ctx_tiles
        scale = jnp.where(j == j_kr, K_SCALE, 1.0).astype(F32)
        cos = cos_ref[...]
        sin = sin_ref[...]
        for hd in range(TN // DH):
            a = acc[:, hd * DH:(hd + 1) * DH]
            r = jnp.where(is_lat, _rope(a, cos, sin), a)
            p_ref[:, hd * DH:(hd + 1) * DH] = (r * scale).astype(BF16)

    @pl.when((j != j_km) & (j != j_qr) & (j != j_kr))
    def _():
        p_ref[...] = acc.astype(BF16)


def _proj(x, g1, mods, w2, wg, bg, cos_t, sin_t, layer):
    n_ctx_tiles = N_CTX // TM
    tiles_per_seq = DEC_SEQ // TM
    return pl.pallas_call(
        functools.partial(_proj_kernel, n_ctx_tiles=n_ctx_tiles),
        grid=(NTOK // TM, P_COLS // TN),
        in_specs=[
            pl.BlockSpec((TM, D_MODEL), lambda i, j: (i, 0)),
            pl.BlockSpec((1, 1, D_MODEL), lambda i, j: (layer, 0, 0)),
            pl.BlockSpec((1, 1, 6, D_MODEL), lambda i, j: (layer, _group_of_tile(i, TM), 0, 0)),
            pl.BlockSpec((1, D_MODEL, TN), lambda i, j: (layer, 0, j)),
            pl.BlockSpec((1, D_MODEL, LANES), lambda i, j: (layer, 0, 0)),
            pl.BlockSpec((1, 1, LANES), lambda i, j: (layer, 0, 0)),
            pl.BlockSpec((TM, DH), lambda i, j: (i % tiles_per_seq, 0)),
            pl.BlockSpec((TM, DH), lambda i, j: (i % tiles_per_seq, 0)),
        ],
        out_specs=[
            pl.BlockSpec((TM, TN), lambda i, j: (i, j)),
            pl.BlockSpec((TM, LANES), lambda i, j: (i, 0)),
        ],
        out_shape=[
            jax.ShapeDtypeStruct((NTOK, P_COLS), BF16),
            jax.ShapeDtypeStruct((NTOK, LANES), F32),
        ],
        scratch_shapes=[pltpu.VMEM((TM, D_MODEL), BF16)],
        compiler_params=pltpu.CompilerParams(
            dimension_semantics=("arbitrary", "arbitrary"), vmem_limit_bytes=VMEM_LIMIT),
        name="proj",
    )(x, g1, mods, w2, wg, bg, cos_t, sin_t)


def _split3(x):
    hi = x.astype(BF16)
    r1 = x - hi.astype(F32)
    mid = r1.astype(BF16)
    lo = (r1 - mid.astype(F32)).astype(BF16)
    return hi, mid, lo


def _dot(a, b):
    return jnp.dot(a, b, preferred_element_type=F32)


def _dot_nt(a, b):
    return lax.dot_general(a, b, (((1,), (1,)), ((), ())), preferred_element_type=F32)


def _tri_dot_left(tri, x):
    hi, mid, lo = _split3(x)
    return _dot(tri, hi) + _dot(tri, mid) + _dot(tri, lo)


def _tri_dot_right(x, tri):
    hi, mid, lo = _split3(x)
    return _dot(hi, tri) + _dot(mid, tri) + _dot(lo, tri)


def _scan_kernel(*refs, T, has_state):
    if has_state:
        (p_ref, g_ref, dl_ref, nm_ref, nr_ref, C0_ref, n0_ref, m0_ref, S0_ref, _yprev_ref,
         y_ref, C_s, n_s, m_s, S_s, hf_s, hb_s, dm_s, dq_s, dk_s) = refs
    else:
        (p_ref, g_ref, dl_ref, nm_ref, nr_ref,
         y_ref, C_out, n_out, m_out, S_out,
         C_s, n_s, m_s, S_s, hf_s, hb_s, dm_s, dq_s, dk_s) = refs
    L = CHUNK
    n_chunks = T // L
    row_i = lax.broadcasted_iota(jnp.int32, (L, L), 0)
    col_j = lax.broadcasted_iota(jnp.int32, (L, L), 1)
    lower = col_j <= row_i
    upper = col_j >= row_i
    tril = lower.astype(BF16)
    triu = upper.astype(BF16)
    pos = row_i.astype(F32)

    for d in range(2):
        for h in range(H_M):
            k = d * H_M + h
            if has_state:
                C_s[k] = C0_ref[0, 0, d, h]
                n_s[k] = n0_ref[0, 0, d, h:h + 1, :]
                m_s[k] = m0_ref[0, 0, k:k + 1, :]
                S_s[k] = S0_ref[0, 0, d, h]
            else:
                C_s[k] = jnp.zeros((DH, DH), F32)
                n_s[k] = jnp.zeros((1, DH), F32)
                m_s[k] = jnp.zeros((1, LANES), F32)
                S_s[k] = jnp.zeros((DH, DH), F32)
            lg = jnp.broadcast_to(_log_sigmoid(dl_ref[0, k:k + 1, :]), (L, L))
            rel = (row_i - col_j if d == 0 else col_j - row_i).astype(F32)
            dm_s[k] = jnp.where(rel >= 0, jnp.exp(lg * jnp.maximum(rel, 0.0)), 0.0)
            dq_s[k] = jnp.exp(lg * (pos + 1.0 if d == 0 else L - pos))
            dk_s[k] = jnp.exp(lg * (L - 1.0 - pos if d == 0 else pos))

    def chunk_step(c, carry):
        for d in range(2):
            r0 = pl.multiple_of((c if d == 0 else n_chunks - 1 - c) * L, L)
            mask = lower if d == 0 else upper
            G = g_ref[pl.ds(r0, L), :]
            FL = _log_sigmoid(G)
            GT = G.T
            FLT = FL.T
            Bc = _tri_dot_left(tril if d == 0 else triu, FL)
            Br = _tri_dot_right(FLT, triu if d == 0 else tril)
            h_dst = hf_s if d == 0 else hb_s
            for h in range(H_M):
                k = d * H_M + h
                ci = d * H_M + h
                cf = 2 * H_M + d * H_M + h
                q = p_ref[pl.ds(r0, L), h * DH:(h + 1) * DH]
                kk = p_ref[pl.ds(r0, L), W_M + h * DH:W_M + (h + 1) * DH]
                v = p_ref[pl.ds(r0, L), 2 * W_M + h * DH:2 * W_M + (h + 1) * DH]
                b_col = jnp.broadcast_to(Bc[:, cf:cf + 1], (L, L))
                b_row = jnp.broadcast_to(Br[cf:cf + 1, :], (L, L))
                i_row = jnp.broadcast_to(GT[ci:ci + 1, :], (L, L))
                i_col = jnp.broadcast_to(G[:, ci:ci + 1], (L, L))
                m_prev = m_s[k]
                C = C_s[k]
                n_row = n_s[k]
                logD = jnp.where(mask, b_col - b_row + i_row, -jnp.inf)
                m_inter = b_col + m_prev
                m_row = jnp.maximum(m_inter, jnp.max(logD, -1, keepdims=True))
                s = _dot_nt(q, kk) * jnp.exp(logD - m_row)
                w_inter = jnp.exp(m_inter - m_row)
                qf = q.astype(F32)
                num = _dot(s.astype(BF16), v) + w_inter * _dot(q, C.astype(BF16))
                den = jnp.sum(s, -1, keepdims=True) + w_inter * jnp.sum(qf * n_row, -1, keepdims=True)
                hh = num / jnp.maximum(jnp.abs(den), jnp.exp(-m_row))
                h_dst[pl.ds(r0, L), h * DH:(h + 1) * DH] = hh
                b_end = b_col[L - 1:L, :] if d == 0 else b_col[0:1, :]
                log_w = b_end - b_col + i_col
                m_new = jnp.maximum(b_end + m_prev, jnp.max(log_w, 0, keepdims=True))
                decay = jnp.exp(b_end + m_prev - m_new)
                wk = jnp.exp(log_w - m_new) * kk.astype(F32)
                C_s[k] = decay * C + _dot(wk.T.astype(BF16), v)
                n_s[k] = decay * n_row + jnp.sum(wk, 0, keepdims=True)
                m_s[k] = m_new
                c0 = 4 * W_M
                qr = p_ref[pl.ds(r0, L), c0 + h * DH:c0 + (h + 1) * DH]
                kr = p_ref[pl.ds(r0, L), c0 + W_R + h * DH:c0 + W_R + (h + 1) * DH]
                vr = p_ref[pl.ds(r0, L), c0 + 2 * W_R + h * DH:c0 + 2 * W_R + (h + 1) * DH]
                S = S_s[k]
                sr = _dot_nt(qr, kr) * dm_s[k]
                o = _dot(sr.astype(BF16), vr) + dq_s[k] * _dot(qr, S.astype(BF16))
                (hf_s if d == 0 else hb_s)[pl.ds(r0, L), W_M + h * DH:W_M + (h + 1) * DH] = o
                kd = kr.astype(F32) * dk_s[k]
                d_L = jnp.exp(_log_sigmoid(dl_ref[0, k:k + 1, :]) * float(L))
                S_s[k] = d_L * S + _dot(kd.T.astype(BF16), vr)
        return carry

    lax.fori_loop(0, n_chunks, chunk_step, 0)

    for h in range(H_M):
        sl = slice(h * DH, (h + 1) * DH)
        hs = hf_s[:, sl] + hb_s[:, sl]
        yn = _rmsnorm(hs, nm_ref[0, :, sl])
        om = p_ref[:, 3 * W_M + h * DH:3 * W_M + (h + 1) * DH].astype(F32)
        y_ref[:, sl] = (jax.nn.sigmoid(om) * yn).astype(BF16)
        slr = slice(W_M + h * DH, W_M + (h + 1) * DH)
        hr = hf_s[:, slr] + hb_s[:, slr]
        ynr = _rmsnorm(hr, nr_ref[0, :, sl])
        gr = p_ref[:, 4 * W_M + 3 * W_R + h * DH:4 * W_M + 3 * W_R + (h + 1) * DH].astype(F32)
        y_ref[:, slr] = (_silu(gr) * ynr).astype(BF16)

    if not has_state:
        for d in range(2):
            for h in range(H_M):
                k = d * H_M + h
                C_out[0, d, h] = C_s[k]
                S_out[0, d, h] = S_s[k]
                n_out[0, d, h:h + 1, :] = n_s[k]
                m_out[0, k:k + 1, :] = m_s[k]


def _scan_scratch(T):
    return [
        pltpu.VMEM((2 * H_M, DH, DH), F32),
        pltpu.VMEM((2 * H_M, 1, DH), F32),
        pltpu.VMEM((2 * H_M, 1, LANES), F32),
        pltpu.VMEM((2 * H_R, DH, DH), F32),
        pltpu.VMEM((T, W_M + W_R), F32),
        pltpu.VMEM((T, W_M + W_R), F32),
        pltpu.VMEM((2 * H_R, CHUNK, CHUNK), F32),
        pltpu.VMEM((2 * H_R, CHUNK, CHUNK), F32),
        pltpu.VMEM((2 * H_R, CHUNK, CHUNK), F32),
    ]


def _scan_ctx(p, gates, dl, nm, nr, layer):
    T = SEQ
    common = [
        pl.BlockSpec((T, P_COLS), lambda b: (b, 0)),
        pl.BlockSpec((T, LANES), lambda b: (b, 0)),
        pl.BlockSpec((1, 2 * H_R, LANES), lambda b: (layer, 0, 0)),
        pl.BlockSpec((1, 1, W_M), lambda b: (layer, 0, 0)),
        pl.BlockSpec((1, 1, W_R), lambda b: (layer, 0, 0)),
    ]
    return pl.pallas_call(
        functools.partial(_scan_kernel, T=T, has_state=False),
        grid=(BATCH,),
        in_specs=common,
        out_specs=[
            pl.BlockSpec((T, D_MODEL), lambda b: (b, 0)),
            pl.BlockSpec((1, 2, H_M, DH, DH), lambda b: (b, 0, 0, 0, 0)),
            pl.BlockSpec((1, 2, H_M, DH), lambda b: (b, 0, 0, 0)),
            pl.BlockSpec((1, 2 * H_M, LANES), lambda b: (b, 0, 0)),
            pl.BlockSpec((1, 2, H_R, DH, DH), lambda b: (b, 0, 0, 0, 0)),
        ],
        out_shape=[
            jax.ShapeDtypeStruct((NTOK, D_MODEL), BF16),
            jax.ShapeDtypeStruct((BATCH, 2, H_M, DH, DH), F32),
            jax.ShapeDtypeStruct((BATCH, 2, H_M, DH), F32),
            jax.ShapeDtypeStruct((BATCH, 2 * H_M, LANES), F32),
            jax.ShapeDtypeStruct((BATCH, 2, H_R, DH, DH), F32),
        ],
        scratch_shapes=_scan_scratch(T),
        compiler_params=pltpu.CompilerParams(
            dimension_semantics=("arbitrary",), vmem_limit_bytes=VMEM_LIMIT),
        name="scan_ctx",
    )(p, gates, dl, nm, nr)


def _scan_lat(p, gates, dl, nm, nr, C0, n0, m0, S0, y_prev, layer):
    T = DEC_SEQ
    off = N_CTX // T
    in_specs = [
        pl.BlockSpec((T, P_COLS), lambda b: (off + b, 0)),
        pl.BlockSpec((T, LANES), lambda b: (off + b, 0)),
        pl.BlockSpec((1, 2 * H_R, LANES), lambda b: (layer, 0, 0)),
        pl.BlockSpec((1, 1, W_M), lambda b: (layer, 0, 0)),
        pl.BlockSpec((1, 1, W_R), lambda b: (layer, 0, 0)),
        pl.BlockSpec((1, 1, 2, H_M, DH, DH), lambda b: (b, layer, 0, 0, 0, 0)),
        pl.BlockSpec((1, 1, 2, H_M, DH), lambda b: (b, layer, 0, 0, 0)),
        pl.BlockSpec((1, 1, 2 * H_M, LANES), lambda b: (b, layer, 0, 0)),
        pl.BlockSpec((1, 1, 2, H_R, DH, DH), lambda b: (b, layer, 0, 0, 0, 0)),
        pl.BlockSpec(memory_space=pl.ANY),
    ]
    return pl.pallas_call(
        functools.partial(_scan_kernel, T=T, has_state=True),
        grid=(DEC_BATCH,),
        in_specs=in_specs,
        out_specs=pl.BlockSpec((T, D_MODEL), lambda b: (off + b, 0)),
        out_shape=jax.ShapeDtypeStruct((NTOK, D_MODEL), BF16),
        input_output_aliases={9: 0},
        scratch_shapes=_scan_scratch(T),
        compiler_params=pltpu.CompilerParams(
            dimension_semantics=("arbitrary",), vmem_limit_bytes=VMEM_LIMIT),
        name="scan_lat",
    )(p, gates, dl, nm, nr, C0, n0, m0, S0, y_prev)


def _top2(logits):
    lane = lax.broadcasted_iota(jnp.int32, logits.shape, 1)
    v1 = jnp.max(logits, -1, keepdims=True)
    i1 = jnp.min(jnp.where(logits == v1, lane, LANES), -1, keepdims=True)
    rest = jnp.where(lane == i1, -jnp.inf, logits)
    v2 = jnp.max(rest, -1, keepdims=True)
    i2 = jnp.min(jnp.where(rest == v2, lane, LANES), -1, keepdims=True)
    e2 = jnp.exp(v2 - v1)
    return i1, i2, 1.0 / (1.0 + e2), e2 / (1.0 + e2)


def _split2(x):
    hi = x.astype(BF16)
    return hi, (x - hi.astype(F32)).astype(BF16)


def _dot_f32x3(a, b):
    a_hi, a_lo = _split2(a)
    b_hi, b_lo = _split2(b)
    return _dot(a_hi, b_hi) + _dot(a_hi, b_lo) + _dot(a_lo, b_hi)


R_E1, R_E2, R_W1, R_W2, R_S1, R_S2 = range(6)


def _out_kernel(*refs, with_router):
    if with_router:
        (y_ref, x_ref, w_ref, g_ref, mod_ref, wr_ref,
         x1_ref, h2_ref, rinfo_ref, cnt_ref, w_scr, tri_scr, cnt_scr) = refs
    else:
        y_ref, x_ref, w_ref, g_ref, mod_ref, x1_ref, h2_ref, w_scr = refs

    @pl.when(pl.program_id(0) == 0)
    def _():
        w_scr[...] = w_ref[0].astype(BF16)
        if with_router:
            r = lax.broadcasted_iota(jnp.int32, (TM, TM), 0)
            c = lax.broadcasted_iota(jnp.int32, (TM, TM), 1)
            tri_scr[...] = (c < r).astype(BF16)
            cnt_scr[...] = jnp.zeros_like(cnt_scr)

    o = jnp.dot(y_ref[...], w_scr[...], preferred_element_type=F32)
    x1 = x_ref[...] + mod_ref[0, 0, 2:3, :] * o
    x1_ref[...] = x1
    h2 = _rmsnorm(x1, g_ref[0]) * (1.0 + mod_ref[0, 0, 4:5, :]) + mod_ref[0, 0, 3:4, :]
    if not with_router:
        h2_ref[...] = h2.astype(BF16)
    else:
        h2_ref[...] = h2
        logits = _dot_f32x3(h2, wr_ref[0])
        lane = lax.broadcasted_iota(jnp.int32, logits.shape, 1)
        i1, i2, w1, w2 = _top2(jnp.where(lane < N_EXPERTS, logits, -jnp.inf))
        oh1 = lane == i1
        oh2 = lane == i2
        sel = jnp.where(oh1 | oh2, 1.0, 0.0)
        rank = _dot(tri_scr[...], sel.astype(BF16)) + cnt_scr[...]
        r1 = jnp.sum(jnp.where(oh1, rank, 0.0), -1, keepdims=True)
        r2 = jnp.sum(jnp.where(oh2, rank, 0.0), -1, keepdims=True)
        s1 = i1.astype(F32) * float(REG) + r1
        s2 = i2.astype(F32) * float(REG) + r2
        info = jnp.zeros(logits.shape, F32)
        for col, val in ((R_E1, i1.astype(F32)), (R_E2, i2.astype(F32)), (R_W1, w1), (R_W2, w2),
                         (R_S1, s1), (R_S2, s2)):
            info = jnp.where(lane == col, val, info)
        rinfo_ref[...] = info
        cnt_scr[...] += jnp.sum(sel, 0, keepdims=True)
        cnt_ref[...] = cnt_scr[...]


def _out(y, x, w_out, g2, mods, layer, w_router_pad=None, router_idx=0):
    with_router = w_router_pad is not None
    in_specs = [
        pl.BlockSpec((TM, D_MODEL), lambda i: (i, 0)),
        pl.BlockSpec((TM, D_MODEL), lambda i: (i, 0)),
        pl.BlockSpec((1, D_MODEL, D_MODEL), lambda i: (layer, 0, 0)),
        pl.BlockSpec((1, 1, D_MODEL), lambda i: (layer, 0, 0)),
        pl.BlockSpec((1, 1, 6, D_MODEL), lambda i: (layer, _group_of_tile(i, TM), 0, 0)),
    ]
    out_specs = [
        pl.BlockSpec((TM, D_MODEL), lambda i: (i, 0)),
        pl.BlockSpec((TM, D_MODEL), lambda i: (i, 0)),
    ]
    out_shape = [
        jax.ShapeDtypeStruct((NTOK, D_MODEL), F32),
        jax.ShapeDtypeStruct((NTOK, D_MODEL), F32 if with_router else BF16),
    ]
    args = [y, x, w_out, g2, mods]
    scratch = [pltpu.VMEM((D_MODEL, D_MODEL), BF16)]
    if with_router:
        in_specs.append(pl.BlockSpec((1, D_MODEL, LANES), lambda i: (router_idx, 0, 0)))
        out_specs += [pl.BlockSpec((TM, LANES), lambda i: (i, 0)),
                      pl.BlockSpec((1, LANES), lambda i: (0, 0))]
        out_shape += [jax.ShapeDtypeStruct((NTOK, LANES), F32),
                      jax.ShapeDtypeStruct((1, LANES), F32)]
        args.append(w_router_pad)
        scratch += [pltpu.VMEM((TM, TM), BF16), pltpu.VMEM((1, LANES), F32)]
    return pl.pallas_call(
        functools.partial(_out_kernel, with_router=with_router),
        grid=(NTOK // TM,),
        in_specs=in_specs,
        out_specs=out_specs,
        out_shape=out_shape,
        scratch_shapes=scratch,
        compiler_params=pltpu.CompilerParams(
            dimension_semantics=("arbitrary",), vmem_limit_bytes=VMEM_LIMIT),
        name="out_router" if with_router else "out",
    )(*args)


def _swiglu_step(h, wg_ref, wu_ref, wd_ref, acc):
    w_cat = jnp.concatenate([wg_ref[0].astype(BF16), wu_ref[0].astype(BF16)], axis=1)
    ab = jnp.dot(h, w_cat, preferred_element_type=F32)
    t = (_silu(ab[:, :FC]) * ab[:, FC:]).astype(BF16)
    acc[...] += jnp.dot(t, wd_ref[0].astype(BF16), preferred_element_type=F32)


def _ffn_kernel(h_ref, res_ref, wg_ref, wu_ref, wd_ref, mod_ref, o_ref, acc):
    f = pl.program_id(1)

    @pl.when(f == 0)
    def _():
        acc[...] = jnp.zeros_like(acc)

    _swiglu_step(h_ref[...], wg_ref, wu_ref, wd_ref, acc)

    @pl.when(f == pl.num_programs(1) - 1)
    def _():
        o_ref[...] = res_ref[...] + mod_ref[0, 0, 5:6, :] * acc[...]


def _ffn(h2, res, wg, wu, wd, mods, layer, w_idx):
    return pl.pallas_call(
        _ffn_kernel,
        grid=(NTOK // TM, D_FF // FC),
        in_specs=[
            pl.BlockSpec((TM, D_MODEL), lambda i, f: (i, 0)),
            pl.BlockSpec((TM, D_MODEL), lambda i, f: (i, 0)),
            pl.BlockSpec((1, D_MODEL, FC), lambda i, f: (w_idx, 0, f)),
            pl.BlockSpec((1, D_MODEL, FC), lambda i, f: (w_idx, 0, f)),
            pl.BlockSpec((1, FC, D_MODEL), lambda i, f: (w_idx, f, 0)),
            pl.BlockSpec((1, 1, 6, D_MODEL), lambda i, f: (layer, _group_of_tile(i, TM), 0, 0)),
        ],
        out_specs=pl.BlockSpec((TM, D_MODEL), lambda i, f: (i, 0)),
        out_shape=jax.ShapeDtypeStruct((NTOK, D_MODEL), F32),
        scratch_shapes=[pltpu.VMEM((TM, D_MODEL), F32)],
        compiler_params=pltpu.CompilerParams(
            dimension_semantics=("arbitrary", "arbitrary"), vmem_limit_bytes=VMEM_LIMIT),
        name="ffn",
    )(h2, res, wg, wu, wd, mods)


def _tile_plan(counts):
    nt = (counts + TR - 1) // TR
    cum = jnp.cumsum(nt)
    total = cum[-1]
    t = jnp.arange(MAX_TILES, dtype=jnp.int32)
    tt = jnp.minimum(t, total - 1)
    e = jnp.sum((cum[None, :] <= tt[:, None]).astype(jnp.int32), axis=1)
    k = tt - (cum - nt)[e]
    n = jnp.where(t < total, jnp.clip(counts[e] - k * TR, 0, TR), 0)
    return e.astype(jnp.int32), (e * REG_TILES + k).astype(jnp.int32), n.astype(jnp.int32)


def _row_copy(src, src_row, dst, dst_row, sem):
    return pltpu.make_async_copy(src.at[pl.ds(src_row, 1)], dst.at[pl.ds(dst_row, 1)], sem)


def _dispatch_kernel(slot_ref, h_ref, xs_ref, sem):
    base = pl.program_id(0) * (TOP_K * TD)

    def issue(r, carry):
        for k in range(TOP_K):
            _row_copy(h_ref, r, xs_ref, slot_ref[base + TOP_K * r + k], sem).start()
        return carry

    lax.fori_loop(0, TD, issue, 0, unroll=8)
    for k in range(TOP_K):
        pltpu.make_async_copy(h_ref, xs_ref.at[pl.ds(0, TD)], sem).wait()


def _dispatch(slots, h2f):
    return pl.pallas_call(
        _dispatch_kernel,
        grid_spec=pltpu.PrefetchScalarGridSpec(
            num_scalar_prefetch=1,
            grid=(NTOK // TD,),
            in_specs=[pl.BlockSpec((TD, D_MODEL), lambda i, s: (i, 0))],
            out_specs=pl.BlockSpec(memory_space=pl.ANY),
            scratch_shapes=[pltpu.SemaphoreType.DMA],
        ),
        out_shape=jax.ShapeDtypeStruct((N_EXPERTS * REG, D_MODEL), F32),
        compiler_params=pltpu.CompilerParams(
            dimension_semantics=("arbitrary",), vmem_limit_bytes=VMEM_LIMIT),
        name="moe_dispatch",
    )(slots, h2f)


def _gffn_kernel(te_ref, tb_ref, tn_ref, x_ref, wg_ref, wu_ref, wd_ref, o_ref, h_scr, acc):
    del te_ref, tb_ref
    f = pl.program_id(1)
    n = tn_ref[pl.program_id(0)]

    @pl.when(n > 0)
    def _():
        @pl.when(f == 0)
        def _():
            row = lax.broadcasted_iota(jnp.int32, (TR, D_MODEL), 0)
            h_scr[...] = jnp.where(row < n, x_ref[...], 0.0).astype(BF16)
            acc[...] = jnp.zeros_like(acc)

        _swiglu_step(h_scr[...], wg_ref, wu_ref, wd_ref, acc)

        @pl.when(f == pl.num_programs(1) - 1)
        def _():
            o_ref[...] = acc[...]


def _gffn(tile_e, tile_blk, tile_n, xs, wg, wu, wd, w_base):
    nf = D_FF // FC

    def f_eff(t, f, tn):
        return jnp.where(tn[t] > 0, f, nf - 1)

    return pl.pallas_call(
        _gffn_kernel,
        grid_spec=pltpu.PrefetchScalarGridSpec(
            num_scalar_prefetch=3,
            grid=(MAX_TILES, nf),
            in_specs=[
                pl.BlockSpec((TR, D_MODEL), lambda t, f, te, tb, tn: (tb[t], 0)),
                pl.BlockSpec((1, D_MODEL, FC), lambda t, f, te, tb, tn: (w_base + te[t], 0, f_eff(t, f, tn))),
                pl.BlockSpec((1, D_MODEL, FC), lambda t, f, te, tb, tn: (w_base + te[t], 0, f_eff(t, f, tn))),
                pl.BlockSpec((1, FC, D_MODEL), lambda t, f, te, tb, tn: (w_base + te[t], f_eff(t, f, tn), 0)),
            ],
            out_specs=pl.BlockSpec((TR, D_MODEL), lambda t, f, te, tb, tn: (tb[t], 0)),
            scratch_shapes=[pltpu.VMEM((TR, D_MODEL), BF16), pltpu.VMEM((TR, D_MODEL), F32)],
        ),
        out_shape=jax.ShapeDtypeStruct((N_EXPERTS * REG, D_MODEL), F32),
        compiler_params=pltpu.CompilerParams(
            dimension_semantics=("arbitrary", "arbitrary"), vmem_limit_bytes=VMEM_LIMIT),
        name="moe_ffn",
    )(tile_e, tile_blk, tile_n, xs, wg, wu, wd)


def _combine_kernel(slot_ref, x1_ref, rinfo_ref, mod_ref, ys_ref, o_ref, buf, sem):
    base = pl.program_id(0) * (TOP_K * TD)

    def issue(r, carry):
        for k in range(TOP_K):
            _row_copy(ys_ref, slot_ref[base + TOP_K * r + k], buf.at[k], r, sem).start()
        return carry

    lax.fori_loop(0, TD, issue, 0, unroll=8)
    for k in range(TOP_K):
        pltpu.make_async_copy(ys_ref.at[pl.ds(0, TD)], buf.at[k], sem).wait()
    y = rinfo_ref[:, R_W1:R_W1 + 1] * buf[0] + rinfo_ref[:, R_W2:R_W2 + 1] * buf[1]
    o_ref[...] = x1_ref[...] + mod_ref[0, 0, 5:6, :] * y


def _combine(slots, x1, rinfo, mods, ys, layer):
    return pl.pallas_call(
        _combine_kernel,
        grid_spec=pltpu.PrefetchScalarGridSpec(
            num_scalar_prefetch=1,
            grid=(NTOK // TD,),
            in_specs=[
                pl.BlockSpec((TD, D_MODEL), lambda i, s: (i, 0)),
                pl.BlockSpec((TD, LANES), lambda i, s: (i, 0)),
                pl.BlockSpec((1, 1, 6, D_MODEL), lambda i, s: (layer, _group_of_tile(i, TD), 0, 0)),
                pl.BlockSpec(memory_space=pl.ANY),
            ],
            out_specs=pl.BlockSpec((TD, D_MODEL), lambda i, s: (i, 0)),
            scratch_shapes=[pltpu.VMEM((TOP_K, TD, D_MODEL), F32), pltpu.SemaphoreType.DMA],
        ),
        out_shape=jax.ShapeDtypeStruct((NTOK, D_MODEL), F32),
        compiler_params=pltpu.CompilerParams(
            dimension_semantics=("arbitrary",), vmem_limit_bytes=VMEM_LIMIT),
        name="moe_combine",
    )(slots, x1, rinfo, mods, ys)


def _final_kernel(x_ref, g_ref, o_ref):
    o_ref[...] = _rmsnorm(x_ref[...], g_ref[...])


def _final(x, g, row_off, rows):
    off = row_off // TM
    return pl.pallas_call(
        _final_kernel,
        grid=(rows // TM,),
        in_specs=[
            pl.BlockSpec((TM, D_MODEL), lambda i: (off + i, 0)),
            pl.BlockSpec((1, D_MODEL), lambda i: (0, 0)),
        ],
        out_specs=pl.BlockSpec((TM, D_MODEL), lambda i: (i, 0)),
        out_shape=jax.ShapeDtypeStruct((rows, D_MODEL), F32),
        compiler_params=pltpu.CompilerParams(
            dimension_semantics=("arbitrary",), vmem_limit_bytes=VMEM_LIMIT),
        name="final_norm",
    )(x, g)


def kernel(x_prompt, x_sample, state_mlstm_C, state_mlstm_n, state_mlstm_m, state_ret_S, c, c_ctx,
           norm1_g, norm2_g, norm_f_g, w_ada, b_ada, w_in, b_gates, ret_decay_logit,
           mlstm_norm_g, ret_norm_g, w_out, ffn_w_gate, ffn_w_up, ffn_w_down,
           moe_w_router, moe_w_gate, moe_w_up, moe_w_down):
    x = jnp.concatenate([x_prompt.reshape(N_CTX, D_MODEL), x_sample.reshape(N_LAT, D_MODEL)], 0)
    cvec = jnp.concatenate(
        [c_ctx[None, :], c, jnp.zeros((N_GROUPS - 1 - DEC_BATCH, D_MODEL), F32)], 0)
    mods = _ada(cvec, w_ada, b_ada).reshape(DEPTH, N_GROUPS, 6, D_MODEL)

    n_m = 4 * W_M
    w2 = jnp.concatenate([w_in[:, :, :n_m], w_in[:, :, n_m + N_GATES:]], -1)
    wg = jnp.pad(w_in[:, :, n_m:n_m + N_GATES], ((0, 0), (0, 0), (0, LANES - N_GATES)))
    bg = jnp.pad(b_gates, ((0, 0), (0, LANES - N_GATES))).reshape(DEPTH, 1, LANES)
    cos_np, sin_np = _rope_tables()
    cos_t, sin_t = jnp.asarray(cos_np), jnp.asarray(sin_np)
    dl = jnp.broadcast_to(ret_decay_logit.reshape(DEPTH, 2 * H_R, 1), (DEPTH, 2 * H_R, LANES))
    m0 = jnp.broadcast_to(state_mlstm_m.reshape(DEC_BATCH, DEPTH, 2 * H_M, 1),
                          (DEC_BATCH, DEPTH, 2 * H_M, LANES))
    g1 = norm1_g.reshape(DEPTH, 1, D_MODEL)
    g2 = norm2_g.reshape(DEPTH, 1, D_MODEL)
    nm = mlstm_norm_g.reshape(DEPTH, 1, W_M)
    nr = ret_norm_g.reshape(DEPTH, 1, W_R)
    n_moe = moe_w_router.shape[0]
    wr_pad = jnp.pad(moe_w_router, ((0, 0), (0, 0), (0, LANES - N_EXPERTS)))
    moe_g = moe_w_gate.reshape(n_moe * N_EXPERTS, D_MODEL, D_FF)
    moe_u = moe_w_up.reshape(n_moe * N_EXPERTS, D_MODEL, D_FF)
    moe_d = moe_w_down.reshape(n_moe * N_EXPERTS, D_FF, D_MODEL)

    new_C, new_n, new_m, new_S = [], [], [], []
    for l in range(DEPTH):
        jl = l // 2
        p, gates = _proj(x, g1, mods, w2, wg, bg, cos_t, sin_t, l)
        y, C_l, n_l, m_l, S_l = _scan_ctx(p, gates, dl, nm, nr, l)
        y = _scan_lat(p, gates, dl, nm, nr, state_mlstm_C, state_mlstm_n, m0, state_ret_S, y, l)
        new_C.append(C_l)
        new_n.append(n_l)
        new_m.append(m_l[:, :, 0].reshape(BATCH, 2, H_M))
        new_S.append(S_l)
        if l % 2 == 0:
            x1, h2 = _out(y, x, w_out, g2, mods, l)
            x = _ffn(h2, x1, ffn_w_gate, ffn_w_up, ffn_w_down, mods, l, jl)
        else:
            x1, h2f, rinfo, cnt = _out(y, x, w_out, g2, mods, l, wr_pad, jl)
            slots = rinfo[:, R_S1:R_S2 + 1].astype(jnp.int32).reshape(TOP_K * NTOK)
            tile_e, tile_blk, tile_n = _tile_plan(cnt[0, :N_EXPERTS].astype(jnp.int32))
            xs = _dispatch(slots, h2f)
            ys = _gffn(tile_e, tile_blk, tile_n, xs, moe_g, moe_u, moe_d, jl * N_EXPERTS)
            x = _combine(slots, x1, rinfo, mods, ys, l)

    y_prompt = _final(x, norm_f_g.reshape(1, D_MODEL), 0, N_CTX).reshape(BATCH, SEQ, D_MODEL)
    y_sample = _final(x, norm_f_g.reshape(1, D_MODEL), N_CTX, N_LAT).reshape(DEC_BATCH, DEC_SEQ, D_MODEL)
    return (y_prompt, y_sample, jnp.stack(new_C, 1), jnp.stack(new_n, 1),
            jnp.stack(new_m, 1), jnp.stack(new_S, 1))
```

```python
import functools

import numpy as np
import jax
import jax.numpy as jnp
from jax import lax
from jax.experimental import pallas as pl
from jax.experimental.pallas import tpu as pltpu

D_MODEL = 1024
BATCH = 32
SEQ = 256
DEPTH = 2
DEC_BATCH = 2
DEC_SEQ = 1024
GRID_W = 64
H_M = 4
DH = 128
H_R = 4
W_M = H_M * DH
W_R = H_R * DH
N_GATES = 4 * H_M
CHUNK = 128
D_FF = 2816
N_EXPERTS = 8
ROPE_BASE = 10000.0
EPS = 1e-6

N_CTX = BATCH * SEQ
N_LAT = DEC_BATCH * DEC_SEQ
NTOK = N_CTX + N_LAT
N_GROUPS = 8
K_SCALE = DH ** -0.5
P_COLS = 4 * W_M + 4 * W_R
LANES = 128
VMEM_LIMIT = 56 * 1024 * 1024

F32 = jnp.float32
BF16 = jnp.bfloat16
HIGHEST = lax.Precision.HIGHEST

TM = 1024
TN = 1024
FC = 256
TOP_K = 2
TR = 896
REG_TILES = -(-NTOK // TR)
REG = REG_TILES * TR
MAX_TILES = -(-TOP_K * NTOK // TR) + N_EXPERTS
TD = 512


def _group_of_tile(i, tm):
    return jnp.maximum(i * tm // DEC_SEQ - (N_CTX // DEC_SEQ - 1), 0)


def _silu(x):
    return x * jax.nn.sigmoid(x)


def _log_sigmoid(x):
    return jnp.minimum(x, 0.0) - jnp.log(1.0 + jnp.exp(-jnp.abs(x)))


def _rmsnorm(x, g):
    return x * lax.rsqrt(jnp.mean(x * x, -1, keepdims=True) + EPS) * g


def _ada_kernel(cv_ref, w_ref, b_ref, o_ref):
    s = _silu(cv_ref[...]).astype(BF16)
    o_ref[0] = jnp.dot(s, w_ref[0].astype(BF16), preferred_element_type=F32) + b_ref[0]


def _ada(cvec, w_ada, b_ada):
    tn = 1536
    n = 6 * D_MODEL
    return pl.pallas_call(
        _ada_kernel,
        grid=(DEPTH, n // tn),
        in_specs=[
            pl.BlockSpec((N_GROUPS, D_MODEL), lambda l, j: (0, 0)),
            pl.BlockSpec((1, D_MODEL, tn), lambda l, j: (l, 0, j)),
            pl.BlockSpec((1, 1, tn), lambda l, j: (l, 0, j)),
        ],
        out_specs=pl.BlockSpec((1, N_GROUPS, tn), lambda l, j: (l, 0, j)),
        out_shape=jax.ShapeDtypeStruct((DEPTH, N_GROUPS, n), F32),
        compiler_params=pltpu.CompilerParams(
            dimension_semantics=("arbitrary", "arbitrary"), vmem_limit_bytes=VMEM_LIMIT),
        name="ada",
    )(cvec, w_ada, b_ada.reshape(DEPTH, 1, n))


def _rope_tables():
    half = DH // 4
    freqs = ROPE_BASE ** (-np.arange(half, dtype=np.float64) / half)
    t = np.arange(DEC_SEQ)
    pos = np.stack([t // GRID_W, t % GRID_W], 1).astype(np.float64)
    d = np.arange(DH)
    ang = pos[:, d // (DH // 2)] * freqs[d % half][None, :]
    sign = np.where((d % (DH // 2)) < half, -1.0, 1.0)[None, :]
    return np.cos(ang).astype(np.float32), (sign * np.sin(ang)).astype(np.float32)


def _rope(a, cos, sin):
    lane = lax.broadcasted_iota(jnp.int32, a.shape, 1)
    first = (lane % (DH // 2)) < (DH // 4)
    partner = jnp.where(first, pltpu.roll(a, DH - DH // 4, 1), pltpu.roll(a, DH // 4, 1))
    return a * cos + partner * sin


def _proj_kernel(*refs, n_ctx_tiles, split_x):
    if split_x:
        (xp_ref, xl_ref, g_ref, mod_ref, wa_ref, wb_ref, wg_ref, bg_ref, cos_ref, sin_ref,
         p_ref, gate_ref, h_scr) = refs
    else:
        (x_ref, g_ref, mod_ref, wa_ref, wb_ref, wg_ref, bg_ref, cos_ref, sin_ref,
         p_ref, gate_ref, h_scr) = refs
    i = pl.program_id(0)
    j = pl.program_id(1)
    is_lat = i >= n_ctx_tiles
    half = TN // 2

    def prologue(x):
        h = _rmsnorm(x, g_ref[0]) * (1.0 + mod_ref[0, 0, 1:2, :]) + mod_ref[0, 0, 0:1, :]
        h_scr[...] = h.astype(BF16)
        gate_ref[...] = _dot_f32x3(h, wg_ref[0]) + bg_ref[0]

    @pl.when(j == 0)
    def _():
        if split_x:
            pl.when(jnp.logical_not(is_lat))(lambda: prologue(xp_ref[...]))
            pl.when(is_lat)(lambda: prologue(xl_ref[...]))
        else:
            prologue(x_ref[...])

    def matmul(w_ref):
        return jnp.dot(h_scr[...], w_ref[0].astype(BF16), preferred_element_type=F32)

    @pl.when(j == 0)
    def _():
        acc = matmul(wa_ref)
        p_ref[:, :half] = acc[:, :half].astype(BF16)
        p_ref[:, half:] = (acc[:, half:] * K_SCALE).astype(BF16)

    @pl.when(j == 1)
    def _():
        p_ref[...] = matmul(wa_ref).astype(BF16)

    @pl.when(j == 2)
    def _():
        acc = matmul(wb_ref)

        @pl.when(is_lat)
        def _():
            cos = cos_ref[...]
            sin = sin_ref[...]
            for hd in range(TN // DH):
                sl = slice(hd * DH, (hd + 1) * DH)
                r = _rope(acc[:, sl], cos, sin)
                p_ref[:, sl] = (r * K_SCALE if hd * DH >= half else r).astype(BF16)

        @pl.when(jnp.logical_not(is_lat))
        def _():
            p_ref[:, :half] = acc[:, :half].astype(BF16)
            p_ref[:, half:] = (acc[:, half:] * K_SCALE).astype(BF16)

    @pl.when(j == 3)
    def _():
        p_ref[...] = matmul(wb_ref).astype(BF16)


def _proj(xs, g1, mods, w_in, w_ret, wg, bg, cos_t, sin_t, layer):
    n_ctx_tiles = N_CTX // TM
    tiles_per_seq = DEC_SEQ // TM
    split_x = len(xs) == 2
    if split_x:
        x_specs = [pl.BlockSpec((TM, D_MODEL), lambda i, j: (jnp.minimum(i, n_ctx_tiles - 1), 0)),
                   pl.BlockSpec((TM, D_MODEL), lambda i, j: (jnp.maximum(i - n_ctx_tiles, 0), 0))]
    else:
        x_specs = [pl.BlockSpec((TM, D_MODEL), lambda i, j: (i, 0))]
    n_a = 4 * W_M // TN
    return pl.pallas_call(
        functools.partial(_proj_kernel, n_ctx_tiles=n_ctx_tiles, split_x=split_x),
        grid=(NTOK // TM, P_COLS // TN),
        in_specs=x_specs + [
            pl.BlockSpec((1, 1, D_MODEL), lambda i, j: (layer, 0, 0)),
            pl.BlockSpec((1, 1, 6, D_MODEL), lambda i, j: (layer, _group_of_tile(i, TM), 0, 0)),
            pl.BlockSpec((1, D_MODEL, TN), lambda i, j: (layer, 0, jnp.minimum(j, n_a - 1))),
            pl.BlockSpec((1, D_MODEL, TN), lambda i, j: (layer, 0, jnp.maximum(j - n_a, 0))),
            pl.BlockSpec((1, D_MODEL, LANES), lambda i, j: (layer, 0, 0)),
            pl.BlockSpec((1, 1, LANES), lambda i, j: (layer, 0, 0)),
            pl.BlockSpec((TM, DH), lambda i, j: (i % tiles_per_seq, 0)),
            pl.BlockSpec((TM, DH), lambda i, j: (i % tiles_per_seq, 0)),
        ],
        out_specs=[
            pl.BlockSpec((TM, TN), lambda i, j: (i, j)),
            pl.BlockSpec((TM, LANES), lambda i, j: (i, 0)),
        ],
        out_shape=[
            jax.ShapeDtypeStruct((NTOK, P_COLS), BF16),
            jax.ShapeDtypeStruct((NTOK, LANES), F32),
        ],
        scratch_shapes=[pltpu.VMEM((TM, D_MODEL), BF16)],
        compiler_params=pltpu.CompilerParams(
            dimension_semantics=("arbitrary", "arbitrary"), vmem_limit_bytes=VMEM_LIMIT),
        name="proj",
    )(*xs, g1, mods, w_in, w_ret, wg, bg, cos_t, sin_t)


def _split3(x):
    hi = x.astype(BF16)
    r1 = x - hi.astype(F32)
    mid = r1.astype(BF16)
    lo = (r1 - mid.astype(F32)).astype(BF16)
    return hi, mid, lo


def _dot(a, b):
    return jnp.dot(a, b, preferred_element_type=F32)


def _dot_nt(a, b):
    return lax.dot_general(a, b, (((1,), (1,)), ((), ())), preferred_element_type=F32)


def _tri_dot_left(tri, x):
    hi, mid, lo = _split3(x)
    return _dot(tri, hi) + _dot(tri, mid) + _dot(tri, lo)


def _tri_dot_right(x, tri):
    hi, mid, lo = _split3(x)
    return _dot(hi, tri) + _dot(mid, tri) + _dot(lo, tri)


def _scan_kernel(*refs, T, has_state, n_aliased=0):
    if has_state:
        (p_ref, g_ref, dl_ref, nm_ref, nr_ref, C0_ref, n0_ref, m0_ref, S0_ref, _yprev_ref,
         y_ref, C_s, n_s, m_s, S_s, hf_s, hb_s, dm_s, dq_s, dk_s) = refs
    else:
        p_ref, g_ref, dl_ref, nm_ref, nr_ref = refs[:5]
        (y_ref, C_out, n_out, m_out, S_out,
         C_s, n_s, m_s, S_s, hf_s, hb_s, dm_s, dq_s, dk_s) = refs[5 + n_aliased:]
    L = CHUNK
    n_chunks = T // L
    row_i = lax.broadcasted_iota(jnp.int32, (L, L), 0)
    col_j = lax.broadcasted_iota(jnp.int32, (L, L), 1)
    lower = col_j <= row_i
    upper = col_j >= row_i
    tril = lower.astype(BF16)
    triu = upper.astype(BF16)
    pos = row_i.astype(F32)

    for d in range(2):
        for h in range(H_M):
            k = d * H_M + h
            if has_state:
                C_s[k] = C0_ref[0, 0, d, h]
                n_s[k] = n0_ref[0, 0, d, h:h + 1, :]
                m_s[k] = m0_ref[0, 0, k:k + 1, :]
                S_s[k] = S0_ref[0, 0, d, h]
            else:
                C_s[k] = jnp.zeros((DH, DH), F32)
                n_s[k] = jnp.zeros((1, DH), F32)
                m_s[k] = jnp.zeros((1, LANES), F32)
                S_s[k] = jnp.zeros((DH, DH), F32)

    @pl.when(pl.program_id(0) == 0)
    def _():
        for d in range(2):
            for h in range(H_R):
                k = d * H_R + h
                lg = jnp.broadcast_to(_log_sigmoid(dl_ref[0, k:k + 1, :]), (L, L))
                rel = (row_i - col_j if d == 0 else col_j - row_i).astype(F32)
                dm_s[k] = jnp.where(rel >= 0, jnp.exp(lg * jnp.maximum(rel, 0.0)), 0.0)
                dq_s[k] = jnp.exp(lg * (pos + 1.0 if d == 0 else L - pos))
                dk_s[k] = jnp.exp(lg * (L - 1.0 - pos if d == 0 else pos))

    def chunk_step(c, carry):
        for d in range(2):
            r0 = pl.multiple_of((c if d == 0 else n_chunks - 1 - c) * L, L)
            mask = lower if d == 0 else upper
            G = g_ref[pl.ds(r0, L), :]
            FL = _log_sigmoid(G)
            GT = G.T
            FLT = FL.T
            Bc = _tri_dot_left(tril if d == 0 else triu, FL)
            Br = _tri_dot_right(FLT, triu if d == 0 else tril)
            h_dst = hf_s if d == 0 else hb_s
            for h in range(H_M):
                k = d * H_M + h
                ci = d * H_M + h
                cf = 2 * H_M + d * H_M + h
                q = p_ref[pl.ds(r0, L), h * DH:(h + 1) * DH]
                kk = p_ref[pl.ds(r0, L), W_M + h * DH:W_M + (h + 1) * DH]
                v = p_ref[pl.ds(r0, L), 2 * W_M + h * DH:2 * W_M + (h + 1) * DH]
                b_col = jnp.broadcast_to(Bc[:, cf:cf + 1], (L, L))
                b_row = jnp.broadcast_to(Br[cf:cf + 1, :], (L, L))
                i_row = jnp.broadcast_to(GT[ci:ci + 1, :], (L, L))
                i_col = jnp.broadcast_to(G[:, ci:ci + 1], (L, L))
                m_prev = m_s[k]
                C = C_s[k]
                n_row = n_s[k]
                logD = jnp.where(mask, b_col - b_row + i_row, -jnp.inf)
                m_inter = b_col + m_prev
                m_row = jnp.maximum(m_inter, jnp.max(logD, -1, keepdims=True))
                s = _dot_nt(q, kk) * jnp.exp(logD - m_row)
                w_inter = jnp.exp(m_inter - m_row)
                qf = q.astype(F32)
                num = _dot(s.astype(BF16), v) + w_inter * _dot(q, C.astype(BF16))
                den = jnp.sum(s, -1, keepdims=True) + w_inter * jnp.sum(qf * n_row, -1, keepdims=True)
                hh = num / jnp.maximum(jnp.abs(den), jnp.exp(-m_row))
                h_dst[pl.ds(r0, L), h * DH:(h + 1) * DH] = hh
                b_end = b_col[L - 1:L, :] if d == 0 else b_col[0:1, :]
                log_w = b_end - b_col + i_col
                m_new = jnp.maximum(b_end + m_prev, jnp.max(log_w, 0, keepdims=True))
                decay = jnp.exp(b_end + m_prev - m_new)
                wk = jnp.exp(log_w - m_new) * kk.astype(F32)
                C_s[k] = decay * C + _dot(wk.T.astype(BF16), v)
                n_s[k] = decay * n_row + jnp.sum(wk, 0, keepdims=True)
                m_s[k] = m_new
                c0 = 4 * W_M
                qr = p_ref[pl.ds(r0, L), c0 + h * DH:c0 + (h + 1) * DH]
                kr = p_ref[pl.ds(r0, L), c0 + W_R + h * DH:c0 + W_R + (h + 1) * DH]
                vr = p_ref[pl.ds(r0, L), c0 + 2 * W_R + h * DH:c0 + 2 * W_R + (h + 1) * DH]
                S = S_s[k]
                sr = _dot_nt(qr, kr) * dm_s[k]
                o = _dot(sr.astype(BF16), vr) + dq_s[k] * _dot(qr, S.astype(BF16))
                (hf_s if d == 0 else hb_s)[pl.ds(r0, L), W_M + h * DH:W_M + (h + 1) * DH] = o
                kd = kr.astype(F32) * dk_s[k]
                d_L = jnp.exp(_log_sigmoid(dl_ref[0, k:k + 1, :]) * float(L))
                S_s[k] = d_L * S + _dot(kd.T.astype(BF16), vr)
        return carry

    lax.fori_loop(0, n_chunks, chunk_step, 0)

    for h in range(H_M):
        sl = slice(h * DH, (h + 1) * DH)
        hs = hf_s[:, sl] + hb_s[:, sl]
        yn = _rmsnorm(hs, nm_ref[0, :, sl])
        om = p_ref[:, 3 * W_M + h * DH:3 * W_M + (h + 1) * DH].astype(F32)
        y_ref[:, sl] = (jax.nn.sigmoid(om) * yn).astype(BF16)
        slr = slice(W_M + h * DH, W_M + (h + 1) * DH)
        hr = hf_s[:, slr] + hb_s[:, slr]
        ynr = _rmsnorm(hr, nr_ref[0, :, sl])
        gr = p_ref[:, 4 * W_M + 3 * W_R + h * DH:4 * W_M + 3 * W_R + (h + 1) * DH].astype(F32)
        y_ref[:, slr] = (_silu(gr) * ynr).astype(BF16)

    if not has_state:
        for d in range(2):
            for h in range(H_M):
                k = d * H_M + h
                C_out[0, 0, d, h] = C_s[k]
                S_out[0, 0, d, h] = S_s[k]
                n_out[0, 0, d, h:h + 1, :] = n_s[k]
                m_out[0, 0, k:k + 1, :] = m_s[k]


def _scan_scratch(T):
    return [
        pltpu.VMEM((2 * H_M, DH, DH), F32),
        pltpu.VMEM((2 * H_M, 1, DH), F32),
        pltpu.VMEM((2 * H_M, 1, LANES), F32),
        pltpu.VMEM((2 * H_R, DH, DH), F32),
        pltpu.VMEM((T, W_M + W_R), F32),
        pltpu.VMEM((T, W_M + W_R), F32),
        pltpu.VMEM((2 * H_R, CHUNK, CHUNK), F32),
        pltpu.VMEM((2 * H_R, CHUNK, CHUNK), F32),
        pltpu.VMEM((2 * H_R, CHUNK, CHUNK), F32),
    ]


def _scan_ctx(p, gates, dl, nm, nr, layer, prev_states=()):
    T = SEQ
    common = [
        pl.BlockSpec((T, P_COLS), lambda b: (b, 0)),
        pl.BlockSpec((T, LANES), lambda b: (b, 0)),
        pl.BlockSpec((1, 2 * H_R, LANES), lambda b: (layer, 0, 0)),
        pl.BlockSpec((1, 1, W_M), lambda b: (layer, 0, 0)),
        pl.BlockSpec((1, 1, W_R), lambda b: (layer, 0, 0)),
    ]
    n_al = len(prev_states)
    return pl.pallas_call(
        functools.partial(_scan_kernel, T=T, has_state=False, n_aliased=n_al),
        grid=(BATCH,),
        in_specs=common + [pl.BlockSpec(memory_space=pl.ANY)] * n_al,
        out_specs=[
            pl.BlockSpec((T, D_MODEL), lambda b: (b, 0)),
            pl.BlockSpec((1, 1, 2, H_M, DH, DH), lambda b: (b, layer, 0, 0, 0, 0)),
            pl.BlockSpec((1, 1, 2, H_M, DH), lambda b: (b, layer, 0, 0, 0)),
            pl.BlockSpec((1, 1, 2 * H_M, LANES), lambda b: (b, layer, 0, 0)),
            pl.BlockSpec((1, 1, 2, H_R, DH, DH), lambda b: (b, layer, 0, 0, 0, 0)),
        ],
        out_shape=[
            jax.ShapeDtypeStruct((NTOK, D_MODEL), BF16),
            jax.ShapeDtypeStruct((BATCH, DEPTH, 2, H_M, DH, DH), F32),
            jax.ShapeDtypeStruct((BATCH, DEPTH, 2, H_M, DH), F32),
            jax.ShapeDtypeStruct((BATCH, DEPTH, 2 * H_M, LANES), F32),
            jax.ShapeDtypeStruct((BATCH, DEPTH, 2, H_R, DH, DH), F32),
        ],
        input_output_aliases={len(common) + a: 1 + a for a in range(n_al)},
        scratch_shapes=_scan_scratch(T),
        compiler_params=pltpu.CompilerParams(
            dimension_semantics=("arbitrary",), vmem_limit_bytes=VMEM_LIMIT),
        name="scan_ctx",
    )(p, gates, dl, nm, nr, *prev_states)


def _scan_lat(p, gates, dl, nm, nr, C0, n0, m0, S0, y_prev, layer):
    T = DEC_SEQ
    off = N_CTX // T
    in_specs = [
        pl.BlockSpec((T, P_COLS), lambda b: (off + b, 0)),
        pl.BlockSpec((T, LANES), lambda b: (off + b, 0)),
        pl.BlockSpec((1, 2 * H_R, LANES), lambda b: (layer, 0, 0)),
        pl.BlockSpec((1, 1, W_M), lambda b: (layer, 0, 0)),
        pl.BlockSpec((1, 1, W_R), lambda b: (layer, 0, 0)),
        pl.BlockSpec((1, 1, 2, H_M, DH, DH), lambda b: (b, layer, 0, 0, 0, 0)),
        pl.BlockSpec((1, 1, 2, H_M, DH), lambda b: (b, layer, 0, 0, 0)),
        pl.BlockSpec((1, 1, 2 * H_M, LANES), lambda b: (b, layer, 0, 0)),
        pl.BlockSpec((1, 1, 2, H_R, DH, DH), lambda b: (b, layer, 0, 0, 0, 0)),
        pl.BlockSpec(memory_space=pl.ANY),
    ]
    return pl.pallas_call(
        functools.partial(_scan_kernel, T=T, has_state=True),
        grid=(DEC_BATCH,),
        in_specs=in_specs,
        out_specs=pl.BlockSpec((T, D_MODEL), lambda b: (off + b, 0)),
        out_shape=jax.ShapeDtypeStruct((NTOK, D_MODEL), BF16),
        input_output_aliases={9: 0},
        scratch_shapes=_scan_scratch(T),
        compiler_params=pltpu.CompilerParams(
            dimension_semantics=("arbitrary",), vmem_limit_bytes=VMEM_LIMIT),
        name="scan_lat",
    )(p, gates, dl, nm, nr, C0, n0, m0, S0, y_prev)


def _top2(logits):
    lane = lax.broadcasted_iota(jnp.int32, logits.shape, 1)
    v1 = jnp.max(logits, -1, keepdims=True)
    i1 = jnp.min(jnp.where(logits == v1, lane, LANES), -1, keepdims=True)
    rest = jnp.where(lane == i1, -jnp.inf, logits)
    v2 = jnp.max(rest, -1, keepdims=True)
    i2 = jnp.min(jnp.where(rest == v2, lane, LANES), -1, keepdims=True)
    e2 = jnp.exp(v2 - v1)
    return i1, i2, 1.0 / (1.0 + e2), e2 / (1.0 + e2)


def _split2(x):
    hi = x.astype(BF16)
    return hi, (x - hi.astype(F32)).astype(BF16)


def _dot_f32x3(a, b):
    a_hi, a_lo = _split2(a)
    b_hi, b_lo = _split2(b)
    return _dot(a_hi, b_hi) + _dot(a_hi, b_lo) + _dot(a_lo, b_hi)


R_E1, R_E2, R_W1, R_W2, R_S1, R_S2 = range(6)


def _out_kernel(*refs, with_router, split_x):
    y_ref = refs[0]
    if split_x:
        xp_ref, xl_ref = refs[1:3]
        x_in = jnp.where(pl.program_id(0) >= N_CTX // TM, xl_ref[...], xp_ref[...])
    else:
        x_in = refs[1][...]
    refs = refs[3:] if split_x else refs[2:]
    if with_router:
        (w_ref, g_ref, mod_ref, wr_ref,
         x1_ref, h2_ref, rinfo_ref, cnt_ref, w_scr, tri_scr, cnt_scr) = refs
    else:
        w_ref, g_ref, mod_ref, x1_ref, h2_ref, w_scr = refs

    @pl.when(pl.program_id(0) == 0)
    def _():
        w_scr[...] = w_ref[0].astype(BF16)
        if with_router:
            r = lax.broadcasted_iota(jnp.int32, (TM, TM), 0)
            c = lax.broadcasted_iota(jnp.int32, (TM, TM), 1)
            tri_scr[...] = (c < r).astype(BF16)
            cnt_scr[...] = jnp.zeros_like(cnt_scr)

    o = jnp.dot(y_ref[...], w_scr[...], preferred_element_type=F32)
    x1 = x_in + mod_ref[0, 0, 2:3, :] * o
    x1_ref[...] = x1
    h2 = _rmsnorm(x1, g_ref[0]) * (1.0 + mod_ref[0, 0, 4:5, :]) + mod_ref[0, 0, 3:4, :]
    if not with_router:
        h2_ref[...] = h2.astype(BF16)
    else:
        h2_ref[...] = h2
        logits = _dot_f32x3(h2, wr_ref[0])
        lane = lax.broadcasted_iota(jnp.int32, logits.shape, 1)
        i1, i2, w1, w2 = _top2(jnp.where(lane < N_EXPERTS, logits, -jnp.inf))
        oh1 = lane == i1
        oh2 = lane == i2
        sel = jnp.where(oh1 | oh2, 1.0, 0.0)
        rank = _dot(tri_scr[...], sel.astype(BF16)) + cnt_scr[...]
        r1 = jnp.sum(jnp.where(oh1, rank, 0.0), -1, keepdims=True)
        r2 = jnp.sum(jnp.where(oh2, rank, 0.0), -1, keepdims=True)
        s1 = i1.astype(F32) * float(REG) + r1
        s2 = i2.astype(F32) * float(REG) + r2
        info = jnp.zeros(logits.shape, F32)
        for col, val in ((R_E1, i1.astype(F32)), (R_E2, i2.astype(F32)), (R_W1, w1), (R_W2, w2),
                         (R_S1, s1), (R_S2, s2)):
            info = jnp.where(lane == col, val, info)
        rinfo_ref[...] = info
        cnt_scr[...] += jnp.sum(sel, 0, keepdims=True)
        cnt_ref[...] = cnt_scr[...]


def _out(y, xs, w_out, g2, mods, layer, w_router_pad=None, router_idx=0):
    with_router = w_router_pad is not None
    split_x = len(xs) == 2
    n_ctx_tiles = N_CTX // TM
    if split_x:
        x_specs = [pl.BlockSpec((TM, D_MODEL), lambda i: (jnp.minimum(i, n_ctx_tiles - 1), 0)),
                   pl.BlockSpec((TM, D_MODEL), lambda i: (jnp.maximum(i - n_ctx_tiles, 0), 0))]
    else:
        x_specs = [pl.BlockSpec((TM, D_MODEL), lambda i: (i, 0))]
    in_specs = [pl.BlockSpec((TM, D_MODEL), lambda i: (i, 0))] + x_specs + [
        pl.BlockSpec((1, D_MODEL, D_MODEL), lambda i: (layer, 0, 0)),
        pl.BlockSpec((1, 1, D_MODEL), lambda i: (layer, 0, 0)),
        pl.BlockSpec((1, 1, 6, D_MODEL), lambda i: (layer, _group_of_tile(i, TM), 0, 0)),
    ]
    out_specs = [
        pl.BlockSpec((TM, D_MODEL), lambda i: (i, 0)),
        pl.BlockSpec((TM, D_MODEL), lambda i: (i, 0)),
    ]
    out_shape = [
        jax.ShapeDtypeStruct((NTOK, D_MODEL), F32),
        jax.ShapeDtypeStruct((NTOK, D_MODEL), F32 if with_router else BF16),
    ]
    args = [y, *xs, w_out, g2, mods]
    scratch = [pltpu.VMEM((D_MODEL, D_MODEL), BF16)]
    if with_router:
        in_specs.append(pl.BlockSpec((1, D_MODEL, LANES), lambda i: (router_idx, 0, 0)))
        out_specs += [pl.BlockSpec((TM, LANES), lambda i: (i, 0)),
                      pl.BlockSpec((1, LANES), lambda i: (0, 0))]
        out_shape += [jax.ShapeDtypeStruct((NTOK, LANES), F32),
                      jax.ShapeDtypeStruct((1, LANES), F32)]
        args.append(w_router_pad)
        scratch += [pltpu.VMEM((TM, TM), BF16), pltpu.VMEM((1, LANES), F32)]
    return pl.pallas_call(
        functools.partial(_out_kernel, with_router=with_router, split_x=split_x),
        grid=(NTOK // TM,),
        in_specs=in_specs,
        out_specs=out_specs,
        out_shape=out_shape,
        scratch_shapes=scratch,
        compiler_params=pltpu.CompilerParams(
            dimension_semantics=("arbitrary",), vmem_limit_bytes=VMEM_LIMIT),
        name="out_router" if with_router else "out",
    )(*args)


def _swiglu_step(h, wg_ref, wu_ref, wd_ref, acc):
    w_cat = jnp.concatenate([wg_ref[0].astype(BF16), wu_ref[0].astype(BF16)], axis=1)
    ab = jnp.dot(h, w_cat, preferred_element_type=F32)
    t = (_silu(ab[:, :FC]) * ab[:, FC:]).astype(BF16)
    acc[...] += jnp.dot(t, wd_ref[0].astype(BF16), preferred_element_type=F32)


def _ffn_kernel(h_ref, res_ref, wg_ref, wu_ref, wd_ref, mod_ref, o_ref, acc):
    f = pl.program_id(1)

    @pl.when(f == 0)
    def _():
        acc[...] = jnp.zeros_like(acc)

    _swiglu_step(h_ref[...], wg_ref, wu_ref, wd_ref, acc)

    @pl.when(f == pl.num_programs(1) - 1)
    def _():
        o_ref[...] = res_ref[...] + mod_ref[0, 0, 5:6, :] * acc[...]


def _ffn(h2, res, wg, wu, wd, mods, layer, w_idx):
    return pl.pallas_call(
        _ffn_kernel,
        grid=(NTOK // TM, D_FF // FC),
        in_specs=[
            pl.BlockSpec((TM, D_MODEL), lambda i, f: (i, 0)),
            pl.BlockSpec((TM, D_MODEL), lambda i, f: (i, 0)),
            pl.BlockSpec((1, D_MODEL, FC), lambda i, f: (w_idx, 0, f)),
            pl.BlockSpec((1, D_MODEL, FC), lambda i, f: (w_idx, 0, f)),
            pl.BlockSpec((1, FC, D_MODEL), lambda i, f: (w_idx, f, 0)),
            pl.BlockSpec((1, 1, 6, D_MODEL), lambda i, f: (layer, _group_of_tile(i, TM), 0, 0)),
        ],
        out_specs=pl.BlockSpec((TM, D_MODEL), lambda i, f: (i, 0)),
        out_shape=jax.ShapeDtypeStruct((NTOK, D_MODEL), F32),
        scratch_shapes=[pltpu.VMEM((TM, D_MODEL), F32)],
        compiler_params=pltpu.CompilerParams(
            dimension_semantics=("arbitrary", "arbitrary"), vmem_limit_bytes=VMEM_LIMIT),
        name="ffn",
    )(h2, res, wg, wu, wd, mods)


def _tile_plan(counts):
    nt = (counts + TR - 1) // TR
    cum = jnp.cumsum(nt)
    total = cum[-1]
    t = jnp.arange(MAX_TILES, dtype=jnp.int32)
    tt = jnp.minimum(t, total - 1)
    e = jnp.sum((cum[None, :] <= tt[:, None]).astype(jnp.int32), axis=1)
    k = tt - (cum - nt)[e]
    n = jnp.where(t < total, jnp.clip(counts[e] - k * TR, 0, TR), 0)
    return e.astype(jnp.int32), (e * REG_TILES + k).astype(jnp.int32), n.astype(jnp.int32)


def _row_copy(src, src_row, dst, dst_row, sem):
    return pltpu.make_async_copy(src.at[pl.ds(src_row, 1)], dst.at[pl.ds(dst_row, 1)], sem)


def _dispatch_kernel(slot_ref, h_ref, xs_ref, sem):
    base = pl.program_id(0) * (TOP_K * TD)

    def issue(r, carry):
        for k in range(TOP_K):
            _row_copy(h_ref, r, xs_ref, slot_ref[base + TOP_K * r + k], sem).start()
        return carry

    lax.fori_loop(0, TD, issue, 0, unroll=8)
    for k in range(TOP_K):
        pltpu.make_async_copy(h_ref, xs_ref.at[pl.ds(0, TD)], sem).wait()


def _dispatch(slots, h2f):
    return pl.pallas_call(
        _dispatch_kernel,
        grid_spec=pltpu.PrefetchScalarGridSpec(
            num_scalar_prefetch=1,
            grid=(NTOK // TD,),
            in_specs=[pl.BlockSpec((TD, D_MODEL), lambda i, s: (i, 0))],
            out_specs=pl.BlockSpec(memory_space=pl.ANY),
            scratch_shapes=[pltpu.SemaphoreType.DMA],
        ),
        out_shape=jax.ShapeDtypeStruct((N_EXPERTS * REG, D_MODEL), F32),
        compiler_params=pltpu.CompilerParams(
            dimension_semantics=("arbitrary",), vmem_limit_bytes=VMEM_LIMIT),
        name="moe_dispatch",
    )(slots, h2f)


def _gffn_kernel(te_ref, tb_ref, tn_ref, x_ref, wg_ref, wu_ref, wd_ref, o_ref, h_scr, acc):
    del te_ref, tb_ref
    f = pl.program_id(1)
    n = tn_ref[pl.program_id(0)]

    @pl.when(n > 0)
    def _():
        @pl.when(f == 0)
        def _():
            row = lax.broadcasted_iota(jnp.int32, (TR, D_MODEL), 0)
            h_scr[...] = jnp.where(row < n, x_ref[...], 0.0).astype(BF16)
            acc[...] = jnp.zeros_like(acc)

        _swiglu_step(h_scr[...], wg_ref, wu_ref, wd_ref, acc)

        @pl.when(f == pl.num_programs(1) - 1)
        def _():
            o_ref[...] = acc[...]


def _gffn(tile_e, tile_blk, tile_n, xs, wg, wu, wd, w_base):
    nf = D_FF // FC

    def f_eff(t, f, tn):
        return jnp.where(tn[t] > 0, f, nf - 1)

    return pl.pallas_call(
        _gffn_kernel,
        grid_spec=pltpu.PrefetchScalarGridSpec(
            num_scalar_prefetch=3,
            grid=(MAX_TILES, nf),
            in_specs=[
                pl.BlockSpec((TR, D_MODEL), lambda t, f, te, tb, tn: (tb[t], 0)),
                pl.BlockSpec((1, D_MODEL, FC), lambda t, f, te, tb, tn: (w_base + te[t], 0, f_eff(t, f, tn))),
                pl.BlockSpec((1, D_MODEL, FC), lambda t, f, te, tb, tn: (w_base + te[t], 0, f_eff(t, f, tn))),
                pl.BlockSpec((1, FC, D_MODEL), lambda t, f, te, tb, tn: (w_base + te[t], f_eff(t, f, tn), 0)),
            ],
            out_specs=pl.BlockSpec((TR, D_MODEL), lambda t, f, te, tb, tn: (tb[t], 0)),
            scratch_shapes=[pltpu.VMEM((TR, D_MODEL), BF16), pltpu.VMEM((TR, D_MODEL), F32)],
        ),
        out_shape=jax.ShapeDtypeStruct((N_EXPERTS * REG, D_MODEL), F32),
        compiler_params=pltpu.CompilerParams(
            dimension_semantics=("arbitrary", "arbitrary"), vmem_limit_bytes=VMEM_LIMIT),
        name="moe_ffn",
    )(tile_e, tile_blk, tile_n, xs, wg, wu, wd)


def _combine_kernel(slot_ref, x1_ref, rinfo_ref, mod_ref, ys_ref, o_ref, buf, sem):
    base = pl.program_id(0) * (TOP_K * TD)

    def issue(r, carry):
        for k in range(TOP_K):
            _row_copy(ys_ref, slot_ref[base + TOP_K * r + k], buf.at[k], r, sem).start()
        return carry

    lax.fori_loop(0, TD, issue, 0, unroll=8)
    for k in range(TOP_K):
        pltpu.make_async_copy(ys_ref.at[pl.ds(0, TD)], buf.at[k], sem).wait()
    y = rinfo_ref[:, R_W1:R_W1 + 1] * buf[0] + rinfo_ref[:, R_W2:R_W2 + 1] * buf[1]
    o_ref[...] = x1_ref[...] + mod_ref[0, 0, 5:6, :] * y


def _combine(slots, x1, rinfo, mods, ys, layer):
    return pl.pallas_call(
        _combine_kernel,
        grid_spec=pltpu.PrefetchScalarGridSpec(
            num_scalar_prefetch=1,
            grid=(NTOK // TD,),
            in_specs=[
                pl.BlockSpec((TD, D_MODEL), lambda i, s: (i, 0)),
                pl.BlockSpec((TD, LANES), lambda i, s: (i, 0)),
                pl.BlockSpec((1, 1, 6, D_MODEL), lambda i, s: (layer, _group_of_tile(i, TD), 0, 0)),
                pl.BlockSpec(memory_space=pl.ANY),
            ],
            out_specs=pl.BlockSpec((TD, D_MODEL), lambda i, s: (i, 0)),
            scratch_shapes=[pltpu.VMEM((TOP_K, TD, D_MODEL), F32), pltpu.SemaphoreType.DMA],
        ),
        out_shape=jax.ShapeDtypeStruct((NTOK, D_MODEL), F32),
        compiler_params=pltpu.CompilerParams(
            dimension_semantics=("arbitrary",), vmem_limit_bytes=VMEM_LIMIT),
        name="moe_combine",
    )(slots, x1, rinfo, mods, ys)


def _final_kernel(x_ref, g_ref, o_ref):
    o_ref[...] = _rmsnorm(x_ref[...], g_ref[...])


def _final(x, g, row_off, rows):
    off = row_off // TM
    return pl.pallas_call(
        _final_kernel,
        grid=(rows // TM,),
        in_specs=[
            pl.BlockSpec((TM, D_MODEL), lambda i: (off + i, 0)),
            pl.BlockSpec((1, D_MODEL), lambda i: (0, 0)),
        ],
        out_specs=pl.BlockSpec((TM, D_MODEL), lambda i: (i, 0)),
        out_shape=jax.ShapeDtypeStruct((rows, D_MODEL), F32),
        compiler_params=pltpu.CompilerParams(
            dimension_semantics=("arbitrary",), vmem_limit_bytes=VMEM_LIMIT),
        name="final_norm",
    )(x, g)


def kernel(x_prompt, x_sample, state_mlstm_C, state_mlstm_n, state_mlstm_m, state_ret_S, c, c_ctx,
           norm1_g, norm2_g, norm_f_g, w_ada, b_ada, w_in, b_gates, ret_decay_logit,
           mlstm_norm_g, ret_norm_g, w_out, ffn_w_gate, ffn_w_up, ffn_w_down,
           moe_w_router, moe_w_gate, moe_w_up, moe_w_down):
    xs_in = (x_prompt.reshape(N_CTX, D_MODEL), x_sample.reshape(N_LAT, D_MODEL))
    cvec = jnp.concatenate(
        [c_ctx[None, :], c, jnp.zeros((N_GROUPS - 1 - DEC_BATCH, D_MODEL), F32)], 0)
    mods = _ada(cvec, w_ada, b_ada).reshape(DEPTH, N_GROUPS, 6, D_MODEL)

    n_m = 4 * W_M
    w_ret = w_in[:, :, n_m + N_GATES:]
    wg = jnp.pad(w_in[:, :, n_m:n_m + N_GATES], ((0, 0), (0, 0), (0, LANES - N_GATES)))
    bg = jnp.pad(b_gates, ((0, 0), (0, LANES - N_GATES))).reshape(DEPTH, 1, LANES)
    cos_np, sin_np = _rope_tables()
    cos_t, sin_t = jnp.asarray(cos_np), jnp.asarray(sin_np)
    dl = jnp.broadcast_to(ret_decay_logit.reshape(DEPTH, 2 * H_R, 1), (DEPTH, 2 * H_R, LANES))
    m0 = jnp.broadcast_to(state_mlstm_m.reshape(DEC_BATCH, DEPTH, 2 * H_M, 1),
                          (DEC_BATCH, DEPTH, 2 * H_M, LANES))
    g1 = norm1_g.reshape(DEPTH, 1, D_MODEL)
    g2 = norm2_g.reshape(DEPTH, 1, D_MODEL)
    nm = mlstm_norm_g.reshape(DEPTH, 1, W_M)
    nr = ret_norm_g.reshape(DEPTH, 1, W_R)
    n_moe = moe_w_router.shape[0]
    wr_pad = jnp.pad(moe_w_router, ((0, 0), (0, 0), (0, LANES - N_EXPERTS)))
    moe_g = moe_w_gate.reshape(n_moe * N_EXPERTS, D_MODEL, D_FF)
    moe_u = moe_w_up.reshape(n_moe * N_EXPERTS, D_MODEL, D_FF)
    moe_d = moe_w_down.reshape(n_moe * N_EXPERTS, D_FF, D_MODEL)

    states = ()
    xs = xs_in
    for l in range(DEPTH):
        jl = l // 2
        p, gates = _proj(xs, g1, mods, w_in, w_ret, wg, bg, cos_t, sin_t, l)
        y, *states = _scan_ctx(p, gates, dl, nm, nr, l, states)
        y = _scan_lat(p, gates, dl, nm, nr, state_mlstm_C, state_mlstm_n, m0, state_ret_S, y, l)
        if l % 2 == 0:
            x1, h2 = _out(y, xs, w_out, g2, mods, l)
            x = _ffn(h2, x1, ffn_w_gate, ffn_w_up, ffn_w_down, mods, l, jl)
        else:
            x1, h2f, rinfo, cnt = _out(y, xs, w_out, g2, mods, l, wr_pad, jl)
            slots = rinfo[:, R_S1:R_S2 + 1].astype(jnp.int32).reshape(TOP_K * NTOK)
            tile_e, tile_blk, tile_n = _tile_plan(cnt[0, :N_EXPERTS].astype(jnp.int32))
            xd = _dispatch(slots, h2f)
            yd = _gffn(tile_e, tile_blk, tile_n, xd, moe_g, moe_u, moe_d, jl * N_EXPERTS)
            x = _combine(slots, x1, rinfo, mods, yd, l)
        xs = (x,)

    y_prompt = _final(x, norm_f_g.reshape(1, D_MODEL), 0, N_CTX).reshape(BATCH, SEQ, D_MODEL)
    y_sample = _final(x, norm_f_g.reshape(1, D_MODEL), N_CTX, N_LAT).reshape(DEC_BATCH, DEC_SEQ, D_MODEL)
    new_C, new_n, new_m, new_S = states
    return (y_prompt, y_sample, new_C, new_n,
            new_m[:, :, :, 0].reshape(BATCH, DEPTH, 2, H_M), new_S)
```

```python
import functools

import numpy as np
import jax
import jax.numpy as jnp
from jax import lax
from jax.experimental import pallas as pl
from jax.experimental.pallas import tpu as pltpu

D_MODEL = 1024
BATCH = 32
SEQ = 256
DEPTH = 2
DEC_BATCH = 2
DEC_SEQ = 1024
GRID_W = 64
H_M = 4
DH = 128
H_R = 4
W_M = H_M * DH
W_R = H_R * DH
N_GATES = 4 * H_M
CHUNK = 128
D_FF = 2816
N_EXPERTS = 8
ROPE_BASE = 10000.0
EPS = 1e-6

N_CTX = BATCH * SEQ
N_LAT = DEC_BATCH * DEC_SEQ
NTOK = N_CTX + N_LAT
N_GROUPS = 8
K_SCALE = DH ** -0.5
P_COLS = 4 * W_M + 4 * W_R
LANES = 128
GATE_LANES = 2 * LANES
VMEM_LIMIT = 56 * 1024 * 1024

F32 = jnp.float32
BF16 = jnp.bfloat16
HIGHEST = lax.Precision.HIGHEST

TM = 1024
TN = 1024
FC = 256
TOP_K = 2
TR = 896
REG_TILES = -(-NTOK // TR)
REG = REG_TILES * TR
MAX_TILES = -(-TOP_K * NTOK // TR) + N_EXPERTS
TD = 512


def _group_of_tile(i, tm):
    return jnp.maximum(i * tm // DEC_SEQ - (N_CTX // DEC_SEQ - 1), 0)


def _silu(x):
    return x * jax.nn.sigmoid(x)


def _log_sigmoid(x):
    return jnp.minimum(x, 0.0) - jnp.log(1.0 + jnp.exp(-jnp.abs(x)))


def _rmsnorm(x, g):
    return x * lax.rsqrt(jnp.mean(x * x, -1, keepdims=True) + EPS) * g


def _ada_kernel(cv_ref, w_ref, b_ref, o_ref):
    s = _silu(cv_ref[...]).astype(BF16)
    o_ref[0] = jnp.dot(s, w_ref[0].astype(BF16), preferred_element_type=F32) + b_ref[0]


def _ada(cvec, w_ada, b_ada):
    tn = 1536
    n = 6 * D_MODEL
    return pl.pallas_call(
        _ada_kernel,
        grid=(DEPTH, n // tn),
        in_specs=[
            pl.BlockSpec((N_GROUPS, D_MODEL), lambda l, j: (0, 0)),
            pl.BlockSpec((1, D_MODEL, tn), lambda l, j: (l, 0, j)),
            pl.BlockSpec((1, 1, tn), lambda l, j: (l, 0, j)),
        ],
        out_specs=pl.BlockSpec((1, N_GROUPS, tn), lambda l, j: (l, 0, j)),
        out_shape=jax.ShapeDtypeStruct((DEPTH, N_GROUPS, n), F32),
        compiler_params=pltpu.CompilerParams(
            dimension_semantics=("arbitrary", "arbitrary"), vmem_limit_bytes=VMEM_LIMIT),
        name="ada",
    )(cvec, w_ada, b_ada.reshape(DEPTH, 1, n))


def _rope_tables():
    half = DH // 4
    freqs = ROPE_BASE ** (-np.arange(half, dtype=np.float64) / half)
    t = np.arange(DEC_SEQ)
    pos = np.stack([t // GRID_W, t % GRID_W], 1).astype(np.float64)
    d = np.arange(DH)
    ang = pos[:, d // (DH // 2)] * freqs[d % half][None, :]
    sign = np.where((d % (DH // 2)) < half, -1.0, 1.0)[None, :]
    return np.cos(ang).astype(np.float32), (sign * np.sin(ang)).astype(np.float32)


def _rope(a, cos, sin):
    lane = lax.broadcasted_iota(jnp.int32, a.shape, 1)
    first = (lane % (DH // 2)) < (DH // 4)
    partner = jnp.where(first, pltpu.roll(a, DH - DH // 4, 1), pltpu.roll(a, DH // 4, 1))
    return a * cos + partner * sin


def _proj_kernel(*refs, n_ctx_tiles, split_x):
    if split_x:
        (xp_ref, xl_ref, g_ref, mod_ref, wa_ref, wb_ref, wg_ref, bg_ref, cos_ref, sin_ref,
         p_ref, gate_ref, h_scr) = refs
    else:
        (x_ref, g_ref, mod_ref, wa_ref, wb_ref, wg_ref, bg_ref, cos_ref, sin_ref,
         p_ref, gate_ref, h_scr) = refs
    i = pl.program_id(0)
    j = pl.program_id(1)
    is_lat = i >= n_ctx_tiles
    half = TN // 2

    def prologue(x):
        h = _rmsnorm(x, g_ref[0]) * (1.0 + mod_ref[0, 0, 1:2, :]) + mod_ref[0, 0, 0:1, :]
        h_scr[...] = h.astype(BF16)
        gate_ref[...] = _dot_f32x3(h, wg_ref[0]) + bg_ref[0]

    @pl.when(j == 0)
    def _():
        if split_x:
            pl.when(jnp.logical_not(is_lat))(lambda: prologue(xp_ref[...]))
            pl.when(is_lat)(lambda: prologue(xl_ref[...]))
        else:
            prologue(x_ref[...])

    def matmul(w_ref):
        return jnp.dot(h_scr[...], w_ref[0].astype(BF16), preferred_element_type=F32)

    @pl.when(j == 0)
    def _():
        acc = matmul(wa_ref)
        p_ref[:, :half] = acc[:, :half].astype(BF16)
        p_ref[:, half:] = (acc[:, half:] * K_SCALE).astype(BF16)

    @pl.when(j == 1)
    def _():
        p_ref[...] = matmul(wa_ref).astype(BF16)

    @pl.when(j == 2)
    def _():
        acc = matmul(wb_ref)

        @pl.when(is_lat)
        def _():
            cos = cos_ref[...]
            sin = sin_ref[...]
            for hd in range(TN // DH):
                sl = slice(hd * DH, (hd + 1) * DH)
                r = _rope(acc[:, sl], cos, sin)
                p_ref[:, sl] = (r * K_SCALE if hd * DH >= half else r).astype(BF16)

        @pl.when(jnp.logical_not(is_lat))
        def _():
            p_ref[:, :half] = acc[:, :half].astype(BF16)
            p_ref[:, half:] = (acc[:, half:] * K_SCALE).astype(BF16)

    @pl.when(j == 3)
    def _():
        p_ref[...] = matmul(wb_ref).astype(BF16)


def _proj(xs, g1, mods, w_in, w_ret, wg, bg, cos_t, sin_t, layer):
    n_ctx_tiles = N_CTX // TM
    tiles_per_seq = DEC_SEQ // TM
    split_x = len(xs) == 2
    if split_x:
        x_specs = [pl.BlockSpec((TM, D_MODEL), lambda i, j: (jnp.minimum(i, n_ctx_tiles - 1), 0)),
                   pl.BlockSpec((TM, D_MODEL), lambda i, j: (jnp.maximum(i - n_ctx_tiles, 0), 0))]
    else:
        x_specs = [pl.BlockSpec((TM, D_MODEL), lambda i, j: (i, 0))]
    n_a = 4 * W_M // TN
    return pl.pallas_call(
        functools.partial(_proj_kernel, n_ctx_tiles=n_ctx_tiles, split_x=split_x),
        grid=(NTOK // TM, P_COLS // TN),
        in_specs=x_specs + [
            pl.BlockSpec((1, 1, D_MODEL), lambda i, j: (layer, 0, 0)),
            pl.BlockSpec((1, 1, 6, D_MODEL), lambda i, j: (layer, _group_of_tile(i, TM), 0, 0)),
            pl.BlockSpec((1, D_MODEL, TN), lambda i, j: (layer, 0, jnp.minimum(j, n_a - 1))),
            pl.BlockSpec((1, D_MODEL, TN), lambda i, j: (layer, 0, jnp.maximum(j - n_a, 0))),
            pl.BlockSpec((1, D_MODEL, GATE_LANES), lambda i, j: (layer, 0, 0)),
            pl.BlockSpec((1, 1, GATE_LANES), lambda i, j: (layer, 0, 0)),
            pl.BlockSpec((TM, DH), lambda i, j: (i % tiles_per_seq, 0)),
            pl.BlockSpec((TM, DH), lambda i, j: (i % tiles_per_seq, 0)),
        ],
        out_specs=[
            pl.BlockSpec((TM, TN), lambda i, j: (i, j)),
            pl.BlockSpec((TM, GATE_LANES), lambda i, j: (i, 0)),
        ],
        out_shape=[
            jax.ShapeDtypeStruct((NTOK, P_COLS), BF16),
            jax.ShapeDtypeStruct((NTOK, GATE_LANES), F32),
        ],
        scratch_shapes=[pltpu.VMEM((TM, D_MODEL), BF16)],
        compiler_params=pltpu.CompilerParams(
            dimension_semantics=("arbitrary", "arbitrary"), vmem_limit_bytes=VMEM_LIMIT),
        name="proj",
    )(*xs, g1, mods, w_in, w_ret, wg, bg, cos_t, sin_t)


def _split3(x):
    hi = x.astype(BF16)
    r1 = x - hi.astype(F32)
    mid = r1.astype(BF16)
    lo = (r1 - mid.astype(F32)).astype(BF16)
    return hi, mid, lo


def _dot(a, b):
    return jnp.dot(a, b, preferred_element_type=F32)


def _dot_nt(a, b):
    return lax.dot_general(a, b, (((1,), (1,)), ((), ())), preferred_element_type=F32)


def _tri_dot_left(tri, x):
    hi, mid, lo = _split3(x)
    return _dot(tri, hi) + _dot(tri, mid) + _dot(tri, lo)


def _tri_dot_right(x, tri):
    hi, mid, lo = _split3(x)
    return _dot(hi, tri) + _dot(mid, tri) + _dot(lo, tri)


def _run_max(x, reverse):
    n_tiles = x.shape[0] // 8
    sub = lax.broadcasted_iota(jnp.int32, (8, LANES), 0)
    out = [None] * n_tiles
    carry = None
    for t in (range(n_tiles - 1, -1, -1) if reverse else range(n_tiles)):
        v = x[8 * t:8 * t + 8, :]
        for s in (1, 2, 4):
            if reverse:
                v = jnp.maximum(v, jnp.where(sub < 8 - s, pltpu.roll(v, 8 - s, 0), -jnp.inf))
            else:
                v = jnp.maximum(v, jnp.where(sub >= s, pltpu.roll(v, s, 0), -jnp.inf))
        if carry is not None:
            v = jnp.maximum(v, carry)
        carry = jnp.broadcast_to(v[0:1, :] if reverse else v[7:8, :], (8, LANES))
        out[t] = v
    return jnp.concatenate(out, axis=0)


def _scan_kernel(*refs, T, has_state, n_aliased=0):
    if has_state:
        (p_ref, g_ref, dl_ref, nm_ref, nr_ref, C0_ref, n0_ref, m0_ref, S0_ref, _yprev_ref,
         y_ref, CN_s, S_s, m_s, hf_s, hb_s, dm_s, dq_s, dk_s, dL_s, kT_s) = refs
    else:
        p_ref, g_ref, dl_ref, nm_ref, nr_ref = refs[:5]
        (y_ref, C_out, n_out, m_out, S_out,
         CN_s, S_s, m_s, hf_s, hb_s, dm_s, dq_s, dk_s, dL_s, kT_s) = refs[5 + n_aliased:]
    L = CHUNK
    n_chunks = T // L
    row_i = lax.broadcasted_iota(jnp.int32, (L, L), 0)
    col_j = lax.broadcasted_iota(jnp.int32, (L, L), 1)
    lower = col_j <= row_i
    upper = col_j >= row_i
    tril = lower.astype(BF16)
    triu = upper.astype(BF16)
    ones = jnp.ones((L, DH), BF16)
    c_km = W_M
    c_vm = 2 * W_M
    c_om = 3 * W_M
    c_qr = 4 * W_M
    c_kr = c_qr + W_R
    c_vr = c_qr + 2 * W_R
    c_gr = c_qr + 3 * W_R

    for d in range(2):
        for h in range(H_M):
            k = d * H_M + h
            if has_state:
                CN_s[k, :, :DH] = C0_ref[0, 0, d, h]
                CN_s[k, :, DH:] = jnp.broadcast_to(n0_ref[0, 0, d, h:h + 1, :], (DH, DH)).T
                S_s[k] = S0_ref[0, 0, d, h]
            else:
                CN_s[k] = jnp.zeros((DH, 2 * DH), F32)
                S_s[k] = jnp.zeros((DH, DH), F32)
    m_s[...] = m0_ref[0, 0] if has_state else jnp.zeros((1, LANES), F32)

    @pl.when(pl.program_id(0) == 0)
    def _():
        pos_i = row_i.astype(F32)
        pos_j = col_j.astype(F32)
        for d in range(2):
            for h in range(H_R):
                k = d * H_R + h
                lg_row = _log_sigmoid(dl_ref[0, k:k + 1, :])
                lg = jnp.broadcast_to(lg_row, (L, L))
                rel = (row_i - col_j if d == 0 else col_j - row_i).astype(F32)
                dm_s[k] = jnp.where(rel >= 0, jnp.exp(lg * jnp.maximum(rel, 0.0)), 0.0)
                dq_s[k] = jnp.exp(lg * (pos_i + 1.0 if d == 0 else L - pos_i))
                dk_s[k] = jnp.exp(lg * (L - 1.0 - pos_j if d == 0 else pos_j))
                dL_s[k] = jnp.exp(lg_row * float(L))

    def transpose_keys(c, carry):
        r0 = pl.multiple_of(c * L, L)
        for h in range(H_M):
            kT_s[h, c] = p_ref[pl.ds(r0, L), c_km + h * DH:c_km + (h + 1) * DH].astype(F32).T
            kT_s[H_M + h, c] = p_ref[pl.ds(r0, L), c_kr + h * DH:c_kr + (h + 1) * DH].astype(F32).T
        return carry

    lax.fori_loop(0, n_chunks, transpose_keys, 0)

    def chunk_step(c, carry):
        m_prev = m_s[...]
        m_new = []
        prep = []
        for d in range(2):
            ci = c if d == 0 else n_chunks - 1 - c
            r0 = pl.multiple_of(ci * L, L)
            mask = lower if d == 0 else upper
            e_row = L - 1 if d == 0 else 0
            FL = _log_sigmoid(g_ref[pl.ds(r0, L), LANES:2 * LANES])
            Bc = _tri_dot_left(tril if d == 0 else triu, FL)
            Zc = g_ref[pl.ds(r0, L), 0:LANES] - Bc
            M = jnp.maximum(_run_max(Zc, reverse=(d == 1)), m_prev)
            m_row = Bc + M
            M_end = M[e_row:e_row + 1, :]
            m_new.append(Bc[e_row:e_row + 1, :] + M_end)
            decay = jnp.exp(m_prev - M_end)
            prep.append(dict(ci=ci, r0=r0, mask=mask, M=M, m_row=m_row, decay=decay,
                             ZT=Zc.T,
                             WT=jnp.exp(Zc - M_end).T))
        pairs = [(d, h) for d in range(2) for h in range(H_M)]

        def rows(d, col):
            return p_ref[pl.ds(prep[d]["r0"], L), col:col + DH]

        qk, qkr = {}, {}
        for d, h in pairs:
            qk[d, h] = _dot_nt(rows(d, h * DH), rows(d, c_km + h * DH))
            qkr[d, h] = _dot_nt(rows(d, c_qr + h * DH), rows(d, c_kr + h * DH))
        upd, updr = {}, {}
        for d, h in pairs:
            k = d * H_M + h
            ci = prep[d]["ci"]
            vo = jnp.concatenate([rows(d, c_vm + h * DH), ones], axis=1)
            wkT = (kT_s[h, ci] * jnp.broadcast_to(prep[d]["WT"][k:k + 1, :], (DH, L))).astype(BF16)
            upd[d, h] = _dot(wkT, vo)
            kdT = (kT_s[H_M + h, ci] * dk_s[k]).astype(BF16)
            updr[d, h] = _dot(kdT, rows(d, c_vr + h * DH))
        for d, h in pairs:
            k = d * H_M + h
            r0 = prep[d]["r0"]
            h_dst = hf_s if d == 0 else hb_s
            q = rows(d, h * DH)
            M_col = jnp.broadcast_to(prep[d]["M"][:, k:k + 1], (L, L))
            z_row = jnp.broadcast_to(prep[d]["ZT"][k:k + 1, :], (L, L))
            D = jnp.where(prep[d]["mask"], jnp.exp(z_row - M_col), 0.0)
            s = (qk[d, h] * D).astype(BF16)
            w_inter = jnp.exp(jnp.broadcast_to(m_prev[:, k:k + 1], (L, L)) - M_col)
            wq = (w_inter * q.astype(F32)).astype(BF16)
            vo = jnp.concatenate([rows(d, c_vm + h * DH), ones], axis=1)
            CN = CN_s[k]
            res = _dot(jnp.concatenate([s, wq], axis=1),
                       jnp.concatenate([vo, CN.astype(BF16)], axis=0))
            floor = jnp.exp(-jnp.broadcast_to(prep[d]["m_row"][:, k:k + 1], (L, L)))
            h_dst[pl.ds(r0, L), h * DH:(h + 1) * DH] = res[:, :DH] / jnp.maximum(jnp.abs(res[:, DH:]), floor)
            CN_s[k] = jnp.broadcast_to(prep[d]["decay"][:, k:k + 1], (DH, 2 * DH)) * CN + upd[d, h]
            qr = rows(d, c_qr + h * DH)
            S = S_s[k]
            sr = (qkr[d, h] * dm_s[k]).astype(BF16)
            qd = (qr.astype(F32) * dq_s[k]).astype(BF16)
            h_dst[pl.ds(r0, L), W_M + h * DH:W_M + (h + 1) * DH] = _dot(
                jnp.concatenate([sr, qd], axis=1),
                jnp.concatenate([rows(d, c_vr + h * DH), S.astype(BF16)], axis=0))
            S_s[k] = dL_s[k] * S + updr[d, h]
        lane = lax.broadcasted_iota(jnp.int32, (1, LANES), 1)
        m_s[...] = jnp.where(lane < H_M, m_new[0], m_new[1])
        return carry

    lax.fori_loop(0, n_chunks, chunk_step, 0)

    for h in range(H_M):
        sl = slice(h * DH, (h + 1) * DH)
        hs = hf_s[:, sl] + hb_s[:, sl]
        yn = _rmsnorm(hs, nm_ref[0, :, sl])
        om = p_ref[:, c_om + h * DH:c_om + (h + 1) * DH].astype(F32)
        y_ref[:, sl] = (jax.nn.sigmoid(om) * yn).astype(BF16)
        slr = slice(W_M + h * DH, W_M + (h + 1) * DH)
        hr = hf_s[:, slr] + hb_s[:, slr]
        ynr = _rmsnorm(hr, nr_ref[0, :, sl])
        gr = p_ref[:, c_gr + h * DH:c_gr + (h + 1) * DH].astype(F32)
        y_ref[:, slr] = (_silu(gr) * ynr).astype(BF16)

    if not has_state:
        for d in range(2):
            for h in range(H_M):
                k = d * H_M + h
                C_out[0, 0, d, h] = CN_s[k, :, :DH]
                n_out[0, 0, d, h:h + 1, :] = CN_s[k, :, DH:].T[0:1, :]
                S_out[0, 0, d, h] = S_s[k]
        m_out[0, 0] = m_s[...]


def _scan_scratch(T):
    return [
        pltpu.VMEM((2 * H_M, DH, 2 * DH), F32),
        pltpu.VMEM((2 * H_R, DH, DH), F32),
        pltpu.VMEM((1, LANES), F32),
        pltpu.VMEM((T, W_M + W_R), F32),
        pltpu.VMEM((T, W_M + W_R), F32),
        pltpu.VMEM((2 * H_R, CHUNK, CHUNK), F32),
        pltpu.VMEM((2 * H_R, CHUNK, CHUNK), F32),
        pltpu.VMEM((2 * H_R, CHUNK, CHUNK), F32),
        pltpu.VMEM((2 * H_R, 1, LANES), F32),
        pltpu.VMEM((H_M + H_R, T // CHUNK, DH, CHUNK), F32),
    ]


def _scan_ctx(p, gates, dl, nm, nr, layer, prev_states=()):
    T = SEQ
    common = [
        pl.BlockSpec((T, P_COLS), lambda b: (b, 0)),
        pl.BlockSpec((T, GATE_LANES), lambda b: (b, 0)),
        pl.BlockSpec((1, 2 * H_R, LANES), lambda b: (layer, 0, 0)),
        pl.BlockSpec((1, 1, W_M), lambda b: (layer, 0, 0)),
        pl.BlockSpec((1, 1, W_R), lambda b: (layer, 0, 0)),
    ]
    n_al = len(prev_states)
    return pl.pallas_call(
        functools.partial(_scan_kernel, T=T, has_state=False, n_aliased=n_al),
        grid=(BATCH,),
        in_specs=common + [pl.BlockSpec(memory_space=pl.ANY)] * n_al,
        out_specs=[
            pl.BlockSpec((T, D_MODEL), lambda b: (b, 0)),
            pl.BlockSpec((1, 1, 2, H_M, DH, DH), lambda b: (b, layer, 0, 0, 0, 0)),
            pl.BlockSpec((1, 1, 2, H_M, DH), lambda b: (b, layer, 0, 0, 0)),
            pl.BlockSpec((1, 1, 1, LANES), lambda b: (b, layer, 0, 0)),
            pl.BlockSpec((1, 1, 2, H_R, DH, DH), lambda b: (b, layer, 0, 0, 0, 0)),
        ],
        out_shape=[
            jax.ShapeDtypeStruct((NTOK, D_MODEL), BF16),
            jax.ShapeDtypeStruct((BATCH, DEPTH, 2, H_M, DH, DH), F32),
            jax.ShapeDtypeStruct((BATCH, DEPTH, 2, H_M, DH), F32),
            jax.ShapeDtypeStruct((BATCH, DEPTH, 1, LANES), F32),
            jax.ShapeDtypeStruct((BATCH, DEPTH, 2, H_R, DH, DH), F32),
        ],
        input_output_aliases={len(common) + a: 1 + a for a in range(n_al)},
        scratch_shapes=_scan_scratch(T),
        compiler_params=pltpu.CompilerParams(
            dimension_semantics=("arbitrary",), vmem_limit_bytes=VMEM_LIMIT),
        name="scan_ctx",
    )(p, gates, dl, nm, nr, *prev_states)


def _scan_lat(p, gates, dl, nm, nr, C0, n0, m0, S0, y_prev, layer):
    T = DEC_SEQ
    off = N_CTX // T
    in_specs = [
        pl.BlockSpec((T, P_COLS), lambda b: (off + b, 0)),
        pl.BlockSpec((T, GATE_LANES), lambda b: (off + b, 0)),
        pl.BlockSpec((1, 2 * H_R, LANES), lambda b: (layer, 0, 0)),
        pl.BlockSpec((1, 1, W_M), lambda b: (layer, 0, 0)),
        pl.BlockSpec((1, 1, W_R), lambda b: (layer, 0, 0)),
        pl.BlockSpec((1, 1, 2, H_M, DH, DH), lambda b: (b, layer, 0, 0, 0, 0)),
        pl.BlockSpec((1, 1, 2, H_M, DH), lambda b: (b, layer, 0, 0, 0)),
        pl.BlockSpec((1, 1, 1, LANES), lambda b: (b, layer, 0, 0)),
        pl.BlockSpec((1, 1, 2, H_R, DH, DH), lambda b: (b, layer, 0, 0, 0, 0)),
        pl.BlockSpec(memory_space=pl.ANY),
    ]
    return pl.pallas_call(
        functools.partial(_scan_kernel, T=T, has_state=True),
        grid=(DEC_BATCH,),
        in_specs=in_specs,
        out_specs=pl.BlockSpec((T, D_MODEL), lambda b: (off + b, 0)),
        out_shape=jax.ShapeDtypeStruct((NTOK, D_MODEL), BF16),
        input_output_aliases={9: 0},
        scratch_shapes=_scan_scratch(T),
        compiler_params=pltpu.CompilerParams(
            dimension_semantics=("arbitrary",), vmem_limit_bytes=VMEM_LIMIT),
        name="scan_lat",
    )(p, gates, dl, nm, nr, C0, n0, m0, S0, y_prev)


def _top2(logits):
    lane = lax.broadcasted_iota(jnp.int32, logits.shape, 1)
    v1 = jnp.max(logits, -1, keepdims=True)
    i1 = jnp.min(jnp.where(logits == v1, lane, LANES), -1, keepdims=True)
    rest = jnp.where(lane == i1, -jnp.inf, logits)
    v2 = jnp.max(rest, -1, keepdims=True)
    i2 = jnp.min(jnp.where(rest == v2, lane, LANES), -1, keepdims=True)
    e2 = jnp.exp(v2 - v1)
    return i1, i2, 1.0 / (1.0 + e2), e2 / (1.0 + e2)


def _split2(x):
    hi = x.astype(BF16)
    return hi, (x - hi.astype(F32)).astype(BF16)


def _dot_f32x3(a, b):
    a_hi, a_lo = _split2(a)
    b_hi, b_lo = _split2(b)
    return _dot(a_hi, b_hi) + _dot(a_hi, b_lo) + _dot(a_lo, b_hi)


R_E1, R_E2, R_W1, R_W2, R_S1, R_S2 = range(6)


def _out_kernel(*refs, with_router, split_x):
    y_ref = refs[0]
    if split_x:
        xp_ref, xl_ref = refs[1:3]
        x_in = jnp.where(pl.program_id(0) >= N_CTX // TM, xl_ref[...], xp_ref[...])
    else:
        x_in = refs[1][...]
    refs = refs[3:] if split_x else refs[2:]
    if with_router:
        (w_ref, g_ref, mod_ref, wr_ref,
         x1_ref, h2_ref, rinfo_ref, cnt_ref, w_scr, tri_scr, cnt_scr) = refs
    else:
        w_ref, g_ref, mod_ref, x1_ref, h2_ref, w_scr = refs

    @pl.when(pl.program_id(0) == 0)
    def _():
        w_scr[...] = w_ref[0].astype(BF16)
        if with_router:
            r = lax.broadcasted_iota(jnp.int32, (TM, TM), 0)
            c = lax.broadcasted_iota(jnp.int32, (TM, TM), 1)
            tri_scr[...] = (c < r).astype(BF16)
            cnt_scr[...] = jnp.zeros_like(cnt_scr)

    o = jnp.dot(y_ref[...], w_scr[...], preferred_element_type=F32)
    x1 = x_in + mod_ref[0, 0, 2:3, :] * o
    x1_ref[...] = x1
    h2 = _rmsnorm(x1, g_ref[0]) * (1.0 + mod_ref[0, 0, 4:5, :]) + mod_ref[0, 0, 3:4, :]
    if not with_router:
        h2_ref[...] = h2.astype(BF16)
    else:
        h2_ref[...] = h2
        logits = _dot_f32x3(h2, wr_ref[0])
        lane = lax.broadcasted_iota(jnp.int32, logits.shape, 1)
        i1, i2, w1, w2 = _top2(jnp.where(lane < N_EXPERTS, logits, -jnp.inf))
        oh1 = lane == i1
        oh2 = lane == i2
        sel = jnp.where(oh1 | oh2, 1.0, 0.0)
        rank = _dot(tri_scr[...], sel.astype(BF16)) + cnt_scr[...]
        r1 = jnp.sum(jnp.where(oh1, rank, 0.0), -1, keepdims=True)
        r2 = jnp.sum(jnp.where(oh2, rank, 0.0), -1, keepdims=True)
        s1 = i1.astype(F32) * float(REG) + r1
        s2 = i2.astype(F32) * float(REG) + r2
        info = jnp.zeros(logits.shape, F32)
        for col, val in ((R_E1, i1.astype(F32)), (R_E2, i2.astype(F32)), (R_W1, w1), (R_W2, w2),
                         (R_S1, s1), (R_S2, s2)):
            info = jnp.where(lane == col, val, info)
        rinfo_ref[...] = info
        cnt_scr[...] += jnp.sum(sel, 0, keepdims=True)
        cnt_ref[...] = cnt_scr[...]


def _out(y, xs, w_out, g2, mods, layer, w_router_pad=None, router_idx=0):
    with_router = w_router_pad is not None
    split_x = len(xs) == 2
    n_ctx_tiles = N_CTX // TM
    if split_x:
        x_specs = [pl.BlockSpec((TM, D_MODEL), lambda i: (jnp.minimum(i, n_ctx_tiles - 1), 0)),
                   pl.BlockSpec((TM, D_MODEL), lambda i: (jnp.maximum(i - n_ctx_tiles, 0), 0))]
    else:
        x_specs = [pl.BlockSpec((TM, D_MODEL), lambda i: (i, 0))]
    in_specs = [pl.BlockSpec((TM, D_MODEL), lambda i: (i, 0))] + x_specs + [
        pl.BlockSpec((1, D_MODEL, D_MODEL), lambda i: (layer, 0, 0)),
        pl.BlockSpec((1, 1, D_MODEL), lambda i: (layer, 0, 0)),
        pl.BlockSpec((1, 1, 6, D_MODEL), lambda i: (layer, _group_of_tile(i, TM), 0, 0)),
    ]
    out_specs = [
        pl.BlockSpec((TM, D_MODEL), lambda i: (i, 0)),
        pl.BlockSpec((TM, D_MODEL), lambda i: (i, 0)),
    ]
    out_shape = [
        jax.ShapeDtypeStruct((NTOK, D_MODEL), F32),
        jax.ShapeDtypeStruct((NTOK, D_MODEL), F32 if with_router else BF16),
    ]
    args = [y, *xs, w_out, g2, mods]
    scratch = [pltpu.VMEM((D_MODEL, D_MODEL), BF16)]
    if with_router:
        in_specs.append(pl.BlockSpec((1, D_MODEL, LANES), lambda i: (router_idx, 0, 0)))
        out_specs += [pl.BlockSpec((TM, LANES), lambda i: (i, 0)),
                      pl.BlockSpec((1, LANES), lambda i: (0, 0))]
        out_shape += [jax.ShapeDtypeStruct((NTOK, LANES), F32),
                      jax.ShapeDtypeStruct((1, LANES), F32)]
        args.append(w_router_pad)
        scratch += [pltpu.VMEM((TM, TM), BF16), pltpu.VMEM((1, LANES), F32)]
    return pl.pallas_call(
        functools.partial(_out_kernel, with_router=with_router, split_x=split_x),
        grid=(NTOK // TM,),
        in_specs=in_specs,
        out_specs=out_specs,
        out_shape=out_shape,
        scratch_shapes=scratch,
        compiler_params=pltpu.CompilerParams(
            dimension_semantics=("arbitrary",), vmem_limit_bytes=VMEM_LIMIT),
        name="out_router" if with_router else "out",
    )(*args)


def _swiglu_step(h, wg_ref, wu_ref, wd_ref, acc):
    w_cat = jnp.concatenate([wg_ref[0].astype(BF16), wu_ref[0].astype(BF16)], axis=1)
    ab = jnp.dot(h, w_cat, preferred_element_type=F32)
    t = (_silu(ab[:, :FC]) * ab[:, FC:]).astype(BF16)
    acc[...] += jnp.dot(t, wd_ref[0].astype(BF16), preferred_element_type=F32)


def _ffn_kernel(h_ref, res_ref, wg_ref, wu_ref, wd_ref, mod_ref, o_ref, acc):
    f = pl.program_id(1)

    @pl.when(f == 0)
    def _():
        acc[...] = jnp.zeros_like(acc)

    _swiglu_step(h_ref[...], wg_ref, wu_ref, wd_ref, acc)

    @pl.when(f == pl.num_programs(1) - 1)
    def _():
        o_ref[...] = res_ref[...] + mod_ref[0, 0, 5:6, :] * acc[...]


def _ffn(h2, res, wg, wu, wd, mods, layer, w_idx):
    return pl.pallas_call(
        _ffn_kernel,
        grid=(NTOK // TM, D_FF // FC),
        in_specs=[
            pl.BlockSpec((TM, D_MODEL), lambda i, f: (i, 0)),
            pl.BlockSpec((TM, D_MODEL), lambda i, f: (i, 0)),
            pl.BlockSpec((1, D_MODEL, FC), lambda i, f: (w_idx, 0, f)),
            pl.BlockSpec((1, D_MODEL, FC), lambda i, f: (w_idx, 0, f)),
            pl.BlockSpec((1, FC, D_MODEL), lambda i, f: (w_idx, f, 0)),
            pl.BlockSpec((1, 1, 6, D_MODEL), lambda i, f: (layer, _group_of_tile(i, TM), 0, 0)),
        ],
        out_specs=pl.BlockSpec((TM, D_MODEL), lambda i, f: (i, 0)),
        out_shape=jax.ShapeDtypeStruct((NTOK, D_MODEL), F32),
        scratch_shapes=[pltpu.VMEM((TM, D_MODEL), F32)],
        compiler_params=pltpu.CompilerParams(
            dimension_semantics=("arbitrary", "arbitrary"), vmem_limit_bytes=VMEM_LIMIT),
        name="ffn",
    )(h2, res, wg, wu, wd, mods)


def _tile_plan(counts):
    nt = (counts + TR - 1) // TR
    cum = jnp.cumsum(nt)
    total = cum[-1]
    t = jnp.arange(MAX_TILES, dtype=jnp.int32)
    tt = jnp.minimum(t, total - 1)
    e = jnp.sum((cum[None, :] <= tt[:, None]).astype(jnp.int32), axis=1)
    k = tt - (cum - nt)[e]
    n = jnp.where(t < total, jnp.clip(counts[e] - k * TR, 0, TR), 0)
    return e.astype(jnp.int32), (e * REG_TILES + k).astype(jnp.int32), n.astype(jnp.int32)


def _row_copy(src, src_row, dst, dst_row, sem):
    return pltpu.make_async_copy(src.at[pl.ds(src_row, 1)], dst.at[pl.ds(dst_row, 1)], sem)


def _dispatch_kernel(slot_ref, h_ref, xs_ref, sem):
    base = pl.program_id(0) * (TOP_K * TD)

    def issue(r, carry):
        for k in range(TOP_K):
            _row_copy(h_ref, r, xs_ref, slot_ref[base + TOP_K * r + k], sem).start()
        return carry

    lax.fori_loop(0, TD, issue, 0, unroll=8)
    for k in range(TOP_K):
        pltpu.make_async_copy(h_ref, xs_ref.at[pl.ds(0, TD)], sem).wait()


def _dispatch(slots, h2f):
    return pl.pallas_call(
        _dispatch_kernel,
        grid_spec=pltpu.PrefetchScalarGridSpec(
            num_scalar_prefetch=1,
            grid=(NTOK // TD,),
            in_specs=[pl.BlockSpec((TD, D_MODEL), lambda i, s: (i, 0))],
            out_specs=pl.BlockSpec(memory_space=pl.ANY),
            scratch_shapes=[pltpu.SemaphoreType.DMA],
        ),
        out_shape=jax.ShapeDtypeStruct((N_EXPERTS * REG, D_MODEL), F32),
        compiler_params=pltpu.CompilerParams(
            dimension_semantics=("arbitrary",), vmem_limit_bytes=VMEM_LIMIT),
        name="moe_dispatch",
    )(slots, h2f)


def _gffn_kernel(te_ref, tb_ref, tn_ref, x_ref, wg_ref, wu_ref, wd_ref, o_ref, h_scr, acc):
    del te_ref, tb_ref
    f = pl.program_id(1)
    n = tn_ref[pl.program_id(0)]

    @pl.when(n > 0)
    def _():
        @pl.when(f == 0)
        def _():
            row = lax.broadcasted_iota(jnp.int32, (TR, D_MODEL), 0)
            h_scr[...] = jnp.where(row < n, x_ref[...], 0.0).astype(BF16)
            acc[...] = jnp.zeros_like(acc)

        _swiglu_step(h_scr[...], wg_ref, wu_ref, wd_ref, acc)

        @pl.when(f == pl.num_programs(1) - 1)
        def _():
            o_ref[...] = acc[...]


def _gffn(tile_e, tile_blk, tile_n, xs, wg, wu, wd, w_base):
    nf = D_FF // FC

    def f_eff(t, f, tn):
        return jnp.where(tn[t] > 0, f, nf - 1)

    return pl.pallas_call(
        _gffn_kernel,
        grid_spec=pltpu.PrefetchScalarGridSpec(
            num_scalar_prefetch=3,
            grid=(MAX_TILES, nf),
            in_specs=[
                pl.BlockSpec((TR, D_MODEL), lambda t, f, te, tb, tn: (tb[t], 0)),
                pl.BlockSpec((1, D_MODEL, FC), lambda t, f, te, tb, tn: (w_base + te[t], 0, f_eff(t, f, tn))),
                pl.BlockSpec((1, D_MODEL, FC), lambda t, f, te, tb, tn: (w_base + te[t], 0, f_eff(t, f, tn))),
                pl.BlockSpec((1, FC, D_MODEL), lambda t, f, te, tb, tn: (w_base + te[t], f_eff(t, f, tn), 0)),
            ],
            out_specs=pl.BlockSpec((TR, D_MODEL), lambda t, f, te, tb, tn: (tb[t], 0)),
            scratch_shapes=[pltpu.VMEM((TR, D_MODEL), BF16), pltpu.VMEM((TR, D_MODEL), F32)],
        ),
        out_shape=jax.ShapeDtypeStruct((N_EXPERTS * REG, D_MODEL), F32),
        compiler_params=pltpu.CompilerParams(
            dimension_semantics=("arbitrary", "arbitrary"), vmem_limit_bytes=VMEM_LIMIT),
        name="moe_ffn",
    )(tile_e, tile_blk, tile_n, xs, wg, wu, wd)


def _combine_kernel(slot_ref, x1_ref, rinfo_ref, mod_ref, ys_ref, o_ref, buf, sem):
    base = pl.program_id(0) * (TOP_K * TD)

    def issue(r, carry):
        for k in range(TOP_K):
            _row_copy(ys_ref, slot_ref[base + TOP_K * r + k], buf.at[k], r, sem).start()
        return carry

    lax.fori_loop(0, TD, issue, 0, unroll=8)
    for k in range(TOP_K):
        pltpu.make_async_copy(ys_ref.at[pl.ds(0, TD)], buf.at[k], sem).wait()
    y = rinfo_ref[:, R_W1:R_W1 + 1] * buf[0] + rinfo_ref[:, R_W2:R_W2 + 1] * buf[1]
    o_ref[...] = x1_ref[...] + mod_ref[0, 0, 5:6, :] * y


def _combine(slots, x1, rinfo, mods, ys, layer):
    return pl.pallas_call(
        _combine_kernel,
        grid_spec=pltpu.PrefetchScalarGridSpec(
            num_scalar_prefetch=1,
            grid=(NTOK // TD,),
            in_specs=[
                pl.BlockSpec((TD, D_MODEL), lambda i, s: (i, 0)),
                pl.BlockSpec((TD, LANES), lambda i, s: (i, 0)),
                pl.BlockSpec((1, 1, 6, D_MODEL), lambda i, s: (layer, _group_of_tile(i, TD), 0, 0)),
                pl.BlockSpec(memory_space=pl.ANY),
            ],
            out_specs=pl.BlockSpec((TD, D_MODEL), lambda i, s: (i, 0)),
            scratch_shapes=[pltpu.VMEM((TOP_K, TD, D_MODEL), F32), pltpu.SemaphoreType.DMA],
        ),
        out_shape=jax.ShapeDtypeStruct((NTOK, D_MODEL), F32),
        compiler_params=pltpu.CompilerParams(
            dimension_semantics=("arbitrary",), vmem_limit_bytes=VMEM_LIMIT),
        name="moe_combine",
    )(slots, x1, rinfo, mods, ys)


def _final_kernel(x_ref, g_ref, o_ref):
    o_ref[...] = _rmsnorm(x_ref[...], g_ref[...])


def _final(x, g, row_off, rows):
    off = row_off // TM
    return pl.pallas_call(
        _final_kernel,
        grid=(rows // TM,),
        in_specs=[
            pl.BlockSpec((TM, D_MODEL), lambda i: (off + i, 0)),
            pl.BlockSpec((1, D_MODEL), lambda i: (0, 0)),
        ],
        out_specs=pl.BlockSpec((TM, D_MODEL), lambda i: (i, 0)),
        out_shape=jax.ShapeDtypeStruct((rows, D_MODEL), F32),
        compiler_params=pltpu.CompilerParams(
            dimension_semantics=("arbitrary",), vmem_limit_bytes=VMEM_LIMIT),
        name="final_norm",
    )(x, g)


def kernel(x_prompt, x_sample, state_mlstm_C, state_mlstm_n, state_mlstm_m, state_ret_S, c, c_ctx,
           norm1_g, norm2_g, norm_f_g, w_ada, b_ada, w_in, b_gates, ret_decay_logit,
           mlstm_norm_g, ret_norm_g, w_out, ffn_w_gate, ffn_w_up, ffn_w_down,
           moe_w_router, moe_w_gate, moe_w_up, moe_w_down):
    xs_in = (x_prompt.reshape(N_CTX, D_MODEL), x_sample.reshape(N_LAT, D_MODEL))
    cvec = jnp.concatenate(
        [c_ctx[None, :], c, jnp.zeros((N_GROUPS - 1 - DEC_BATCH, D_MODEL), F32)], 0)
    mods = _ada(cvec, w_ada, b_ada).reshape(DEPTH, N_GROUPS, 6, D_MODEL)

    n_m = 4 * W_M
    w_ret = w_in[:, :, n_m + N_GATES:]
    n_if = N_GATES // 2
    lane_pad = ((0, 0), (0, 0), (0, LANES - n_if))
    wg = jnp.concatenate([jnp.pad(w_in[:, :, n_m:n_m + n_if], lane_pad),
                          jnp.pad(w_in[:, :, n_m + n_if:n_m + N_GATES], lane_pad)], -1)
    bg = jnp.concatenate([jnp.pad(b_gates[:, None, :n_if], lane_pad),
                          jnp.pad(b_gates[:, None, n_if:], lane_pad)], -1)
    cos_np, sin_np = _rope_tables()
    cos_t, sin_t = jnp.asarray(cos_np), jnp.asarray(sin_np)
    dl = jnp.broadcast_to(ret_decay_logit.reshape(DEPTH, 2 * H_R, 1), (DEPTH, 2 * H_R, LANES))
    m0 = jnp.pad(state_mlstm_m.reshape(DEC_BATCH, DEPTH, 1, 2 * H_M),
                 ((0, 0), (0, 0), (0, 0), (0, LANES - 2 * H_M)))
    g1 = norm1_g.reshape(DEPTH, 1, D_MODEL)
    g2 = norm2_g.reshape(DEPTH, 1, D_MODEL)
    nm = mlstm_norm_g.reshape(DEPTH, 1, W_M)
    nr = ret_norm_g.reshape(DEPTH, 1, W_R)
    n_moe = moe_w_router.shape[0]
    wr_pad = jnp.pad(moe_w_router, ((0, 0), (0, 0), (0, LANES - N_EXPERTS)))
    moe_g = moe_w_gate.reshape(n_moe * N_EXPERTS, D_MODEL, D_FF)
    moe_u = moe_w_up.reshape(n_moe * N_EXPERTS, D_MODEL, D_FF)
    moe_d = moe_w_down.reshape(n_moe * N_EXPERTS, D_FF, D_MODEL)

    states = ()
    xs = xs_in
    for l in range(DEPTH):
        jl = l // 2
        p, gates = _proj(xs, g1, mods, w_in, w_ret, wg, bg, cos_t, sin_t, l)
        y, *states = _scan_ctx(p, gates, dl, nm, nr, l, states)
        y = _scan_lat(p, gates, dl, nm, nr, state_mlstm_C, state_mlstm_n, m0, state_ret_S, y, l)
        if l % 2 == 0:
            x1, h2 = _out(y, xs, w_out, g2, mods, l)
            x = _ffn(h2, x1, ffn_w_gate, ffn_w_up, ffn_w_down, mods, l, jl)
        else:
            x1, h2f, rinfo, cnt = _out(y, xs, w_out, g2, mods, l, wr_pad, jl)
            slots = rinfo[:, R_S1:R_S2 + 1].astype(jnp.int32).reshape(TOP_K * NTOK)
            tile_e, tile_blk, tile_n = _tile_plan(cnt[0, :N_EXPERTS].astype(jnp.int32))
            xd = _dispatch(slots, h2f)
            yd = _gffn(tile_e, tile_blk, tile_n, xd, moe_g, moe_u, moe_d, jl * N_EXPERTS)
            x = _combine(slots, x1, rinfo, mods, yd, l)
        xs = (x,)

    y_prompt = _final(x, norm_f_g.reshape(1, D_MODEL), 0, N_CTX).reshape(BATCH, SEQ, D_MODEL)
    y_sample = _final(x, norm_f_g.reshape(1, D_MODEL), N_CTX, N_LAT).reshape(DEC_BATCH, DEC_SEQ, D_MODEL)
    new_C, new_n, new_m, new_S = states
    return (y_prompt, y_sample, new_C, new_n,
            new_m[:, :, 0, :2 * H_M].reshape(BATCH, DEPTH, 2, H_M), new_S)
```

```python
import functools

import numpy as np
import jax
import jax.numpy as jnp
from jax import lax
from jax.experimental import pallas as pl
from jax.experimental.pallas import tpu as pltpu

D_MODEL = 1024
BATCH = 32
SEQ = 256
DEPTH = 2
DEC_BATCH = 2
DEC_SEQ = 1024
GRID_W = 64
H_M = 4
DH = 128
H_R = 4
W_M = H_M * DH
W_R = H_R * DH
N_GATES = 4 * H_M
CHUNK = 128
D_FF = 2816
N_EXPERTS = 8
ROPE_BASE = 10000.0
EPS = 1e-6

N_CTX = BATCH * SEQ
N_LAT = DEC_BATCH * DEC_SEQ
NTOK = N_CTX + N_LAT
N_GROUPS = 8
K_SCALE = DH ** -0.5
P_COLS = 4 * W_M + 4 * W_R
LANES = 128
GATE_LANES = 2 * LANES
VMEM_LIMIT = 56 * 1024 * 1024

F32 = jnp.float32
BF16 = jnp.bfloat16
HIGHEST = lax.Precision.HIGHEST

TM = 1024
TN = 1024
FC = 256
TOP_K = 2
TR = 896
REG_TILES = -(-NTOK // TR)
REG = REG_TILES * TR
MAX_TILES = -(-TOP_K * NTOK // TR) + N_EXPERTS
TD = 512


def _group_of_tile(i, tm):
    return jnp.maximum(i * tm // DEC_SEQ - (N_CTX // DEC_SEQ - 1), 0)


def _silu(x):
    return x * jax.nn.sigmoid(x)


def _log_sigmoid(x):
    return jnp.minimum(x, 0.0) - jnp.log(1.0 + jnp.exp(-jnp.abs(x)))


def _rmsnorm(x, g):
    return x * lax.rsqrt(jnp.mean(x * x, -1, keepdims=True) + EPS) * g


def _ada_kernel(cv_ref, w_ref, b_ref, o_ref):
    s = _silu(cv_ref[...]).astype(BF16)
    o_ref[0] = jnp.dot(s, w_ref[0].astype(BF16), preferred_element_type=F32) + b_ref[0]


def _ada(cvec, w_ada, b_ada):
    tn = 1536
    n = 6 * D_MODEL
    return pl.pallas_call(
        _ada_kernel,
        grid=(DEPTH, n // tn),
        in_specs=[
            pl.BlockSpec((N_GROUPS, D_MODEL), lambda l, j: (0, 0)),
            pl.BlockSpec((1, D_MODEL, tn), lambda l, j: (l, 0, j)),
            pl.BlockSpec((1, 1, tn), lambda l, j: (l, 0, j)),
        ],
        out_specs=pl.BlockSpec((1, N_GROUPS, tn), lambda l, j: (l, 0, j)),
        out_shape=jax.ShapeDtypeStruct((DEPTH, N_GROUPS, n), F32),
        compiler_params=pltpu.CompilerParams(
            dimension_semantics=("arbitrary", "arbitrary"), vmem_limit_bytes=VMEM_LIMIT),
        name="ada",
    )(cvec, w_ada, b_ada.reshape(DEPTH, 1, n))


def _rope_tables():
    half = DH // 4
    freqs = ROPE_BASE ** (-np.arange(half, dtype=np.float64) / half)
    t = np.arange(DEC_SEQ)
    pos = np.stack([t // GRID_W, t % GRID_W], 1).astype(np.float64)
    d = np.arange(DH)
    ang = pos[:, d // (DH // 2)] * freqs[d % half][None, :]
    sign = np.where((d % (DH // 2)) < half, -1.0, 1.0)[None, :]
    return np.cos(ang).astype(np.float32), (sign * np.sin(ang)).astype(np.float32)


def _rope(a, cos, sin):
    lane = lax.broadcasted_iota(jnp.int32, a.shape, 1)
    first = (lane % (DH // 2)) < (DH // 4)
    partner = jnp.where(first, pltpu.roll(a, DH - DH // 4, 1), pltpu.roll(a, DH // 4, 1))
    return a * cos + partner * sin


def _proj_kernel(*refs, n_ctx_tiles, split_x):
    if split_x:
        (xp_ref, xl_ref, g_ref, mod_ref, wa_ref, wb_ref, wg_ref, bg_ref, cos_ref, sin_ref,
         p_ref, gate_ref, h_scr, w_res) = refs
    else:
        (x_ref, g_ref, mod_ref, wa_ref, wb_ref, wg_ref, bg_ref, cos_ref, sin_ref,
         p_ref, gate_ref, h_scr, w_res) = refs
    i = pl.program_id(0)
    j = pl.program_id(1)
    is_lat = i >= n_ctx_tiles
    half = TN // 2

    def prologue(x):
        h = _rmsnorm(x, g_ref[0]) * (1.0 + mod_ref[0, 0, 1:2, :]) + mod_ref[0, 0, 0:1, :]
        h_scr[...] = h.astype(BF16)
        gate_ref[...] = _dot_f32x3(h, wg_ref[0]) + bg_ref[0]

    @pl.when(j == 0)
    def _():
        if split_x:
            pl.when(jnp.logical_not(is_lat))(lambda: prologue(xp_ref[...]))
            pl.when(is_lat)(lambda: prologue(xl_ref[...]))
        else:
            prologue(x_ref[...])

    def matmul(w_ref, jj):
        @pl.when(i == 0)
        def _():
            w_res[jj] = w_ref[0].astype(BF16)

        return jnp.dot(h_scr[...], w_res[jj], preferred_element_type=F32)

    @pl.when(j == 0)
    def _():
        acc = matmul(wa_ref, 0)
        p_ref[:, :half] = acc[:, :half].astype(BF16)
        p_ref[:, half:] = (acc[:, half:] * K_SCALE).astype(BF16)

    @pl.when(j == 1)
    def _():
        p_ref[...] = matmul(wa_ref, 1).astype(BF16)

    @pl.when(j == 2)
    def _():
        acc = matmul(wb_ref, 2)

        @pl.when(is_lat)
        def _():
            cos = cos_ref[...]
            sin = sin_ref[...]
            for hd in range(TN // DH):
                sl = slice(hd * DH, (hd + 1) * DH)
                r = _rope(acc[:, sl], cos, sin)
                p_ref[:, sl] = (r * K_SCALE if hd * DH >= half else r).astype(BF16)

        @pl.when(jnp.logical_not(is_lat))
        def _():
            p_ref[:, :half] = acc[:, :half].astype(BF16)
            p_ref[:, half:] = (acc[:, half:] * K_SCALE).astype(BF16)

    @pl.when(j == 3)
    def _():
        p_ref[...] = matmul(wb_ref, 3).astype(BF16)


def _proj(xs, g1, mods, w_in, w_ret, wg, bg, cos_t, sin_t, layer):
    n_ctx_tiles = N_CTX // TM
    tiles_per_seq = DEC_SEQ // TM
    split_x = len(xs) == 2
    if split_x:
        x_specs = [pl.BlockSpec((TM, D_MODEL), lambda i, j: (jnp.minimum(i, n_ctx_tiles - 1), 0)),
                   pl.BlockSpec((TM, D_MODEL), lambda i, j: (jnp.maximum(i - n_ctx_tiles, 0), 0))]
    else:
        x_specs = [pl.BlockSpec((TM, D_MODEL), lambda i, j: (i, 0))]
    n_a = 4 * W_M // TN
    return pl.pallas_call(
        functools.partial(_proj_kernel, n_ctx_tiles=n_ctx_tiles, split_x=split_x),
        grid=(NTOK // TM, P_COLS // TN),
        in_specs=x_specs + [
            pl.BlockSpec((1, 1, D_MODEL), lambda i, j: (layer, 0, 0)),
            pl.BlockSpec((1, 1, 6, D_MODEL), lambda i, j: (layer, _group_of_tile(i, TM), 0, 0)),
            pl.BlockSpec((1, D_MODEL, TN),
                         lambda i, j: (layer, 0, jnp.where(i == 0, jnp.minimum(j, n_a - 1), n_a - 1)),
                         pipeline_mode=pl.Buffered(1)),
            pl.BlockSpec((1, D_MODEL, TN),
                         lambda i, j: (layer, 0, jnp.where(i == 0, jnp.maximum(j - n_a, 0), n_a - 1)),
                         pipeline_mode=pl.Buffered(1)),
            pl.BlockSpec((1, D_MODEL, GATE_LANES), lambda i, j: (layer, 0, 0)),
            pl.BlockSpec((1, 1, GATE_LANES), lambda i, j: (layer, 0, 0)),
            pl.BlockSpec((TM, DH), lambda i, j: (i % tiles_per_seq, 0)),
            pl.BlockSpec((TM, DH), lambda i, j: (i % tiles_per_seq, 0)),
        ],
        out_specs=[
            pl.BlockSpec((TM, TN), lambda i, j: (i, j)),
            pl.BlockSpec((TM, GATE_LANES), lambda i, j: (i, 0)),
        ],
        out_shape=[
            jax.ShapeDtypeStruct((NTOK, P_COLS), BF16),
            jax.ShapeDtypeStruct((NTOK, GATE_LANES), F32),
        ],
        scratch_shapes=[pltpu.VMEM((TM, D_MODEL), BF16),
                        pltpu.VMEM((P_COLS // TN, D_MODEL, TN), BF16)],
        compiler_params=pltpu.CompilerParams(
            dimension_semantics=("arbitrary", "arbitrary"), vmem_limit_bytes=VMEM_LIMIT),
        name="proj",
    )(*xs, g1, mods, w_in, w_ret, wg, bg, cos_t, sin_t)


def _split3(x):
    hi = x.astype(BF16)
    r1 = x - hi.astype(F32)
    mid = r1.astype(BF16)
    lo = (r1 - mid.astype(F32)).astype(BF16)
    return hi, mid, lo


def _dot(a, b):
    return jnp.dot(a, b, preferred_element_type=F32)


def _dot_nt(a, b):
    return lax.dot_general(a, b, (((1,), (1,)), ((), ())), preferred_element_type=F32)


def _tri_dot_left(tri, x):
    hi, mid, lo = _split3(x)
    return _dot(tri, hi) + _dot(tri, mid) + _dot(tri, lo)


def _tri_dot_right(x, tri):
    hi, mid, lo = _split3(x)
    return _dot(hi, tri) + _dot(mid, tri) + _dot(lo, tri)


def _run_max(x, reverse):
    n_tiles = x.shape[0] // 8
    sub = lax.broadcasted_iota(jnp.int32, (8, LANES), 0)
    out = [None] * n_tiles
    carry = None
    for t in (range(n_tiles - 1, -1, -1) if reverse else range(n_tiles)):
        v = x[8 * t:8 * t + 8, :]
        for s in (1, 2, 4):
            if reverse:
                v = jnp.maximum(v, jnp.where(sub < 8 - s, pltpu.roll(v, 8 - s, 0), -jnp.inf))
            else:
                v = jnp.maximum(v, jnp.where(sub >= s, pltpu.roll(v, s, 0), -jnp.inf))
        if carry is not None:
            v = jnp.maximum(v, carry)
        carry = jnp.broadcast_to(v[0:1, :] if reverse else v[7:8, :], (8, LANES))
        out[t] = v
    return jnp.concatenate(out, axis=0)


def _scan_kernel(*refs, T, has_state, n_aliased=0):
    if has_state:
        (p_ref, g_ref, dl_ref, nm_ref, nr_ref, C0_ref, n0_ref, m0_ref, S0_ref, _yprev_ref,
         y_ref, CN_s, S_s, m_s, hf_s, hb_s, dm_s, dq_s, dk_s, dL_s, kT_s) = refs
    else:
        p_ref, g_ref, dl_ref, nm_ref, nr_ref = refs[:5]
        (y_ref, C_out, n_out, m_out, S_out,
         CN_s, S_s, m_s, hf_s, hb_s, dm_s, dq_s, dk_s, dL_s, kT_s) = refs[5 + n_aliased:]
    L = CHUNK
    n_chunks = T // L
    row_i = lax.broadcasted_iota(jnp.int32, (L, L), 0)
    col_j = lax.broadcasted_iota(jnp.int32, (L, L), 1)
    lower = col_j <= row_i
    upper = col_j >= row_i
    tril = lower.astype(BF16)
    triu = upper.astype(BF16)
    ones = jnp.ones((L, DH), BF16)
    c_km = W_M
    c_vm = 2 * W_M
    c_om = 3 * W_M
    c_qr = 4 * W_M
    c_kr = c_qr + W_R
    c_vr = c_qr + 2 * W_R
    c_gr = c_qr + 3 * W_R

    for d in range(2):
        for h in range(H_M):
            k = d * H_M + h
            if has_state:
                CN_s[k, :, :DH] = C0_ref[0, 0, d, h]
                CN_s[k, :, DH:] = jnp.broadcast_to(n0_ref[0, 0, d, h:h + 1, :], (DH, DH)).T
                S_s[k] = S0_ref[0, 0, d, h]
            else:
                CN_s[k] = jnp.zeros((DH, 2 * DH), F32)
                S_s[k] = jnp.zeros((DH, DH), F32)
    m_s[...] = m0_ref[0, 0] if has_state else jnp.zeros((1, LANES), F32)

    @pl.when(pl.program_id(0) == 0)
    def _():
        pos_i = row_i.astype(F32)
        pos_j = col_j.astype(F32)
        for d in range(2):
            for h in range(H_R):
                k = d * H_R + h
                lg_row = _log_sigmoid(dl_ref[0, k:k + 1, :])
                lg = jnp.broadcast_to(lg_row, (L, L))
                rel = (row_i - col_j if d == 0 else col_j - row_i).astype(F32)
                dm_s[k] = jnp.where(rel >= 0, jnp.exp(lg * jnp.maximum(rel, 0.0)), 0.0)
                dq_s[k] = jnp.exp(lg * (pos_i + 1.0 if d == 0 else L - pos_i))
                dk_s[k] = jnp.exp(lg * (L - 1.0 - pos_j if d == 0 else pos_j))
                dL_s[k] = jnp.exp(lg_row * float(L))

    def transpose_keys(c, carry):
        r0 = pl.multiple_of(c * L, L)
        for h in range(H_M):
            kT_s[h, c] = p_ref[pl.ds(r0, L), c_km + h * DH:c_km + (h + 1) * DH].astype(F32).T
            kT_s[H_M + h, c] = p_ref[pl.ds(r0, L), c_kr + h * DH:c_kr + (h + 1) * DH].astype(F32).T
        return carry

    lax.fori_loop(0, n_chunks, transpose_keys, 0)

    def chunk_step(c, carry):
        m_prev = m_s[...]
        m_new = []
        prep = []
        for d in range(2):
            ci = c if d == 0 else n_chunks - 1 - c
            r0 = pl.multiple_of(ci * L, L)
            mask = lower if d == 0 else upper
            e_row = L - 1 if d == 0 else 0
            FL = _log_sigmoid(g_ref[pl.ds(r0, L), LANES:2 * LANES])
            Bc = _tri_dot_left(tril if d == 0 else triu, FL)
            Zc = g_ref[pl.ds(r0, L), 0:LANES] - Bc
            M = jnp.maximum(_run_max(Zc, reverse=(d == 1)), m_prev)
            m_row = Bc + M
            M_end = M[e_row:e_row + 1, :]
            m_new.append(Bc[e_row:e_row + 1, :] + M_end)
            decay = jnp.exp(m_prev - M_end)
            prep.append(dict(ci=ci, r0=r0, mask=mask, M=M, m_row=m_row, decay=decay,
                             ZT=Zc.T,
                             WT=jnp.exp(Zc - M_end).T))
        pairs = [(d, h) for d in range(2) for h in range(H_M)]

        def rows(d, col):
            return p_ref[pl.ds(prep[d]["r0"], L), col:col + DH]

        qk, qkr = {}, {}
        for d, h in pairs:
            qk[d, h] = _dot_nt(rows(d, h * DH), rows(d, c_km + h * DH))
            qkr[d, h] = _dot_nt(rows(d, c_qr + h * DH), rows(d, c_kr + h * DH))
        upd, updr = {}, {}
        for d, h in pairs:
            k = d * H_M + h
            ci = prep[d]["ci"]
            vo = jnp.concatenate([rows(d, c_vm + h * DH), ones], axis=1)
            wkT = (kT_s[h, ci] * jnp.broadcast_to(prep[d]["WT"][k:k + 1, :], (DH, L))).astype(BF16)
            upd[d, h] = _dot(wkT, vo)
            kdT = (kT_s[H_M + h, ci] * dk_s[k]).astype(BF16)
            updr[d, h] = _dot(kdT, rows(d, c_vr + h * DH))
        for d, h in pairs:
            k = d * H_M + h
            r0 = prep[d]["r0"]
            h_dst = hf_s if d == 0 else hb_s
            q = rows(d, h * DH)
            M_col = jnp.broadcast_to(prep[d]["M"][:, k:k + 1], (L, L))
            z_row = jnp.broadcast_to(prep[d]["ZT"][k:k + 1, :], (L, L))
            D = jnp.where(prep[d]["mask"], jnp.exp(z_row - M_col), 0.0)
            s = (qk[d, h] * D).astype(BF16)
            w_inter = jnp.exp(jnp.broadcast_to(m_prev[:, k:k + 1], (L, L)) - M_col)
            wq = (w_inter * q.astype(F32)).astype(BF16)
            vo = jnp.concatenate([rows(d, c_vm + h * DH), ones], axis=1)
            CN = CN_s[k]
            res = _dot(jnp.concatenate([s, wq], axis=1),
                       jnp.concatenate([vo, CN.astype(BF16)], axis=0))
            floor = jnp.exp(-jnp.broadcast_to(prep[d]["m_row"][:, k:k + 1], (L, L)))
            h_dst[pl.ds(r0, L), h * DH:(h + 1) * DH] = res[:, :DH] / jnp.maximum(jnp.abs(res[:, DH:]), floor)
            CN_s[k] = jnp.broadcast_to(prep[d]["decay"][:, k:k + 1], (DH, 2 * DH)) * CN + upd[d, h]
            qr = rows(d, c_qr + h * DH)
            S = S_s[k]
            sr = (qkr[d, h] * dm_s[k]).astype(BF16)
            qd = (qr.astype(F32) * dq_s[k]).astype(BF16)
            h_dst[pl.ds(r0, L), W_M + h * DH:W_M + (h + 1) * DH] = _dot(
                jnp.concatenate([sr, qd], axis=1),
                jnp.concatenate([rows(d, c_vr + h * DH), S.astype(BF16)], axis=0))
            S_s[k] = dL_s[k] * S + updr[d, h]
        lane = lax.broadcasted_iota(jnp.int32, (1, LANES), 1)
        m_s[...] = jnp.where(lane < H_M, m_new[0], m_new[1])
        return carry

    lax.fori_loop(0, n_chunks, chunk_step, 0)

    for h in range(H_M):
        sl = slice(h * DH, (h + 1) * DH)
        hs = hf_s[:, sl] + hb_s[:, sl]
        yn = _rmsnorm(hs, nm_ref[0, :, sl])
        om = p_ref[:, c_om + h * DH:c_om + (h + 1) * DH].astype(F32)
        y_ref[:, sl] = (jax.nn.sigmoid(om) * yn).astype(BF16)
        slr = slice(W_M + h * DH, W_M + (h + 1) * DH)
        hr = hf_s[:, slr] + hb_s[:, slr]
        ynr = _rmsnorm(hr, nr_ref[0, :, sl])
        gr = p_ref[:, c_gr + h * DH:c_gr + (h + 1) * DH].astype(F32)
        y_ref[:, slr] = (_silu(gr) * ynr).astype(BF16)

    if not has_state:
        for d in range(2):
            for h in range(H_M):
                k = d * H_M + h
                C_out[0, 0, d, h] = CN_s[k, :, :DH]
                n_out[0, 0, d, h:h + 1, :] = CN_s[k, :, DH:].T[0:1, :]
                S_out[0, 0, d, h] = S_s[k]
        m_out[0, 0] = m_s[...]


def _scan_scratch(T):
    return [
        pltpu.VMEM((2 * H_M, DH, 2 * DH), F32),
        pltpu.VMEM((2 * H_R, DH, DH), F32),
        pltpu.VMEM((1, LANES), F32),
        pltpu.VMEM((T, W_M + W_R), F32),
        pltpu.VMEM((T, W_M + W_R), F32),
        pltpu.VMEM((2 * H_R, CHUNK, CHUNK), F32),
        pltpu.VMEM((2 * H_R, CHUNK, CHUNK), F32),
        pltpu.VMEM((2 * H_R, CHUNK, CHUNK), F32),
        pltpu.VMEM((2 * H_R, 1, LANES), F32),
        pltpu.VMEM((H_M + H_R, T // CHUNK, DH, CHUNK), F32),
    ]


def _scan_ctx(p, gates, dl, nm, nr, layer, prev_states=()):
    T = SEQ
    common = [
        pl.BlockSpec((T, P_COLS), lambda b: (b, 0)),
        pl.BlockSpec((T, GATE_LANES), lambda b: (b, 0)),
        pl.BlockSpec((1, 2 * H_R, LANES), lambda b: (layer, 0, 0)),
        pl.BlockSpec((1, 1, W_M), lambda b: (layer, 0, 0)),
        pl.BlockSpec((1, 1, W_R), lambda b: (layer, 0, 0)),
    ]
    n_al = len(prev_states)
    return pl.pallas_call(
        functools.partial(_scan_kernel, T=T, has_state=False, n_aliased=n_al),
        grid=(BATCH,),
        in_specs=common + [pl.BlockSpec(memory_space=pl.ANY)] * n_al,
        out_specs=[
            pl.BlockSpec((T, D_MODEL), lambda b: (b, 0)),
            pl.BlockSpec((1, 1, 2, H_M, DH, DH), lambda b: (b, layer, 0, 0, 0, 0)),
            pl.BlockSpec((1, 1, 2, H_M, DH), lambda b: (b, layer, 0, 0, 0)),
            pl.BlockSpec((1, 1, 1, LANES), lambda b: (b, layer, 0, 0)),
            pl.BlockSpec((1, 1, 2, H_R, DH, DH), lambda b: (b, layer, 0, 0, 0, 0)),
        ],
        out_shape=[
            jax.ShapeDtypeStruct((NTOK, D_MODEL), BF16),
            jax.ShapeDtypeStruct((BATCH, DEPTH, 2, H_M, DH, DH), F32),
            jax.ShapeDtypeStruct((BATCH, DEPTH, 2, H_M, DH), F32),
            jax.ShapeDtypeStruct((BATCH, DEPTH, 1, LANES), F32),
            jax.ShapeDtypeStruct((BATCH, DEPTH, 2, H_R, DH, DH), F32),
        ],
        input_output_aliases={len(common) + a: 1 + a for a in range(n_al)},
        scratch_shapes=_scan_scratch(T),
        compiler_params=pltpu.CompilerParams(
            dimension_semantics=("arbitrary",), vmem_limit_bytes=VMEM_LIMIT),
        name="scan_ctx",
    )(p, gates, dl, nm, nr, *prev_states)


def _scan_lat(p, gates, dl, nm, nr, C0, n0, m0, S0, y_prev, layer):
    T = DEC_SEQ
    off = N_CTX // T
    in_specs = [
        pl.BlockSpec((T, P_COLS), lambda b: (off + b, 0)),
        pl.BlockSpec((T, GATE_LANES), lambda b: (off + b, 0)),
        pl.BlockSpec((1, 2 * H_R, LANES), lambda b: (layer, 0, 0)),
        pl.BlockSpec((1, 1, W_M), lambda b: (layer, 0, 0)),
        pl.BlockSpec((1, 1, W_R), lambda b: (layer, 0, 0)),
        pl.BlockSpec((1, 1, 2, H_M, DH, DH), lambda b: (b, layer, 0, 0, 0, 0)),
        pl.BlockSpec((1, 1, 2, H_M, DH), lambda b: (b, layer, 0, 0, 0)),
        pl.BlockSpec((1, 1, 1, LANES), lambda b: (b, layer, 0, 0)),
        pl.BlockSpec((1, 1, 2, H_R, DH, DH), lambda b: (b, layer, 0, 0, 0, 0)),
        pl.BlockSpec(memory_space=pl.ANY),
    ]
    return pl.pallas_call(
        functools.partial(_scan_kernel, T=T, has_state=True),
        grid=(DEC_BATCH,),
        in_specs=in_specs,
        out_specs=pl.BlockSpec((T, D_MODEL), lambda b: (off + b, 0)),
        out_shape=jax.ShapeDtypeStruct((NTOK, D_MODEL), BF16),
        input_output_aliases={9: 0},
        scratch_shapes=_scan_scratch(T),
        compiler_params=pltpu.CompilerParams(
            dimension_semantics=("arbitrary",), vmem_limit_bytes=VMEM_LIMIT),
        name="scan_lat",
    )(p, gates, dl, nm, nr, C0, n0, m0, S0, y_prev)


def _top2(logits):
    lane = lax.broadcasted_iota(jnp.int32, logits.shape, 1)
    v1 = jnp.max(logits, -1, keepdims=True)
    i1 = jnp.min(jnp.where(logits == v1, lane, LANES), -1, keepdims=True)
    rest = jnp.where(lane == i1, -jnp.inf, logits)
    v2 = jnp.max(rest, -1, keepdims=True)
    i2 = jnp.min(jnp.where(rest == v2, lane, LANES), -1, keepdims=True)
    e2 = jnp.exp(v2 - v1)
    return i1, i2, 1.0 / (1.0 + e2), e2 / (1.0 + e2)


def _split2(x):
    hi = x.astype(BF16)
    return hi, (x - hi.astype(F32)).astype(BF16)


def _dot_f32x3(a, b):
    a_hi, a_lo = _split2(a)
    b_hi, b_lo = _split2(b)
    return _dot(a_hi, b_hi) + _dot(a_hi, b_lo) + _dot(a_lo, b_hi)


R_E1, R_E2, R_W1, R_W2, R_S1, R_S2 = range(6)


def _out_kernel(*refs, with_router, split_x):
    y_ref = refs[0]
    if split_x:
        xp_ref, xl_ref = refs[1:3]
        x_in = jnp.where(pl.program_id(0) >= N_CTX // TM, xl_ref[...], xp_ref[...])
    else:
        x_in = refs[1][...]
    refs = refs[3:] if split_x else refs[2:]
    if with_router:
        (w_ref, g_ref, mod_ref, wr_ref,
         x1_ref, h2_ref, rinfo_ref, cnt_ref, w_scr, tri_scr, cnt_scr) = refs
    else:
        w_ref, g_ref, mod_ref, x1_ref, h2_ref, w_scr = refs

    @pl.when(pl.program_id(0) == 0)
    def _():
        w_scr[...] = w_ref[0].astype(BF16)
        if with_router:
            r = lax.broadcasted_iota(jnp.int32, (TM, TM), 0)
            c = lax.broadcasted_iota(jnp.int32, (TM, TM), 1)
            tri_scr[...] = (c < r).astype(BF16)
            cnt_scr[...] = jnp.zeros_like(cnt_scr)

    o = jnp.dot(y_ref[...], w_scr[...], preferred_element_type=F32)
    x1 = x_in + mod_ref[0, 0, 2:3, :] * o
    x1_ref[...] = x1
    h2 = _rmsnorm(x1, g_ref[0]) * (1.0 + mod_ref[0, 0, 4:5, :]) + mod_ref[0, 0, 3:4, :]
    if not with_router:
        h2_ref[...] = h2.astype(BF16)
    else:
        h2_ref[...] = h2
        logits = _dot_f32x3(h2, wr_ref[0])
        lane = lax.broadcasted_iota(jnp.int32, logits.shape, 1)
        i1, i2, w1, w2 = _top2(jnp.where(lane < N_EXPERTS, logits, -jnp.inf))
        oh1 = lane == i1
        oh2 = lane == i2
        sel = jnp.where(oh1 | oh2, 1.0, 0.0)
        rank = _dot(tri_scr[...], sel.astype(BF16)) + cnt_scr[...]
        r1 = jnp.sum(jnp.where(oh1, rank, 0.0), -1, keepdims=True)
        r2 = jnp.sum(jnp.where(oh2, rank, 0.0), -1, keepdims=True)
        s1 = i1.astype(F32) * float(REG) + r1
        s2 = i2.astype(F32) * float(REG) + r2
        info = jnp.zeros(logits.shape, F32)
        for col, val in ((R_E1, i1.astype(F32)), (R_E2, i2.astype(F32)), (R_W1, w1), (R_W2, w2),
                         (R_S1, s1), (R_S2, s2)):
            info = jnp.where(lane == col, val, info)
        rinfo_ref[...] = info
        cnt_scr[...] += jnp.sum(sel, 0, keepdims=True)
        cnt_ref[...] = cnt_scr[...]


def _out(y, xs, w_out, g2, mods, layer, w_router_pad=None, router_idx=0):
    with_router = w_router_pad is not None
    split_x = len(xs) == 2
    n_ctx_tiles = N_CTX // TM
    if split_x:
        x_specs = [pl.BlockSpec((TM, D_MODEL), lambda i: (jnp.minimum(i, n_ctx_tiles - 1), 0)),
                   pl.BlockSpec((TM, D_MODEL), lambda i: (jnp.maximum(i - n_ctx_tiles, 0), 0))]
    else:
        x_specs = [pl.BlockSpec((TM, D_MODEL), lambda i: (i, 0))]
    in_specs = [pl.BlockSpec((TM, D_MODEL), lambda i: (i, 0))] + x_specs + [
        pl.BlockSpec((1, D_MODEL, D_MODEL), lambda i: (layer, 0, 0)),
        pl.BlockSpec((1, 1, D_MODEL), lambda i: (layer, 0, 0)),
        pl.BlockSpec((1, 1, 6, D_MODEL), lambda i: (layer, _group_of_tile(i, TM), 0, 0)),
    ]
    out_specs = [
        pl.BlockSpec((TM, D_MODEL), lambda i: (i, 0)),
        pl.BlockSpec((TM, D_MODEL), lambda i: (i, 0)),
    ]
    out_shape = [
        jax.ShapeDtypeStruct((NTOK, D_MODEL), F32),
        jax.ShapeDtypeStruct((NTOK, D_MODEL), F32 if with_router else BF16),
    ]
    args = [y, *xs, w_out, g2, mods]
    scratch = [pltpu.VMEM((D_MODEL, D_MODEL), BF16)]
    if with_router:
        in_specs.append(pl.BlockSpec((1, D_MODEL, LANES), lambda i: (router_idx, 0, 0)))
        out_specs += [pl.BlockSpec((TM, LANES), lambda i: (i, 0)),
                      pl.BlockSpec((1, LANES), lambda i: (0, 0))]
        out_shape += [jax.ShapeDtypeStruct((NTOK, LANES), F32),
                      jax.ShapeDtypeStruct((1, LANES), F32)]
        args.append(w_router_pad)
        scratch += [pltpu.VMEM((TM, TM), BF16), pltpu.VMEM((1, LANES), F32)]
    return pl.pallas_call(
        functools.partial(_out_kernel, with_router=with_router, split_x=split_x),
        grid=(NTOK // TM,),
        in_specs=in_specs,
        out_specs=out_specs,
        out_shape=out_shape,
        scratch_shapes=scratch,
        compiler_params=pltpu.CompilerParams(
            dimension_semantics=("arbitrary",), vmem_limit_bytes=VMEM_LIMIT),
        name="out_router" if with_router else "out",
    )(*args)


N_FC = D_FF // FC
W_SLOTS = 4
W_AHEAD = W_SLOTS - 1


def _swiglu_tile(h, wg_hbm, wu_hbm, wd_hbm, wbuf, wdbuf, sem, acc):
    def copies(f):
        s = f % W_SLOTS
        cols = pl.ds(f * FC, FC)
        return (pltpu.make_async_copy(wg_hbm.at[:, cols], wbuf.at[s, 0], sem.at[s, 0]),
                pltpu.make_async_copy(wu_hbm.at[:, cols], wbuf.at[s, 1], sem.at[s, 1]),
                pltpu.make_async_copy(wd_hbm.at[cols, :], wdbuf.at[s], sem.at[s, 2]))

    def up(f):
        s = f % W_SLOTS
        w_cat = jnp.concatenate([wbuf[s, 0].astype(BF16), wbuf[s, 1].astype(BF16)], axis=1)
        return jnp.dot(h, w_cat, preferred_element_type=F32)

    for f in range(min(W_AHEAD, N_FC)):
        for c in copies(f):
            c.start()
    for c in copies(0):
        c.wait()
    ab = up(0)
    for f in range(N_FC):
        if f + 1 < N_FC:
            for c in copies(f + 1):
                c.wait()
        if f + W_AHEAD < N_FC:
            for c in copies(f + W_AHEAD):
                c.start()
        ab_next = up(f + 1) if f + 1 < N_FC else None
        t = (_silu(ab[:, :FC]) * ab[:, FC:]).astype(BF16)
        contrib = jnp.dot(t, wdbuf[f % W_SLOTS].astype(BF16), preferred_element_type=F32)
        if f == 0:
            acc[...] = contrib
        else:
            acc[...] += contrib
        ab = ab_next


def _ffn_weight_scratch():
    return [
        pltpu.VMEM((W_SLOTS, 2, D_MODEL, FC), F32),
        pltpu.VMEM((W_SLOTS, FC, D_MODEL), F32),
        pltpu.SemaphoreType.DMA((W_SLOTS, 3)),
    ]


def _ffn_kernel(h_ref, res_ref, wg_ref, wu_ref, wd_ref, mod_ref, o_ref, acc, wbuf, wdbuf, sem, *, w_idx):
    _swiglu_tile(h_ref[...], wg_ref.at[w_idx], wu_ref.at[w_idx], wd_ref.at[w_idx], wbuf, wdbuf, sem, acc)
    o_ref[...] = res_ref[...] + mod_ref[0, 0, 5:6, :] * acc[...]


def _ffn(h2, res, wg, wu, wd, mods, layer, w_idx):
    return pl.pallas_call(
        functools.partial(_ffn_kernel, w_idx=w_idx),
        grid=(NTOK // TM,),
        in_specs=[
            pl.BlockSpec((TM, D_MODEL), lambda i: (i, 0)),
            pl.BlockSpec((TM, D_MODEL), lambda i: (i, 0)),
            pl.BlockSpec(memory_space=pl.ANY),
            pl.BlockSpec(memory_space=pl.ANY),
            pl.BlockSpec(memory_space=pl.ANY),
            pl.BlockSpec((1, 1, 6, D_MODEL), lambda i: (layer, _group_of_tile(i, TM), 0, 0)),
        ],
        out_specs=pl.BlockSpec((TM, D_MODEL), lambda i: (i, 0)),
        out_shape=jax.ShapeDtypeStruct((NTOK, D_MODEL), F32),
        scratch_shapes=[pltpu.VMEM((TM, D_MODEL), F32)] + _ffn_weight_scratch(),
        compiler_params=pltpu.CompilerParams(
            dimension_semantics=("arbitrary",), vmem_limit_bytes=VMEM_LIMIT),
        name="ffn",
    )(h2, res, wg, wu, wd, mods)


def _tile_plan(counts):
    nt = (counts + TR - 1) // TR
    cum = jnp.cumsum(nt)
    total = cum[-1]
    t = jnp.arange(MAX_TILES, dtype=jnp.int32)
    tt = jnp.minimum(t, total - 1)
    e = jnp.sum((cum[None, :] <= tt[:, None]).astype(jnp.int32), axis=1)
    k = tt - (cum - nt)[e]
    n = jnp.where(t < total, jnp.clip(counts[e] - k * TR, 0, TR), 0)
    return e.astype(jnp.int32), (e * REG_TILES + k).astype(jnp.int32), n.astype(jnp.int32)


def _row_copy(src, src_row, dst, dst_row, sem):
    return pltpu.make_async_copy(src.at[pl.ds(src_row, 1)], dst.at[pl.ds(dst_row, 1)], sem)


def _dispatch_kernel(slot_ref, h_ref, xs_ref, sem):
    base = pl.program_id(0) * (TOP_K * TD)

    def issue(r, carry):
        for k in range(TOP_K):
            _row_copy(h_ref, r, xs_ref, slot_ref[base + TOP_K * r + k], sem).start()
        return carry

    lax.fori_loop(0, TD, issue, 0, unroll=8)
    for k in range(TOP_K):
        pltpu.make_async_copy(h_ref, xs_ref.at[pl.ds(0, TD)], sem).wait()


def _dispatch(slots, h2f):
    return pl.pallas_call(
        _dispatch_kernel,
        grid_spec=pltpu.PrefetchScalarGridSpec(
            num_scalar_prefetch=1,
            grid=(NTOK // TD,),
            in_specs=[pl.BlockSpec((TD, D_MODEL), lambda i, s: (i, 0))],
            out_specs=pl.BlockSpec(memory_space=pl.ANY),
            scratch_shapes=[pltpu.SemaphoreType.DMA],
        ),
        out_shape=jax.ShapeDtypeStruct((N_EXPERTS * REG, D_MODEL), F32),
        compiler_params=pltpu.CompilerParams(
            dimension_semantics=("arbitrary",), vmem_limit_bytes=VMEM_LIMIT),
        name="moe_dispatch",
    )(slots, h2f)


def _gffn_kernel(te_ref, tb_ref, tn_ref, x_ref, wg_ref, wu_ref, wd_ref, o_ref, acc, wbuf, wdbuf, sem, *, w_base):
    del tb_ref
    t = pl.program_id(0)
    n = tn_ref[t]

    @pl.when(n > 0)
    def _():
        e = w_base + te_ref[t]
        row = lax.broadcasted_iota(jnp.int32, (TR, D_MODEL), 0)
        h = jnp.where(row < n, x_ref[...], 0.0).astype(BF16)
        _swiglu_tile(h, wg_ref.at[e], wu_ref.at[e], wd_ref.at[e], wbuf, wdbuf, sem, acc)
        o_ref[...] = acc[...]


def _gffn(tile_e, tile_blk, tile_n, xs, wg, wu, wd, w_base):
    return pl.pallas_call(
        functools.partial(_gffn_kernel, w_base=w_base),
        grid_spec=pltpu.PrefetchScalarGridSpec(
            num_scalar_prefetch=3,
            grid=(MAX_TILES,),
            in_specs=[
                pl.BlockSpec((TR, D_MODEL), lambda t, te, tb, tn: (tb[t], 0)),
                pl.BlockSpec(memory_space=pl.ANY),
                pl.BlockSpec(memory_space=pl.ANY),
                pl.BlockSpec(memory_space=pl.ANY),
            ],
            out_specs=pl.BlockSpec((TR, D_MODEL), lambda t, te, tb, tn: (tb[t], 0)),
            scratch_shapes=[pltpu.VMEM((TR, D_MODEL), F32)] + _ffn_weight_scratch(),
        ),
        out_shape=jax.ShapeDtypeStruct((N_EXPERTS * REG, D_MODEL), F32),
        compiler_params=pltpu.CompilerParams(
            dimension_semantics=("arbitrary",), vmem_limit_bytes=VMEM_LIMIT),
        name="moe_ffn",
    )(tile_e, tile_blk, tile_n, xs, wg, wu, wd)


def _combine_kernel(slot_ref, x1_ref, rinfo_ref, mod_ref, ys_ref, o_ref, buf, sem):
    base = pl.program_id(0) * (TOP_K * TD)

    def issue(r, carry):
        for k in range(TOP_K):
            _row_copy(ys_ref, slot_ref[base + TOP_K * r + k], buf.at[k], r, sem).start()
        return carry

    lax.fori_loop(0, TD, issue, 0, unroll=8)
    for k in range(TOP_K):
        pltpu.make_async_copy(ys_ref.at[pl.ds(0, TD)], buf.at[k], sem).wait()
    y = rinfo_ref[:, R_W1:R_W1 + 1] * buf[0] + rinfo_ref[:, R_W2:R_W2 + 1] * buf[1]
    o_ref[...] = x1_ref[...] + mod_ref[0, 0, 5:6, :] * y


def _combine(slots, x1, rinfo, mods, ys, layer):
    return pl.pallas_call(
        _combine_kernel,
        grid_spec=pltpu.PrefetchScalarGridSpec(
            num_scalar_prefetch=1,
            grid=(NTOK // TD,),
            in_specs=[
                pl.BlockSpec((TD, D_MODEL), lambda i, s: (i, 0)),
                pl.BlockSpec((TD, LANES), lambda i, s: (i, 0)),
                pl.BlockSpec((1, 1, 6, D_MODEL), lambda i, s: (layer, _group_of_tile(i, TD), 0, 0)),
                pl.BlockSpec(memory_space=pl.ANY),
            ],
            out_specs=pl.BlockSpec((TD, D_MODEL), lambda i, s: (i, 0)),
            scratch_shapes=[pltpu.VMEM((TOP_K, TD, D_MODEL), F32), pltpu.SemaphoreType.DMA],
        ),
        out_shape=jax.ShapeDtypeStruct((NTOK, D_MODEL), F32),
        compiler_params=pltpu.CompilerParams(
            dimension_semantics=("arbitrary",), vmem_limit_bytes=VMEM_LIMIT),
        name="moe_combine",
    )(slots, x1, rinfo, mods, ys)


def _final_kernel(x_ref, g_ref, o_ref):
    o_ref[...] = _rmsnorm(x_ref[...], g_ref[...])


def _final(x, g, row_off, rows):
    off = row_off // TM
    return pl.pallas_call(
        _final_kernel,
        grid=(rows // TM,),
        in_specs=[
            pl.BlockSpec((TM, D_MODEL), lambda i: (off + i, 0)),
            pl.BlockSpec((1, D_MODEL), lambda i: (0, 0)),
        ],
        out_specs=pl.BlockSpec((TM, D_MODEL), lambda i: (i, 0)),
        out_shape=jax.ShapeDtypeStruct((rows, D_MODEL), F32),
        compiler_params=pltpu.CompilerParams(
            dimension_semantics=("arbitrary",), vmem_limit_bytes=VMEM_LIMIT),
        name="final_norm",
    )(x, g)


def kernel(x_prompt, x_sample, state_mlstm_C, state_mlstm_n, state_mlstm_m, state_ret_S, c, c_ctx,
           norm1_g, norm2_g, norm_f_g, w_ada, b_ada, w_in, b_gates, ret_decay_logit,
           mlstm_norm_g, ret_norm_g, w_out, ffn_w_gate, ffn_w_up, ffn_w_down,
           moe_w_router, moe_w_gate, moe_w_up, moe_w_down):
    xs_in = (x_prompt.reshape(N_CTX, D_MODEL), x_sample.reshape(N_LAT, D_MODEL))
    cvec = jnp.concatenate(
        [c_ctx[None, :], c, jnp.zeros((N_GROUPS - 1 - DEC_BATCH, D_MODEL), F32)], 0)
    mods = _ada(cvec, w_ada, b_ada).reshape(DEPTH, N_GROUPS, 6, D_MODEL)

    n_m = 4 * W_M
    w_ret = w_in[:, :, n_m + N_GATES:]
    n_if = N_GATES // 2
    lane_pad = ((0, 0), (0, 0), (0, LANES - n_if))
    wg = jnp.concatenate([jnp.pad(w_in[:, :, n_m:n_m + n_if], lane_pad),
                          jnp.pad(w_in[:, :, n_m + n_if:n_m + N_GATES], lane_pad)], -1)
    bg = jnp.concatenate([jnp.pad(b_gates[:, None, :n_if], lane_pad),
                          jnp.pad(b_gates[:, None, n_if:], lane_pad)], -1)
    cos_np, sin_np = _rope_tables()
    cos_t, sin_t = jnp.asarray(cos_np), jnp.asarray(sin_np)
    dl = jnp.broadcast_to(ret_decay_logit.reshape(DEPTH, 2 * H_R, 1), (DEPTH, 2 * H_R, LANES))
    m0 = jnp.pad(state_mlstm_m.reshape(DEC_BATCH, DEPTH, 1, 2 * H_M),
                 ((0, 0), (0, 0), (0, 0), (0, LANES - 2 * H_M)))
    g1 = norm1_g.reshape(DEPTH, 1, D_MODEL)
    g2 = norm2_g.reshape(DEPTH, 1, D_MODEL)
    nm = mlstm_norm_g.reshape(DEPTH, 1, W_M)
    nr = ret_norm_g.reshape(DEPTH, 1, W_R)
    n_moe = moe_w_router.shape[0]
    wr_pad = jnp.pad(moe_w_router, ((0, 0), (0, 0), (0, LANES - N_EXPERTS)))
    moe_g = moe_w_gate.reshape(n_moe * N_EXPERTS, D_MODEL, D_FF)
    moe_u = moe_w_up.reshape(n_moe * N_EXPERTS, D_MODEL, D_FF)
    moe_d = moe_w_down.reshape(n_moe * N_EXPERTS, D_FF, D_MODEL)

    states = ()
    xs = xs_in
    for l in range(DEPTH):
        jl = l // 2
        p, gates = _proj(xs, g1, mods, w_in, w_ret, wg, bg, cos_t, sin_t, l)
        y, *states = _scan_ctx(p, gates, dl, nm, nr, l, states)
        y = _scan_lat(p, gates, dl, nm, nr, state_mlstm_C, state_mlstm_n, m0, state_ret_S, y, l)
        if l % 2 == 0:
            x1, h2 = _out(y, xs, w_out, g2, mods, l)
            x = _ffn(h2, x1, ffn_w_gate, ffn_w_up, ffn_w_down, mods, l, jl)
        else:
            x1, h2f, rinfo, cnt = _out(y, xs, w_out, g2, mods, l, wr_pad, jl)
            slots = rinfo[:, R_S1:R_S2 + 1].astype(jnp.int32).reshape(TOP_K * NTOK)
            tile_e, tile_blk, tile_n = _tile_plan(cnt[0, :N_EXPERTS].astype(jnp.int32))
            xd = _dispatch(slots, h2f)
            yd = _gffn(tile_e, tile_blk, tile_n, xd, moe_g, moe_u, moe_d, jl * N_EXPERTS)
            x = _combine(slots, x1, rinfo, mods, yd, l)
        xs = (x,)

    y_prompt = _final(x, norm_f_g.reshape(1, D_MODEL), 0, N_CTX).reshape(BATCH, SEQ, D_MODEL)
    y_sample = _final(x, norm_f_g.reshape(1, D_MODEL), N_CTX, N_LAT).reshape(DEC_BATCH, DEC_SEQ, D_MODEL)
    new_C, new_n, new_m, new_S = states
    return (y_prompt, y_sample, new_C, new_n,
            new_m[:, :, 0, :2 * H_M].reshape(BATCH, DEPTH, 2, H_M), new_S)
```

```python
import functools

import numpy as np
import jax
import jax.numpy as jnp
from jax import lax
from jax.experimental import pallas as pl
from jax.experimental.pallas import tpu as pltpu

D_MODEL = 1024
BATCH = 32
SEQ = 256
DEPTH = 2
DEC_BATCH = 2
DEC_SEQ = 1024
GRID_W = 64
H_M = 4
DH = 128
H_R = 4
W_M = H_M * DH
W_R = H_R * DH
N_GATES = 4 * H_M
CHUNK = 128
D_FF = 2816
N_EXPERTS = 8
ROPE_BASE = 10000.0
EPS = 1e-6

N_CTX = BATCH * SEQ
N_LAT = DEC_BATCH * DEC_SEQ
NTOK = N_CTX + N_LAT
N_GROUPS = 8
K_SCALE = DH ** -0.5
P_COLS = 4 * W_M + 4 * W_R
LANES = 128
GATE_LANES = 2 * LANES
VMEM_LIMIT = 56 * 1024 * 1024

F32 = jnp.float32
BF16 = jnp.bfloat16
HIGHEST = lax.Precision.HIGHEST

TM = 1024
TN = 1024
FC = 256
TOP_K = 2
TR = 896
REG_TILES = -(-NTOK // TR)
REG = REG_TILES * TR
MAX_TILES = -(-TOP_K * NTOK // TR) + N_EXPERTS
TD = 512


def _group_of_tile(i, tm):
    return jnp.maximum(i * tm // DEC_SEQ - (N_CTX // DEC_SEQ - 1), 0)


def _silu(x):
    return x * jax.nn.sigmoid(x)


def _log_sigmoid(x):
    return jnp.minimum(x, 0.0) - jnp.log(1.0 + jnp.exp(-jnp.abs(x)))


def _rmsnorm(x, g):
    return x * lax.rsqrt(jnp.mean(x * x, -1, keepdims=True) + EPS) * g


def _ada_kernel(cv_ref, w_ref, b_ref, o_ref):
    s = _silu(cv_ref[...]).astype(BF16)
    o_ref[0] = jnp.dot(s, w_ref[0].astype(BF16), preferred_element_type=F32) + b_ref[0]


def _ada(cvec, w_ada, b_ada):
    tn = 1536
    n = 6 * D_MODEL
    return pl.pallas_call(
        _ada_kernel,
        grid=(DEPTH, n // tn),
        in_specs=[
            pl.BlockSpec((N_GROUPS, D_MODEL), lambda l, j: (0, 0)),
            pl.BlockSpec((1, D_MODEL, tn), lambda l, j: (l, 0, j)),
            pl.BlockSpec((1, 1, tn), lambda l, j: (l, 0, j)),
        ],
        out_specs=pl.BlockSpec((1, N_GROUPS, tn), lambda l, j: (l, 0, j)),
        out_shape=jax.ShapeDtypeStruct((DEPTH, N_GROUPS, n), F32),
        compiler_params=pltpu.CompilerParams(
            dimension_semantics=("arbitrary", "arbitrary"), vmem_limit_bytes=VMEM_LIMIT),
        name="ada",
    )(cvec, w_ada, b_ada.reshape(DEPTH, 1, n))


def _rope_tables():
    half = DH // 4
    freqs = ROPE_BASE ** (-np.arange(half, dtype=np.float64) / half)
    t = np.arange(DEC_SEQ)
    pos = np.stack([t // GRID_W, t % GRID_W], 1).astype(np.float64)
    d = np.arange(DH)
    ang = pos[:, d // (DH // 2)] * freqs[d % half][None, :]
    sign = np.where((d % (DH // 2)) < half, -1.0, 1.0)[None, :]
    return np.cos(ang).astype(np.float32), (sign * np.sin(ang)).astype(np.float32)


def _rope(a, cos, sin):
    lane = lax.broadcasted_iota(jnp.int32, a.shape, 1)
    first = (lane % (DH // 2)) < (DH // 4)
    partner = jnp.where(first, pltpu.roll(a, DH - DH // 4, 1), pltpu.roll(a, DH // 4, 1))
    return a * cos + partner * sin


def _proj_kernel(*refs, n_ctx_tiles, split_x):
    if split_x:
        (xp_ref, xl_ref, g_ref, mod_ref, wa_ref, wb_ref, wg_ref, bg_ref, cos_ref, sin_ref,
         p_ref, gate_ref, h_scr, w_res) = refs
    else:
        (x_ref, g_ref, mod_ref, wa_ref, wb_ref, wg_ref, bg_ref, cos_ref, sin_ref,
         p_ref, gate_ref, h_scr, w_res) = refs
    p_ref = p_ref.at[0]
    i = pl.program_id(0)
    j = pl.program_id(1)
    is_lat = i >= n_ctx_tiles
    half = TN // 2

    def prologue(x):
        h = _rmsnorm(x, g_ref[0]) * (1.0 + mod_ref[0, 0, 1:2, :]) + mod_ref[0, 0, 0:1, :]
        h_scr[...] = h.astype(BF16)
        gate_ref[...] = _dot_f32x3(h, wg_ref[0]) + bg_ref[0]

    @pl.when(j == 0)
    def _():
        if split_x:
            pl.when(jnp.logical_not(is_lat))(lambda: prologue(xp_ref[...]))
            pl.when(is_lat)(lambda: prologue(xl_ref[...]))
        else:
            prologue(x_ref[...])

    def matmul(w_ref, jj):
        @pl.when(i == 0)
        def _():
            w_res[jj] = w_ref[0].astype(BF16)

        return jnp.dot(h_scr[...], w_res[jj], preferred_element_type=F32)

    @pl.when(j == 0)
    def _():
        acc = matmul(wa_ref, 0)
        p_ref[:, :half] = acc[:, :half].astype(BF16)
        p_ref[:, half:] = (acc[:, half:] * K_SCALE).astype(BF16)

    @pl.when(j == 1)
    def _():
        p_ref[...] = matmul(wa_ref, 1).astype(BF16)

    @pl.when(j == 2)
    def _():
        acc = matmul(wb_ref, 2)

        @pl.when(is_lat)
        def _():
            cos = cos_ref[...]
            sin = sin_ref[...]
            for hd in range(TN // DH):
                sl = slice(hd * DH, (hd + 1) * DH)
                r = _rope(acc[:, sl], cos, sin)
                p_ref[:, sl] = (r * K_SCALE if hd * DH >= half else r).astype(BF16)

        @pl.when(jnp.logical_not(is_lat))
        def _():
            p_ref[:, :half] = acc[:, :half].astype(BF16)
            p_ref[:, half:] = (acc[:, half:] * K_SCALE).astype(BF16)

    @pl.when(j == 3)
    def _():
        p_ref[...] = matmul(wb_ref, 3).astype(BF16)


def _proj(xs, g1, mods, w_in, w_ret, wg, bg, cos_t, sin_t, layer):
    n_ctx_tiles = N_CTX // TM
    tiles_per_seq = DEC_SEQ // TM
    split_x = len(xs) == 2
    if split_x:
        x_specs = [pl.BlockSpec((TM, D_MODEL), lambda i, j: (jnp.minimum(i, n_ctx_tiles - 1), 0)),
                   pl.BlockSpec((TM, D_MODEL), lambda i, j: (jnp.maximum(i - n_ctx_tiles, 0), 0))]
    else:
        x_specs = [pl.BlockSpec((TM, D_MODEL), lambda i, j: (i, 0))]
    n_a = 4 * W_M // TN
    return pl.pallas_call(
        functools.partial(_proj_kernel, n_ctx_tiles=n_ctx_tiles, split_x=split_x),
        grid=(NTOK // TM, P_COLS // TN),
        in_specs=x_specs + [
            pl.BlockSpec((1, 1, D_MODEL), lambda i, j: (layer, 0, 0)),
            pl.BlockSpec((1, 1, 6, D_MODEL), lambda i, j: (layer, _group_of_tile(i, TM), 0, 0)),
            pl.BlockSpec((1, D_MODEL, TN),
                         lambda i, j: (layer, 0, jnp.where(i == 0, jnp.minimum(j, n_a - 1), n_a - 1)),
                         pipeline_mode=pl.Buffered(1)),
            pl.BlockSpec((1, D_MODEL, TN),
                         lambda i, j: (layer, 0, jnp.where(i == 0, jnp.maximum(j - n_a, 0), n_a - 1)),
                         pipeline_mode=pl.Buffered(1)),
            pl.BlockSpec((1, D_MODEL, GATE_LANES), lambda i, j: (layer, 0, 0)),
            pl.BlockSpec((1, 1, GATE_LANES), lambda i, j: (layer, 0, 0)),
            pl.BlockSpec((TM, DH), lambda i, j: (i % tiles_per_seq, 0)),
            pl.BlockSpec((TM, DH), lambda i, j: (i % tiles_per_seq, 0)),
        ],
        out_specs=[
            pl.BlockSpec((1, TM, TN), lambda i, j: (j, i, 0)),
            pl.BlockSpec((TM, GATE_LANES), lambda i, j: (i, 0)),
        ],
        out_shape=[
            jax.ShapeDtypeStruct((P_COLS // TN, NTOK, TN), BF16),
            jax.ShapeDtypeStruct((NTOK, GATE_LANES), F32),
        ],
        scratch_shapes=[pltpu.VMEM((TM, D_MODEL), BF16),
                        pltpu.VMEM((P_COLS // TN, D_MODEL, TN), BF16)],
        compiler_params=pltpu.CompilerParams(
            dimension_semantics=("arbitrary", "arbitrary"), vmem_limit_bytes=VMEM_LIMIT),
        name="proj",
    )(*xs, g1, mods, w_in, w_ret, wg, bg, cos_t, sin_t)


def _split3(x):
    hi = x.astype(BF16)
    r1 = x - hi.astype(F32)
    mid = r1.astype(BF16)
    lo = (r1 - mid.astype(F32)).astype(BF16)
    return hi, mid, lo


def _dot(a, b):
    return jnp.dot(a, b, preferred_element_type=F32)


def _dot_nt(a, b):
    return lax.dot_general(a, b, (((1,), (1,)), ((), ())), preferred_element_type=F32)


def _tri_dot_left(tri, x):
    hi, mid, lo = _split3(x)
    return _dot(tri, hi) + _dot(tri, mid) + _dot(tri, lo)


def _tri_dot_right(x, tri):
    hi, mid, lo = _split3(x)
    return _dot(hi, tri) + _dot(mid, tri) + _dot(lo, tri)


def _run_max(x, reverse):
    n_tiles = x.shape[0] // 8
    sub = lax.broadcasted_iota(jnp.int32, (8, LANES), 0)
    out = [None] * n_tiles
    carry = None
    for t in (range(n_tiles - 1, -1, -1) if reverse else range(n_tiles)):
        v = x[8 * t:8 * t + 8, :]
        for s in (1, 2, 4):
            if reverse:
                v = jnp.maximum(v, jnp.where(sub < 8 - s, pltpu.roll(v, 8 - s, 0), -jnp.inf))
            else:
                v = jnp.maximum(v, jnp.where(sub >= s, pltpu.roll(v, s, 0), -jnp.inf))
        if carry is not None:
            v = jnp.maximum(v, carry)
        carry = jnp.broadcast_to(v[0:1, :] if reverse else v[7:8, :], (8, LANES))
        out[t] = v
    return jnp.concatenate(out, axis=0)


def _scan_kernel(*refs, T, has_state, n_aliased=0):
    if has_state:
        (p_ref, g_ref, dl_ref, nm_ref, nr_ref, C0_ref, n0_ref, m0_ref, S0_ref, _yprev_ref,
         y_ref, CN_s, S_s, m_s, hf_s, hb_s, dm_s, dq_s, dk_s, dL_s, kT_s) = refs
    else:
        p_ref, g_ref, dl_ref, nm_ref, nr_ref = refs[:5]
        (y_ref, C_out, n_out, m_out, S_out,
         CN_s, S_s, m_s, hf_s, hb_s, dm_s, dq_s, dk_s, dL_s, kT_s) = refs[5 + n_aliased:]
    L = CHUNK
    n_chunks = T // L
    row_i = lax.broadcasted_iota(jnp.int32, (L, L), 0)
    col_j = lax.broadcasted_iota(jnp.int32, (L, L), 1)
    lower = col_j <= row_i
    upper = col_j >= row_i
    tril = lower.astype(BF16)
    triu = upper.astype(BF16)
    ones = jnp.ones((L, DH), BF16)
    c_km = W_M
    c_vm = 2 * W_M
    c_om = 3 * W_M
    c_qr = 4 * W_M
    c_kr = c_qr + W_R
    c_vr = c_qr + 2 * W_R
    c_gr = c_qr + 3 * W_R

    def pcols(rows, col):
        return p_ref[col // TN, rows, col % TN:col % TN + DH]

    for d in range(2):
        for h in range(H_M):
            k = d * H_M + h
            if has_state:
                CN_s[k, :, :DH] = C0_ref[0, 0, d, h]
                CN_s[k, :, DH:] = jnp.broadcast_to(n0_ref[0, 0, d, h:h + 1, :], (DH, DH)).T
                S_s[k] = S0_ref[0, 0, d, h]
            else:
                CN_s[k] = jnp.zeros((DH, 2 * DH), F32)
                S_s[k] = jnp.zeros((DH, DH), F32)
    m_s[...] = m0_ref[0, 0] if has_state else jnp.zeros((1, LANES), F32)

    @pl.when(pl.program_id(0) == 0)
    def _():
        pos_i = row_i.astype(F32)
        pos_j = col_j.astype(F32)
        for d in range(2):
            for h in range(H_R):
                k = d * H_R + h
                lg_row = _log_sigmoid(dl_ref[0, k:k + 1, :])
                lg = jnp.broadcast_to(lg_row, (L, L))
                rel = (row_i - col_j if d == 0 else col_j - row_i).astype(F32)
                dm_s[k] = jnp.where(rel >= 0, jnp.exp(lg * jnp.maximum(rel, 0.0)), 0.0)
                dq_s[k] = jnp.exp(lg * (pos_i + 1.0 if d == 0 else L - pos_i))
                dk_s[k] = jnp.exp(lg * (L - 1.0 - pos_j if d == 0 else pos_j))
                dL_s[k] = jnp.exp(lg_row * float(L))

    def transpose_keys(c, carry):
        r0 = pl.multiple_of(c * L, L)
        for h in range(H_M):
            kT_s[h, c] = pcols(pl.ds(r0, L), c_km + h * DH).astype(F32).T
            kT_s[H_M + h, c] = pcols(pl.ds(r0, L), c_kr + h * DH).astype(F32).T
        return carry

    lax.fori_loop(0, n_chunks, transpose_keys, 0)

    def chunk_step(c, carry):
        m_prev = m_s[...]
        m_new = []
        prep = []
        for d in range(2):
            ci = c if d == 0 else n_chunks - 1 - c
            r0 = pl.multiple_of(ci * L, L)
            mask = lower if d == 0 else upper
            e_row = L - 1 if d == 0 else 0
            FL = _log_sigmoid(g_ref[pl.ds(r0, L), LANES:2 * LANES])
            Bc = _tri_dot_left(tril if d == 0 else triu, FL)
            Zc = g_ref[pl.ds(r0, L), 0:LANES] - Bc
            M = jnp.maximum(_run_max(Zc, reverse=(d == 1)), m_prev)
            m_row = Bc + M
            M_end = M[e_row:e_row + 1, :]
            m_new.append(Bc[e_row:e_row + 1, :] + M_end)
            decay = jnp.exp(m_prev - M_end)
            prep.append(dict(ci=ci, r0=r0, mask=mask, M=M, m_row=m_row, decay=decay,
                             ZT=Zc.T,
                             WT=jnp.exp(Zc - M_end).T))
        pairs = [(d, h) for d in range(2) for h in range(H_M)]

        def rows(d, col):
            return pcols(pl.ds(prep[d]["r0"], L), col)

        qk, qkr = {}, {}
        for d, h in pairs:
            qk[d, h] = _dot_nt(rows(d, h * DH), rows(d, c_km + h * DH))
            qkr[d, h] = _dot_nt(rows(d, c_qr + h * DH), rows(d, c_kr + h * DH))
        upd, updr = {}, {}
        for d, h in pairs:
            k = d * H_M + h
            ci = prep[d]["ci"]
            vo = jnp.concatenate([rows(d, c_vm + h * DH), ones], axis=1)
            wkT = (kT_s[h, ci] * jnp.broadcast_to(prep[d]["WT"][k:k + 1, :], (DH, L))).astype(BF16)
            upd[d, h] = _dot(wkT, vo)
            kdT = (kT_s[H_M + h, ci] * dk_s[k]).astype(BF16)
            updr[d, h] = _dot(kdT, rows(d, c_vr + h * DH))
        for d, h in pairs:
            k = d * H_M + h
            r0 = prep[d]["r0"]
            h_dst = hf_s if d == 0 else hb_s
            q = rows(d, h * DH)
            M_col = jnp.broadcast_to(prep[d]["M"][:, k:k + 1], (L, L))
            z_row = jnp.broadcast_to(prep[d]["ZT"][k:k + 1, :], (L, L))
            D = jnp.where(prep[d]["mask"], jnp.exp(z_row - M_col), 0.0)
            s = (qk[d, h] * D).astype(BF16)
            w_inter = jnp.exp(jnp.broadcast_to(m_prev[:, k:k + 1], (L, L)) - M_col)
            wq = (w_inter * q.astype(F32)).astype(BF16)
            vo = jnp.concatenate([rows(d, c_vm + h * DH), ones], axis=1)
            CN = CN_s[k]
            res = _dot(jnp.concatenate([s, wq], axis=1),
                       jnp.concatenate([vo, CN.astype(BF16)], axis=0))
            floor = jnp.exp(-jnp.broadcast_to(prep[d]["m_row"][:, k:k + 1], (L, L)))
            h_dst[pl.ds(r0, L), h * DH:(h + 1) * DH] = res[:, :DH] / jnp.maximum(jnp.abs(res[:, DH:]), floor)
            CN_s[k] = jnp.broadcast_to(prep[d]["decay"][:, k:k + 1], (DH, 2 * DH)) * CN + upd[d, h]
            qr = rows(d, c_qr + h * DH)
            S = S_s[k]
            sr = (qkr[d, h] * dm_s[k]).astype(BF16)
            qd = (qr.astype(F32) * dq_s[k]).astype(BF16)
            h_dst[pl.ds(r0, L), W_M + h * DH:W_M + (h + 1) * DH] = _dot(
                jnp.concatenate([sr, qd], axis=1),
                jnp.concatenate([rows(d, c_vr + h * DH), S.astype(BF16)], axis=0))
            S_s[k] = dL_s[k] * S + updr[d, h]
        lane = lax.broadcasted_iota(jnp.int32, (1, LANES), 1)
        m_s[...] = jnp.where(lane < H_M, m_new[0], m_new[1])
        return carry

    lax.fori_loop(0, n_chunks, chunk_step, 0)

    for h in range(H_M):
        sl = slice(h * DH, (h + 1) * DH)
        hs = hf_s[:, sl] + hb_s[:, sl]
        yn = _rmsnorm(hs, nm_ref[0, :, sl])
        om = pcols(slice(None), c_om + h * DH).astype(F32)
        y_ref[:, sl] = (jax.nn.sigmoid(om) * yn).astype(BF16)
        slr = slice(W_M + h * DH, W_M + (h + 1) * DH)
        hr = hf_s[:, slr] + hb_s[:, slr]
        ynr = _rmsnorm(hr, nr_ref[0, :, sl])
        gr = pcols(slice(None), c_gr + h * DH).astype(F32)
        y_ref[:, slr] = (_silu(gr) * ynr).astype(BF16)

    if not has_state:
        for d in range(2):
            for h in range(H_M):
                k = d * H_M + h
                C_out[0, 0, d, h] = CN_s[k, :, :DH]
                n_out[0, 0, d, h:h + 1, :] = CN_s[k, :, DH:].T[0:1, :]
                S_out[0, 0, d, h] = S_s[k]
        m_out[0, 0] = m_s[...]


def _scan_scratch(T):
    return [
        pltpu.VMEM((2 * H_M, DH, 2 * DH), F32),
        pltpu.VMEM((2 * H_R, DH, DH), F32),
        pltpu.VMEM((1, LANES), F32),
        pltpu.VMEM((T, W_M + W_R), F32),
        pltpu.VMEM((T, W_M + W_R), F32),
        pltpu.VMEM((2 * H_R, CHUNK, CHUNK), F32),
        pltpu.VMEM((2 * H_R, CHUNK, CHUNK), F32),
        pltpu.VMEM((2 * H_R, CHUNK, CHUNK), F32),
        pltpu.VMEM((2 * H_R, 1, LANES), F32),
        pltpu.VMEM((H_M + H_R, T // CHUNK, DH, CHUNK), F32),
    ]


def _scan_ctx(p, gates, dl, nm, nr, layer, prev_states=()):
    T = SEQ
    common = [
        pl.BlockSpec((P_COLS // TN, T, TN), lambda b: (0, b, 0)),
        pl.BlockSpec((T, GATE_LANES), lambda b: (b, 0)),
        pl.BlockSpec((1, 2 * H_R, LANES), lambda b: (layer, 0, 0)),
        pl.BlockSpec((1, 1, W_M), lambda b: (layer, 0, 0)),
        pl.BlockSpec((1, 1, W_R), lambda b: (layer, 0, 0)),
    ]
    n_al = len(prev_states)
    return pl.pallas_call(
        functools.partial(_scan_kernel, T=T, has_state=False, n_aliased=n_al),
        grid=(BATCH,),
        in_specs=common + [pl.BlockSpec(memory_space=pl.ANY)] * n_al,
        out_specs=[
            pl.BlockSpec((T, D_MODEL), lambda b: (b, 0)),
            pl.BlockSpec((1, 1, 2, H_M, DH, DH), lambda b: (b, layer, 0, 0, 0, 0)),
            pl.BlockSpec((1, 1, 2, H_M, DH), lambda b: (b, layer, 0, 0, 0)),
            pl.BlockSpec((1, 1, 1, LANES), lambda b: (b, layer, 0, 0)),
            pl.BlockSpec((1, 1, 2, H_R, DH, DH), lambda b: (b, layer, 0, 0, 0, 0)),
        ],
        out_shape=[
            jax.ShapeDtypeStruct((NTOK, D_MODEL), BF16),
            jax.ShapeDtypeStruct((BATCH, DEPTH, 2, H_M, DH, DH), F32),
            jax.ShapeDtypeStruct((BATCH, DEPTH, 2, H_M, DH), F32),
            jax.ShapeDtypeStruct((BATCH, DEPTH, 1, LANES), F32),
            jax.ShapeDtypeStruct((BATCH, DEPTH, 2, H_R, DH, DH), F32),
        ],
        input_output_aliases={len(common) + a: 1 + a for a in range(n_al)},
        scratch_shapes=_scan_scratch(T),
        compiler_params=pltpu.CompilerParams(
            dimension_semantics=("arbitrary",), vmem_limit_bytes=VMEM_LIMIT),
        name="scan_ctx",
    )(p, gates, dl, nm, nr, *prev_states)


def _scan_lat(p, gates, dl, nm, nr, C0, n0, m0, S0, y_prev, layer):
    T = DEC_SEQ
    off = N_CTX // T
    in_specs = [
        pl.BlockSpec((P_COLS // TN, T, TN), lambda b: (0, off + b, 0)),
        pl.BlockSpec((T, GATE_LANES), lambda b: (off + b, 0)),
        pl.BlockSpec((1, 2 * H_R, LANES), lambda b: (layer, 0, 0)),
        pl.BlockSpec((1, 1, W_M), lambda b: (layer, 0, 0)),
        pl.BlockSpec((1, 1, W_R), lambda b: (layer, 0, 0)),
        pl.BlockSpec((1, 1, 2, H_M, DH, DH), lambda b: (b, layer, 0, 0, 0, 0)),
        pl.BlockSpec((1, 1, 2, H_M, DH), lambda b: (b, layer, 0, 0, 0)),
        pl.BlockSpec((1, 1, 1, LANES), lambda b: (b, layer, 0, 0)),
        pl.BlockSpec((1, 1, 2, H_R, DH, DH), lambda b: (b, layer, 0, 0, 0, 0)),
        pl.BlockSpec(memory_space=pl.ANY),
    ]
    return pl.pallas_call(
        functools.partial(_scan_kernel, T=T, has_state=True),
        grid=(DEC_BATCH,),
        in_specs=in_specs,
        out_specs=pl.BlockSpec((T, D_MODEL), lambda b: (off + b, 0)),
        out_shape=jax.ShapeDtypeStruct((NTOK, D_MODEL), BF16),
        input_output_aliases={9: 0},
        scratch_shapes=_scan_scratch(T),
        compiler_params=pltpu.CompilerParams(
            dimension_semantics=("arbitrary",), vmem_limit_bytes=VMEM_LIMIT),
        name="scan_lat",
    )(p, gates, dl, nm, nr, C0, n0, m0, S0, y_prev)


def _top2(logits):
    lane = lax.broadcasted_iota(jnp.int32, logits.shape, 1)
    v1 = jnp.max(logits, -1, keepdims=True)
    i1 = jnp.min(jnp.where(logits == v1, lane, LANES), -1, keepdims=True)
    rest = jnp.where(lane == i1, -jnp.inf, logits)
    v2 = jnp.max(rest, -1, keepdims=True)
    i2 = jnp.min(jnp.where(rest == v2, lane, LANES), -1, keepdims=True)
    e2 = jnp.exp(v2 - v1)
    return i1, i2, 1.0 / (1.0 + e2), e2 / (1.0 + e2)


def _split2(x):
    hi = x.astype(BF16)
    return hi, (x - hi.astype(F32)).astype(BF16)


def _dot_f32x3(a, b):
    a_hi, a_lo = _split2(a)
    b_hi, b_lo = _split2(b)
    return _dot(a_hi, b_hi) + _dot(a_hi, b_lo) + _dot(a_lo, b_hi)


R_E1, R_E2, R_W1, R_W2, R_S1, R_S2 = range(6)


def _out_kernel(*refs, with_router, split_x):
    y_ref = refs[0]
    if split_x:
        xp_ref, xl_ref = refs[1:3]
        x_in = jnp.where(pl.program_id(0) >= N_CTX // TM, xl_ref[...], xp_ref[...])
    else:
        x_in = refs[1][...]
    refs = refs[3:] if split_x else refs[2:]
    if with_router:
        (w_ref, g_ref, mod_ref, wr_ref,
         x1_ref, h2_ref, rinfo_ref, cnt_ref, w_scr, tri_scr, cnt_scr) = refs
    else:
        w_ref, g_ref, mod_ref, x1_ref, h2_ref, w_scr = refs

    @pl.when(pl.program_id(0) == 0)
    def _():
        w_scr[...] = w_ref[0].astype(BF16)
        if with_router:
            r = lax.broadcasted_iota(jnp.int32, (TM, TM), 0)
            c = lax.broadcasted_iota(jnp.int32, (TM, TM), 1)
            tri_scr[...] = (c < r).astype(BF16)
            cnt_scr[...] = jnp.zeros_like(cnt_scr)

    o = jnp.dot(y_ref[...], w_scr[...], preferred_element_type=F32)
    x1 = x_in + mod_ref[0, 0, 2:3, :] * o
    x1_ref[...] = x1
    h2 = _rmsnorm(x1, g_ref[0]) * (1.0 + mod_ref[0, 0, 4:5, :]) + mod_ref[0, 0, 3:4, :]
    if not with_router:
        h2_ref[...] = h2.astype(BF16)
    else:
        h2_ref[...] = h2
        logits = _dot_f32x3(h2, wr_ref[0])
        lane = lax.broadcasted_iota(jnp.int32, logits.shape, 1)
        i1, i2, w1, w2 = _top2(jnp.where(lane < N_EXPERTS, logits, -jnp.inf))
        oh1 = lane == i1
        oh2 = lane == i2
        sel = jnp.where(oh1 | oh2, 1.0, 0.0)
        rank = _dot(tri_scr[...], sel.astype(BF16)) + cnt_scr[...]
        r1 = jnp.sum(jnp.where(oh1, rank, 0.0), -1, keepdims=True)
        r2 = jnp.sum(jnp.where(oh2, rank, 0.0), -1, keepdims=True)
        s1 = i1.astype(F32) * float(REG) + r1
        s2 = i2.astype(F32) * float(REG) + r2
        info = jnp.zeros(logits.shape, F32)
        for col, val in ((R_E1, i1.astype(F32)), (R_E2, i2.astype(F32)), (R_W1, w1), (R_W2, w2),
                         (R_S1, s1), (R_S2, s2)):
            info = jnp.where(lane == col, val, info)
        rinfo_ref[...] = info
        cnt_scr[...] += jnp.sum(sel, 0, keepdims=True)
        cnt_ref[...] = cnt_scr[...]


def _out(y, xs, w_out, g2, mods, layer, w_router_pad=None, router_idx=0):
    with_router = w_router_pad is not None
    split_x = len(xs) == 2
    n_ctx_tiles = N_CTX // TM
    if split_x:
        x_specs = [pl.BlockSpec((TM, D_MODEL), lambda i: (jnp.minimum(i, n_ctx_tiles - 1), 0)),
                   pl.BlockSpec((TM, D_MODEL), lambda i: (jnp.maximum(i - n_ctx_tiles, 0), 0))]
    else:
        x_specs = [pl.BlockSpec((TM, D_MODEL), lambda i: (i, 0))]
    in_specs = [pl.BlockSpec((TM, D_MODEL), lambda i: (i, 0))] + x_specs + [
        pl.BlockSpec((1, D_MODEL, D_MODEL), lambda i: (layer, 0, 0)),
        pl.BlockSpec((1, 1, D_MODEL), lambda i: (layer, 0, 0)),
        pl.BlockSpec((1, 1, 6, D_MODEL), lambda i: (layer, _group_of_tile(i, TM), 0, 0)),
    ]
    out_specs = [
        pl.BlockSpec((TM, D_MODEL), lambda i: (i, 0)),
        pl.BlockSpec((TM, D_MODEL), lambda i: (i, 0)),
    ]
    out_shape = [
        jax.ShapeDtypeStruct((NTOK, D_MODEL), F32),
        jax.ShapeDtypeStruct((NTOK, D_MODEL), F32 if with_router else BF16),
    ]
    args = [y, *xs, w_out, g2, mods]
    scratch = [pltpu.VMEM((D_MODEL, D_MODEL), BF16)]
    if with_router:
        in_specs.append(pl.BlockSpec((1, D_MODEL, LANES), lambda i: (router_idx, 0, 0)))
        out_specs += [pl.BlockSpec((TM, LANES), lambda i: (i, 0)),
                      pl.BlockSpec((1, LANES), lambda i: (0, 0))]
        out_shape += [jax.ShapeDtypeStruct((NTOK, LANES), F32),
                      jax.ShapeDtypeStruct((1, LANES), F32)]
        args.append(w_router_pad)
        scratch += [pltpu.VMEM((TM, TM), BF16), pltpu.VMEM((1, LANES), F32)]
    return pl.pallas_call(
        functools.partial(_out_kernel, with_router=with_router, split_x=split_x),
        grid=(NTOK // TM,),
        in_specs=in_specs,
        out_specs=out_specs,
        out_shape=out_shape,
        scratch_shapes=scratch,
        compiler_params=pltpu.CompilerParams(
            dimension_semantics=("arbitrary",), vmem_limit_bytes=VMEM_LIMIT),
        name="out_router" if with_router else "out",
    )(*args)


N_FC = D_FF // FC
W_BLOCKS = 16
WGU_ROWS = D_MODEL // W_BLOCKS
WD_ROWS = D_FF // W_BLOCKS


def _load_swiglu_weights(wg_hbm, wu_hbm, wd_hbm, w_gu, w_d, stg_gu, stg_d, sem):
    def copies(k):
        s = k % 2
        return (pltpu.make_async_copy(wg_hbm.at[pl.ds(k * WGU_ROWS, WGU_ROWS), :], stg_gu.at[s, 0], sem.at[s, 0]),
                pltpu.make_async_copy(wu_hbm.at[pl.ds(k * WGU_ROWS, WGU_ROWS), :], stg_gu.at[s, 1], sem.at[s, 1]),
                pltpu.make_async_copy(wd_hbm.at[pl.ds(k * WD_ROWS, WD_ROWS), :], stg_d.at[s], sem.at[s, 2]))

    for c in copies(0):
        c.start()
    for k in range(W_BLOCKS):
        if k + 1 < W_BLOCKS:
            for c in copies(k + 1):
                c.start()
        for c in copies(k):
            c.wait()
        s = k % 2
        rows = slice(k * WGU_ROWS, (k + 1) * WGU_ROWS)
        for f in range(N_FC):
            cols = slice(f * FC, (f + 1) * FC)
            w_gu[f, rows, :FC] = stg_gu[s, 0, :, cols].astype(BF16)
            w_gu[f, rows, FC:] = stg_gu[s, 1, :, cols].astype(BF16)
        w_d[k * WD_ROWS:(k + 1) * WD_ROWS, :] = stg_d[s].astype(BF16)


def _swiglu_tile(h, w_gu, w_d, acc):
    def up(f):
        return jnp.dot(h, w_gu[f], preferred_element_type=F32)

    ab = up(0)
    for f in range(N_FC):
        ab_next = up(f + 1) if f + 1 < N_FC else None
        t = (_silu(ab[:, :FC]) * ab[:, FC:]).astype(BF16)
        contrib = jnp.dot(t, w_d[f * FC:(f + 1) * FC, :], preferred_element_type=F32)
        if f == 0:
            acc[...] = contrib
        else:
            acc[...] += contrib
        ab = ab_next


def _ffn_weight_scratch():
    return [
        pltpu.VMEM((N_FC, D_MODEL, 2 * FC), BF16),
        pltpu.VMEM((D_FF, D_MODEL), BF16),
        pltpu.VMEM((2, 2, WGU_ROWS, D_FF), F32),
        pltpu.VMEM((2, WD_ROWS, D_MODEL), F32),
        pltpu.SemaphoreType.DMA((2, 3)),
    ]


def _ffn_kernel(h_ref, res_ref, wg_ref, wu_ref, wd_ref, mod_ref, o_ref,
                acc, w_gu, w_d, stg_gu, stg_d, sem, *, w_idx):
    @pl.when(pl.program_id(0) == 0)
    def _():
        _load_swiglu_weights(wg_ref.at[w_idx], wu_ref.at[w_idx], wd_ref.at[w_idx],
                             w_gu, w_d, stg_gu, stg_d, sem)

    _swiglu_tile(h_ref[...], w_gu, w_d, acc)
    o_ref[...] = res_ref[...] + mod_ref[0, 0, 5:6, :] * acc[...]


def _ffn(h2, res, wg, wu, wd, mods, layer, w_idx):
    return pl.pallas_call(
        functools.partial(_ffn_kernel, w_idx=w_idx),
        grid=(NTOK // TM,),
        in_specs=[
            pl.BlockSpec((TM, D_MODEL), lambda i: (i, 0)),
            pl.BlockSpec((TM, D_MODEL), lambda i: (i, 0)),
            pl.BlockSpec(memory_space=pl.ANY),
            pl.BlockSpec(memory_space=pl.ANY),
            pl.BlockSpec(memory_space=pl.ANY),
            pl.BlockSpec((1, 1, 6, D_MODEL), lambda i: (layer, _group_of_tile(i, TM), 0, 0)),
        ],
        out_specs=pl.BlockSpec((TM, D_MODEL), lambda i: (i, 0)),
        out_shape=jax.ShapeDtypeStruct((NTOK, D_MODEL), F32),
        scratch_shapes=[pltpu.VMEM((TM, D_MODEL), F32)] + _ffn_weight_scratch(),
        compiler_params=pltpu.CompilerParams(
            dimension_semantics=("arbitrary",), vmem_limit_bytes=VMEM_LIMIT),
        name="ffn",
    )(h2, res, wg, wu, wd, mods)


def _tile_plan(counts):
    nt = (counts + TR - 1) // TR
    cum = jnp.cumsum(nt)
    total = cum[-1]
    t = jnp.arange(MAX_TILES, dtype=jnp.int32)
    tt = jnp.minimum(t, total - 1)
    e = jnp.sum((cum[None, :] <= tt[:, None]).astype(jnp.int32), axis=1)
    k = tt - (cum - nt)[e]
    n = jnp.where(t < total, jnp.clip(counts[e] - k * TR, 0, TR), 0)
    return e.astype(jnp.int32), (e * REG_TILES + k).astype(jnp.int32), n.astype(jnp.int32)


def _row_copy(src, src_row, dst, dst_row, sem):
    return pltpu.make_async_copy(src.at[pl.ds(src_row, 1)], dst.at[pl.ds(dst_row, 1)], sem)


def _dispatch_kernel(slot_ref, h_ref, xs_ref, sem):
    base = pl.program_id(0) * (TOP_K * TD)

    def issue(r, carry):
        for k in range(TOP_K):
            _row_copy(h_ref, r, xs_ref, slot_ref[base + TOP_K * r + k], sem).start()
        return carry

    lax.fori_loop(0, TD, issue, 0, unroll=8)
    for k in range(TOP_K):
        pltpu.make_async_copy(h_ref, xs_ref.at[pl.ds(0, TD)], sem).wait()


def _dispatch(slots, h2f):
    return pl.pallas_call(
        _dispatch_kernel,
        grid_spec=pltpu.PrefetchScalarGridSpec(
            num_scalar_prefetch=1,
            grid=(NTOK // TD,),
            in_specs=[pl.BlockSpec((TD, D_MODEL), lambda i, s: (i, 0))],
            out_specs=pl.BlockSpec(memory_space=pl.ANY),
            scratch_shapes=[pltpu.SemaphoreType.DMA],
        ),
        out_shape=jax.ShapeDtypeStruct((N_EXPERTS * REG, D_MODEL), F32),
        compiler_params=pltpu.CompilerParams(
            dimension_semantics=("arbitrary",), vmem_limit_bytes=VMEM_LIMIT),
        name="moe_dispatch",
    )(slots, h2f)


def _gffn_kernel(te_ref, tb_ref, tn_ref, x_ref, wg_ref, wu_ref, wd_ref, o_ref,
                 acc, w_gu, w_d, stg_gu, stg_d, sem, *, w_base):
    t = pl.program_id(0)
    n = tn_ref[t]

    @pl.when(n > 0)
    def _():
        @pl.when(tb_ref[t] % REG_TILES == 0)
        def _():
            e = w_base + te_ref[t]
            _load_swiglu_weights(wg_ref.at[e], wu_ref.at[e], wd_ref.at[e], w_gu, w_d, stg_gu, stg_d, sem)

        row = lax.broadcasted_iota(jnp.int32, (TR, D_MODEL), 0)
        h = jnp.where(row < n, x_ref[...], 0.0).astype(BF16)
        _swiglu_tile(h, w_gu, w_d, acc)
        o_ref[...] = acc[...]


def _gffn(tile_e, tile_blk, tile_n, xs, wg, wu, wd, w_base):
    return pl.pallas_call(
        functools.partial(_gffn_kernel, w_base=w_base),
        grid_spec=pltpu.PrefetchScalarGridSpec(
            num_scalar_prefetch=3,
            grid=(MAX_TILES,),
            in_specs=[
                pl.BlockSpec((TR, D_MODEL), lambda t, te, tb, tn: (tb[t], 0)),
                pl.BlockSpec(memory_space=pl.ANY),
                pl.BlockSpec(memory_space=pl.ANY),
                pl.BlockSpec(memory_space=pl.ANY),
            ],
            out_specs=pl.BlockSpec((TR, D_MODEL), lambda t, te, tb, tn: (tb[t], 0)),
            scratch_shapes=[pltpu.VMEM((TR, D_MODEL), F32)] + _ffn_weight_scratch(),
        ),
        out_shape=jax.ShapeDtypeStruct((N_EXPERTS * REG, D_MODEL), F32),
        compiler_params=pltpu.CompilerParams(
            dimension_semantics=("arbitrary",), vmem_limit_bytes=VMEM_LIMIT),
        name="moe_ffn",
    )(tile_e, tile_blk, tile_n, xs, wg, wu, wd)


def _combine_kernel(slot_ref, x1_ref, rinfo_ref, mod_ref, ys_ref, o_ref, buf, sem):
    base = pl.program_id(0) * (TOP_K * TD)

    def issue(r, carry):
        for k in range(TOP_K):
            _row_copy(ys_ref, slot_ref[base + TOP_K * r + k], buf.at[k], r, sem).start()
        return carry

    lax.fori_loop(0, TD, issue, 0, unroll=8)
    for k in range(TOP_K):
        pltpu.make_async_copy(ys_ref.at[pl.ds(0, TD)], buf.at[k], sem).wait()
    y = rinfo_ref[:, R_W1:R_W1 + 1] * buf[0] + rinfo_ref[:, R_W2:R_W2 + 1] * buf[1]
    o_ref[...] = x1_ref[...] + mod_ref[0, 0, 5:6, :] * y


def _combine(slots, x1, rinfo, mods, ys, layer):
    return pl.pallas_call(
        _combine_kernel,
        grid_spec=pltpu.PrefetchScalarGridSpec(
            num_scalar_prefetch=1,
            grid=(NTOK // TD,),
            in_specs=[
                pl.BlockSpec((TD, D_MODEL), lambda i, s: (i, 0)),
                pl.BlockSpec((TD, LANES), lambda i, s: (i, 0)),
                pl.BlockSpec((1, 1, 6, D_MODEL), lambda i, s: (layer, _group_of_tile(i, TD), 0, 0)),
                pl.BlockSpec(memory_space=pl.ANY),
            ],
            out_specs=pl.BlockSpec((TD, D_MODEL), lambda i, s: (i, 0)),
            scratch_shapes=[pltpu.VMEM((TOP_K, TD, D_MODEL), F32), pltpu.SemaphoreType.DMA],
        ),
        out_shape=jax.ShapeDtypeStruct((NTOK, D_MODEL), F32),
        compiler_params=pltpu.CompilerParams(
            dimension_semantics=("arbitrary",), vmem_limit_bytes=VMEM_LIMIT),
        name="moe_combine",
    )(slots, x1, rinfo, mods, ys)


def _final_kernel(x_ref, g_ref, o_ref):
    o_ref[...] = _rmsnorm(x_ref[...], g_ref[...])


def _final(x, g, row_off, rows):
    off = row_off // TM
    return pl.pallas_call(
        _final_kernel,
        grid=(rows // TM,),
        in_specs=[
            pl.BlockSpec((TM, D_MODEL), lambda i: (off + i, 0)),
            pl.BlockSpec((1, D_MODEL), lambda i: (0, 0)),
        ],
        out_specs=pl.BlockSpec((TM, D_MODEL), lambda i: (i, 0)),
        out_shape=jax.ShapeDtypeStruct((rows, D_MODEL), F32),
        compiler_params=pltpu.CompilerParams(
            dimension_semantics=("arbitrary",), vmem_limit_bytes=VMEM_LIMIT),
        name="final_norm",
    )(x, g)


def kernel(x_prompt, x_sample, state_mlstm_C, state_mlstm_n, state_mlstm_m, state_ret_S, c, c_ctx,
           norm1_g, norm2_g, norm_f_g, w_ada, b_ada, w_in, b_gates, ret_decay_logit,
           mlstm_norm_g, ret_norm_g, w_out, ffn_w_gate, ffn_w_up, ffn_w_down,
           moe_w_router, moe_w_gate, moe_w_up, moe_w_down):
    xs_in = (x_prompt.reshape(N_CTX, D_MODEL), x_sample.reshape(N_LAT, D_MODEL))
    cvec = jnp.concatenate(
        [c_ctx[None, :], c, jnp.zeros((N_GROUPS - 1 - DEC_BATCH, D_MODEL), F32)], 0)
    mods = _ada(cvec, w_ada, b_ada).reshape(DEPTH, N_GROUPS, 6, D_MODEL)

    n_m = 4 * W_M
    w_ret = w_in[:, :, n_m + N_GATES:]
    n_if = N_GATES // 2
    lane_pad = ((0, 0), (0, 0), (0, LANES - n_if))
    wg = jnp.concatenate([jnp.pad(w_in[:, :, n_m:n_m + n_if], lane_pad),
                          jnp.pad(w_in[:, :, n_m + n_if:n_m + N_GATES], lane_pad)], -1)
    bg = jnp.concatenate([jnp.pad(b_gates[:, None, :n_if], lane_pad),
                          jnp.pad(b_gates[:, None, n_if:], lane_pad)], -1)
    cos_np, sin_np = _rope_tables()
    cos_t, sin_t = jnp.asarray(cos_np), jnp.asarray(sin_np)
    dl = jnp.broadcast_to(ret_decay_logit.reshape(DEPTH, 2 * H_R, 1), (DEPTH, 2 * H_R, LANES))
    m0 = jnp.pad(state_mlstm_m.reshape(DEC_BATCH, DEPTH, 1, 2 * H_M),
                 ((0, 0), (0, 0), (0, 0), (0, LANES - 2 * H_M)))
    g1 = norm1_g.reshape(DEPTH, 1, D_MODEL)
    g2 = norm2_g.reshape(DEPTH, 1, D_MODEL)
    nm = mlstm_norm_g.reshape(DEPTH, 1, W_M)
    nr = ret_norm_g.reshape(DEPTH, 1, W_R)
    n_moe = moe_w_router.shape[0]
    wr_pad = jnp.pad(moe_w_router, ((0, 0), (0, 0), (0, LANES - N_EXPERTS)))
    moe_g = moe_w_gate.reshape(n_moe * N_EXPERTS, D_MODEL, D_FF)
    moe_u = moe_w_up.reshape(n_moe * N_EXPERTS, D_MODEL, D_FF)
    moe_d = moe_w_down.reshape(n_moe * N_EXPERTS, D_FF, D_MODEL)

    states = ()
    xs = xs_in
    for l in range(DEPTH):
        jl = l // 2
        p, gates = _proj(xs, g1, mods, w_in, w_ret, wg, bg, cos_t, sin_t, l)
        y, *states = _scan_ctx(p, gates, dl, nm, nr, l, states)
        y = _scan_lat(p, gates, dl, nm, nr, state_mlstm_C, state_mlstm_n, m0, state_ret_S, y, l)
        if l % 2 == 0:
            x1, h2 = _out(y, xs, w_out, g2, mods, l)
            x = _ffn(h2, x1, ffn_w_gate, ffn_w_up, ffn_w_down, mods, l, jl)
        else:
            x1, h2f, rinfo, cnt = _out(y, xs, w_out, g2, mods, l, wr_pad, jl)
            slots = rinfo[:, R_S1:R_S2 + 1].astype(jnp.int32).reshape(TOP_K * NTOK)
            tile_e, tile_blk, tile_n = _tile_plan(cnt[0, :N_EXPERTS].astype(jnp.int32))
            xd = _dispatch(slots, h2f)
            yd = _gffn(tile_e, tile_blk, tile_n, xd, moe_g, moe_u, moe_d, jl * N_EXPERTS)
            x = _combine(slots, x1, rinfo, mods, yd, l)
        xs = (x,)

    y_prompt = _final(x, norm_f_g.reshape(1, D_MODEL), 0, N_CTX).reshape(BATCH, SEQ, D_MODEL)
    y_sample = _final(x, norm_f_g.reshape(1, D_MODEL), N_CTX, N_LAT).reshape(DEC_BATCH, DEC_SEQ, D_MODEL)
    new_C, new_n, new_m, new_S = states
    return (y_prompt, y_sample, new_C, new_n,
            new_m[:, :, 0, :2 * H_M].reshape(BATCH, DEPTH, 2, H_M), new_S)
```

```python
import functools

import numpy as np
import jax
import jax.numpy as jnp
from jax import lax
from jax.experimental import pallas as pl
from jax.experimental.pallas import tpu as pltpu

D_MODEL = 1024
BATCH = 32
SEQ = 256
DEPTH = 2
DEC_BATCH = 2
DEC_SEQ = 1024
GRID_W = 64
H_M = 4
DH = 128
H_R = 4
W_M = H_M * DH
W_R = H_R * DH
N_GATES = 4 * H_M
CHUNK = 128
D_FF = 2816
N_EXPERTS = 8
ROPE_BASE = 10000.0
EPS = 1e-6

N_CTX = BATCH * SEQ
N_LAT = DEC_BATCH * DEC_SEQ
NTOK = N_CTX + N_LAT
N_GROUPS = 8
K_SCALE = DH ** -0.5
P_COLS = 4 * W_M + 4 * W_R
LANES = 128
GATE_LANES = 2 * LANES
VMEM_LIMIT = 56 * 1024 * 1024

F32 = jnp.float32
BF16 = jnp.bfloat16
HIGHEST = lax.Precision.HIGHEST

TM = 1024
TN = 1024
FC = 256
TOP_K = 2
TR = 896
REG_TILES = -(-NTOK // TR)
REG = REG_TILES * TR
MAX_TILES = -(-TOP_K * NTOK // TR) + N_EXPERTS
TD = 512


def _group_of_tile(i, tm):
    return jnp.maximum(i * tm // DEC_SEQ - (N_CTX // DEC_SEQ - 1), 0)


def _silu(x):
    return x * jax.nn.sigmoid(x)


def _log_sigmoid(x):
    return jnp.minimum(x, 0.0) - jnp.log(1.0 + jnp.exp(-jnp.abs(x)))


def _rmsnorm(x, g):
    return x * lax.rsqrt(jnp.mean(x * x, -1, keepdims=True) + EPS) * g


def _ada_kernel(cv_ref, w_ref, b_ref, o_ref):
    s = _silu(cv_ref[...]).astype(BF16)
    o_ref[0] = jnp.dot(s, w_ref[0].astype(BF16), preferred_element_type=F32) + b_ref[0]


def _ada(cvec, w_ada, b_ada):
    tn = 1536
    n = 6 * D_MODEL
    return pl.pallas_call(
        _ada_kernel,
        grid=(DEPTH, n // tn),
        in_specs=[
            pl.BlockSpec((N_GROUPS, D_MODEL), lambda l, j: (0, 0)),
            pl.BlockSpec((1, D_MODEL, tn), lambda l, j: (l, 0, j)),
            pl.BlockSpec((1, 1, tn), lambda l, j: (l, 0, j)),
        ],
        out_specs=pl.BlockSpec((1, N_GROUPS, tn), lambda l, j: (l, 0, j)),
        out_shape=jax.ShapeDtypeStruct((DEPTH, N_GROUPS, n), F32),
        compiler_params=pltpu.CompilerParams(
            dimension_semantics=("arbitrary", "arbitrary"), vmem_limit_bytes=VMEM_LIMIT),
        name="ada",
    )(cvec, w_ada, b_ada.reshape(DEPTH, 1, n))


def _rope_tables():
    half = DH // 4
    freqs = ROPE_BASE ** (-np.arange(half, dtype=np.float64) / half)
    t = np.arange(DEC_SEQ)
    pos = np.stack([t // GRID_W, t % GRID_W], 1).astype(np.float64)
    d = np.arange(DH)
    ang = pos[:, d // (DH // 2)] * freqs[d % half][None, :]
    sign = np.where((d % (DH // 2)) < half, -1.0, 1.0)[None, :]
    return np.cos(ang).astype(np.float32), (sign * np.sin(ang)).astype(np.float32)


def _rope(a, cos, sin):
    lane = lax.broadcasted_iota(jnp.int32, a.shape, 1)
    first = (lane % (DH // 2)) < (DH // 4)
    partner = jnp.where(first, pltpu.roll(a, DH - DH // 4, 1), pltpu.roll(a, DH // 4, 1))
    return a * cos + partner * sin


def _proj_kernel(*refs, n_ctx_tiles, split_x):
    if split_x:
        (xp_ref, xl_ref, g_ref, mod_ref, wa_ref, wb_ref, wg_ref, bg_ref, cos_ref, sin_ref,
         p_ref, gate_ref, h_scr, w_res) = refs
    else:
        (x_ref, g_ref, mod_ref, wa_ref, wb_ref, wg_ref, bg_ref, cos_ref, sin_ref,
         p_ref, gate_ref, h_scr, w_res) = refs
    p_ref = p_ref.at[0]
    i = pl.program_id(0)
    j = pl.program_id(1)
    is_lat = i >= n_ctx_tiles
    half = TN // 2

    def prologue(x):
        h = _rmsnorm(x, g_ref[0]) * (1.0 + mod_ref[0, 0, 1:2, :]) + mod_ref[0, 0, 0:1, :]
        h_scr[...] = h.astype(BF16)
        gate_ref[...] = _dot_f32x3(h, wg_ref[0]) + bg_ref[0]

    @pl.when(j == 0)
    def _():
        if split_x:
            pl.when(jnp.logical_not(is_lat))(lambda: prologue(xp_ref[...]))
            pl.when(is_lat)(lambda: prologue(xl_ref[...]))
        else:
            prologue(x_ref[...])

    def matmul(w_ref, jj):
        @pl.when(i == 0)
        def _():
            w_res[jj] = w_ref[0].astype(BF16)

        return jnp.dot(h_scr[...], w_res[jj], preferred_element_type=F32)

    @pl.when(j == 0)
    def _():
        acc = matmul(wa_ref, 0)
        p_ref[:, :half] = acc[:, :half].astype(BF16)
        p_ref[:, half:] = (acc[:, half:] * K_SCALE).astype(BF16)

    @pl.when(j == 1)
    def _():
        p_ref[...] = matmul(wa_ref, 1).astype(BF16)

    @pl.when(j == 2)
    def _():
        acc = matmul(wb_ref, 2)

        @pl.when(is_lat)
        def _():
            cos = cos_ref[...]
            sin = sin_ref[...]
            for hd in range(TN // DH):
                sl = slice(hd * DH, (hd + 1) * DH)
                r = _rope(acc[:, sl], cos, sin)
                p_ref[:, sl] = (r * K_SCALE if hd * DH >= half else r).astype(BF16)

        @pl.when(jnp.logical_not(is_lat))
        def _():
            p_ref[:, :half] = acc[:, :half].astype(BF16)
            p_ref[:, half:] = (acc[:, half:] * K_SCALE).astype(BF16)

    @pl.when(j == 3)
    def _():
        p_ref[...] = matmul(wb_ref, 3).astype(BF16)


def _proj(xs, g1, mods, w_in, w_ret, wg, bg, cos_t, sin_t, layer):
    n_ctx_tiles = N_CTX // TM
    tiles_per_seq = DEC_SEQ // TM
    split_x = len(xs) == 2
    if split_x:
        x_specs = [pl.BlockSpec((TM, D_MODEL), lambda i, j: (jnp.minimum(i, n_ctx_tiles - 1), 0)),
                   pl.BlockSpec((TM, D_MODEL), lambda i, j: (jnp.maximum(i - n_ctx_tiles, 0), 0))]
    else:
        x_specs = [pl.BlockSpec((TM, D_MODEL), lambda i, j: (i, 0))]
    n_a = 4 * W_M // TN
    return pl.pallas_call(
        functools.partial(_proj_kernel, n_ctx_tiles=n_ctx_tiles, split_x=split_x),
        grid=(NTOK // TM, P_COLS // TN),
        in_specs=x_specs + [
            pl.BlockSpec((1, 1, D_MODEL), lambda i, j: (layer, 0, 0)),
            pl.BlockSpec((1, 1, 6, D_MODEL), lambda i, j: (layer, _group_of_tile(i, TM), 0, 0)),
            pl.BlockSpec((1, D_MODEL, TN),
                         lambda i, j: (layer, 0, jnp.where(i == 0, jnp.minimum(j, n_a - 1), n_a - 1)),
                         pipeline_mode=pl.Buffered(1)),
            pl.BlockSpec((1, D_MODEL, TN),
                         lambda i, j: (layer, 0, jnp.where(i == 0, jnp.maximum(j - n_a, 0), n_a - 1)),
                         pipeline_mode=pl.Buffered(1)),
            pl.BlockSpec((1, D_MODEL, GATE_LANES), lambda i, j: (layer, 0, 0)),
            pl.BlockSpec((1, 1, GATE_LANES), lambda i, j: (layer, 0, 0)),
            pl.BlockSpec((TM, DH), lambda i, j: (i % tiles_per_seq, 0)),
            pl.BlockSpec((TM, DH), lambda i, j: (i % tiles_per_seq, 0)),
        ],
        out_specs=[
            pl.BlockSpec((1, TM, TN), lambda i, j: (j, i, 0)),
            pl.BlockSpec((TM, GATE_LANES), lambda i, j: (i, 0)),
        ],
        out_shape=[
            jax.ShapeDtypeStruct((P_COLS // TN, NTOK, TN), BF16),
            jax.ShapeDtypeStruct((NTOK, GATE_LANES), F32),
        ],
        scratch_shapes=[pltpu.VMEM((TM, D_MODEL), BF16),
                        pltpu.VMEM((P_COLS // TN, D_MODEL, TN), BF16)],
        compiler_params=pltpu.CompilerParams(
            dimension_semantics=("arbitrary", "arbitrary"), vmem_limit_bytes=VMEM_LIMIT),
        name="proj",
    )(*xs, g1, mods, w_in, w_ret, wg, bg, cos_t, sin_t)


def _split3(x):
    hi = x.astype(BF16)
    r1 = x - hi.astype(F32)
    mid = r1.astype(BF16)
    lo = (r1 - mid.astype(F32)).astype(BF16)
    return hi, mid, lo


def _dot(a, b):
    return jnp.dot(a, b, preferred_element_type=F32)


def _dot_nt(a, b):
    return lax.dot_general(a, b, (((1,), (1,)), ((), ())), preferred_element_type=F32)


def _tri_dot_left(tri, x):
    hi, mid, lo = _split3(x)
    return _dot(tri, hi) + _dot(tri, mid) + _dot(tri, lo)


def _tri_dot_right(x, tri):
    hi, mid, lo = _split3(x)
    return _dot(hi, tri) + _dot(mid, tri) + _dot(lo, tri)


def _run_max(x, reverse):
    n_tiles = x.shape[0] // 8
    sub = lax.broadcasted_iota(jnp.int32, (8, LANES), 0)
    out = [None] * n_tiles
    carry = None
    for t in (range(n_tiles - 1, -1, -1) if reverse else range(n_tiles)):
        v = x[8 * t:8 * t + 8, :]
        for s in (1, 2, 4):
            if reverse:
                v = jnp.maximum(v, jnp.where(sub < 8 - s, pltpu.roll(v, 8 - s, 0), -jnp.inf))
            else:
                v = jnp.maximum(v, jnp.where(sub >= s, pltpu.roll(v, s, 0), -jnp.inf))
        if carry is not None:
            v = jnp.maximum(v, carry)
        carry = jnp.broadcast_to(v[0:1, :] if reverse else v[7:8, :], (8, LANES))
        out[t] = v
    return jnp.concatenate(out, axis=0)


def _scan_kernel(*refs, T, has_state, n_aliased=0):
    if has_state:
        (p_ref, g_ref, dl_ref, nm_ref, nr_ref, C0_ref, n0_ref, m0_ref, S0_ref, _yprev_ref,
         y_ref, CN_s, S_s, m_s, hf_s, hb_s, dm_s, dq_s, dk_s, dL_s, kT_s) = refs
    else:
        p_ref, g_ref, dl_ref, nm_ref, nr_ref = refs[:5]
        (y_ref, C_out, n_out, m_out, S_out,
         CN_s, S_s, m_s, hf_s, hb_s, dm_s, dq_s, dk_s, dL_s, kT_s) = refs[5 + n_aliased:]
    L = CHUNK
    n_chunks = T // L
    row_i = lax.broadcasted_iota(jnp.int32, (L, L), 0)
    col_j = lax.broadcasted_iota(jnp.int32, (L, L), 1)
    lower = col_j <= row_i
    upper = col_j >= row_i
    tril = lower.astype(BF16)
    triu = upper.astype(BF16)
    ones = jnp.ones((L, DH), BF16)
    c_km = W_M
    c_vm = 2 * W_M
    c_om = 3 * W_M
    c_qr = 4 * W_M
    c_kr = c_qr + W_R
    c_vr = c_qr + 2 * W_R
    c_gr = c_qr + 3 * W_R

    def pcols(rows, col):
        return p_ref[col // TN, rows, col % TN:col % TN + DH]

    for d in range(2):
        for h in range(H_M):
            k = d * H_M + h
            if has_state:
                CN_s[k, :, :DH] = C0_ref[0, 0, d, h]
                CN_s[k, :, DH:] = jnp.broadcast_to(n0_ref[0, 0, d, h:h + 1, :], (DH, DH)).T
                S_s[k] = S0_ref[0, 0, d, h]
            else:
                CN_s[k] = jnp.zeros((DH, 2 * DH), F32)
                S_s[k] = jnp.zeros((DH, DH), F32)
    m_s[...] = m0_ref[0, 0] if has_state else jnp.zeros((1, LANES), F32)

    @pl.when(pl.program_id(0) == 0)
    def _():
        pos_i = row_i.astype(F32)
        pos_j = col_j.astype(F32)
        for d in range(2):
            for h in range(H_R):
                k = d * H_R + h
                lg_row = _log_sigmoid(dl_ref[0, k:k + 1, :])
                lg = jnp.broadcast_to(lg_row, (L, L))
                rel = (row_i - col_j if d == 0 else col_j - row_i).astype(F32)
                dm_s[k] = jnp.where(rel >= 0, jnp.exp(lg * jnp.maximum(rel, 0.0)), 0.0)
                dq_s[k] = jnp.exp(lg * (pos_i + 1.0 if d == 0 else L - pos_i))
                dk_s[k] = jnp.exp(lg * (L - 1.0 - pos_j if d == 0 else pos_j))
                dL_s[k] = jnp.exp(lg_row * float(L))

    def transpose_keys(c, carry):
        r0 = pl.multiple_of(c * L, L)
        for h in range(H_M):
            kT_s[h, c] = pcols(pl.ds(r0, L), c_km + h * DH).astype(F32).T
            kT_s[H_M + h, c] = pcols(pl.ds(r0, L), c_kr + h * DH).astype(F32).T
        return carry

    lax.fori_loop(0, n_chunks, transpose_keys, 0)

    def chunk_step(c, carry):
        m_prev = m_s[...]
        m_new = []
        prep = []
        for d in range(2):
            ci = c if d == 0 else n_chunks - 1 - c
            r0 = pl.multiple_of(ci * L, L)
            mask = lower if d == 0 else upper
            e_row = L - 1 if d == 0 else 0
            FL = _log_sigmoid(g_ref[pl.ds(r0, L), LANES:2 * LANES])
            Bc = _tri_dot_left(tril if d == 0 else triu, FL)
            Zc = g_ref[pl.ds(r0, L), 0:LANES] - Bc
            M = jnp.maximum(_run_max(Zc, reverse=(d == 1)), m_prev)
            m_row = Bc + M
            M_end = M[e_row:e_row + 1, :]
            m_new.append(Bc[e_row:e_row + 1, :] + M_end)
            decay = jnp.exp(m_prev - M_end)
            prep.append(dict(ci=ci, r0=r0, mask=mask, M=M, m_row=m_row, decay=decay,
                             ZT=Zc.T,
                             WT=jnp.exp(Zc - M_end).T))
        pairs = [(d, h) for d in range(2) for h in range(H_M)]

        def rows(d, col):
            return pcols(pl.ds(prep[d]["r0"], L), col)

        qk, qkr = {}, {}
        for d, h in pairs:
            qk[d, h] = _dot_nt(rows(d, h * DH), rows(d, c_km + h * DH))
            qkr[d, h] = _dot_nt(rows(d, c_qr + h * DH), rows(d, c_kr + h * DH))
        upd, updr = {}, {}
        for d, h in pairs:
            k = d * H_M + h
            ci = prep[d]["ci"]
            vo = jnp.concatenate([rows(d, c_vm + h * DH), ones], axis=1)
            wkT = (kT_s[h, ci] * jnp.broadcast_to(prep[d]["WT"][k:k + 1, :], (DH, L))).astype(BF16)
            upd[d, h] = _dot(wkT, vo)
            kdT = (kT_s[H_M + h, ci] * dk_s[k]).astype(BF16)
            updr[d, h] = _dot(kdT, rows(d, c_vr + h * DH))
        for d, h in pairs:
            k = d * H_M + h
            r0 = prep[d]["r0"]
            h_dst = hf_s if d == 0 else hb_s
            q = rows(d, h * DH)
            M_col = jnp.broadcast_to(prep[d]["M"][:, k:k + 1], (L, L))
            z_row = jnp.broadcast_to(prep[d]["ZT"][k:k + 1, :], (L, L))
            D = jnp.where(prep[d]["mask"], jnp.exp(z_row - M_col), 0.0)
            s = (qk[d, h] * D).astype(BF16)
            w_inter = jnp.exp(jnp.broadcast_to(m_prev[:, k:k + 1], (L, L)) - M_col)
            wq = (w_inter * q.astype(F32)).astype(BF16)
            vo = jnp.concatenate([rows(d, c_vm + h * DH), ones], axis=1)
            CN = CN_s[k]
            res = _dot(jnp.concatenate([s, wq], axis=1),
                       jnp.concatenate([vo, CN.astype(BF16)], axis=0))
            floor = jnp.exp(-jnp.broadcast_to(prep[d]["m_row"][:, k:k + 1], (L, L)))
            h_dst[pl.ds(r0, L), h * DH:(h + 1) * DH] = res[:, :DH] / jnp.maximum(jnp.abs(res[:, DH:]), floor)
            CN_s[k] = jnp.broadcast_to(prep[d]["decay"][:, k:k + 1], (DH, 2 * DH)) * CN + upd[d, h]
            qr = rows(d, c_qr + h * DH)
            S = S_s[k]
            sr = (qkr[d, h] * dm_s[k]).astype(BF16)
            qd = (qr.astype(F32) * dq_s[k]).astype(BF16)
            h_dst[pl.ds(r0, L), W_M + h * DH:W_M + (h + 1) * DH] = _dot(
                jnp.concatenate([sr, qd], axis=1),
                jnp.concatenate([rows(d, c_vr + h * DH), S.astype(BF16)], axis=0))
            S_s[k] = dL_s[k] * S + updr[d, h]
        lane = lax.broadcasted_iota(jnp.int32, (1, LANES), 1)
        m_s[...] = jnp.where(lane < H_M, m_new[0], m_new[1])
        return carry

    lax.fori_loop(0, n_chunks, chunk_step, 0)

    for h in range(H_M):
        sl = slice(h * DH, (h + 1) * DH)
        hs = hf_s[:, sl] + hb_s[:, sl]
        yn = _rmsnorm(hs, nm_ref[0, :, sl])
        om = pcols(slice(None), c_om + h * DH).astype(F32)
        y_ref[:, sl] = (jax.nn.sigmoid(om) * yn).astype(BF16)
        slr = slice(W_M + h * DH, W_M + (h + 1) * DH)
        hr = hf_s[:, slr] + hb_s[:, slr]
        ynr = _rmsnorm(hr, nr_ref[0, :, sl])
        gr = pcols(slice(None), c_gr + h * DH).astype(F32)
        y_ref[:, slr] = (_silu(gr) * ynr).astype(BF16)

    if not has_state:
        for d in range(2):
            for h in range(H_M):
                k = d * H_M + h
                C_out[0, 0, d, h] = CN_s[k, :, :DH]
                n_out[0, 0, d, h:h + 1, :] = CN_s[k, :, DH:].T[0:1, :]
                S_out[0, 0, d, h] = S_s[k]
        m_out[0, 0] = m_s[...]


def _scan_scratch(T):
    return [
        pltpu.VMEM((2 * H_M, DH, 2 * DH), F32),
        pltpu.VMEM((2 * H_R, DH, DH), F32),
        pltpu.VMEM((1, LANES), F32),
        pltpu.VMEM((T, W_M + W_R), F32),
        pltpu.VMEM((T, W_M + W_R), F32),
        pltpu.VMEM((2 * H_R, CHUNK, CHUNK), F32),
        pltpu.VMEM((2 * H_R, CHUNK, CHUNK), F32),
        pltpu.VMEM((2 * H_R, CHUNK, CHUNK), F32),
        pltpu.VMEM((2 * H_R, 1, LANES), F32),
        pltpu.VMEM((H_M + H_R, T // CHUNK, DH, CHUNK), F32),
    ]


def _scan_ctx(p, gates, dl, nm, nr, layer, prev_states=()):
    T = SEQ
    common = [
        pl.BlockSpec((P_COLS // TN, T, TN), lambda b: (0, b, 0)),
        pl.BlockSpec((T, GATE_LANES), lambda b: (b, 0)),
        pl.BlockSpec((1, 2 * H_R, LANES), lambda b: (layer, 0, 0)),
        pl.BlockSpec((1, 1, W_M), lambda b: (layer, 0, 0)),
        pl.BlockSpec((1, 1, W_R), lambda b: (layer, 0, 0)),
    ]
    n_al = len(prev_states)
    return pl.pallas_call(
        functools.partial(_scan_kernel, T=T, has_state=False, n_aliased=n_al),
        grid=(BATCH,),
        in_specs=common + [pl.BlockSpec(memory_space=pl.ANY)] * n_al,
        out_specs=[
            pl.BlockSpec((T, D_MODEL), lambda b: (b, 0)),
            pl.BlockSpec((1, 1, 2, H_M, DH, DH), lambda b: (b, layer, 0, 0, 0, 0)),
            pl.BlockSpec((1, 1, 2, H_M, DH), lambda b: (b, layer, 0, 0, 0)),
            pl.BlockSpec((1, 1, 1, LANES), lambda b: (b, layer, 0, 0)),
            pl.BlockSpec((1, 1, 2, H_R, DH, DH), lambda b: (b, layer, 0, 0, 0, 0)),
        ],
        out_shape=[
            jax.ShapeDtypeStruct((NTOK, D_MODEL), BF16),
            jax.ShapeDtypeStruct((BATCH, DEPTH, 2, H_M, DH, DH), F32),
            jax.ShapeDtypeStruct((BATCH, DEPTH, 2, H_M, DH), F32),
            jax.ShapeDtypeStruct((BATCH, DEPTH, 1, LANES), F32),
            jax.ShapeDtypeStruct((BATCH, DEPTH, 2, H_R, DH, DH), F32),
        ],
        input_output_aliases={len(common) + a: 1 + a for a in range(n_al)},
        scratch_shapes=_scan_scratch(T),
        compiler_params=pltpu.CompilerParams(
            dimension_semantics=("arbitrary",), vmem_limit_bytes=VMEM_LIMIT),
        name="scan_ctx",
    )(p, gates, dl, nm, nr, *prev_states)


def _scan_lat(p, gates, dl, nm, nr, C0, n0, m0, S0, y_prev, layer):
    T = DEC_SEQ
    off = N_CTX // T
    in_specs = [
        pl.BlockSpec((P_COLS // TN, T, TN), lambda b: (0, off + b, 0)),
        pl.BlockSpec((T, GATE_LANES), lambda b: (off + b, 0)),
        pl.BlockSpec((1, 2 * H_R, LANES), lambda b: (layer, 0, 0)),
        pl.BlockSpec((1, 1, W_M), lambda b: (layer, 0, 0)),
        pl.BlockSpec((1, 1, W_R), lambda b: (layer, 0, 0)),
        pl.BlockSpec((1, 1, 2, H_M, DH, DH), lambda b: (b, layer, 0, 0, 0, 0)),
        pl.BlockSpec((1, 1, 2, H_M, DH), lambda b: (b, layer, 0, 0, 0)),
        pl.BlockSpec((1, 1, 1, LANES), lambda b: (b, layer, 0, 0)),
        pl.BlockSpec((1, 1, 2, H_R, DH, DH), lambda b: (b, layer, 0, 0, 0, 0)),
        pl.BlockSpec(memory_space=pl.ANY),
    ]
    return pl.pallas_call(
        functools.partial(_scan_kernel, T=T, has_state=True),
        grid=(DEC_BATCH,),
        in_specs=in_specs,
        out_specs=pl.BlockSpec((T, D_MODEL), lambda b: (off + b, 0)),
        out_shape=jax.ShapeDtypeStruct((NTOK, D_MODEL), BF16),
        input_output_aliases={9: 0},
        scratch_shapes=_scan_scratch(T),
        compiler_params=pltpu.CompilerParams(
            dimension_semantics=("arbitrary",), vmem_limit_bytes=VMEM_LIMIT),
        name="scan_lat",
    )(p, gates, dl, nm, nr, C0, n0, m0, S0, y_prev)


def _top2(logits):
    lane = lax.broadcasted_iota(jnp.int32, logits.shape, 1)
    v1 = jnp.max(logits, -1, keepdims=True)
    i1 = jnp.min(jnp.where(logits == v1, lane, LANES), -1, keepdims=True)
    rest = jnp.where(lane == i1, -jnp.inf, logits)
    v2 = jnp.max(rest, -1, keepdims=True)
    i2 = jnp.min(jnp.where(rest == v2, lane, LANES), -1, keepdims=True)
    e2 = jnp.exp(v2 - v1)
    return i1, i2, 1.0 / (1.0 + e2), e2 / (1.0 + e2)


def _split2(x):
    hi = x.astype(BF16)
    return hi, (x - hi.astype(F32)).astype(BF16)


def _dot_f32x3(a, b):
    a_hi, a_lo = _split2(a)
    b_hi, b_lo = _split2(b)
    return _dot(a_hi, b_hi) + _dot(a_hi, b_lo) + _dot(a_lo, b_hi)


R_E1, R_E2, R_W1, R_W2, R_S1, R_S2 = range(6)


def _out_kernel(*refs, with_router, split_x):
    y_ref = refs[0]
    if split_x:
        xp_ref, xl_ref = refs[1:3]
        x_in = jnp.where(pl.program_id(0) >= N_CTX // TM, xl_ref[...], xp_ref[...])
    else:
        x_in = refs[1][...]
    refs = refs[3:] if split_x else refs[2:]
    if with_router:
        (w_ref, g_ref, mod_ref, wr_ref,
         x1_ref, h2_ref, rinfo_ref, cnt_ref, w_scr, tri_scr, cnt_scr) = refs
    else:
        w_ref, g_ref, mod_ref, x1_ref, h2_ref, w_scr = refs

    @pl.when(pl.program_id(0) == 0)
    def _():
        w_scr[...] = w_ref[0].astype(BF16)
        if with_router:
            r = lax.broadcasted_iota(jnp.int32, (TM, TM), 0)
            c = lax.broadcasted_iota(jnp.int32, (TM, TM), 1)
            tri_scr[...] = (c < r).astype(BF16)
            cnt_scr[...] = jnp.zeros_like(cnt_scr)

    o = jnp.dot(y_ref[...], w_scr[...], preferred_element_type=F32)
    x1 = x_in + mod_ref[0, 0, 2:3, :] * o
    x1_ref[...] = x1
    h2 = _rmsnorm(x1, g_ref[0]) * (1.0 + mod_ref[0, 0, 4:5, :]) + mod_ref[0, 0, 3:4, :]
    if not with_router:
        h2_ref[...] = h2.astype(BF16)
    else:
        h2_ref[...] = h2
        logits = _dot_f32x3(h2, wr_ref[0])
        lane = lax.broadcasted_iota(jnp.int32, logits.shape, 1)
        i1, i2, w1, w2 = _top2(jnp.where(lane < N_EXPERTS, logits, -jnp.inf))
        oh1 = lane == i1
        oh2 = lane == i2
        sel = jnp.where(oh1 | oh2, 1.0, 0.0)
        rank = _dot(tri_scr[...], sel.astype(BF16)) + cnt_scr[...]
        r1 = jnp.sum(jnp.where(oh1, rank, 0.0), -1, keepdims=True)
        r2 = jnp.sum(jnp.where(oh2, rank, 0.0), -1, keepdims=True)
        s1 = i1.astype(F32) * float(REG) + r1
        s2 = i2.astype(F32) * float(REG) + r2
        info = jnp.zeros(logits.shape, F32)
        for col, val in ((R_E1, i1.astype(F32)), (R_E2, i2.astype(F32)), (R_W1, w1), (R_W2, w2),
                         (R_S1, s1), (R_S2, s2)):
            info = jnp.where(lane == col, val, info)
        rinfo_ref[...] = info
        cnt_scr[...] += jnp.sum(sel, 0, keepdims=True)
        cnt_ref[...] = cnt_scr[...]


def _out(y, xs, w_out, g2, mods, layer, w_router_pad=None, router_idx=0):
    with_router = w_router_pad is not None
    split_x = len(xs) == 2
    n_ctx_tiles = N_CTX // TM
    if split_x:
        x_specs = [pl.BlockSpec((TM, D_MODEL), lambda i: (jnp.minimum(i, n_ctx_tiles - 1), 0)),
                   pl.BlockSpec((TM, D_MODEL), lambda i: (jnp.maximum(i - n_ctx_tiles, 0), 0))]
    else:
        x_specs = [pl.BlockSpec((TM, D_MODEL), lambda i: (i, 0))]
    in_specs = [pl.BlockSpec((TM, D_MODEL), lambda i: (i, 0))] + x_specs + [
        pl.BlockSpec((1, D_MODEL, D_MODEL), lambda i: (layer, 0, 0)),
        pl.BlockSpec((1, 1, D_MODEL), lambda i: (layer, 0, 0)),
        pl.BlockSpec((1, 1, 6, D_MODEL), lambda i: (layer, _group_of_tile(i, TM), 0, 0)),
    ]
    out_specs = [
        pl.BlockSpec((TM, D_MODEL), lambda i: (i, 0)),
        pl.BlockSpec((TM, D_MODEL), lambda i: (i, 0)),
    ]
    out_shape = [
        jax.ShapeDtypeStruct((NTOK, D_MODEL), F32),
        jax.ShapeDtypeStruct((NTOK, D_MODEL), F32 if with_router else BF16),
    ]
    args = [y, *xs, w_out, g2, mods]
    scratch = [pltpu.VMEM((D_MODEL, D_MODEL), BF16)]
    if with_router:
        in_specs.append(pl.BlockSpec((1, D_MODEL, LANES), lambda i: (router_idx, 0, 0)))
        out_specs += [pl.BlockSpec((TM, LANES), lambda i: (i, 0)),
                      pl.BlockSpec((1, LANES), lambda i: (0, 0))]
        out_shape += [jax.ShapeDtypeStruct((NTOK, LANES), F32),
                      jax.ShapeDtypeStruct((1, LANES), F32)]
        args.append(w_router_pad)
        scratch += [pltpu.VMEM((TM, TM), BF16), pltpu.VMEM((1, LANES), F32)]
    return pl.pallas_call(
        functools.partial(_out_kernel, with_router=with_router, split_x=split_x),
        grid=(NTOK // TM,),
        in_specs=in_specs,
        out_specs=out_specs,
        out_shape=out_shape,
        scratch_shapes=scratch,
        compiler_params=pltpu.CompilerParams(
            dimension_semantics=("arbitrary",), vmem_limit_bytes=VMEM_LIMIT),
        name="out_router" if with_router else "out",
    )(*args)


N_FC = D_FF // FC
def _swiglu_tile(h, w_gu, w_d, acc, fetch=None):
    if fetch is not None:
        wg_hbm, wu_hbm, wd_hbm, stg_gu, stg_d, sem = fetch

        def copies(f):
            s = f % 2
            cols = pl.ds(f * FC, FC)
            return (pltpu.make_async_copy(wg_hbm.at[:, cols], stg_gu.at[s, 0], sem.at[s, 0]),
                    pltpu.make_async_copy(wu_hbm.at[:, cols], stg_gu.at[s, 1], sem.at[s, 1]),
                    pltpu.make_async_copy(wd_hbm.at[cols, :], stg_d.at[s], sem.at[s, 2]))

        def start(f):
            for c in copies(f):
                c.start()

        def land(f):
            for c in copies(f):
                c.wait()
            s = f % 2
            w_gu[f, :, :FC] = stg_gu[s, 0].astype(BF16)
            w_gu[f, :, FC:] = stg_gu[s, 1].astype(BF16)
            w_d[f * FC:(f + 1) * FC, :] = stg_d[s].astype(BF16)
    else:
        start = land = lambda f: None

    def up(f):
        return jnp.dot(h, w_gu[f], preferred_element_type=F32)

    start(0)
    if N_FC > 1:
        start(1)
    land(0)
    ab = up(0)
    for f in range(N_FC):
        if f + 2 < N_FC:
            start(f + 2)
        if f + 1 < N_FC:
            land(f + 1)
            ab_next = up(f + 1)
        t = (_silu(ab[:, :FC]) * ab[:, FC:]).astype(BF16)
        contrib = jnp.dot(t, w_d[f * FC:(f + 1) * FC, :], preferred_element_type=F32)
        if f == 0:
            acc[...] = contrib
        else:
            acc[...] += contrib
        if f + 1 < N_FC:
            ab = ab_next


def _ffn_weight_scratch():
    return [
        pltpu.VMEM((N_FC, D_MODEL, 2 * FC), BF16),
        pltpu.VMEM((D_FF, D_MODEL), BF16),
        pltpu.VMEM((2, 2, D_MODEL, FC), F32),
        pltpu.VMEM((2, FC, D_MODEL), F32),
        pltpu.SemaphoreType.DMA((2, 3)),
    ]


def _ffn_kernel(h_ref, res_ref, wg_ref, wu_ref, wd_ref, mod_ref, o_ref,
                w_gu, w_d, stg_gu, stg_d, sem, *, w_idx):
    first = pl.program_id(0) == 0

    @pl.when(first)
    def _():
        _swiglu_tile(h_ref[...], w_gu, w_d, o_ref,
                     (wg_ref.at[w_idx], wu_ref.at[w_idx], wd_ref.at[w_idx], stg_gu, stg_d, sem))

    @pl.when(jnp.logical_not(first))
    def _():
        _swiglu_tile(h_ref[...], w_gu, w_d, o_ref)

    o_ref[...] = res_ref[...] + mod_ref[0, 0, 5:6, :] * o_ref[...]


def _ffn(h2, res, wg, wu, wd, mods, layer, w_idx):
    return pl.pallas_call(
        functools.partial(_ffn_kernel, w_idx=w_idx),
        grid=(NTOK // TM,),
        in_specs=[
            pl.BlockSpec((TM, D_MODEL), lambda i: (i, 0)),
            pl.BlockSpec((TM, D_MODEL), lambda i: (i, 0)),
            pl.BlockSpec(memory_space=pl.ANY),
            pl.BlockSpec(memory_space=pl.ANY),
            pl.BlockSpec(memory_space=pl.ANY),
            pl.BlockSpec((1, 1, 6, D_MODEL), lambda i: (layer, _group_of_tile(i, TM), 0, 0)),
        ],
        out_specs=pl.BlockSpec((TM, D_MODEL), lambda i: (i, 0)),
        out_shape=jax.ShapeDtypeStruct((NTOK, D_MODEL), F32),
        scratch_shapes=_ffn_weight_scratch(),
        compiler_params=pltpu.CompilerParams(
            dimension_semantics=("arbitrary",), vmem_limit_bytes=VMEM_LIMIT),
        name="ffn",
    )(h2, res, wg, wu, wd, mods)


def _tile_plan(counts):
    nt = (counts + TR - 1) // TR
    cum = jnp.cumsum(nt)
    total = cum[-1]
    t = jnp.arange(MAX_TILES, dtype=jnp.int32)
    tt = jnp.minimum(t, total - 1)
    e = jnp.sum((cum[None, :] <= tt[:, None]).astype(jnp.int32), axis=1)
    k = tt - (cum - nt)[e]
    n = jnp.where(t < total, jnp.clip(counts[e] - k * TR, 0, TR), 0)
    return e.astype(jnp.int32), (e * REG_TILES + k).astype(jnp.int32), n.astype(jnp.int32)


def _row_copy(src, src_row, dst, dst_row, sem):
    return pltpu.make_async_copy(src.at[pl.ds(src_row, 1)], dst.at[pl.ds(dst_row, 1)], sem)


def _dispatch_kernel(slot_ref, h_ref, xs_ref, sem):
    base = pl.program_id(0) * (TOP_K * TD)

    def issue(r, carry):
        for k in range(TOP_K):
            _row_copy(h_ref, r, xs_ref, slot_ref[base + TOP_K * r + k], sem).start()
        return carry

    lax.fori_loop(0, TD, issue, 0, unroll=8)
    for k in range(TOP_K):
        pltpu.make_async_copy(h_ref, xs_ref.at[pl.ds(0, TD)], sem).wait()


def _dispatch(slots, h2f):
    return pl.pallas_call(
        _dispatch_kernel,
        grid_spec=pltpu.PrefetchScalarGridSpec(
            num_scalar_prefetch=1,
            grid=(NTOK // TD,),
            in_specs=[pl.BlockSpec((TD, D_MODEL), lambda i, s: (i, 0))],
            out_specs=pl.BlockSpec(memory_space=pl.ANY),
            scratch_shapes=[pltpu.SemaphoreType.DMA],
        ),
        out_shape=jax.ShapeDtypeStruct((N_EXPERTS * REG, D_MODEL), F32),
        compiler_params=pltpu.CompilerParams(
            dimension_semantics=("arbitrary",), vmem_limit_bytes=VMEM_LIMIT),
        name="moe_dispatch",
    )(slots, h2f)


def _gffn_kernel(te_ref, tb_ref, tn_ref, x_ref, wg_ref, wu_ref, wd_ref, o_ref,
                 w_gu, w_d, stg_gu, stg_d, sem, *, w_base):
    t = pl.program_id(0)
    n = tn_ref[t]

    @pl.when(n > 0)
    def _():
        row = lax.broadcasted_iota(jnp.int32, (TR, D_MODEL), 0)
        h = jnp.where(row < n, x_ref[...], 0.0).astype(BF16)
        first = tb_ref[t] % REG_TILES == 0

        @pl.when(first)
        def _():
            e = w_base + te_ref[t]
            _swiglu_tile(h, w_gu, w_d, o_ref, (wg_ref.at[e], wu_ref.at[e], wd_ref.at[e], stg_gu, stg_d, sem))

        @pl.when(jnp.logical_not(first))
        def _():
            _swiglu_tile(h, w_gu, w_d, o_ref)


def _gffn(tile_e, tile_blk, tile_n, xs, wg, wu, wd, w_base):
    return pl.pallas_call(
        functools.partial(_gffn_kernel, w_base=w_base),
        grid_spec=pltpu.PrefetchScalarGridSpec(
            num_scalar_prefetch=3,
            grid=(MAX_TILES,),
            in_specs=[
                pl.BlockSpec((TR, D_MODEL), lambda t, te, tb, tn: (tb[t], 0)),
                pl.BlockSpec(memory_space=pl.ANY),
                pl.BlockSpec(memory_space=pl.ANY),
                pl.BlockSpec(memory_space=pl.ANY),
            ],
            out_specs=pl.BlockSpec((TR, D_MODEL), lambda t, te, tb, tn: (tb[t], 0)),
            scratch_shapes=_ffn_weight_scratch(),
        ),
        out_shape=jax.ShapeDtypeStruct((N_EXPERTS * REG, D_MODEL), F32),
        compiler_params=pltpu.CompilerParams(
            dimension_semantics=("arbitrary",), vmem_limit_bytes=VMEM_LIMIT),
        name="moe_ffn",
    )(tile_e, tile_blk, tile_n, xs, wg, wu, wd)


def _combine_kernel(slot_ref, x1_ref, rinfo_ref, mod_ref, ys_ref, *rest, final_norm):
    if final_norm:
        gf_ref, yp_ref, yl_ref, buf, sem = rest
    else:
        o_ref, buf, sem = rest
    base = pl.program_id(0) * (TOP_K * TD)

    def issue(r, carry):
        for k in range(TOP_K):
            _row_copy(ys_ref, slot_ref[base + TOP_K * r + k], buf.at[k], r, sem).start()
        return carry

    lax.fori_loop(0, TD, issue, 0, unroll=8)
    for k in range(TOP_K):
        pltpu.make_async_copy(ys_ref.at[pl.ds(0, TD)], buf.at[k], sem).wait()
    y = rinfo_ref[:, R_W1:R_W1 + 1] * buf[0] + rinfo_ref[:, R_W2:R_W2 + 1] * buf[1]
    x = x1_ref[...] + mod_ref[0, 0, 5:6, :] * y
    if final_norm:
        out = _rmsnorm(x, gf_ref[...])
        is_lat = pl.program_id(0) >= N_CTX // TD

        @pl.when(jnp.logical_not(is_lat))
        def _():
            yp_ref[...] = out

        @pl.when(is_lat)
        def _():
            yl_ref[...] = out
    else:
        o_ref[...] = x


def _combine(slots, x1, rinfo, mods, ys, layer, norm_f=None):
    final_norm = norm_f is not None
    n_ctx_t = N_CTX // TD
    in_specs = [
        pl.BlockSpec((TD, D_MODEL), lambda i, s: (i, 0)),
        pl.BlockSpec((TD, LANES), lambda i, s: (i, 0)),
        pl.BlockSpec((1, 1, 6, D_MODEL), lambda i, s: (layer, _group_of_tile(i, TD), 0, 0)),
        pl.BlockSpec(memory_space=pl.ANY),
    ]
    args = [slots, x1, rinfo, mods, ys]
    if final_norm:
        in_specs.append(pl.BlockSpec((1, D_MODEL), lambda i, s: (0, 0)))
        args.append(norm_f)
        out_specs = [pl.BlockSpec((TD, D_MODEL), lambda i, s: (jnp.minimum(i, n_ctx_t - 1), 0)),
                     pl.BlockSpec((TD, D_MODEL), lambda i, s: (jnp.maximum(i - n_ctx_t, 0), 0))]
        out_shape = [jax.ShapeDtypeStruct((N_CTX, D_MODEL), F32), jax.ShapeDtypeStruct((N_LAT, D_MODEL), F32)]
    else:
        out_specs = pl.BlockSpec((TD, D_MODEL), lambda i, s: (i, 0))
        out_shape = jax.ShapeDtypeStruct((NTOK, D_MODEL), F32)
    return pl.pallas_call(
        functools.partial(_combine_kernel, final_norm=final_norm),
        grid_spec=pltpu.PrefetchScalarGridSpec(
            num_scalar_prefetch=1,
            grid=(NTOK // TD,),
            in_specs=in_specs,
            out_specs=out_specs,
            scratch_shapes=[pltpu.VMEM((TOP_K, TD, D_MODEL), F32), pltpu.SemaphoreType.DMA],
        ),
        out_shape=out_shape,
        compiler_params=pltpu.CompilerParams(
            dimension_semantics=("arbitrary",), vmem_limit_bytes=VMEM_LIMIT),
        name="moe_combine",
    )(*args)


def _final_kernel(x_ref, g_ref, o_ref):
    o_ref[...] = _rmsnorm(x_ref[...], g_ref[...])


def _final(x, g, row_off, rows):
    off = row_off // TM
    return pl.pallas_call(
        _final_kernel,
        grid=(rows // TM,),
        in_specs=[
            pl.BlockSpec((TM, D_MODEL), lambda i: (off + i, 0)),
            pl.BlockSpec((1, D_MODEL), lambda i: (0, 0)),
        ],
        out_specs=pl.BlockSpec((TM, D_MODEL), lambda i: (i, 0)),
        out_shape=jax.ShapeDtypeStruct((rows, D_MODEL), F32),
        compiler_params=pltpu.CompilerParams(
            dimension_semantics=("arbitrary",), vmem_limit_bytes=VMEM_LIMIT),
        name="final_norm",
    )(x, g)


def kernel(x_prompt, x_sample, state_mlstm_C, state_mlstm_n, state_mlstm_m, state_ret_S, c, c_ctx,
           norm1_g, norm2_g, norm_f_g, w_ada, b_ada, w_in, b_gates, ret_decay_logit,
           mlstm_norm_g, ret_norm_g, w_out, ffn_w_gate, ffn_w_up, ffn_w_down,
           moe_w_router, moe_w_gate, moe_w_up, moe_w_down):
    xs_in = (x_prompt.reshape(N_CTX, D_MODEL), x_sample.reshape(N_LAT, D_MODEL))
    cvec = jnp.concatenate(
        [c_ctx[None, :], c, jnp.zeros((N_GROUPS - 1 - DEC_BATCH, D_MODEL), F32)], 0)
    mods = _ada(cvec, w_ada, b_ada).reshape(DEPTH, N_GROUPS, 6, D_MODEL)

    n_m = 4 * W_M
    w_ret = w_in[:, :, n_m + N_GATES:]
    n_if = N_GATES // 2
    lane_pad = ((0, 0), (0, 0), (0, LANES - n_if))
    wg = jnp.concatenate([jnp.pad(w_in[:, :, n_m:n_m + n_if], lane_pad),
                          jnp.pad(w_in[:, :, n_m + n_if:n_m + N_GATES], lane_pad)], -1)
    bg = jnp.concatenate([jnp.pad(b_gates[:, None, :n_if], lane_pad),
                          jnp.pad(b_gates[:, None, n_if:], lane_pad)], -1)
    cos_np, sin_np = _rope_tables()
    cos_t, sin_t = jnp.asarray(cos_np), jnp.asarray(sin_np)
    dl = jnp.broadcast_to(ret_decay_logit.reshape(DEPTH, 2 * H_R, 1), (DEPTH, 2 * H_R, LANES))
    m0 = jnp.pad(state_mlstm_m.reshape(DEC_BATCH, DEPTH, 1, 2 * H_M),
                 ((0, 0), (0, 0), (0, 0), (0, LANES - 2 * H_M)))
    g1 = norm1_g.reshape(DEPTH, 1, D_MODEL)
    g2 = norm2_g.reshape(DEPTH, 1, D_MODEL)
    nm = mlstm_norm_g.reshape(DEPTH, 1, W_M)
    nr = ret_norm_g.reshape(DEPTH, 1, W_R)
    n_moe = moe_w_router.shape[0]
    wr_pad = jnp.pad(moe_w_router, ((0, 0), (0, 0), (0, LANES - N_EXPERTS)))
    moe_g = moe_w_gate.reshape(n_moe * N_EXPERTS, D_MODEL, D_FF)
    moe_u = moe_w_up.reshape(n_moe * N_EXPERTS, D_MODEL, D_FF)
    moe_d = moe_w_down.reshape(n_moe * N_EXPERTS, D_FF, D_MODEL)

    states = ()
    xs = xs_in
    for l in range(DEPTH):
        jl = l // 2
        p, gates = _proj(xs, g1, mods, w_in, w_ret, wg, bg, cos_t, sin_t, l)
        y, *states = _scan_ctx(p, gates, dl, nm, nr, l, states)
        y = _scan_lat(p, gates, dl, nm, nr, state_mlstm_C, state_mlstm_n, m0, state_ret_S, y, l)
        if l % 2 == 0:
            x1, h2 = _out(y, xs, w_out, g2, mods, l)
            x = _ffn(h2, x1, ffn_w_gate, ffn_w_up, ffn_w_down, mods, l, jl)
        else:
            x1, h2f, rinfo, cnt = _out(y, xs, w_out, g2, mods, l, wr_pad, jl)
            slots = rinfo[:, R_S1:R_S2 + 1].astype(jnp.int32).reshape(TOP_K * NTOK)
            tile_e, tile_blk, tile_n = _tile_plan(cnt[0, :N_EXPERTS].astype(jnp.int32))
            xd = _dispatch(slots, h2f)
            yd = _gffn(tile_e, tile_blk, tile_n, xd, moe_g, moe_u, moe_d, jl * N_EXPERTS)
            if l == DEPTH - 1:
                y_ctx, y_lat = _combine(slots, x1, rinfo, mods, yd, l, norm_f_g.reshape(1, D_MODEL))
            else:
                x = _combine(slots, x1, rinfo, mods, yd, l)
        xs = (x,)

    if DEPTH % 2 == 1:
        y_ctx = _final(x, norm_f_g.reshape(1, D_MODEL), 0, N_CTX)
        y_lat = _final(x, norm_f_g.reshape(1, D_MODEL), N_CTX, N_LAT)
    y_prompt = y_ctx.reshape(BATCH, SEQ, D_MODEL)
    y_sample = y_lat.reshape(DEC_BATCH, DEC_SEQ, D_MODEL)
    new_C, new_n, new_m, new_S = states
    return (y_prompt, y_sample, new_C, new_n,
            new_m[:, :, 0, :2 * H_M].reshape(BATCH, DEPTH, 2, H_M), new_S)
```

```python
import functools

import numpy as np
import jax
import jax.numpy as jnp
from jax import lax
from jax.experimental import pallas as pl
from jax.experimental.pallas import tpu as pltpu

D_MODEL = 1024
BATCH = 32
SEQ = 256
DEPTH = 2
DEC_BATCH = 2
DEC_SEQ = 1024
GRID_W = 64
H_M = 4
DH = 128
H_R = 4
W_M = H_M * DH
W_R = H_R * DH
N_GATES = 4 * H_M
CHUNK = 128
D_FF = 2816
N_EXPERTS = 8
ROPE_BASE = 10000.0
EPS = 1e-6

N_CTX = BATCH * SEQ
N_LAT = DEC_BATCH * DEC_SEQ
NTOK = N_CTX + N_LAT
N_GROUPS = 8
K_SCALE = DH ** -0.5
P_COLS = 4 * W_M + 4 * W_R
LANES = 128
GATE_LANES = 2 * LANES
VMEM_LIMIT = 56 * 1024 * 1024

F32 = jnp.float32
BF16 = jnp.bfloat16
HIGHEST = lax.Precision.HIGHEST

TM = 1024
TN = 1024
FC = 256
TM_F = 512
TOP_K = 2
TR = 896
REG_TILES = -(-NTOK // TR)
REG = REG_TILES * TR
MAX_TILES = -(-TOP_K * NTOK // TR) + N_EXPERTS
TD = 512


def _group_of_tile(i, tm):
    return jnp.maximum(i * tm // DEC_SEQ - (N_CTX // DEC_SEQ - 1), 0)


def _silu(x):
    return x * jax.nn.sigmoid(x)


def _log_sigmoid(x):
    return jnp.minimum(x, 0.0) - jnp.log(1.0 + jnp.exp(-jnp.abs(x)))


def _rmsnorm(x, g):
    return x * lax.rsqrt(jnp.mean(x * x, -1, keepdims=True) + EPS) * g


def _ada_kernel(cv_ref, w_ref, b_ref, o_ref):
    s = _silu(cv_ref[...]).astype(BF16)
    o_ref[0] = jnp.dot(s, w_ref[0].astype(BF16), preferred_element_type=F32) + b_ref[0]


def _ada(cvec, w_ada, b_ada):
    tn = 1536
    n = 6 * D_MODEL
    return pl.pallas_call(
        _ada_kernel,
        grid=(DEPTH, n // tn),
        in_specs=[
            pl.BlockSpec((N_GROUPS, D_MODEL), lambda l, j: (0, 0)),
            pl.BlockSpec((1, D_MODEL, tn), lambda l, j: (l, 0, j)),
            pl.BlockSpec((1, 1, tn), lambda l, j: (l, 0, j)),
        ],
        out_specs=pl.BlockSpec((1, N_GROUPS, tn), lambda l, j: (l, 0, j)),
        out_shape=jax.ShapeDtypeStruct((DEPTH, N_GROUPS, n), F32),
        compiler_params=pltpu.CompilerParams(
            dimension_semantics=("arbitrary", "arbitrary"), vmem_limit_bytes=VMEM_LIMIT),
        name="ada",
    )(cvec, w_ada, b_ada.reshape(DEPTH, 1, n))


def _rope_tables():
    half = DH // 4
    freqs = ROPE_BASE ** (-np.arange(half, dtype=np.float64) / half)
    t = np.arange(DEC_SEQ)
    pos = np.stack([t // GRID_W, t % GRID_W], 1).astype(np.float64)
    d = np.arange(DH)
    ang = pos[:, d // (DH // 2)] * freqs[d % half][None, :]
    sign = np.where((d % (DH // 2)) < half, -1.0, 1.0)[None, :]
    return np.cos(ang).astype(np.float32), (sign * np.sin(ang)).astype(np.float32)


def _rope(a, cos, sin):
    lane = lax.broadcasted_iota(jnp.int32, a.shape, 1)
    first = (lane % (DH // 2)) < (DH // 4)
    partner = jnp.where(first, pltpu.roll(a, DH - DH // 4, 1), pltpu.roll(a, DH // 4, 1))
    return a * cos + partner * sin


def _proj_kernel(*refs, n_ctx_tiles, split_x):
    if split_x:
        (xp_ref, xl_ref, g_ref, mod_ref, wa_ref, wb_ref, wg_ref, bg_ref, cos_ref, sin_ref,
         p_ref, gate_ref, h_scr, w_res) = refs
    else:
        (x_ref, g_ref, mod_ref, wa_ref, wb_ref, wg_ref, bg_ref, cos_ref, sin_ref,
         p_ref, gate_ref, h_scr, w_res) = refs
    p_ref = p_ref.at[0]
    i = pl.program_id(0)
    j = pl.program_id(1)
    is_lat = i >= n_ctx_tiles
    half = TN // 2

    def prologue(x):
        h = _rmsnorm(x, g_ref[0]) * (1.0 + mod_ref[0, 0, 1:2, :]) + mod_ref[0, 0, 0:1, :]
        h_scr[...] = h.astype(BF16)
        gate_ref[...] = _dot_f32x3(h, wg_ref[0]) + bg_ref[0]

    @pl.when(j == 0)
    def _():
        if split_x:
            pl.when(jnp.logical_not(is_lat))(lambda: prologue(xp_ref[...]))
            pl.when(is_lat)(lambda: prologue(xl_ref[...]))
        else:
            prologue(x_ref[...])

    def matmul(w_ref, jj):
        @pl.when(i == 0)
        def _():
            w_res[jj] = w_ref[0].astype(BF16)

        return jnp.dot(h_scr[...], w_res[jj], preferred_element_type=F32)

    @pl.when(j == 0)
    def _():
        acc = matmul(wa_ref, 0)
        p_ref[:, :half] = acc[:, :half].astype(BF16)
        p_ref[:, half:] = (acc[:, half:] * K_SCALE).astype(BF16)

    @pl.when(j == 1)
    def _():
        p_ref[...] = matmul(wa_ref, 1).astype(BF16)

    @pl.when(j == 2)
    def _():
        acc = matmul(wb_ref, 2)

        @pl.when(is_lat)
        def _():
            cos = cos_ref[...]
            sin = sin_ref[...]
            for hd in range(TN // DH):
                sl = slice(hd * DH, (hd + 1) * DH)
                r = _rope(acc[:, sl], cos, sin)
                p_ref[:, sl] = (r * K_SCALE if hd * DH >= half else r).astype(BF16)

        @pl.when(jnp.logical_not(is_lat))
        def _():
            p_ref[:, :half] = acc[:, :half].astype(BF16)
            p_ref[:, half:] = (acc[:, half:] * K_SCALE).astype(BF16)

    @pl.when(j == 3)
    def _():
        p_ref[...] = matmul(wb_ref, 3).astype(BF16)


def _proj(xs, g1, mods, w_in, w_ret, wg, bg, cos_t, sin_t, layer):
    n_ctx_tiles = N_CTX // TM
    tiles_per_seq = DEC_SEQ // TM
    split_x = len(xs) == 2
    if split_x:
        x_specs = [pl.BlockSpec((TM, D_MODEL), lambda i, j: (jnp.minimum(i, n_ctx_tiles - 1), 0)),
                   pl.BlockSpec((TM, D_MODEL), lambda i, j: (jnp.maximum(i - n_ctx_tiles, 0), 0))]
    else:
        x_specs = [pl.BlockSpec((TM, D_MODEL), lambda i, j: (i, 0))]
    n_a = 4 * W_M // TN
    return pl.pallas_call(
        functools.partial(_proj_kernel, n_ctx_tiles=n_ctx_tiles, split_x=split_x),
        grid=(NTOK // TM, P_COLS // TN),
        in_specs=x_specs + [
            pl.BlockSpec((1, 1, D_MODEL), lambda i, j: (layer, 0, 0)),
            pl.BlockSpec((1, 1, 6, D_MODEL), lambda i, j: (layer, _group_of_tile(i, TM), 0, 0)),
            pl.BlockSpec((1, D_MODEL, TN),
                         lambda i, j: (layer, 0, jnp.where(i == 0, jnp.minimum(j, n_a - 1), n_a - 1)),
                         pipeline_mode=pl.Buffered(1)),
            pl.BlockSpec((1, D_MODEL, TN),
                         lambda i, j: (layer, 0, jnp.where(i == 0, jnp.maximum(j - n_a, 0), n_a - 1)),
                         pipeline_mode=pl.Buffered(1)),
            pl.BlockSpec((1, D_MODEL, GATE_LANES), lambda i, j: (layer, 0, 0)),
            pl.BlockSpec((1, 1, GATE_LANES), lambda i, j: (layer, 0, 0)),
            pl.BlockSpec((TM, DH), lambda i, j: (i % tiles_per_seq, 0)),
            pl.BlockSpec((TM, DH), lambda i, j: (i % tiles_per_seq, 0)),
        ],
        out_specs=[
            pl.BlockSpec((1, TM, TN), lambda i, j: (j, i, 0)),
            pl.BlockSpec((TM, GATE_LANES), lambda i, j: (i, 0)),
        ],
        out_shape=[
            jax.ShapeDtypeStruct((P_COLS // TN, NTOK, TN), BF16),
            jax.ShapeDtypeStruct((NTOK, GATE_LANES), F32),
        ],
        scratch_shapes=[pltpu.VMEM((TM, D_MODEL), BF16),
                        pltpu.VMEM((P_COLS // TN, D_MODEL, TN), BF16)],
        compiler_params=pltpu.CompilerParams(
            dimension_semantics=("arbitrary", "arbitrary"), vmem_limit_bytes=VMEM_LIMIT),
        name="proj",
    )(*xs, g1, mods, w_in, w_ret, wg, bg, cos_t, sin_t)


def _split3(x):
    hi = x.astype(BF16)
    r1 = x - hi.astype(F32)
    mid = r1.astype(BF16)
    lo = (r1 - mid.astype(F32)).astype(BF16)
    return hi, mid, lo


def _dot(a, b):
    return jnp.dot(a, b, preferred_element_type=F32)


def _dot_nt(a, b):
    return lax.dot_general(a, b, (((1,), (1,)), ((), ())), preferred_element_type=F32)


def _tri_dot_left(tri, x):
    hi, mid, lo = _split3(x)
    return _dot(tri, hi) + _dot(tri, mid) + _dot(tri, lo)


def _tri_dot_right(x, tri):
    hi, mid, lo = _split3(x)
    return _dot(hi, tri) + _dot(mid, tri) + _dot(lo, tri)


def _run_max(x, reverse):
    n_tiles = x.shape[0] // 8
    sub = lax.broadcasted_iota(jnp.int32, (8, LANES), 0)
    out = [None] * n_tiles
    carry = None
    for t in (range(n_tiles - 1, -1, -1) if reverse else range(n_tiles)):
        v = x[8 * t:8 * t + 8, :]
        for s in (1, 2, 4):
            if reverse:
                v = jnp.maximum(v, jnp.where(sub < 8 - s, pltpu.roll(v, 8 - s, 0), -jnp.inf))
            else:
                v = jnp.maximum(v, jnp.where(sub >= s, pltpu.roll(v, s, 0), -jnp.inf))
        if carry is not None:
            v = jnp.maximum(v, carry)
        carry = jnp.broadcast_to(v[0:1, :] if reverse else v[7:8, :], (8, LANES))
        out[t] = v
    return jnp.concatenate(out, axis=0)


def _scan_kernel(*refs, T, has_state, n_prev=0):
    if has_state:
        (p_ref, g_ref, dl_ref, nm_ref, nr_ref, C0_ref, n0_ref, m0_ref, S0_ref, _yprev_ref,
         y_ref, CN_s, S_s, m_s, hf_s, hb_s, dm_s, dq_s, dk_s, dL_s, kT_s) = refs
    else:
        p_ref, g_ref, dl_ref, nm_ref, nr_ref = refs[:5]
        prev_refs = refs[5:9] if n_prev else ()
        (y_ref, C_out, n_out, m_out, S_out,
         CN_s, S_s, m_s, hf_s, hb_s, dm_s, dq_s, dk_s, dL_s, kT_s) = refs[5 + len(prev_refs):]
    L = CHUNK
    n_chunks = T // L
    row_i = lax.broadcasted_iota(jnp.int32, (L, L), 0)
    col_j = lax.broadcasted_iota(jnp.int32, (L, L), 1)
    lower = col_j <= row_i
    upper = col_j >= row_i
    tril = lower.astype(BF16)
    triu = upper.astype(BF16)
    ones = jnp.ones((L, DH), BF16)
    c_km = W_M
    c_vm = 2 * W_M
    c_om = 3 * W_M
    c_qr = 4 * W_M
    c_kr = c_qr + W_R
    c_vr = c_qr + 2 * W_R
    c_gr = c_qr + 3 * W_R

    def pcols(rows, col):
        return p_ref[col // TN, rows, col % TN:col % TN + DH]

    for d in range(2):
        for h in range(H_M):
            k = d * H_M + h
            if has_state:
                CN_s[k, :, :DH] = C0_ref[0, 0, d, h]
                CN_s[k, :, DH:] = jnp.broadcast_to(n0_ref[0, 0, d, h:h + 1, :], (DH, DH)).T
                S_s[k] = S0_ref[0, 0, d, h]
            else:
                CN_s[k] = jnp.zeros((DH, 2 * DH), F32)
                S_s[k] = jnp.zeros((DH, DH), F32)
    m_s[...] = m0_ref[0, 0] if has_state else jnp.zeros((1, LANES), F32)

    @pl.when(pl.program_id(0) == 0)
    def _():
        pos_i = row_i.astype(F32)
        pos_j = col_j.astype(F32)
        for d in range(2):
            for h in range(H_R):
                k = d * H_R + h
                lg_row = _log_sigmoid(dl_ref[0, k:k + 1, :])
                lg = jnp.broadcast_to(lg_row, (L, L))
                rel = (row_i - col_j if d == 0 else col_j - row_i).astype(F32)
                dm_s[k] = jnp.where(rel >= 0, jnp.exp(lg * jnp.maximum(rel, 0.0)), 0.0)
                dq_s[k] = jnp.exp(lg * (pos_i + 1.0 if d == 0 else L - pos_i))
                dk_s[k] = jnp.exp(lg * (L - 1.0 - pos_j if d == 0 else pos_j))
                dL_s[k] = jnp.exp(lg_row * float(L))

    def transpose_keys(c, carry):
        r0 = pl.multiple_of(c * L, L)
        for h in range(H_M):
            kT_s[h, c] = pcols(pl.ds(r0, L), c_km + h * DH).astype(F32).T
            kT_s[H_M + h, c] = pcols(pl.ds(r0, L), c_kr + h * DH).astype(F32).T
        return carry

    lax.fori_loop(0, n_chunks, transpose_keys, 0)

    def chunk_step(c, carry):
        m_prev = m_s[...]
        m_new = []
        prep = []
        for d in range(2):
            ci = c if d == 0 else n_chunks - 1 - c
            r0 = pl.multiple_of(ci * L, L)
            mask = lower if d == 0 else upper
            e_row = L - 1 if d == 0 else 0
            FL = _log_sigmoid(g_ref[pl.ds(r0, L), LANES:2 * LANES])
            Bc = _tri_dot_left(tril if d == 0 else triu, FL)
            Zc = g_ref[pl.ds(r0, L), 0:LANES] - Bc
            M = jnp.maximum(_run_max(Zc, reverse=(d == 1)), m_prev)
            m_row = Bc + M
            M_end = M[e_row:e_row + 1, :]
            m_new.append(Bc[e_row:e_row + 1, :] + M_end)
            decay = jnp.exp(m_prev - M_end)
            prep.append(dict(ci=ci, r0=r0, mask=mask, M=M, m_row=m_row, decay=decay,
                             ZT=Zc.T,
                             WT=jnp.exp(Zc - M_end).T))
        pairs = [(d, h) for d in range(2) for h in range(H_M)]

        def rows(d, col):
            return pcols(pl.ds(prep[d]["r0"], L), col)

        qk, qkr = {}, {}
        for d, h in pairs:
            qk[d, h] = _dot_nt(rows(d, h * DH), rows(d, c_km + h * DH))
            qkr[d, h] = _dot_nt(rows(d, c_qr + h * DH), rows(d, c_kr + h * DH))
        upd, updr = {}, {}
        for d, h in pairs:
            k = d * H_M + h
            ci = prep[d]["ci"]
            vo = jnp.concatenate([rows(d, c_vm + h * DH), ones], axis=1)
            wkT = (kT_s[h, ci] * jnp.broadcast_to(prep[d]["WT"][k:k + 1, :], (DH, L))).astype(BF16)
            upd[d, h] = _dot(wkT, vo)
            kdT = (kT_s[H_M + h, ci] * dk_s[k]).astype(BF16)
            updr[d, h] = _dot(kdT, rows(d, c_vr + h * DH))
        for d, h in pairs:
            k = d * H_M + h
            r0 = prep[d]["r0"]
            h_dst = hf_s if d == 0 else hb_s
            q = rows(d, h * DH)
            M_col = jnp.broadcast_to(prep[d]["M"][:, k:k + 1], (L, L))
            z_row = jnp.broadcast_to(prep[d]["ZT"][k:k + 1, :], (L, L))
            D = jnp.where(prep[d]["mask"], jnp.exp(z_row - M_col), 0.0)
            s = (qk[d, h] * D).astype(BF16)
            w_inter = jnp.exp(jnp.broadcast_to(m_prev[:, k:k + 1], (L, L)) - M_col)
            wq = (w_inter * q.astype(F32)).astype(BF16)
            vo = jnp.concatenate([rows(d, c_vm + h * DH), ones], axis=1)
            CN = CN_s[k]
            res = _dot(jnp.concatenate([s, wq], axis=1),
                       jnp.concatenate([vo, CN.astype(BF16)], axis=0))
            floor = jnp.exp(-jnp.broadcast_to(prep[d]["m_row"][:, k:k + 1], (L, L)))
            h_dst[pl.ds(r0, L), h * DH:(h + 1) * DH] = res[:, :DH] / jnp.maximum(jnp.abs(res[:, DH:]), floor)
            CN_s[k] = jnp.broadcast_to(prep[d]["decay"][:, k:k + 1], (DH, 2 * DH)) * CN + upd[d, h]
            qr = rows(d, c_qr + h * DH)
            S = S_s[k]
            sr = (qkr[d, h] * dm_s[k]).astype(BF16)
            qd = (qr.astype(F32) * dq_s[k]).astype(BF16)
            h_dst[pl.ds(r0, L), W_M + h * DH:W_M + (h + 1) * DH] = _dot(
                jnp.concatenate([sr, qd], axis=1),
                jnp.concatenate([rows(d, c_vr + h * DH), S.astype(BF16)], axis=0))
            S_s[k] = dL_s[k] * S + updr[d, h]
        lane = lax.broadcasted_iota(jnp.int32, (1, LANES), 1)
        m_s[...] = jnp.where(lane < H_M, m_new[0], m_new[1])
        return carry

    lax.fori_loop(0, n_chunks, chunk_step, 0)

    for h in range(H_M):
        sl = slice(h * DH, (h + 1) * DH)
        hs = hf_s[:, sl] + hb_s[:, sl]
        yn = _rmsnorm(hs, nm_ref[0, :, sl])
        om = pcols(slice(None), c_om + h * DH).astype(F32)
        y_ref[:, sl] = (jax.nn.sigmoid(om) * yn).astype(BF16)
        slr = slice(W_M + h * DH, W_M + (h + 1) * DH)
        hr = hf_s[:, slr] + hb_s[:, slr]
        ynr = _rmsnorm(hr, nr_ref[0, :, sl])
        gr = pcols(slice(None), c_gr + h * DH).astype(F32)
        y_ref[:, slr] = (_silu(gr) * ynr).astype(BF16)

    if not has_state:
        for prev, out in zip(prev_refs, (C_out, n_out, m_out, S_out)):
            out[0, :n_prev] = prev[0]
        for d in range(2):
            for h in range(H_M):
                k = d * H_M + h
                C_out[0, n_prev, d, h] = CN_s[k, :, :DH]
                n_out[0, n_prev, d, h:h + 1, :] = CN_s[k, :, DH:].T[0:1, :]
                S_out[0, n_prev, d, h] = S_s[k]
        m_out[0, n_prev] = m_s[...]


def _scan_scratch(T):
    return [
        pltpu.VMEM((2 * H_M, DH, 2 * DH), F32),
        pltpu.VMEM((2 * H_R, DH, DH), F32),
        pltpu.VMEM((1, LANES), F32),
        pltpu.VMEM((T, W_M + W_R), F32),
        pltpu.VMEM((T, W_M + W_R), F32),
        pltpu.VMEM((2 * H_R, CHUNK, CHUNK), F32),
        pltpu.VMEM((2 * H_R, CHUNK, CHUNK), F32),
        pltpu.VMEM((2 * H_R, CHUNK, CHUNK), F32),
        pltpu.VMEM((2 * H_R, 1, LANES), F32),
        pltpu.VMEM((H_M + H_R, T // CHUNK, DH, CHUNK), F32),
    ]


def _scan_ctx(p, gates, dl, nm, nr, layer, prev_states=()):
    T = SEQ
    n_lay = layer + 1
    state_tails = [(2, H_M, DH, DH), (2, H_M, DH), (1, LANES), (2, H_R, DH, DH)]

    def state_spec(n, tail):
        return pl.BlockSpec((1, n) + tail, lambda b: (b,) + (0,) * (1 + len(tail)))
    common = [
        pl.BlockSpec((P_COLS // TN, T, TN), lambda b: (0, b, 0)),
        pl.BlockSpec((T, GATE_LANES), lambda b: (b, 0)),
        pl.BlockSpec((1, 2 * H_R, LANES), lambda b: (layer, 0, 0)),
        pl.BlockSpec((1, 1, W_M), lambda b: (layer, 0, 0)),
        pl.BlockSpec((1, 1, W_R), lambda b: (layer, 0, 0)),
    ]
    return pl.pallas_call(
        functools.partial(_scan_kernel, T=T, has_state=False, n_prev=layer if prev_states else 0),
        grid=(BATCH,),
        in_specs=common + [state_spec(layer, tail) for tail in state_tails[:len(prev_states)]],
        out_specs=[pl.BlockSpec((T, D_MODEL), lambda b: (b, 0))] + [state_spec(n_lay, tail) for tail in state_tails],
        out_shape=[jax.ShapeDtypeStruct((NTOK, D_MODEL), BF16)] + [
            jax.ShapeDtypeStruct((BATCH, n_lay) + tail, F32) for tail in state_tails],
        scratch_shapes=_scan_scratch(T),
        compiler_params=pltpu.CompilerParams(
            dimension_semantics=("arbitrary",), vmem_limit_bytes=VMEM_LIMIT),
        name="scan_ctx",
    )(p, gates, dl, nm, nr, *prev_states)


def _scan_lat(p, gates, dl, nm, nr, C0, n0, m0, S0, y_prev, layer):
    T = DEC_SEQ
    off = N_CTX // T
    in_specs = [
        pl.BlockSpec((P_COLS // TN, T, TN), lambda b: (0, off + b, 0)),
        pl.BlockSpec((T, GATE_LANES), lambda b: (off + b, 0)),
        pl.BlockSpec((1, 2 * H_R, LANES), lambda b: (layer, 0, 0)),
        pl.BlockSpec((1, 1, W_M), lambda b: (layer, 0, 0)),
        pl.BlockSpec((1, 1, W_R), lambda b: (layer, 0, 0)),
        pl.BlockSpec((1, 1, 2, H_M, DH, DH), lambda b: (b, layer, 0, 0, 0, 0)),
        pl.BlockSpec((1, 1, 2, H_M, DH), lambda b: (b, layer, 0, 0, 0)),
        pl.BlockSpec((1, 1, 1, LANES), lambda b: (b, layer, 0, 0)),
        pl.BlockSpec((1, 1, 2, H_R, DH, DH), lambda b: (b, layer, 0, 0, 0, 0)),
        pl.BlockSpec(memory_space=pl.ANY),
    ]
    return pl.pallas_call(
        functools.partial(_scan_kernel, T=T, has_state=True),
        grid=(DEC_BATCH,),
        in_specs=in_specs,
        out_specs=pl.BlockSpec((T, D_MODEL), lambda b: (off + b, 0)),
        out_shape=jax.ShapeDtypeStruct((NTOK, D_MODEL), BF16),
        input_output_aliases={9: 0},
        scratch_shapes=_scan_scratch(T),
        compiler_params=pltpu.CompilerParams(
            dimension_semantics=("arbitrary",), vmem_limit_bytes=VMEM_LIMIT),
        name="scan_lat",
    )(p, gates, dl, nm, nr, C0, n0, m0, S0, y_prev)


def _top2(logits):
    lane = lax.broadcasted_iota(jnp.int32, logits.shape, 1)
    v1 = jnp.max(logits, -1, keepdims=True)
    i1 = jnp.min(jnp.where(logits == v1, lane, LANES), -1, keepdims=True)
    rest = jnp.where(lane == i1, -jnp.inf, logits)
    v2 = jnp.max(rest, -1, keepdims=True)
    i2 = jnp.min(jnp.where(rest == v2, lane, LANES), -1, keepdims=True)
    e2 = jnp.exp(v2 - v1)
    return i1, i2, 1.0 / (1.0 + e2), e2 / (1.0 + e2)


def _split2(x):
    hi = x.astype(BF16)
    return hi, (x - hi.astype(F32)).astype(BF16)


def _dot_f32x3(a, b):
    a_hi, a_lo = _split2(a)
    b_hi, b_lo = _split2(b)
    return _dot(a_hi, b_hi) + _dot(a_hi, b_lo) + _dot(a_lo, b_hi)


R_E1, R_E2, R_W1, R_W2, R_S1, R_S2 = range(6)


def _out_kernel(*refs, with_router, split_x):
    y_ref = refs[0]
    if split_x:
        xp_ref, xl_ref = refs[1:3]
        x_in = jnp.where(pl.program_id(0) >= N_CTX // TM, xl_ref[...], xp_ref[...])
    else:
        x_in = refs[1][...]
    refs = refs[3:] if split_x else refs[2:]
    if with_router:
        (w_ref, g_ref, mod_ref, wr_ref,
         x1_ref, h2_ref, rinfo_ref, cnt_ref, w_scr, tri_scr, cnt_scr) = refs
    else:
        w_ref, g_ref, mod_ref, x1_ref, h2_ref, w_scr = refs

    @pl.when(pl.program_id(0) == 0)
    def _():
        w_scr[...] = w_ref[0].astype(BF16)
        if with_router:
            r = lax.broadcasted_iota(jnp.int32, (TM, TM), 0)
            c = lax.broadcasted_iota(jnp.int32, (TM, TM), 1)
            tri_scr[...] = (c < r).astype(BF16)
            cnt_scr[...] = jnp.zeros_like(cnt_scr)

    o = jnp.dot(y_ref[...], w_scr[...], preferred_element_type=F32)
    x1 = x_in + mod_ref[0, 0, 2:3, :] * o
    x1_ref[...] = x1
    h2 = _rmsnorm(x1, g_ref[0]) * (1.0 + mod_ref[0, 0, 4:5, :]) + mod_ref[0, 0, 3:4, :]
    if not with_router:
        h2_ref[...] = h2.astype(BF16)
    else:
        h2_ref[...] = h2
        logits = _dot_f32x3(h2, wr_ref[0])
        lane = lax.broadcasted_iota(jnp.int32, logits.shape, 1)
        i1, i2, w1, w2 = _top2(jnp.where(lane < N_EXPERTS, logits, -jnp.inf))
        oh1 = lane == i1
        oh2 = lane == i2
        sel = jnp.where(oh1 | oh2, 1.0, 0.0)
        rank = _dot(tri_scr[...], sel.astype(BF16)) + cnt_scr[...]
        r1 = jnp.sum(jnp.where(oh1, rank, 0.0), -1, keepdims=True)
        r2 = jnp.sum(jnp.where(oh2, rank, 0.0), -1, keepdims=True)
        s1 = i1.astype(F32) * float(REG) + r1
        s2 = i2.astype(F32) * float(REG) + r2
        info = jnp.zeros(logits.shape, F32)
        for col, val in ((R_E1, i1.astype(F32)), (R_E2, i2.astype(F32)), (R_W1, w1), (R_W2, w2),
                         (R_S1, s1), (R_S2, s2)):
            info = jnp.where(lane == col, val, info)
        rinfo_ref[...] = info
        cnt_scr[...] += jnp.sum(sel, 0, keepdims=True)
        cnt_ref[...] = cnt_scr[...]


def _out(y, xs, w_out, g2, mods, layer, w_router_pad=None, router_idx=0):
    with_router = w_router_pad is not None
    split_x = len(xs) == 2
    n_ctx_tiles = N_CTX // TM
    if split_x:
        x_specs = [pl.BlockSpec((TM, D_MODEL), lambda i: (jnp.minimum(i, n_ctx_tiles - 1), 0)),
                   pl.BlockSpec((TM, D_MODEL), lambda i: (jnp.maximum(i - n_ctx_tiles, 0), 0))]
    else:
        x_specs = [pl.BlockSpec((TM, D_MODEL), lambda i: (i, 0))]
    in_specs = [pl.BlockSpec((TM, D_MODEL), lambda i: (i, 0))] + x_specs + [
        pl.BlockSpec((1, D_MODEL, D_MODEL), lambda i: (layer, 0, 0)),
        pl.BlockSpec((1, 1, D_MODEL), lambda i: (layer, 0, 0)),
        pl.BlockSpec((1, 1, 6, D_MODEL), lambda i: (layer, _group_of_tile(i, TM), 0, 0)),
    ]
    out_specs = [
        pl.BlockSpec((TM, D_MODEL), lambda i: (i, 0)),
        pl.BlockSpec((TM, D_MODEL), lambda i: (i, 0)),
    ]
    out_shape = [
        jax.ShapeDtypeStruct((NTOK, D_MODEL), F32),
        jax.ShapeDtypeStruct((NTOK, D_MODEL), F32 if with_router else BF16),
    ]
    args = [y, *xs, w_out, g2, mods]
    scratch = [pltpu.VMEM((D_MODEL, D_MODEL), BF16)]
    if with_router:
        in_specs.append(pl.BlockSpec((1, D_MODEL, LANES), lambda i: (router_idx, 0, 0)))
        out_specs += [pl.BlockSpec((TM, LANES), lambda i: (i, 0)),
                      pl.BlockSpec((1, LANES), lambda i: (0, 0))]
        out_shape += [jax.ShapeDtypeStruct((NTOK, LANES), F32),
                      jax.ShapeDtypeStruct((1, LANES), F32)]
        args.append(w_router_pad)
        scratch += [pltpu.VMEM((TM, TM), BF16), pltpu.VMEM((1, LANES), F32)]
    return pl.pallas_call(
        functools.partial(_out_kernel, with_router=with_router, split_x=split_x),
        grid=(NTOK // TM,),
        in_specs=in_specs,
        out_specs=out_specs,
        out_shape=out_shape,
        scratch_shapes=scratch,
        compiler_params=pltpu.CompilerParams(
            dimension_semantics=("arbitrary",), vmem_limit_bytes=VMEM_LIMIT),
        name="out_router" if with_router else "out",
    )(*args)


N_FC = D_FF // FC
def _swiglu_tile(h, w_gu, w_d, acc, fetch=None):
    if fetch is not None:
        wg_hbm, wu_hbm, wd_hbm, stg_gu, stg_d, sem = fetch

        def copies(f):
            s = f % 2
            cols = pl.ds(f * FC, FC)
            return (pltpu.make_async_copy(wg_hbm.at[:, cols], stg_gu.at[s, 0], sem.at[s, 0]),
                    pltpu.make_async_copy(wu_hbm.at[:, cols], stg_gu.at[s, 1], sem.at[s, 1]),
                    pltpu.make_async_copy(wd_hbm.at[cols, :], stg_d.at[s], sem.at[s, 2]))

        def start(f):
            for c in copies(f):
                c.start()

        def land(f):
            for c in copies(f):
                c.wait()
            s = f % 2
            w_gu[f, :, :FC] = stg_gu[s, 0].astype(BF16)
            w_gu[f, :, FC:] = stg_gu[s, 1].astype(BF16)
            w_d[f * FC:(f + 1) * FC, :] = stg_d[s].astype(BF16)
    else:
        start = land = lambda f: None

    def up(f):
        return jnp.dot(h, w_gu[f], preferred_element_type=F32)

    start(0)
    if N_FC > 1:
        start(1)
    land(0)
    ab = up(0)
    for f in range(N_FC):
        if f + 2 < N_FC:
            start(f + 2)
        if f + 1 < N_FC:
            land(f + 1)
            ab_next = up(f + 1)
        t = (_silu(ab[:, :FC]) * ab[:, FC:]).astype(BF16)
        contrib = jnp.dot(t, w_d[f * FC:(f + 1) * FC, :], preferred_element_type=F32)
        if f == 0:
            acc[...] = contrib
        else:
            acc[...] += contrib
        if f + 1 < N_FC:
            ab = ab_next


def _ffn_weight_scratch():
    return [
        pltpu.VMEM((N_FC, D_MODEL, 2 * FC), BF16),
        pltpu.VMEM((D_FF, D_MODEL), BF16),
        pltpu.VMEM((2, 2, D_MODEL, FC), F32),
        pltpu.VMEM((2, FC, D_MODEL), F32),
        pltpu.SemaphoreType.DMA((2, 3)),
    ]


def _ffn_kernel(h_ref, res_ref, wg_ref, wu_ref, wd_ref, mod_ref, o_ref,
                acc, w_gu, w_d, stg_gu, stg_d, sem, *, w_idx):
    first = pl.program_id(0) == 0

    @pl.when(first)
    def _():
        _swiglu_tile(h_ref[...], w_gu, w_d, acc,
                     (wg_ref.at[w_idx], wu_ref.at[w_idx], wd_ref.at[w_idx], stg_gu, stg_d, sem))

    @pl.when(jnp.logical_not(first))
    def _():
        _swiglu_tile(h_ref[...], w_gu, w_d, acc)

    o_ref[...] = res_ref[...] + mod_ref[0, 0, 5:6, :] * acc[...]


def _ffn(h2, res, wg, wu, wd, mods, layer, w_idx):
    return pl.pallas_call(
        functools.partial(_ffn_kernel, w_idx=w_idx),
        grid=(NTOK // TM_F,),
        in_specs=[
            pl.BlockSpec((TM_F, D_MODEL), lambda i: (i, 0)),
            pl.BlockSpec((TM_F, D_MODEL), lambda i: (i, 0)),
            pl.BlockSpec(memory_space=pl.ANY),
            pl.BlockSpec(memory_space=pl.ANY),
            pl.BlockSpec(memory_space=pl.ANY),
            pl.BlockSpec((1, 1, 6, D_MODEL), lambda i: (layer, _group_of_tile(i, TM_F), 0, 0)),
        ],
        out_specs=pl.BlockSpec((TM_F, D_MODEL), lambda i: (i, 0)),
        out_shape=jax.ShapeDtypeStruct((NTOK, D_MODEL), F32),
        scratch_shapes=[pltpu.VMEM((TM_F, D_MODEL), F32)] + _ffn_weight_scratch(),
        compiler_params=pltpu.CompilerParams(
            dimension_semantics=("arbitrary",), vmem_limit_bytes=VMEM_LIMIT),
        name="ffn",
    )(h2, res, wg, wu, wd, mods)


def _tile_plan(counts):
    nt = (counts + TR - 1) // TR
    cum = jnp.cumsum(nt)
    total = cum[-1]
    t = jnp.arange(MAX_TILES, dtype=jnp.int32)
    tt = jnp.minimum(t, total - 1)
    e = jnp.sum((cum[None, :] <= tt[:, None]).astype(jnp.int32), axis=1)
    k = tt - (cum - nt)[e]
    n = jnp.where(t < total, jnp.clip(counts[e] - k * TR, 0, TR), 0)
    return e.astype(jnp.int32), (e * REG_TILES + k).astype(jnp.int32), n.astype(jnp.int32)


def _row_copy(src, src_row, dst, dst_row, sem):
    return pltpu.make_async_copy(src.at[pl.ds(src_row, 1)], dst.at[pl.ds(dst_row, 1)], sem)


def _dispatch_kernel(slot_ref, h_ref, xs_ref, sem):
    base = pl.program_id(0) * (TOP_K * TD)

    def issue(r, carry):
        for k in range(TOP_K):
            _row_copy(h_ref, r, xs_ref, slot_ref[base + TOP_K * r + k], sem).start()
        return carry

    lax.fori_loop(0, TD, issue, 0, unroll=8)
    for k in range(TOP_K):
        pltpu.make_async_copy(h_ref, xs_ref.at[pl.ds(0, TD)], sem).wait()


def _dispatch(slots, h2f):
    return pl.pallas_call(
        _dispatch_kernel,
        grid_spec=pltpu.PrefetchScalarGridSpec(
            num_scalar_prefetch=1,
            grid=(NTOK // TD,),
            in_specs=[pl.BlockSpec((TD, D_MODEL), lambda i, s: (i, 0))],
            out_specs=pl.BlockSpec(memory_space=pl.ANY),
            scratch_shapes=[pltpu.SemaphoreType.DMA],
        ),
        out_shape=jax.ShapeDtypeStruct((N_EXPERTS * REG, D_MODEL), F32),
        compiler_params=pltpu.CompilerParams(
            dimension_semantics=("arbitrary",), vmem_limit_bytes=VMEM_LIMIT),
        name="moe_dispatch",
    )(slots, h2f)


def _gffn_kernel(te_ref, tb_ref, tn_ref, x_ref, wg_ref, wu_ref, wd_ref, o_ref,
                 acc, w_gu, w_d, stg_gu, stg_d, sem, *, w_base):
    t = pl.program_id(0)
    n = tn_ref[t]

    @pl.when(n > 0)
    def _():
        row = lax.broadcasted_iota(jnp.int32, (TR, D_MODEL), 0)
        h = jnp.where(row < n, x_ref[...], 0.0).astype(BF16)
        first = tb_ref[t] % REG_TILES == 0

        @pl.when(first)
        def _():
            e = w_base + te_ref[t]
            _swiglu_tile(h, w_gu, w_d, acc, (wg_ref.at[e], wu_ref.at[e], wd_ref.at[e], stg_gu, stg_d, sem))

        @pl.when(jnp.logical_not(first))
        def _():
            _swiglu_tile(h, w_gu, w_d, acc)

        o_ref[...] = acc[...]


def _gffn(tile_e, tile_blk, tile_n, xs, wg, wu, wd, w_base):
    return pl.pallas_call(
        functools.partial(_gffn_kernel, w_base=w_base),
        grid_spec=pltpu.PrefetchScalarGridSpec(
            num_scalar_prefetch=3,
            grid=(MAX_TILES,),
            in_specs=[
                pl.BlockSpec((TR, D_MODEL), lambda t, te, tb, tn: (tb[t], 0)),
                pl.BlockSpec(memory_space=pl.ANY),
                pl.BlockSpec(memory_space=pl.ANY),
                pl.BlockSpec(memory_space=pl.ANY),
            ],
            out_specs=pl.BlockSpec((TR, D_MODEL), lambda t, te, tb, tn: (tb[t], 0)),
            scratch_shapes=[pltpu.VMEM((TR, D_MODEL), F32)] + _ffn_weight_scratch(),
        ),
        out_shape=jax.ShapeDtypeStruct((N_EXPERTS * REG, D_MODEL), F32),
        compiler_params=pltpu.CompilerParams(
            dimension_semantics=("arbitrary",), vmem_limit_bytes=VMEM_LIMIT),
        name="moe_ffn",
    )(tile_e, tile_blk, tile_n, xs, wg, wu, wd)


def _combine_kernel(slot_ref, x1_ref, rinfo_ref, mod_ref, ys_ref, *rest, final_norm):
    if final_norm:
        gf_ref, yp_ref, yl_ref, buf, sem = rest
    else:
        o_ref, buf, sem = rest
    base = pl.program_id(0) * (TOP_K * TD)

    def issue(r, carry):
        for k in range(TOP_K):
            _row_copy(ys_ref, slot_ref[base + TOP_K * r + k], buf.at[k], r, sem).start()
        return carry

    lax.fori_loop(0, TD, issue, 0, unroll=8)
    for k in range(TOP_K):
        pltpu.make_async_copy(ys_ref.at[pl.ds(0, TD)], buf.at[k], sem).wait()
    y = rinfo_ref[:, R_W1:R_W1 + 1] * buf[0] + rinfo_ref[:, R_W2:R_W2 + 1] * buf[1]
    x = x1_ref[...] + mod_ref[0, 0, 5:6, :] * y
    if final_norm:
        out = _rmsnorm(x, gf_ref[...])
        is_lat = pl.program_id(0) >= N_CTX // TD

        @pl.when(jnp.logical_not(is_lat))
        def _():
            yp_ref[...] = out

        @pl.when(is_lat)
        def _():
            yl_ref[...] = out
    else:
        o_ref[...] = x


def _combine(slots, x1, rinfo, mods, ys, layer, norm_f=None):
    final_norm = norm_f is not None
    n_ctx_t = N_CTX // TD
    in_specs = [
        pl.BlockSpec((TD, D_MODEL), lambda i, s: (i, 0)),
        pl.BlockSpec((TD, LANES), lambda i, s: (i, 0)),
        pl.BlockSpec((1, 1, 6, D_MODEL), lambda i, s: (layer, _group_of_tile(i, TD), 0, 0)),
        pl.BlockSpec(memory_space=pl.ANY),
    ]
    args = [slots, x1, rinfo, mods, ys]
    if final_norm:
        in_specs.append(pl.BlockSpec((1, D_MODEL), lambda i, s: (0, 0)))
        args.append(norm_f)
        out_specs = [pl.BlockSpec((TD, D_MODEL), lambda i, s: (jnp.minimum(i, n_ctx_t - 1), 0)),
                     pl.BlockSpec((TD, D_MODEL), lambda i, s: (jnp.maximum(i - n_ctx_t, 0), 0))]
        out_shape = [jax.ShapeDtypeStruct((N_CTX, D_MODEL), F32), jax.ShapeDtypeStruct((N_LAT, D_MODEL), F32)]
    else:
        out_specs = pl.BlockSpec((TD, D_MODEL), lambda i, s: (i, 0))
        out_shape = jax.ShapeDtypeStruct((NTOK, D_MODEL), F32)
    return pl.pallas_call(
        functools.partial(_combine_kernel, final_norm=final_norm),
        grid_spec=pltpu.PrefetchScalarGridSpec(
            num_scalar_prefetch=1,
            grid=(NTOK // TD,),
            in_specs=in_specs,
            out_specs=out_specs,
            scratch_shapes=[pltpu.VMEM((TOP_K, TD, D_MODEL), F32), pltpu.SemaphoreType.DMA],
        ),
        out_shape=out_shape,
        compiler_params=pltpu.CompilerParams(
            dimension_semantics=("arbitrary",), vmem_limit_bytes=VMEM_LIMIT),
        name="moe_combine",
    )(*args)


def _final_kernel(x_ref, g_ref, o_ref):
    o_ref[...] = _rmsnorm(x_ref[...], g_ref[...])


def _final(x, g, row_off, rows):
    off = row_off // TM
    return pl.pallas_call(
        _final_kernel,
        grid=(rows // TM,),
        in_specs=[
            pl.BlockSpec((TM, D_MODEL), lambda i: (off + i, 0)),
            pl.BlockSpec((1, D_MODEL), lambda i: (0, 0)),
        ],
        out_specs=pl.BlockSpec((TM, D_MODEL), lambda i: (i, 0)),
        out_shape=jax.ShapeDtypeStruct((rows, D_MODEL), F32),
        compiler_params=pltpu.CompilerParams(
            dimension_semantics=("arbitrary",), vmem_limit_bytes=VMEM_LIMIT),
        name="final_norm",
    )(x, g)


def kernel(x_prompt, x_sample, state_mlstm_C, state_mlstm_n, state_mlstm_m, state_ret_S, c, c_ctx,
           norm1_g, norm2_g, norm_f_g, w_ada, b_ada, w_in, b_gates, ret_decay_logit,
           mlstm_norm_g, ret_norm_g, w_out, ffn_w_gate, ffn_w_up, ffn_w_down,
           moe_w_router, moe_w_gate, moe_w_up, moe_w_down):
    xs_in = (x_prompt.reshape(N_CTX, D_MODEL), x_sample.reshape(N_LAT, D_MODEL))
    cvec = jnp.concatenate(
        [c_ctx[None, :], c, jnp.zeros((N_GROUPS - 1 - DEC_BATCH, D_MODEL), F32)], 0)
    mods = _ada(cvec, w_ada, b_ada).reshape(DEPTH, N_GROUPS, 6, D_MODEL)

    n_m = 4 * W_M
    w_ret = w_in[:, :, n_m + N_GATES:]
    n_if = N_GATES // 2
    lane_pad = ((0, 0), (0, 0), (0, LANES - n_if))
    wg = jnp.concatenate([jnp.pad(w_in[:, :, n_m:n_m + n_if], lane_pad),
                          jnp.pad(w_in[:, :, n_m + n_if:n_m + N_GATES], lane_pad)], -1)
    bg = jnp.concatenate([jnp.pad(b_gates[:, None, :n_if], lane_pad),
                          jnp.pad(b_gates[:, None, n_if:], lane_pad)], -1)
    cos_np, sin_np = _rope_tables()
    cos_t, sin_t = jnp.asarray(cos_np), jnp.asarray(sin_np)
    dl = jnp.broadcast_to(ret_decay_logit.reshape(DEPTH, 2 * H_R, 1), (DEPTH, 2 * H_R, LANES))
    m0 = jnp.pad(state_mlstm_m.reshape(DEC_BATCH, DEPTH, 1, 2 * H_M),
                 ((0, 0), (0, 0), (0, 0), (0, LANES - 2 * H_M)))
    g1 = norm1_g.reshape(DEPTH, 1, D_MODEL)
    g2 = norm2_g.reshape(DEPTH, 1, D_MODEL)
    nm = mlstm_norm_g.reshape(DEPTH, 1, W_M)
    nr = ret_norm_g.reshape(DEPTH, 1, W_R)
    n_moe = moe_w_router.shape[0]
    wr_pad = jnp.pad(moe_w_router, ((0, 0), (0, 0), (0, LANES - N_EXPERTS)))
    moe_g = moe_w_gate.reshape(n_moe * N_EXPERTS, D_MODEL, D_FF)
    moe_u = moe_w_up.reshape(n_moe * N_EXPERTS, D_MODEL, D_FF)
    moe_d = moe_w_down.reshape(n_moe * N_EXPERTS, D_FF, D_MODEL)

    states = ()
    xs = xs_in
    for l in range(DEPTH):
        jl = l // 2
        p, gates = _proj(xs, g1, mods, w_in, w_ret, wg, bg, cos_t, sin_t, l)
        y, *states = _scan_ctx(p, gates, dl, nm, nr, l, states)
        y = _scan_lat(p, gates, dl, nm, nr, state_mlstm_C, state_mlstm_n, m0, state_ret_S, y, l)
        if l % 2 == 0:
            x1, h2 = _out(y, xs, w_out, g2, mods, l)
            x = _ffn(h2, x1, ffn_w_gate, ffn_w_up, ffn_w_down, mods, l, jl)
        else:
            x1, h2f, rinfo, cnt = _out(y, xs, w_out, g2, mods, l, wr_pad, jl)
            slots = rinfo[:, R_S1:R_S2 + 1].astype(jnp.int32).reshape(TOP_K * NTOK)
            tile_e, tile_blk, tile_n = _tile_plan(cnt[0, :N_EXPERTS].astype(jnp.int32))
            xd = _dispatch(slots, h2f)
            yd = _gffn(tile_e, tile_blk, tile_n, xd, moe_g, moe_u, moe_d, jl * N_EXPERTS)
            if l == DEPTH - 1:
                y_ctx, y_lat = _combine(slots, x1, rinfo, mods, yd, l, norm_f_g.reshape(1, D_MODEL))
            else:
                x = _combine(slots, x1, rinfo, mods, yd, l)
        xs = (x,)

    if DEPTH % 2 == 1:
        y_ctx = _final(x, norm_f_g.reshape(1, D_MODEL), 0, N_CTX)
        y_lat = _final(x, norm_f_g.reshape(1, D_MODEL), N_CTX, N_LAT)
    y_prompt = y_ctx.reshape(BATCH, SEQ, D_MODEL)
    y_sample = y_lat.reshape(DEC_BATCH, DEC_SEQ, D_MODEL)
    new_C, new_n, new_m, new_S = states
    return (y_prompt, y_sample, new_C, new_n,
            new_m[:, :, 0, :2 * H_M].reshape(BATCH, DEPTH, 2, H_M), new_S)
```

```python
import functools

import numpy as np
import jax
import jax.numpy as jnp
from jax import lax
from jax.experimental import pallas as pl
from jax.experimental.pallas import tpu as pltpu

D_MODEL = 1024
BATCH = 32
SEQ = 256
DEPTH = 2
DEC_BATCH = 2
DEC_SEQ = 1024
GRID_W = 64
H_M = 4
DH = 128
H_R = 4
W_M = H_M * DH
W_R = H_R * DH
N_GATES = 4 * H_M
CHUNK = 128
D_FF = 2816
N_EXPERTS = 8
ROPE_BASE = 10000.0
EPS = 1e-6

N_CTX = BATCH * SEQ
N_LAT = DEC_BATCH * DEC_SEQ
NTOK = N_CTX + N_LAT
N_GROUPS = 8
K_SCALE = DH ** -0.5
P_COLS = 4 * W_M + 4 * W_R
LANES = 128
GATE_LANES = 2 * LANES
VMEM_LIMIT = 56 * 1024 * 1024

F32 = jnp.float32
BF16 = jnp.bfloat16
HIGHEST = lax.Precision.HIGHEST

TM = 1024
TN = 1024
FC = 256
TM_F = 512
TOP_K = 2
TR = 896
REG_TILES = -(-NTOK // TR)
REG = REG_TILES * TR
MAX_TILES = -(-TOP_K * NTOK // TR) + N_EXPERTS
TD = 512


def _group_of_tile(i, tm):
    return jnp.maximum(i * tm // DEC_SEQ - (N_CTX // DEC_SEQ - 1), 0)


def _silu(x):
    return x * jax.nn.sigmoid(x)


def _log_sigmoid(x):
    return jnp.minimum(x, 0.0) - jnp.log(1.0 + jnp.exp(-jnp.abs(x)))


def _rmsnorm(x, g):
    return x * lax.rsqrt(jnp.mean(x * x, -1, keepdims=True) + EPS) * g


def _ada_kernel(cv_ref, w_ref, b_ref, o_ref):
    s = _silu(cv_ref[...]).astype(BF16)
    o_ref[0] = jnp.dot(s, w_ref[0].astype(BF16), preferred_element_type=F32) + b_ref[0]


def _ada(cvec, w_ada, b_ada):
    tn = 1536
    n = 6 * D_MODEL
    return pl.pallas_call(
        _ada_kernel,
        grid=(DEPTH, n // tn),
        in_specs=[
            pl.BlockSpec((N_GROUPS, D_MODEL), lambda l, j: (0, 0)),
            pl.BlockSpec((1, D_MODEL, tn), lambda l, j: (l, 0, j)),
            pl.BlockSpec((1, 1, tn), lambda l, j: (l, 0, j)),
        ],
        out_specs=pl.BlockSpec((1, N_GROUPS, tn), lambda l, j: (l, 0, j)),
        out_shape=jax.ShapeDtypeStruct((DEPTH, N_GROUPS, n), F32),
        compiler_params=pltpu.CompilerParams(
            dimension_semantics=("arbitrary", "arbitrary"), vmem_limit_bytes=VMEM_LIMIT),
        name="ada",
    )(cvec, w_ada, b_ada.reshape(DEPTH, 1, n))


def _rope_tables():
    half = DH // 4
    freqs = ROPE_BASE ** (-np.arange(half, dtype=np.float64) / half)
    t = np.arange(DEC_SEQ)
    pos = np.stack([t // GRID_W, t % GRID_W], 1).astype(np.float64)
    d = np.arange(DH)
    ang = pos[:, d // (DH // 2)] * freqs[d % half][None, :]
    sign = np.where((d % (DH // 2)) < half, -1.0, 1.0)[None, :]
    return np.cos(ang).astype(np.float32), (sign * np.sin(ang)).astype(np.float32)


def _rope(a, cos, sin):
    lane = lax.broadcasted_iota(jnp.int32, a.shape, 1)
    first = (lane % (DH // 2)) < (DH // 4)
    partner = jnp.where(first, pltpu.roll(a, DH - DH // 4, 1), pltpu.roll(a, DH // 4, 1))
    return a * cos + partner * sin


def _proj_kernel(*refs, n_ctx_tiles, split_x, layer):
    if split_x:
        (xp_ref, xl_ref, g_ref, mod_ref, wt_ref, wg_ref, bg_ref, cos_ref, sin_ref,
         p_ref, gate_ref, h_scr, w_res, w_stg, w_sem) = refs
    else:
        (x_ref, g_ref, mod_ref, wt_ref, wg_ref, bg_ref, cos_ref, sin_ref,
         p_ref, gate_ref, h_scr, w_res, w_stg, w_sem) = refs
    p_ref = p_ref.at[0]
    i = pl.program_id(0)
    j = pl.program_id(1)
    is_lat = i >= n_ctx_tiles
    half = TN // 2

    def prologue(x):
        h = _rmsnorm(x, g_ref[0]) * (1.0 + mod_ref[0, 0, 1:2, :]) + mod_ref[0, 0, 0:1, :]
        h_scr[...] = h.astype(BF16)
        gate_ref[...] = _dot_f32x3(h, wg_ref[0]) + bg_ref[0]

    @pl.when(j == 0)
    def _():
        if split_x:
            pl.when(jnp.logical_not(is_lat))(lambda: prologue(xp_ref[...]))
            pl.when(is_lat)(lambda: prologue(xl_ref[...]))
        else:
            prologue(x_ref[...])

    n_col_tiles = P_COLS // TN

    def tile_copy(jj):
        row0 = jj * TN + (N_GATES if jj * TN >= 4 * W_M else 0)
        s = jj % 2
        return pltpu.make_async_copy(wt_ref.at[layer, pl.ds(row0, TN), :], w_stg.at[s], w_sem.at[s])

    def matmul(jj):
        @pl.when(i == 0)
        def _():
            if jj == 0:
                tile_copy(0).start()
            if jj + 1 < n_col_tiles:
                tile_copy(jj + 1).start()
            tile_copy(jj).wait()
            w_res[jj] = w_stg[jj % 2].astype(BF16)

        return _dot_nt(h_scr[...], w_res[jj])

    @pl.when(j == 0)
    def _():
        acc = matmul(0)
        p_ref[:, :half] = acc[:, :half].astype(BF16)
        p_ref[:, half:] = (acc[:, half:] * K_SCALE).astype(BF16)

    @pl.when(j == 1)
    def _():
        p_ref[...] = matmul(1).astype(BF16)

    @pl.when(j == 2)
    def _():
        acc = matmul(2)

        @pl.when(is_lat)
        def _():
            cos = cos_ref[...]
            sin = sin_ref[...]
            for hd in range(TN // DH):
                sl = slice(hd * DH, (hd + 1) * DH)
                r = _rope(acc[:, sl], cos, sin)
                p_ref[:, sl] = (r * K_SCALE if hd * DH >= half else r).astype(BF16)

        @pl.when(jnp.logical_not(is_lat))
        def _():
            p_ref[:, :half] = acc[:, :half].astype(BF16)
            p_ref[:, half:] = (acc[:, half:] * K_SCALE).astype(BF16)

    @pl.when(j == 3)
    def _():
        p_ref[...] = matmul(3).astype(BF16)


def _proj(xs, g1, mods, w_in_t, wg, bg, cos_t, sin_t, layer):
    n_ctx_tiles = N_CTX // TM
    tiles_per_seq = DEC_SEQ // TM
    split_x = len(xs) == 2
    if split_x:
        x_specs = [pl.BlockSpec((TM, D_MODEL), lambda i, j: (jnp.minimum(i, n_ctx_tiles - 1), 0)),
                   pl.BlockSpec((TM, D_MODEL), lambda i, j: (jnp.maximum(i - n_ctx_tiles, 0), 0))]
    else:
        x_specs = [pl.BlockSpec((TM, D_MODEL), lambda i, j: (i, 0))]
    return pl.pallas_call(
        functools.partial(_proj_kernel, n_ctx_tiles=n_ctx_tiles, split_x=split_x, layer=layer),
        grid=(NTOK // TM, P_COLS // TN),
        in_specs=x_specs + [
            pl.BlockSpec((1, 1, D_MODEL), lambda i, j: (layer, 0, 0)),
            pl.BlockSpec((1, 1, 6, D_MODEL), lambda i, j: (layer, _group_of_tile(i, TM), 0, 0)),
            pl.BlockSpec(memory_space=pl.ANY),
            pl.BlockSpec((1, D_MODEL, GATE_LANES), lambda i, j: (layer, 0, 0)),
            pl.BlockSpec((1, 1, GATE_LANES), lambda i, j: (layer, 0, 0)),
            pl.BlockSpec((TM, DH), lambda i, j: (i % tiles_per_seq, 0)),
            pl.BlockSpec((TM, DH), lambda i, j: (i % tiles_per_seq, 0)),
        ],
        out_specs=[
            pl.BlockSpec((1, TM, TN), lambda i, j: (j, i, 0)),
            pl.BlockSpec((TM, GATE_LANES), lambda i, j: (i, 0)),
        ],
        out_shape=[
            jax.ShapeDtypeStruct((P_COLS // TN, NTOK, TN), BF16),
            jax.ShapeDtypeStruct((NTOK, GATE_LANES), F32),
        ],
        scratch_shapes=[pltpu.VMEM((TM, D_MODEL), BF16),
                        pltpu.VMEM((P_COLS // TN, TN, D_MODEL), BF16),
                        pltpu.VMEM((2, TN, D_MODEL), F32),
                        pltpu.SemaphoreType.DMA((2,))],
        compiler_params=pltpu.CompilerParams(
            dimension_semantics=("arbitrary", "arbitrary"), vmem_limit_bytes=VMEM_LIMIT),
        name="proj",
    )(*xs, g1, mods, w_in_t, wg, bg, cos_t, sin_t)


def _split3(x):
    hi = x.astype(BF16)
    r1 = x - hi.astype(F32)
    mid = r1.astype(BF16)
    lo = (r1 - mid.astype(F32)).astype(BF16)
    return hi, mid, lo


def _dot(a, b):
    return jnp.dot(a, b, preferred_element_type=F32)


def _dot_nt(a, b):
    return lax.dot_general(a, b, (((1,), (1,)), ((), ())), preferred_element_type=F32)


def _tri_dot_left(tri, x):
    hi, mid, lo = _split3(x)
    return _dot(tri, hi) + _dot(tri, mid) + _dot(tri, lo)


def _tri_dot_right(x, tri):
    hi, mid, lo = _split3(x)
    return _dot(hi, tri) + _dot(mid, tri) + _dot(lo, tri)


def _run_max(x, reverse):
    n_tiles = x.shape[0] // 8
    sub = lax.broadcasted_iota(jnp.int32, (8, LANES), 0)
    out = [None] * n_tiles
    carry = None
    for t in (range(n_tiles - 1, -1, -1) if reverse else range(n_tiles)):
        v = x[8 * t:8 * t + 8, :]
        for s in (1, 2, 4):
            if reverse:
                v = jnp.maximum(v, jnp.where(sub < 8 - s, pltpu.roll(v, 8 - s, 0), -jnp.inf))
            else:
                v = jnp.maximum(v, jnp.where(sub >= s, pltpu.roll(v, s, 0), -jnp.inf))
        if carry is not None:
            v = jnp.maximum(v, carry)
        carry = jnp.broadcast_to(v[0:1, :] if reverse else v[7:8, :], (8, LANES))
        out[t] = v
    return jnp.concatenate(out, axis=0)


def _scan_kernel(*refs, T, has_state, n_prev=0):
    if has_state:
        (p_ref, g_ref, dl_ref, nm_ref, nr_ref, C0_ref, n0_ref, m0_ref, S0_ref, _yprev_ref,
         y_ref, CN_s, S_s, m_s, hf_s, hb_s, dm_s, dq_s, dk_s, dL_s, kT_s) = refs
    else:
        p_ref, g_ref, dl_ref, nm_ref, nr_ref = refs[:5]
        prev_refs = refs[5:9] if n_prev else ()
        (y_ref, C_out, n_out, m_out, S_out,
         CN_s, S_s, m_s, hf_s, hb_s, dm_s, dq_s, dk_s, dL_s, kT_s) = refs[5 + len(prev_refs):]
    L = CHUNK
    n_chunks = T // L
    row_i = lax.broadcasted_iota(jnp.int32, (L, L), 0)
    col_j = lax.broadcasted_iota(jnp.int32, (L, L), 1)
    lower = col_j <= row_i
    upper = col_j >= row_i
    tril = lower.astype(BF16)
    triu = upper.astype(BF16)
    ones = jnp.ones((L, DH), BF16)
    c_km = W_M
    c_vm = 2 * W_M
    c_om = 3 * W_M
    c_qr = 4 * W_M
    c_kr = c_qr + W_R
    c_vr = c_qr + 2 * W_R
    c_gr = c_qr + 3 * W_R

    def pcols(rows, col):
        return p_ref[col // TN, rows, col % TN:col % TN + DH]

    for d in range(2):
        for h in range(H_M):
            k = d * H_M + h
            if has_state:
                CN_s[k, :, :DH] = C0_ref[0, 0, d, h]
                CN_s[k, :, DH:] = jnp.broadcast_to(n0_ref[0, 0, d, h:h + 1, :], (DH, DH)).T
                S_s[k] = S0_ref[0, 0, d, h]
            else:
                CN_s[k] = jnp.zeros((DH, 2 * DH), F32)
                S_s[k] = jnp.zeros((DH, DH), F32)
    m_s[...] = m0_ref[0, 0] if has_state else jnp.zeros((1, LANES), F32)

    @pl.when(pl.program_id(0) == 0)
    def _():
        pos_i = row_i.astype(F32)
        pos_j = col_j.astype(F32)
        for d in range(2):
            for h in range(H_R):
                k = d * H_R + h
                lg_row = _log_sigmoid(dl_ref[0, k:k + 1, :])
                lg = jnp.broadcast_to(lg_row, (L, L))
                rel = (row_i - col_j if d == 0 else col_j - row_i).astype(F32)
                dm_s[k] = jnp.where(rel >= 0, jnp.exp(lg * jnp.maximum(rel, 0.0)), 0.0)
                dq_s[k] = jnp.exp(lg * (pos_i + 1.0 if d == 0 else L - pos_i))
                dk_s[k] = jnp.exp(lg * (L - 1.0 - pos_j if d == 0 else pos_j))
                dL_s[k] = jnp.exp(lg_row * float(L))

    def transpose_keys(c, carry):
        r0 = pl.multiple_of(c * L, L)
        for h in range(H_M):
            kT_s[h, c] = pcols(pl.ds(r0, L), c_km + h * DH).astype(F32).T
            kT_s[H_M + h, c] = pcols(pl.ds(r0, L), c_kr + h * DH).astype(F32).T
        return carry

    lax.fori_loop(0, n_chunks, transpose_keys, 0)

    def chunk_step(c, carry):
        m_prev = m_s[...]
        m_new = []
        prep = []
        for d in range(2):
            ci = c if d == 0 else n_chunks - 1 - c
            r0 = pl.multiple_of(ci * L, L)
            mask = lower if d == 0 else upper
            e_row = L - 1 if d == 0 else 0
            FL = _log_sigmoid(g_ref[pl.ds(r0, L), LANES:2 * LANES])
            Bc = _tri_dot_left(tril if d == 0 else triu, FL)
            Zc = g_ref[pl.ds(r0, L), 0:LANES] - Bc
            M = jnp.maximum(_run_max(Zc, reverse=(d == 1)), m_prev)
            m_row = Bc + M
            M_end = M[e_row:e_row + 1, :]
            m_new.append(Bc[e_row:e_row + 1, :] + M_end)
            decay = jnp.exp(m_prev - M_end)
            prep.append(dict(ci=ci, r0=r0, mask=mask, M=M, m_row=m_row, decay=decay,
                             ZT=Zc.T,
                             WT=jnp.exp(Zc - M_end).T))
        pairs = [(d, h) for d in range(2) for h in range(H_M)]

        def rows(d, col):
            return pcols(pl.ds(prep[d]["r0"], L), col)

        qk, qkr = {}, {}
        for d, h in pairs:
            qk[d, h] = _dot_nt(rows(d, h * DH), rows(d, c_km + h * DH))
            qkr[d, h] = _dot_nt(rows(d, c_qr + h * DH), rows(d, c_kr + h * DH))
        upd, updr = {}, {}
        for d, h in pairs:
            k = d * H_M + h
            ci = prep[d]["ci"]
            vo = jnp.concatenate([rows(d, c_vm + h * DH), ones], axis=1)
            wkT = (kT_s[h, ci] * jnp.broadcast_to(prep[d]["WT"][k:k + 1, :], (DH, L))).astype(BF16)
            upd[d, h] = _dot(wkT, vo)
            kdT = (kT_s[H_M + h, ci] * dk_s[k]).astype(BF16)
            updr[d, h] = _dot(kdT, rows(d, c_vr + h * DH))
        for d, h in pairs:
            k = d * H_M + h
            r0 = prep[d]["r0"]
            h_dst = hf_s if d == 0 else hb_s
            q = rows(d, h * DH)
            M_col = jnp.broadcast_to(prep[d]["M"][:, k:k + 1], (L, L))
            z_row = jnp.broadcast_to(prep[d]["ZT"][k:k + 1, :], (L, L))
            D = jnp.where(prep[d]["mask"], jnp.exp(z_row - M_col), 0.0)
            s = (qk[d, h] * D).astype(BF16)
            w_inter = jnp.exp(jnp.broadcast_to(m_prev[:, k:k + 1], (L, L)) - M_col)
            wq = (w_inter * q.astype(F32)).astype(BF16)
            vo = jnp.concatenate([rows(d, c_vm + h * DH), ones], axis=1)
            CN = CN_s[k]
            res = _dot(jnp.concatenate([s, wq], axis=1),
                       jnp.concatenate([vo, CN.astype(BF16)], axis=0))
            floor = jnp.exp(-jnp.broadcast_to(prep[d]["m_row"][:, k:k + 1], (L, L)))
            h_dst[pl.ds(r0, L), h * DH:(h + 1) * DH] = res[:, :DH] / jnp.maximum(jnp.abs(res[:, DH:]), floor)
            CN_s[k] = jnp.broadcast_to(prep[d]["decay"][:, k:k + 1], (DH, 2 * DH)) * CN + upd[d, h]
            qr = rows(d, c_qr + h * DH)
            S = S_s[k]
            sr = (qkr[d, h] * dm_s[k]).astype(BF16)
            qd = (qr.astype(F32) * dq_s[k]).astype(BF16)
            h_dst[pl.ds(r0, L), W_M + h * DH:W_M + (h + 1) * DH] = _dot(
                jnp.concatenate([sr, qd], axis=1),
                jnp.concatenate([rows(d, c_vr + h * DH), S.astype(BF16)], axis=0))
            S_s[k] = dL_s[k] * S + updr[d, h]
        lane = lax.broadcasted_iota(jnp.int32, (1, LANES), 1)
        m_s[...] = jnp.where(lane < H_M, m_new[0], m_new[1])
        return carry

    lax.fori_loop(0, n_chunks, chunk_step, 0)

    for h in range(H_M):
        sl = slice(h * DH, (h + 1) * DH)
        hs = hf_s[:, sl] + hb_s[:, sl]
        yn = _rmsnorm(hs, nm_ref[0, :, sl])
        om = pcols(slice(None), c_om + h * DH).astype(F32)
        y_ref[:, sl] = (jax.nn.sigmoid(om) * yn).astype(BF16)
        slr = slice(W_M + h * DH, W_M + (h + 1) * DH)
        hr = hf_s[:, slr] + hb_s[:, slr]
        ynr = _rmsnorm(hr, nr_ref[0, :, sl])
        gr = pcols(slice(None), c_gr + h * DH).astype(F32)
        y_ref[:, slr] = (_silu(gr) * ynr).astype(BF16)

    if not has_state:
        for prev, out in zip(prev_refs, (C_out, n_out, m_out, S_out)):
            out[0, :n_prev] = prev[0]
        for d in range(2):
            for h in range(H_M):
                k = d * H_M + h
                C_out[0, n_prev, d, h] = CN_s[k, :, :DH]
                n_out[0, n_prev, d, h:h + 1, :] = CN_s[k, :, DH:].T[0:1, :]
                S_out[0, n_prev, d, h] = S_s[k]
        m_out[0, n_prev] = m_s[...]


def _scan_scratch(T):
    return [
        pltpu.VMEM((2 * H_M, DH, 2 * DH), F32),
        pltpu.VMEM((2 * H_R, DH, DH), F32),
        pltpu.VMEM((1, LANES), F32),
        pltpu.VMEM((T, W_M + W_R), F32),
        pltpu.VMEM((T, W_M + W_R), F32),
        pltpu.VMEM((2 * H_R, CHUNK, CHUNK), F32),
        pltpu.VMEM((2 * H_R, CHUNK, CHUNK), F32),
        pltpu.VMEM((2 * H_R, CHUNK, CHUNK), F32),
        pltpu.VMEM((2 * H_R, 1, LANES), F32),
        pltpu.VMEM((H_M + H_R, T // CHUNK, DH, CHUNK), F32),
    ]


def _scan_ctx(p, gates, dl, nm, nr, layer, prev_states=()):
    T = SEQ
    n_lay = layer + 1
    state_tails = [(2, H_M, DH, DH), (2, H_M, DH), (1, LANES), (2, H_R, DH, DH)]

    def state_spec(n, tail):
        return pl.BlockSpec((1, n) + tail, lambda b: (b,) + (0,) * (1 + len(tail)))
    common = [
        pl.BlockSpec((P_COLS // TN, T, TN), lambda b: (0, b, 0)),
        pl.BlockSpec((T, GATE_LANES), lambda b: (b, 0)),
        pl.BlockSpec((1, 2 * H_R, LANES), lambda b: (layer, 0, 0)),
        pl.BlockSpec((1, 1, W_M), lambda b: (layer, 0, 0)),
        pl.BlockSpec((1, 1, W_R), lambda b: (layer, 0, 0)),
    ]
    return pl.pallas_call(
        functools.partial(_scan_kernel, T=T, has_state=False, n_prev=layer if prev_states else 0),
        grid=(BATCH,),
        in_specs=common + [state_spec(layer, tail) for tail in state_tails[:len(prev_states)]],
        out_specs=[pl.BlockSpec((T, D_MODEL), lambda b: (b, 0))] + [state_spec(n_lay, tail) for tail in state_tails],
        out_shape=[jax.ShapeDtypeStruct((NTOK, D_MODEL), BF16)] + [
            jax.ShapeDtypeStruct((BATCH, n_lay) + tail, F32) for tail in state_tails],
        scratch_shapes=_scan_scratch(T),
        compiler_params=pltpu.CompilerParams(
            dimension_semantics=("arbitrary",), vmem_limit_bytes=VMEM_LIMIT),
        name="scan_ctx",
    )(p, gates, dl, nm, nr, *prev_states)


def _scan_lat(p, gates, dl, nm, nr, C0, n0, m0, S0, y_prev, layer):
    T = DEC_SEQ
    off = N_CTX // T
    in_specs = [
        pl.BlockSpec((P_COLS // TN, T, TN), lambda b: (0, off + b, 0)),
        pl.BlockSpec((T, GATE_LANES), lambda b: (off + b, 0)),
        pl.BlockSpec((1, 2 * H_R, LANES), lambda b: (layer, 0, 0)),
        pl.BlockSpec((1, 1, W_M), lambda b: (layer, 0, 0)),
        pl.BlockSpec((1, 1, W_R), lambda b: (layer, 0, 0)),
        pl.BlockSpec((1, 1, 2, H_M, DH, DH), lambda b: (b, layer, 0, 0, 0, 0)),
        pl.BlockSpec((1, 1, 2, H_M, DH), lambda b: (b, layer, 0, 0, 0)),
        pl.BlockSpec((1, 1, 1, LANES), lambda b: (b, layer, 0, 0)),
        pl.BlockSpec((1, 1, 2, H_R, DH, DH), lambda b: (b, layer, 0, 0, 0, 0)),
        pl.BlockSpec(memory_space=pl.ANY),
    ]
    return pl.pallas_call(
        functools.partial(_scan_kernel, T=T, has_state=True),
        grid=(DEC_BATCH,),
        in_specs=in_specs,
        out_specs=pl.BlockSpec((T, D_MODEL), lambda b: (off + b, 0)),
        out_shape=jax.ShapeDtypeStruct((NTOK, D_MODEL), BF16),
        input_output_aliases={9: 0},
        scratch_shapes=_scan_scratch(T),
        compiler_params=pltpu.CompilerParams(
            dimension_semantics=("arbitrary",), vmem_limit_bytes=VMEM_LIMIT),
        name="scan_lat",
    )(p, gates, dl, nm, nr, C0, n0, m0, S0, y_prev)


def _top2(logits):
    lane = lax.broadcasted_iota(jnp.int32, logits.shape, 1)
    v1 = jnp.max(logits, -1, keepdims=True)
    i1 = jnp.min(jnp.where(logits == v1, lane, LANES), -1, keepdims=True)
    rest = jnp.where(lane == i1, -jnp.inf, logits)
    v2 = jnp.max(rest, -1, keepdims=True)
    i2 = jnp.min(jnp.where(rest == v2, lane, LANES), -1, keepdims=True)
    e2 = jnp.exp(v2 - v1)
    return i1, i2, 1.0 / (1.0 + e2), e2 / (1.0 + e2)


def _split2(x):
    hi = x.astype(BF16)
    return hi, (x - hi.astype(F32)).astype(BF16)


def _dot_f32x3(a, b):
    a_hi, a_lo = _split2(a)
    b_hi, b_lo = _split2(b)
    return _dot(a_hi, b_hi) + _dot(a_hi, b_lo) + _dot(a_lo, b_hi)


R_E1, R_E2, R_W1, R_W2, R_S1, R_S2 = range(6)


def _out_kernel(*refs, with_router, split_x):
    y_ref = refs[0]
    if split_x:
        xp_ref, xl_ref = refs[1:3]
        x_in = jnp.where(pl.program_id(0) >= N_CTX // TM, xl_ref[...], xp_ref[...])
    else:
        x_in = refs[1][...]
    refs = refs[3:] if split_x else refs[2:]
    if with_router:
        (w_ref, g_ref, mod_ref, wr_ref,
         x1_ref, h2_ref, rinfo_ref, cnt_ref, w_scr, tri_scr, cnt_scr) = refs
    else:
        w_ref, g_ref, mod_ref, x1_ref, h2_ref, w_scr = refs

    @pl.when(pl.program_id(0) == 0)
    def _():
        w_scr[...] = w_ref[0].astype(BF16)
        if with_router:
            r = lax.broadcasted_iota(jnp.int32, (TM, TM), 0)
            c = lax.broadcasted_iota(jnp.int32, (TM, TM), 1)
            tri_scr[...] = (c < r).astype(BF16)
            cnt_scr[...] = jnp.zeros_like(cnt_scr)

    o = jnp.dot(y_ref[...], w_scr[...], preferred_element_type=F32)
    x1 = x_in + mod_ref[0, 0, 2:3, :] * o
    x1_ref[...] = x1
    h2 = _rmsnorm(x1, g_ref[0]) * (1.0 + mod_ref[0, 0, 4:5, :]) + mod_ref[0, 0, 3:4, :]
    if not with_router:
        h2_ref[...] = h2.astype(BF16)
    else:
        h2_ref[...] = h2
        logits = _dot_f32x3(h2, wr_ref[0])
        lane = lax.broadcasted_iota(jnp.int32, logits.shape, 1)
        i1, i2, w1, w2 = _top2(jnp.where(lane < N_EXPERTS, logits, -jnp.inf))
        oh1 = lane == i1
        oh2 = lane == i2
        sel = jnp.where(oh1 | oh2, 1.0, 0.0)
        rank = _dot(tri_scr[...], sel.astype(BF16)) + cnt_scr[...]
        r1 = jnp.sum(jnp.where(oh1, rank, 0.0), -1, keepdims=True)
        r2 = jnp.sum(jnp.where(oh2, rank, 0.0), -1, keepdims=True)
        s1 = i1.astype(F32) * float(REG) + r1
        s2 = i2.astype(F32) * float(REG) + r2
        info = jnp.zeros(logits.shape, F32)
        for col, val in ((R_E1, i1.astype(F32)), (R_E2, i2.astype(F32)), (R_W1, w1), (R_W2, w2),
                         (R_S1, s1), (R_S2, s2)):
            info = jnp.where(lane == col, val, info)
        rinfo_ref[...] = info
        cnt_scr[...] += jnp.sum(sel, 0, keepdims=True)
        cnt_ref[...] = cnt_scr[...]


def _out(y, xs, w_out, g2, mods, layer, w_router_pad=None, router_idx=0):
    with_router = w_router_pad is not None
    split_x = len(xs) == 2
    n_ctx_tiles = N_CTX // TM
    if split_x:
        x_specs = [pl.BlockSpec((TM, D_MODEL), lambda i: (jnp.minimum(i, n_ctx_tiles - 1), 0)),
                   pl.BlockSpec((TM, D_MODEL), lambda i: (jnp.maximum(i - n_ctx_tiles, 0), 0))]
    else:
        x_specs = [pl.BlockSpec((TM, D_MODEL), lambda i: (i, 0))]
    in_specs = [pl.BlockSpec((TM, D_MODEL), lambda i: (i, 0))] + x_specs + [
        pl.BlockSpec((1, D_MODEL, D_MODEL), lambda i: (layer, 0, 0)),
        pl.BlockSpec((1, 1, D_MODEL), lambda i: (layer, 0, 0)),
        pl.BlockSpec((1, 1, 6, D_MODEL), lambda i: (layer, _group_of_tile(i, TM), 0, 0)),
    ]
    out_specs = [
        pl.BlockSpec((TM, D_MODEL), lambda i: (i, 0)),
        pl.BlockSpec((TM, D_MODEL), lambda i: (i, 0)),
    ]
    out_shape = [
        jax.ShapeDtypeStruct((NTOK, D_MODEL), F32),
        jax.ShapeDtypeStruct((NTOK, D_MODEL), F32 if with_router else BF16),
    ]
    args = [y, *xs, w_out, g2, mods]
    scratch = [pltpu.VMEM((D_MODEL, D_MODEL), BF16)]
    if with_router:
        in_specs.append(pl.BlockSpec((1, D_MODEL, LANES), lambda i: (router_idx, 0, 0)))
        out_specs += [pl.BlockSpec((TM, LANES), lambda i: (i, 0)),
                      pl.BlockSpec((1, LANES), lambda i: (0, 0))]
        out_shape += [jax.ShapeDtypeStruct((NTOK, LANES), F32),
                      jax.ShapeDtypeStruct((1, LANES), F32)]
        args.append(w_router_pad)
        scratch += [pltpu.VMEM((TM, TM), BF16), pltpu.VMEM((1, LANES), F32)]
    return pl.pallas_call(
        functools.partial(_out_kernel, with_router=with_router, split_x=split_x),
        grid=(NTOK // TM,),
        in_specs=in_specs,
        out_specs=out_specs,
        out_shape=out_shape,
        scratch_shapes=scratch,
        compiler_params=pltpu.CompilerParams(
            dimension_semantics=("arbitrary",), vmem_limit_bytes=VMEM_LIMIT),
        name="out_router" if with_router else "out",
    )(*args)


N_FC = D_FF // FC
def _swiglu_tile(h, w_gu, w_d, acc, fetch=None):
    if fetch is not None:
        wg_hbm, wu_hbm, wd_hbm, stg_gu, stg_d, sem = fetch

        def copies(f):
            s = f % 2
            cols = pl.ds(f * FC, FC)
            return (pltpu.make_async_copy(wg_hbm.at[:, cols], stg_gu.at[s, 0], sem.at[s, 0]),
                    pltpu.make_async_copy(wu_hbm.at[:, cols], stg_gu.at[s, 1], sem.at[s, 1]),
                    pltpu.make_async_copy(wd_hbm.at[cols, :], stg_d.at[s], sem.at[s, 2]))

        def start(f):
            for c in copies(f):
                c.start()

        def land(f):
            for c in copies(f):
                c.wait()
            s = f % 2
            w_gu[f, :, :FC] = stg_gu[s, 0].astype(BF16)
            w_gu[f, :, FC:] = stg_gu[s, 1].astype(BF16)
            w_d[f * FC:(f + 1) * FC, :] = stg_d[s].astype(BF16)
    else:
        start = land = lambda f: None

    def up(f):
        return jnp.dot(h, w_gu[f], preferred_element_type=F32)

    start(0)
    if N_FC > 1:
        start(1)
    land(0)
    ab = up(0)
    for f in range(N_FC):
        if f + 2 < N_FC:
            start(f + 2)
        if f + 1 < N_FC:
            land(f + 1)
            ab_next = up(f + 1)
        t = (_silu(ab[:, :FC]) * ab[:, FC:]).astype(BF16)
        contrib = jnp.dot(t, w_d[f * FC:(f + 1) * FC, :], preferred_element_type=F32)
        if f == 0:
            acc[...] = contrib
        else:
            acc[...] += contrib
        if f + 1 < N_FC:
            ab = ab_next


def _ffn_weight_scratch():
    return [
        pltpu.VMEM((N_FC, D_MODEL, 2 * FC), BF16),
        pltpu.VMEM((D_FF, D_MODEL), BF16),
        pltpu.VMEM((2, 2, D_MODEL, FC), F32),
        pltpu.VMEM((2, FC, D_MODEL), F32),
        pltpu.SemaphoreType.DMA((2, 3)),
    ]


def _ffn_kernel(h_ref, res_ref, wg_ref, wu_ref, wd_ref, mod_ref, o_ref,
                acc, w_gu, w_d, stg_gu, stg_d, sem, *, w_idx):
    first = pl.program_id(0) == 0

    @pl.when(first)
    def _():
        _swiglu_tile(h_ref[...], w_gu, w_d, acc,
                     (wg_ref.at[w_idx], wu_ref.at[w_idx], wd_ref.at[w_idx], stg_gu, stg_d, sem))

    @pl.when(jnp.logical_not(first))
    def _():
        _swiglu_tile(h_ref[...], w_gu, w_d, acc)

    o_ref[...] = res_ref[...] + mod_ref[0, 0, 5:6, :] * acc[...]


def _ffn(h2, res, wg, wu, wd, mods, layer, w_idx):
    return pl.pallas_call(
        functools.partial(_ffn_kernel, w_idx=w_idx),
        grid=(NTOK // TM_F,),
        in_specs=[
            pl.BlockSpec((TM_F, D_MODEL), lambda i: (i, 0)),
            pl.BlockSpec((TM_F, D_MODEL), lambda i: (i, 0)),
            pl.BlockSpec(memory_space=pl.ANY),
            pl.BlockSpec(memory_space=pl.ANY),
            pl.BlockSpec(memory_space=pl.ANY),
            pl.BlockSpec((1, 1, 6, D_MODEL), lambda i: (layer, _group_of_tile(i, TM_F), 0, 0)),
        ],
        out_specs=pl.BlockSpec((TM_F, D_MODEL), lambda i: (i, 0)),
        out_shape=jax.ShapeDtypeStruct((NTOK, D_MODEL), F32),
        scratch_shapes=[pltpu.VMEM((TM_F, D_MODEL), F32)] + _ffn_weight_scratch(),
        compiler_params=pltpu.CompilerParams(
            dimension_semantics=("arbitrary",), vmem_limit_bytes=VMEM_LIMIT),
        name="ffn",
    )(h2, res, wg, wu, wd, mods)


def _tile_plan(counts):
    nt = (counts + TR - 1) // TR
    cum = jnp.cumsum(nt)
    total = cum[-1]
    t = jnp.arange(MAX_TILES, dtype=jnp.int32)
    tt = jnp.minimum(t, total - 1)
    e = jnp.sum((cum[None, :] <= tt[:, None]).astype(jnp.int32), axis=1)
    k = tt - (cum - nt)[e]
    n = jnp.where(t < total, jnp.clip(counts[e] - k * TR, 0, TR), 0)
    return e.astype(jnp.int32), (e * REG_TILES + k).astype(jnp.int32), n.astype(jnp.int32)


def _row_copy(src, src_row, dst, dst_row, sem):
    return pltpu.make_async_copy(src.at[pl.ds(src_row, 1)], dst.at[pl.ds(dst_row, 1)], sem)


def _dispatch_kernel(slot_ref, h_ref, xs_ref, sem):
    base = pl.program_id(0) * (TOP_K * TD)

    def issue(r, carry):
        for k in range(TOP_K):
            _row_copy(h_ref, r, xs_ref, slot_ref[base + TOP_K * r + k], sem).start()
        return carry

    lax.fori_loop(0, TD, issue, 0, unroll=8)
    for k in range(TOP_K):
        pltpu.make_async_copy(h_ref, xs_ref.at[pl.ds(0, TD)], sem).wait()


def _dispatch(slots, h2f):
    return pl.pallas_call(
        _dispatch_kernel,
        grid_spec=pltpu.PrefetchScalarGridSpec(
            num_scalar_prefetch=1,
            grid=(NTOK // TD,),
            in_specs=[pl.BlockSpec((TD, D_MODEL), lambda i, s: (i, 0))],
            out_specs=pl.BlockSpec(memory_space=pl.ANY),
            scratch_shapes=[pltpu.SemaphoreType.DMA],
        ),
        out_shape=jax.ShapeDtypeStruct((N_EXPERTS * REG, D_MODEL), F32),
        compiler_params=pltpu.CompilerParams(
            dimension_semantics=("arbitrary",), vmem_limit_bytes=VMEM_LIMIT),
        name="moe_dispatch",
    )(slots, h2f)


def _gffn_kernel(te_ref, tb_ref, tn_ref, x_ref, wg_ref, wu_ref, wd_ref, o_ref,
                 acc, w_gu, w_d, stg_gu, stg_d, sem, *, w_base):
    t = pl.program_id(0)
    n = tn_ref[t]

    @pl.when(n > 0)
    def _():
        row = lax.broadcasted_iota(jnp.int32, (TR, D_MODEL), 0)
        h = jnp.where(row < n, x_ref[...], 0.0).astype(BF16)
        first = tb_ref[t] % REG_TILES == 0

        @pl.when(first)
        def _():
            e = w_base + te_ref[t]
            _swiglu_tile(h, w_gu, w_d, acc, (wg_ref.at[e], wu_ref.at[e], wd_ref.at[e], stg_gu, stg_d, sem))

        @pl.when(jnp.logical_not(first))
        def _():
            _swiglu_tile(h, w_gu, w_d, acc)

        o_ref[...] = acc[...]


def _gffn(tile_e, tile_blk, tile_n, xs, wg, wu, wd, w_base):
    return pl.pallas_call(
        functools.partial(_gffn_kernel, w_base=w_base),
        grid_spec=pltpu.PrefetchScalarGridSpec(
            num_scalar_prefetch=3,
            grid=(MAX_TILES,),
            in_specs=[
                pl.BlockSpec((TR, D_MODEL), lambda t, te, tb, tn: (tb[t], 0)),
                pl.BlockSpec(memory_space=pl.ANY),
                pl.BlockSpec(memory_space=pl.ANY),
                pl.BlockSpec(memory_space=pl.ANY),
            ],
            out_specs=pl.BlockSpec((TR, D_MODEL), lambda t, te, tb, tn: (tb[t], 0)),
            scratch_shapes=[pltpu.VMEM((TR, D_MODEL), F32)] + _ffn_weight_scratch(),
        ),
        out_shape=jax.ShapeDtypeStruct((N_EXPERTS * REG, D_MODEL), F32),
        compiler_params=pltpu.CompilerParams(
            dimension_semantics=("arbitrary",), vmem_limit_bytes=VMEM_LIMIT),
        name="moe_ffn",
    )(tile_e, tile_blk, tile_n, xs, wg, wu, wd)


def _combine_kernel(slot_ref, x1_ref, rinfo_ref, mod_ref, ys_ref, *rest, final_norm):
    if final_norm:
        gf_ref, yp_ref, yl_ref, buf, sem = rest
    else:
        o_ref, buf, sem = rest
    base = pl.program_id(0) * (TOP_K * TD)

    def issue(r, carry):
        for k in range(TOP_K):
            _row_copy(ys_ref, slot_ref[base + TOP_K * r + k], buf.at[k], r, sem).start()
        return carry

    lax.fori_loop(0, TD, issue, 0, unroll=8)
    for k in range(TOP_K):
        pltpu.make_async_copy(ys_ref.at[pl.ds(0, TD)], buf.at[k], sem).wait()
    y = rinfo_ref[:, R_W1:R_W1 + 1] * buf[0] + rinfo_ref[:, R_W2:R_W2 + 1] * buf[1]
    x = x1_ref[...] + mod_ref[0, 0, 5:6, :] * y
    if final_norm:
        out = _rmsnorm(x, gf_ref[...])
        is_lat = pl.program_id(0) >= N_CTX // TD

        @pl.when(jnp.logical_not(is_lat))
        def _():
            yp_ref[...] = out

        @pl.when(is_lat)
        def _():
            yl_ref[...] = out
    else:
        o_ref[...] = x


def _combine(slots, x1, rinfo, mods, ys, layer, norm_f=None):
    final_norm = norm_f is not None
    n_ctx_t = N_CTX // TD
    in_specs = [
        pl.BlockSpec((TD, D_MODEL), lambda i, s: (i, 0)),
        pl.BlockSpec((TD, LANES), lambda i, s: (i, 0)),
        pl.BlockSpec((1, 1, 6, D_MODEL), lambda i, s: (layer, _group_of_tile(i, TD), 0, 0)),
        pl.BlockSpec(memory_space=pl.ANY),
    ]
    args = [slots, x1, rinfo, mods, ys]
    if final_norm:
        in_specs.append(pl.BlockSpec((1, D_MODEL), lambda i, s: (0, 0)))
        args.append(norm_f)
        out_specs = [pl.BlockSpec((TD, D_MODEL), lambda i, s: (jnp.minimum(i, n_ctx_t - 1), 0)),
                     pl.BlockSpec((TD, D_MODEL), lambda i, s: (jnp.maximum(i - n_ctx_t, 0), 0))]
        out_shape = [jax.ShapeDtypeStruct((N_CTX, D_MODEL), F32), jax.ShapeDtypeStruct((N_LAT, D_MODEL), F32)]
    else:
        out_specs = pl.BlockSpec((TD, D_MODEL), lambda i, s: (i, 0))
        out_shape = jax.ShapeDtypeStruct((NTOK, D_MODEL), F32)
    return pl.pallas_call(
        functools.partial(_combine_kernel, final_norm=final_norm),
        grid_spec=pltpu.PrefetchScalarGridSpec(
            num_scalar_prefetch=1,
            grid=(NTOK // TD,),
            in_specs=in_specs,
            out_specs=out_specs,
            scratch_shapes=[pltpu.VMEM((TOP_K, TD, D_MODEL), F32), pltpu.SemaphoreType.DMA],
        ),
        out_shape=out_shape,
        compiler_params=pltpu.CompilerParams(
            dimension_semantics=("arbitrary",), vmem_limit_bytes=VMEM_LIMIT),
        name="moe_combine",
    )(*args)


def _final_kernel(x_ref, g_ref, o_ref):
    o_ref[...] = _rmsnorm(x_ref[...], g_ref[...])


def _final(x, g, row_off, rows):
    off = row_off // TM
    return pl.pallas_call(
        _final_kernel,
        grid=(rows // TM,),
        in_specs=[
            pl.BlockSpec((TM, D_MODEL), lambda i: (off + i, 0)),
            pl.BlockSpec((1, D_MODEL), lambda i: (0, 0)),
        ],
        out_specs=pl.BlockSpec((TM, D_MODEL), lambda i: (i, 0)),
        out_shape=jax.ShapeDtypeStruct((rows, D_MODEL), F32),
        compiler_params=pltpu.CompilerParams(
            dimension_semantics=("arbitrary",), vmem_limit_bytes=VMEM_LIMIT),
        name="final_norm",
    )(x, g)


def kernel(x_prompt, x_sample, state_mlstm_C, state_mlstm_n, state_mlstm_m, state_ret_S, c, c_ctx,
           norm1_g, norm2_g, norm_f_g, w_ada, b_ada, w_in, b_gates, ret_decay_logit,
           mlstm_norm_g, ret_norm_g, w_out, ffn_w_gate, ffn_w_up, ffn_w_down,
           moe_w_router, moe_w_gate, moe_w_up, moe_w_down):
    xs_in = (x_prompt.reshape(N_CTX, D_MODEL), x_sample.reshape(N_LAT, D_MODEL))
    cvec = jnp.concatenate(
        [c_ctx[None, :], c, jnp.zeros((N_GROUPS - 1 - DEC_BATCH, D_MODEL), F32)], 0)
    mods = _ada(cvec, w_ada, b_ada).reshape(DEPTH, N_GROUPS, 6, D_MODEL)

    n_m = 4 * W_M
    w_in_t = jnp.swapaxes(w_in, 1, 2)
    n_if = N_GATES // 2
    lane_pad = ((0, 0), (0, 0), (0, LANES - n_if))
    wg = jnp.concatenate([jnp.pad(w_in[:, :, n_m:n_m + n_if], lane_pad),
                          jnp.pad(w_in[:, :, n_m + n_if:n_m + N_GATES], lane_pad)], -1)
    bg = jnp.concatenate([jnp.pad(b_gates[:, None, :n_if], lane_pad),
                          jnp.pad(b_gates[:, None, n_if:], lane_pad)], -1)
    cos_np, sin_np = _rope_tables()
    cos_t, sin_t = jnp.asarray(cos_np), jnp.asarray(sin_np)
    dl = jnp.broadcast_to(ret_decay_logit.reshape(DEPTH, 2 * H_R, 1), (DEPTH, 2 * H_R, LANES))
    m0 = jnp.pad(state_mlstm_m.reshape(DEC_BATCH, DEPTH, 1, 2 * H_M),
                 ((0, 0), (0, 0), (0, 0), (0, LANES - 2 * H_M)))
    g1 = norm1_g.reshape(DEPTH, 1, D_MODEL)
    g2 = norm2_g.reshape(DEPTH, 1, D_MODEL)
    nm = mlstm_norm_g.reshape(DEPTH, 1, W_M)
    nr = ret_norm_g.reshape(DEPTH, 1, W_R)
    n_moe = moe_w_router.shape[0]
    wr_pad = jnp.pad(moe_w_router, ((0, 0), (0, 0), (0, LANES - N_EXPERTS)))
    moe_g = moe_w_gate.reshape(n_moe * N_EXPERTS, D_MODEL, D_FF)
    moe_u = moe_w_up.reshape(n_moe * N_EXPERTS, D_MODEL, D_FF)
    moe_d = moe_w_down.reshape(n_moe * N_EXPERTS, D_FF, D_MODEL)

    states = ()
    xs = xs_in
    for l in range(DEPTH):
        jl = l // 2
        p, gates = _proj(xs, g1, mods, w_in_t, wg, bg, cos_t, sin_t, l)
        y, *states = _scan_ctx(p, gates, dl, nm, nr, l, states)
        y = _scan_lat(p, gates, dl, nm, nr, state_mlstm_C, state_mlstm_n, m0, state_ret_S, y, l)
        if l % 2 == 0:
            x1, h2 = _out(y, xs, w_out, g2, mods, l)
            x = _ffn(h2, x1, ffn_w_gate, ffn_w_up, ffn_w_down, mods, l, jl)
        else:
            x1, h2f, rinfo, cnt = _out(y, xs, w_out, g2, mods, l, wr_pad, jl)
            slots = rinfo[:, R_S1:R_S2 + 1].astype(jnp.int32).reshape(TOP_K * NTOK)
            tile_e, tile_blk, tile_n = _tile_plan(cnt[0, :N_EXPERTS].astype(jnp.int32))
            xd = _dispatch(slots, h2f)
            yd = _gffn(tile_e, tile_blk, tile_n, xd, moe_g, moe_u, moe_d, jl * N_EXPERTS)
            if l == DEPTH - 1:
                y_ctx, y_lat = _combine(slots, x1, rinfo, mods, yd, l, norm_f_g.reshape(1, D_MODEL))
            else:
                x = _combine(slots, x1, rinfo, mods, yd, l)
        xs = (x,)

    if DEPTH % 2 == 1:
        y_ctx = _final(x, norm_f_g.reshape(1, D_MODEL), 0, N_CTX)
        y_lat = _final(x, norm_f_g.reshape(1, D_MODEL), N_CTX, N_LAT)
    y_prompt = y_ctx.reshape(BATCH, SEQ, D_MODEL)
    y_sample = y_lat.reshape(DEC_BATCH, DEC_SEQ, D_MODEL)
    new_C, new_n, new_m, new_S = states
    return (y_prompt, y_sample, new_C, new_n,
            new_m[:, :, 0, :2 * H_M].reshape(BATCH, DEPTH, 2, H_M), new_S)
```

```python
import functools

import numpy as np
import jax
import jax.numpy as jnp
from jax import lax
from jax.experimental import pallas as pl
from jax.experimental.pallas import tpu as pltpu

D_MODEL = 1024
BATCH = 32
SEQ = 256
DEPTH = 2
DEC_BATCH = 2
DEC_SEQ = 1024
GRID_W = 64
H_M = 4
DH = 128
H_R = 4
W_M = H_M * DH
W_R = H_R * DH
N_GATES = 4 * H_M
CHUNK = 128
D_FF = 2816
N_EXPERTS = 8
ROPE_BASE = 10000.0
EPS = 1e-6

N_CTX = BATCH * SEQ
N_LAT = DEC_BATCH * DEC_SEQ
NTOK = N_CTX + N_LAT
N_GROUPS = 8
K_SCALE = DH ** -0.5
P_COLS = 4 * W_M + 4 * W_R
LANES = 128
GATE_LANES = 2 * LANES
VMEM_LIMIT = 56 * 1024 * 1024

F32 = jnp.float32
BF16 = jnp.bfloat16
HIGHEST = lax.Precision.HIGHEST

TM = 1024
TN = 1024
FC = 256
TM_F = 512
TOP_K = 2
TR = 896
REG_TILES = -(-NTOK // TR)
REG = REG_TILES * TR
MAX_TILES = -(-TOP_K * NTOK // TR) + N_EXPERTS
TD = 512


def _group_of_tile(i, tm):
    return jnp.maximum(i * tm // DEC_SEQ - (N_CTX // DEC_SEQ - 1), 0)


def _silu(x):
    return x * jax.nn.sigmoid(x)


def _log_sigmoid(x):
    return jnp.minimum(x, 0.0) - jnp.log(1.0 + jnp.exp(-jnp.abs(x)))


def _rmsnorm(x, g):
    return x * lax.rsqrt(jnp.mean(x * x, -1, keepdims=True) + EPS) * g


def _ada_kernel(cv_ref, w_ref, b_ref, o_ref):
    s = _silu(cv_ref[...]).astype(BF16)
    o_ref[0] = jnp.dot(s, w_ref[0].astype(BF16), preferred_element_type=F32) + b_ref[0]


def _ada(cvec, w_ada, b_ada):
    tn = 1536
    n = 6 * D_MODEL
    return pl.pallas_call(
        _ada_kernel,
        grid=(DEPTH, n // tn),
        in_specs=[
            pl.BlockSpec((N_GROUPS, D_MODEL), lambda l, j: (0, 0)),
            pl.BlockSpec((1, D_MODEL, tn), lambda l, j: (l, 0, j)),
            pl.BlockSpec((1, 1, tn), lambda l, j: (l, 0, j)),
        ],
        out_specs=pl.BlockSpec((1, N_GROUPS, tn), lambda l, j: (l, 0, j)),
        out_shape=jax.ShapeDtypeStruct((DEPTH, N_GROUPS, n), F32),
        compiler_params=pltpu.CompilerParams(
            dimension_semantics=("arbitrary", "arbitrary"), vmem_limit_bytes=VMEM_LIMIT),
        name="ada",
    )(cvec, w_ada, b_ada.reshape(DEPTH, 1, n))


def _rope_tables():
    half = DH // 4
    freqs = ROPE_BASE ** (-np.arange(half, dtype=np.float64) / half)
    t = np.arange(DEC_SEQ)
    pos = np.stack([t // GRID_W, t % GRID_W], 1).astype(np.float64)
    d = np.arange(DH)
    ang = pos[:, d // (DH // 2)] * freqs[d % half][None, :]
    sign = np.where((d % (DH // 2)) < half, -1.0, 1.0)[None, :]
    return np.cos(ang).astype(np.float32), (sign * np.sin(ang)).astype(np.float32)


def _rope(a, cos, sin):
    lane = lax.broadcasted_iota(jnp.int32, a.shape, 1)
    first = (lane % (DH // 2)) < (DH // 4)
    partner = jnp.where(first, pltpu.roll(a, DH - DH // 4, 1), pltpu.roll(a, DH // 4, 1))
    return a * cos + partner * sin


def _proj_kernel(*refs, n_ctx_tiles, split_x, layer):
    if split_x:
        (xp_ref, xl_ref, g_ref, mod_ref, wt_ref, wg_ref, bg_ref, cos_ref, sin_ref,
         p_ref, gate_ref, h_scr, w_res, w_stg, w_sem) = refs
    else:
        (x_ref, g_ref, mod_ref, wt_ref, wg_ref, bg_ref, cos_ref, sin_ref,
         p_ref, gate_ref, h_scr, w_res, w_stg, w_sem) = refs
    p_ref = p_ref.at[0]
    i = pl.program_id(0)
    j = pl.program_id(1)
    is_lat = i >= n_ctx_tiles
    half = TN // 2

    def prologue(x):
        h = _rmsnorm(x, g_ref[0]) * (1.0 + mod_ref[0, 0, 1:2, :]) + mod_ref[0, 0, 0:1, :]
        h_scr[...] = h.astype(BF16)
        gate_ref[...] = _dot_f32x3(h, wg_ref[0]) + bg_ref[0]

    @pl.when(j == 0)
    def _():
        if split_x:
            pl.when(jnp.logical_not(is_lat))(lambda: prologue(xp_ref[...]))
            pl.when(is_lat)(lambda: prologue(xl_ref[...]))
        else:
            prologue(x_ref[...])

    n_col_tiles = P_COLS // TN

    def tile_copy(jj):
        row0 = jj * TN + (N_GATES if jj * TN >= 4 * W_M else 0)
        s = jj % 2
        return pltpu.make_async_copy(wt_ref.at[layer, pl.ds(row0, TN), :], w_stg.at[s], w_sem.at[s])

    def matmul(jj):
        @pl.when(i == 0)
        def _():
            if jj == 0:
                tile_copy(0).start()
            if jj + 1 < n_col_tiles:
                tile_copy(jj + 1).start()
            tile_copy(jj).wait()
            w_res[jj] = w_stg[jj % 2].astype(BF16)

        return _dot_nt(h_scr[...], w_res[jj])

    @pl.when(j == 0)
    def _():
        acc = matmul(0)
        p_ref[:, :half] = acc[:, :half].astype(BF16)
        p_ref[:, half:] = (acc[:, half:] * K_SCALE).astype(BF16)

    @pl.when(j == 1)
    def _():
        p_ref[...] = matmul(1).astype(BF16)

    @pl.when(j == 2)
    def _():
        acc = matmul(2)

        @pl.when(is_lat)
        def _():
            cos = cos_ref[...]
            sin = sin_ref[...]
            for hd in range(TN // DH):
                sl = slice(hd * DH, (hd + 1) * DH)
                r = _rope(acc[:, sl], cos, sin)
                p_ref[:, sl] = (r * K_SCALE if hd * DH >= half else r).astype(BF16)

        @pl.when(jnp.logical_not(is_lat))
        def _():
            p_ref[:, :half] = acc[:, :half].astype(BF16)
            p_ref[:, half:] = (acc[:, half:] * K_SCALE).astype(BF16)

    @pl.when(j == 3)
    def _():
        p_ref[...] = matmul(3).astype(BF16)


def _proj(xs, g1, mods, w_in_t, wg, bg, cos_t, sin_t, layer):
    n_ctx_tiles = N_CTX // TM
    tiles_per_seq = DEC_SEQ // TM
    split_x = len(xs) == 2
    if split_x:
        x_specs = [pl.BlockSpec((TM, D_MODEL), lambda i, j: (jnp.minimum(i, n_ctx_tiles - 1), 0)),
                   pl.BlockSpec((TM, D_MODEL), lambda i, j: (jnp.maximum(i - n_ctx_tiles, 0), 0))]
    else:
        x_specs = [pl.BlockSpec((TM, D_MODEL), lambda i, j: (i, 0))]
    return pl.pallas_call(
        functools.partial(_proj_kernel, n_ctx_tiles=n_ctx_tiles, split_x=split_x, layer=layer),
        grid=(NTOK // TM, P_COLS // TN),
        in_specs=x_specs + [
            pl.BlockSpec((1, 1, D_MODEL), lambda i, j: (layer, 0, 0)),
            pl.BlockSpec((1, 1, 6, D_MODEL), lambda i, j: (layer, _group_of_tile(i, TM), 0, 0)),
            pl.BlockSpec(memory_space=pl.ANY),
            pl.BlockSpec((1, D_MODEL, GATE_LANES), lambda i, j: (layer, 0, 0)),
            pl.BlockSpec((1, 1, GATE_LANES), lambda i, j: (layer, 0, 0)),
            pl.BlockSpec((TM, DH), lambda i, j: (i % tiles_per_seq, 0)),
            pl.BlockSpec((TM, DH), lambda i, j: (i % tiles_per_seq, 0)),
        ],
        out_specs=[
            pl.BlockSpec((1, TM, TN), lambda i, j: (j, i, 0)),
            pl.BlockSpec((TM, GATE_LANES), lambda i, j: (i, 0)),
        ],
        out_shape=[
            jax.ShapeDtypeStruct((P_COLS // TN, NTOK, TN), BF16),
            jax.ShapeDtypeStruct((NTOK, GATE_LANES), F32),
        ],
        scratch_shapes=[pltpu.VMEM((TM, D_MODEL), BF16),
                        pltpu.VMEM((P_COLS // TN, TN, D_MODEL), BF16),
                        pltpu.VMEM((2, TN, D_MODEL), F32),
                        pltpu.SemaphoreType.DMA((2,))],
        compiler_params=pltpu.CompilerParams(
            dimension_semantics=("arbitrary", "arbitrary"), vmem_limit_bytes=VMEM_LIMIT),
        name="proj",
    )(*xs, g1, mods, w_in_t, wg, bg, cos_t, sin_t)


def _split3(x):
    hi = x.astype(BF16)
    r1 = x - hi.astype(F32)
    mid = r1.astype(BF16)
    lo = (r1 - mid.astype(F32)).astype(BF16)
    return hi, mid, lo


def _dot(a, b):
    return jnp.dot(a, b, preferred_element_type=F32)


def _dot_nt(a, b):
    return lax.dot_general(a, b, (((1,), (1,)), ((), ())), preferred_element_type=F32)


def _tri_dot_left(tri, x):
    hi, mid, lo = _split3(x)
    return _dot(tri, hi) + _dot(tri, mid) + _dot(tri, lo)


def _tri_dot_right(x, tri):
    hi, mid, lo = _split3(x)
    return _dot(hi, tri) + _dot(mid, tri) + _dot(lo, tri)


def _run_max(x, reverse):
    n_tiles = x.shape[0] // 8
    sub = lax.broadcasted_iota(jnp.int32, (8, LANES), 0)
    out = [None] * n_tiles
    carry = None
    for t in (range(n_tiles - 1, -1, -1) if reverse else range(n_tiles)):
        v = x[8 * t:8 * t + 8, :]
        for s in (1, 2, 4):
            if reverse:
                v = jnp.maximum(v, jnp.where(sub < 8 - s, pltpu.roll(v, 8 - s, 0), -jnp.inf))
            else:
                v = jnp.maximum(v, jnp.where(sub >= s, pltpu.roll(v, s, 0), -jnp.inf))
        if carry is not None:
            v = jnp.maximum(v, carry)
        carry = jnp.broadcast_to(v[0:1, :] if reverse else v[7:8, :], (8, LANES))
        out[t] = v
    return jnp.concatenate(out, axis=0)


def _scan_kernel(*refs, T, has_state, n_prev=0):
    if has_state:
        (p_ref, g_ref, dl_ref, nm_ref, nr_ref, C0_ref, n0_ref, m0_ref, S0_ref, _yprev_ref,
         y_ref, CN_s, S_s, m_s, hf_s, hb_s, dm_s, dq_s, dk_s, dL_s, kT_s) = refs
    else:
        p_ref, g_ref, dl_ref, nm_ref, nr_ref = refs[:5]
        prev_refs = refs[5:9] if n_prev else ()
        (y_ref, C_out, n_out, m_out, S_out,
         CN_s, S_s, m_s, hf_s, hb_s, dm_s, dq_s, dk_s, dL_s, kT_s) = refs[5 + len(prev_refs):]
    L = CHUNK
    n_chunks = T // L
    row_i = lax.broadcasted_iota(jnp.int32, (L, L), 0)
    col_j = lax.broadcasted_iota(jnp.int32, (L, L), 1)
    lower = col_j <= row_i
    upper = col_j >= row_i
    tril = lower.astype(BF16)
    triu = upper.astype(BF16)
    ones = jnp.ones((L, DH), BF16)
    c_km = W_M
    c_vm = 2 * W_M
    c_om = 3 * W_M
    c_qr = 4 * W_M
    c_kr = c_qr + W_R
    c_vr = c_qr + 2 * W_R
    c_gr = c_qr + 3 * W_R

    def pcols(rows, col):
        return p_ref[col // TN, rows, col % TN:col % TN + DH]

    for d in range(2):
        for h in range(H_M):
            k = d * H_M + h
            if has_state:
                CN_s[k, :, :DH] = C0_ref[0, 0, d, h]
                CN_s[k, :, DH:] = jnp.broadcast_to(n0_ref[0, 0, d, h:h + 1, :], (DH, DH)).T
                S_s[k] = S0_ref[0, 0, d, h]
            else:
                CN_s[k] = jnp.zeros((DH, 2 * DH), F32)
                S_s[k] = jnp.zeros((DH, DH), F32)
    m_s[...] = m0_ref[0, 0] if has_state else jnp.zeros((1, LANES), F32)

    @pl.when(pl.program_id(0) == 0)
    def _():
        pos_i = row_i.astype(F32)
        pos_j = col_j.astype(F32)
        for d in range(2):
            for h in range(H_R):
                k = d * H_R + h
                lg_row = _log_sigmoid(dl_ref[0, k:k + 1, :])
                lg = jnp.broadcast_to(lg_row, (L, L))
                rel = (row_i - col_j if d == 0 else col_j - row_i).astype(F32)
                dm_s[k] = jnp.where(rel >= 0, jnp.exp(lg * jnp.maximum(rel, 0.0)), 0.0)
                dq_s[k] = jnp.exp(lg * (pos_i + 1.0 if d == 0 else L - pos_i))
                dk_s[k] = jnp.exp(lg * (L - 1.0 - pos_j if d == 0 else pos_j))
                dL_s[k] = jnp.exp(lg_row * float(L))

    def transpose_keys(c, carry):
        r0 = pl.multiple_of(c * L, L)
        for h in range(H_M):
            kT_s[h, c] = pcols(pl.ds(r0, L), c_km + h * DH).astype(F32).T
            kT_s[H_M + h, c] = pcols(pl.ds(r0, L), c_kr + h * DH).astype(F32).T
        return carry

    lax.fori_loop(0, n_chunks, transpose_keys, 0)

    def chunk_step(c, carry):
        m_prev = m_s[...]
        m_new = []
        prep = []
        for d in range(2):
            ci = c if d == 0 else n_chunks - 1 - c
            r0 = pl.multiple_of(ci * L, L)
            mask = lower if d == 0 else upper
            e_row = L - 1 if d == 0 else 0
            FL = _log_sigmoid(g_ref[pl.ds(r0, L), LANES:2 * LANES])
            Bc = _tri_dot_left(tril if d == 0 else triu, FL)
            Zc = g_ref[pl.ds(r0, L), 0:LANES] - Bc
            M = jnp.maximum(_run_max(Zc, reverse=(d == 1)), m_prev)
            m_row = Bc + M
            M_end = M[e_row:e_row + 1, :]
            m_new.append(Bc[e_row:e_row + 1, :] + M_end)
            decay = jnp.exp(m_prev - M_end)
            prep.append(dict(ci=ci, r0=r0, mask=mask, M=M, m_row=m_row, decay=decay,
                             ZT=Zc.T,
                             WT=jnp.exp(Zc - M_end).T))
        pairs = [(d, h) for d in range(2) for h in range(H_M)]

        def rows(d, col):
            return pcols(pl.ds(prep[d]["r0"], L), col)

        qk, qkr = {}, {}
        for d, h in pairs:
            qk[d, h] = _dot_nt(rows(d, h * DH), rows(d, c_km + h * DH))
            qkr[d, h] = _dot_nt(rows(d, c_qr + h * DH), rows(d, c_kr + h * DH))
        upd, updr = {}, {}
        for d, h in pairs:
            k = d * H_M + h
            ci = prep[d]["ci"]
            vo = jnp.concatenate([rows(d, c_vm + h * DH), ones], axis=1)
            wkT = (kT_s[h, ci] * jnp.broadcast_to(prep[d]["WT"][k:k + 1, :], (DH, L))).astype(BF16)
            upd[d, h] = _dot(wkT, vo)
            kdT = (kT_s[H_M + h, ci] * dk_s[k]).astype(BF16)
            updr[d, h] = _dot(kdT, rows(d, c_vr + h * DH))
        for d, h in pairs:
            k = d * H_M + h
            r0 = prep[d]["r0"]
            h_dst = hf_s if d == 0 else hb_s
            q = rows(d, h * DH)
            M_col = jnp.broadcast_to(prep[d]["M"][:, k:k + 1], (L, L))
            z_row = jnp.broadcast_to(prep[d]["ZT"][k:k + 1, :], (L, L))
            D = jnp.where(prep[d]["mask"], jnp.exp(z_row - M_col), 0.0)
            s = (qk[d, h] * D).astype(BF16)
            w_inter = jnp.exp(jnp.broadcast_to(m_prev[:, k:k + 1], (L, L)) - M_col)
            wq = (w_inter * q.astype(F32)).astype(BF16)
            vo = jnp.concatenate([rows(d, c_vm + h * DH), ones], axis=1)
            CN = CN_s[k]
            res = _dot(jnp.concatenate([s, wq], axis=1),
                       jnp.concatenate([vo, CN.astype(BF16)], axis=0))
            floor = jnp.exp(-jnp.broadcast_to(prep[d]["m_row"][:, k:k + 1], (L, L)))
            h_dst[pl.ds(r0, L), h * DH:(h + 1) * DH] = res[:, :DH] / jnp.maximum(jnp.abs(res[:, DH:]), floor)
            CN_s[k] = jnp.broadcast_to(prep[d]["decay"][:, k:k + 1], (DH, 2 * DH)) * CN + upd[d, h]
            qr = rows(d, c_qr + h * DH)
            S = S_s[k]
            sr = (qkr[d, h] * dm_s[k]).astype(BF16)
            qd = (qr.astype(F32) * dq_s[k]).astype(BF16)
            h_dst[pl.ds(r0, L), W_M + h * DH:W_M + (h + 1) * DH] = _dot(
                jnp.concatenate([sr, qd], axis=1),
                jnp.concatenate([rows(d, c_vr + h * DH), S.astype(BF16)], axis=0))
            S_s[k] = dL_s[k] * S + updr[d, h]
        lane = lax.broadcasted_iota(jnp.int32, (1, LANES), 1)
        m_s[...] = jnp.where(lane < H_M, m_new[0], m_new[1])
        return carry

    lax.fori_loop(0, n_chunks, chunk_step, 0)

    for h in range(H_M):
        sl = slice(h * DH, (h + 1) * DH)
        hs = hf_s[:, sl] + hb_s[:, sl]
        yn = _rmsnorm(hs, nm_ref[0, :, sl])
        om = pcols(slice(None), c_om + h * DH).astype(F32)
        y_ref[:, sl] = (jax.nn.sigmoid(om) * yn).astype(BF16)
        slr = slice(W_M + h * DH, W_M + (h + 1) * DH)
        hr = hf_s[:, slr] + hb_s[:, slr]
        ynr = _rmsnorm(hr, nr_ref[0, :, sl])
        gr = pcols(slice(None), c_gr + h * DH).astype(F32)
        y_ref[:, slr] = (_silu(gr) * ynr).astype(BF16)

    if not has_state:
        for prev, out in zip(prev_refs, (C_out, n_out, m_out, S_out)):
            out[0, :n_prev] = prev[0]
        for d in range(2):
            for h in range(H_M):
                k = d * H_M + h
                C_out[0, n_prev, d, h] = CN_s[k, :, :DH]
                n_out[0, n_prev, d, h:h + 1, :] = CN_s[k, :, DH:].T[0:1, :]
                S_out[0, n_prev, d, h] = S_s[k]
        m_out[0, n_prev] = m_s[...]


def _scan_scratch(T):
    return [
        pltpu.VMEM((2 * H_M, DH, 2 * DH), F32),
        pltpu.VMEM((2 * H_R, DH, DH), F32),
        pltpu.VMEM((1, LANES), F32),
        pltpu.VMEM((T, W_M + W_R), F32),
        pltpu.VMEM((T, W_M + W_R), F32),
        pltpu.VMEM((2 * H_R, CHUNK, CHUNK), F32),
        pltpu.VMEM((2 * H_R, CHUNK, CHUNK), F32),
        pltpu.VMEM((2 * H_R, CHUNK, CHUNK), F32),
        pltpu.VMEM((2 * H_R, 1, LANES), F32),
        pltpu.VMEM((H_M + H_R, T // CHUNK, DH, CHUNK), F32),
    ]


def _scan_ctx(p, gates, dl, nm, nr, layer, prev_states=()):
    T = SEQ
    n_lay = layer + 1
    state_tails = [(2, H_M, DH, DH), (2, H_M, DH), (1, LANES), (2, H_R, DH, DH)]

    def state_spec(n, tail):
        return pl.BlockSpec((1, n) + tail, lambda b: (b,) + (0,) * (1 + len(tail)))
    common = [
        pl.BlockSpec((P_COLS // TN, T, TN), lambda b: (0, b, 0)),
        pl.BlockSpec((T, GATE_LANES), lambda b: (b, 0)),
        pl.BlockSpec((1, 2 * H_R, LANES), lambda b: (layer, 0, 0)),
        pl.BlockSpec((1, 1, W_M), lambda b: (layer, 0, 0)),
        pl.BlockSpec((1, 1, W_R), lambda b: (layer, 0, 0)),
    ]
    return pl.pallas_call(
        functools.partial(_scan_kernel, T=T, has_state=False, n_prev=layer if prev_states else 0),
        grid=(BATCH,),
        in_specs=common + [state_spec(layer, tail) for tail in state_tails[:len(prev_states)]],
        out_specs=[pl.BlockSpec((T, D_MODEL), lambda b: (b, 0))] + [state_spec(n_lay, tail) for tail in state_tails],
        out_shape=[jax.ShapeDtypeStruct((NTOK, D_MODEL), BF16)] + [
            jax.ShapeDtypeStruct((BATCH, n_lay) + tail, F32) for tail in state_tails],
        scratch_shapes=_scan_scratch(T),
        compiler_params=pltpu.CompilerParams(
            dimension_semantics=("arbitrary",), vmem_limit_bytes=VMEM_LIMIT),
        name="scan_ctx",
    )(p, gates, dl, nm, nr, *prev_states)


def _scan_lat(p, gates, dl, nm, nr, C0, n0, m0, S0, y_prev, layer):
    T = DEC_SEQ
    off = N_CTX // T
    in_specs = [
        pl.BlockSpec((P_COLS // TN, T, TN), lambda b: (0, off + b, 0)),
        pl.BlockSpec((T, GATE_LANES), lambda b: (off + b, 0)),
        pl.BlockSpec((1, 2 * H_R, LANES), lambda b: (layer, 0, 0)),
        pl.BlockSpec((1, 1, W_M), lambda b: (layer, 0, 0)),
        pl.BlockSpec((1, 1, W_R), lambda b: (layer, 0, 0)),
        pl.BlockSpec((1, 1, 2, H_M, DH, DH), lambda b: (b, layer, 0, 0, 0, 0)),
        pl.BlockSpec((1, 1, 2, H_M, DH), lambda b: (b, layer, 0, 0, 0)),
        pl.BlockSpec((1, 1, 1, LANES), lambda b: (b, layer, 0, 0)),
        pl.BlockSpec((1, 1, 2, H_R, DH, DH), lambda b: (b, layer, 0, 0, 0, 0)),
        pl.BlockSpec(memory_space=pl.ANY),
    ]
    return pl.pallas_call(
        functools.partial(_scan_kernel, T=T, has_state=True),
        grid=(DEC_BATCH,),
        in_specs=in_specs,
        out_specs=pl.BlockSpec((T, D_MODEL), lambda b: (off + b, 0)),
        out_shape=jax.ShapeDtypeStruct((NTOK, D_MODEL), BF16),
        input_output_aliases={9: 0},
        scratch_shapes=_scan_scratch(T),
        compiler_params=pltpu.CompilerParams(
            dimension_semantics=("arbitrary",), vmem_limit_bytes=VMEM_LIMIT),
        name="scan_lat",
    )(p, gates, dl, nm, nr, C0, n0, m0, S0, y_prev)


def _top2(logits):
    lane = lax.broadcasted_iota(jnp.int32, logits.shape, 1)
    v1 = jnp.max(logits, -1, keepdims=True)
    i1 = jnp.min(jnp.where(logits == v1, lane, LANES), -1, keepdims=True)
    rest = jnp.where(lane == i1, -jnp.inf, logits)
    v2 = jnp.max(rest, -1, keepdims=True)
    i2 = jnp.min(jnp.where(rest == v2, lane, LANES), -1, keepdims=True)
    e2 = jnp.exp(v2 - v1)
    return i1, i2, 1.0 / (1.0 + e2), e2 / (1.0 + e2)


def _split2(x):
    hi = x.astype(BF16)
    return hi, (x - hi.astype(F32)).astype(BF16)


def _dot_f32x3(a, b):
    a_hi, a_lo = _split2(a)
    b_hi, b_lo = _split2(b)
    return _dot(a_hi, b_hi) + _dot(a_hi, b_lo) + _dot(a_lo, b_hi)


R_E1, R_E2, R_W1, R_W2, R_S1, R_S2 = range(6)


def _out_kernel(*refs, with_router, split_x):
    y_ref = refs[0]
    if split_x:
        xp_ref, xl_ref = refs[1:3]
        x_in = jnp.where(pl.program_id(0) >= N_CTX // TM, xl_ref[...], xp_ref[...])
    else:
        x_in = refs[1][...]
    refs = refs[3:] if split_x else refs[2:]
    if with_router:
        (w_ref, g_ref, mod_ref, wr_ref,
         x1_ref, h2_ref, rinfo_ref, cnt_ref, w_scr, tri_scr, cnt_scr) = refs
    else:
        w_ref, g_ref, mod_ref, x1_ref, h2_ref, w_scr = refs

    @pl.when(pl.program_id(0) == 0)
    def _():
        w_scr[...] = w_ref[0].astype(BF16)
        if with_router:
            r = lax.broadcasted_iota(jnp.int32, (TM, TM), 0)
            c = lax.broadcasted_iota(jnp.int32, (TM, TM), 1)
            tri_scr[...] = (c < r).astype(BF16)
            cnt_scr[...] = jnp.zeros_like(cnt_scr)

    o = jnp.dot(y_ref[...], w_scr[...], preferred_element_type=F32)
    x1 = x_in + mod_ref[0, 0, 2:3, :] * o
    x1_ref[...] = x1
    h2 = _rmsnorm(x1, g_ref[0]) * (1.0 + mod_ref[0, 0, 4:5, :]) + mod_ref[0, 0, 3:4, :]
    if not with_router:
        h2_ref[...] = h2.astype(BF16)
    else:
        h2_ref[...] = h2
        logits = _dot_f32x3(h2, wr_ref[0])
        lane = lax.broadcasted_iota(jnp.int32, logits.shape, 1)
        i1, i2, w1, w2 = _top2(jnp.where(lane < N_EXPERTS, logits, -jnp.inf))
        oh1 = lane == i1
        oh2 = lane == i2
        sel = jnp.where(oh1 | oh2, 1.0, 0.0)
        rank = _dot(tri_scr[...], sel.astype(BF16)) + cnt_scr[...]
        r1 = jnp.sum(jnp.where(oh1, rank, 0.0), -1, keepdims=True)
        r2 = jnp.sum(jnp.where(oh2, rank, 0.0), -1, keepdims=True)
        s1 = i1.astype(F32) * float(REG) + r1
        s2 = i2.astype(F32) * float(REG) + r2
        info = jnp.zeros(logits.shape, F32)
        for col, val in ((R_E1, i1.astype(F32)), (R_E2, i2.astype(F32)), (R_W1, w1), (R_W2, w2),
                         (R_S1, s1), (R_S2, s2)):
            info = jnp.where(lane == col, val, info)
        rinfo_ref[...] = info
        cnt_scr[...] += jnp.sum(sel, 0, keepdims=True)
        cnt_ref[...] = cnt_scr[...]


def _out(y, xs, w_out, g2, mods, layer, w_router_pad=None, router_idx=0):
    with_router = w_router_pad is not None
    split_x = len(xs) == 2
    n_ctx_tiles = N_CTX // TM
    if split_x:
        x_specs = [pl.BlockSpec((TM, D_MODEL), lambda i: (jnp.minimum(i, n_ctx_tiles - 1), 0)),
                   pl.BlockSpec((TM, D_MODEL), lambda i: (jnp.maximum(i - n_ctx_tiles, 0), 0))]
    else:
        x_specs = [pl.BlockSpec((TM, D_MODEL), lambda i: (i, 0))]
    in_specs = [pl.BlockSpec((TM, D_MODEL), lambda i: (i, 0))] + x_specs + [
        pl.BlockSpec((1, D_MODEL, D_MODEL), lambda i: (layer, 0, 0)),
        pl.BlockSpec((1, 1, D_MODEL), lambda i: (layer, 0, 0)),
        pl.BlockSpec((1, 1, 6, D_MODEL), lambda i: (layer, _group_of_tile(i, TM), 0, 0)),
    ]
    out_specs = [
        pl.BlockSpec((TM, D_MODEL), lambda i: (i, 0)),
        pl.BlockSpec((TM, D_MODEL), lambda i: (i, 0)),
    ]
    out_shape = [
        jax.ShapeDtypeStruct((NTOK, D_MODEL), F32),
        jax.ShapeDtypeStruct((NTOK, D_MODEL), F32 if with_router else BF16),
    ]
    args = [y, *xs, w_out, g2, mods]
    scratch = [pltpu.VMEM((D_MODEL, D_MODEL), BF16)]
    if with_router:
        in_specs.append(pl.BlockSpec((1, D_MODEL, LANES), lambda i: (router_idx, 0, 0)))
        out_specs += [pl.BlockSpec((TM, LANES), lambda i: (i, 0)),
                      pl.BlockSpec((1, LANES), lambda i: (0, 0))]
        out_shape += [jax.ShapeDtypeStruct((NTOK, LANES), F32),
                      jax.ShapeDtypeStruct((1, LANES), F32)]
        args.append(w_router_pad)
        scratch += [pltpu.VMEM((TM, TM), BF16), pltpu.VMEM((1, LANES), F32)]
    return pl.pallas_call(
        functools.partial(_out_kernel, with_router=with_router, split_x=split_x),
        grid=(NTOK // TM,),
        in_specs=in_specs,
        out_specs=out_specs,
        out_shape=out_shape,
        scratch_shapes=scratch,
        compiler_params=pltpu.CompilerParams(
            dimension_semantics=("arbitrary",), vmem_limit_bytes=VMEM_LIMIT),
        name="out_router" if with_router else "out",
    )(*args)


N_FC = D_FF // FC
def _swiglu_tile(h, w_gu, w_d, acc, fetch=None):
    if fetch is not None:
        wg_hbm, wu_hbm, wd_hbm, stg_gu, stg_d, sem = fetch

        def copies(f):
            s = f % 2
            cols = pl.ds(f * FC, FC)
            return (pltpu.make_async_copy(wg_hbm.at[:, cols], stg_gu.at[s, 0], sem.at[s, 0]),
                    pltpu.make_async_copy(wu_hbm.at[:, cols], stg_gu.at[s, 1], sem.at[s, 1]),
                    pltpu.make_async_copy(wd_hbm.at[cols, :], stg_d.at[s], sem.at[s, 2]))

        def start(f):
            for c in copies(f):
                c.start()

        def land(f):
            for c in copies(f):
                c.wait()
            s = f % 2
            w_gu[f, :, :FC] = stg_gu[s, 0].astype(BF16)
            w_gu[f, :, FC:] = stg_gu[s, 1].astype(BF16)
            w_d[f * FC:(f + 1) * FC, :] = stg_d[s].astype(BF16)
    else:
        start = land = lambda f: None

    def up(f):
        return jnp.dot(h, w_gu[f], preferred_element_type=F32)

    start(0)
    if N_FC > 1:
        start(1)
    land(0)
    ab = up(0)
    for f in range(N_FC):
        if f + 2 < N_FC:
            start(f + 2)
        if f + 1 < N_FC:
            land(f + 1)
            ab_next = up(f + 1)
        t = (_silu(ab[:, :FC]) * ab[:, FC:]).astype(BF16)
        contrib = jnp.dot(t, w_d[f * FC:(f + 1) * FC, :], preferred_element_type=F32)
        if f == 0:
            acc[...] = contrib
        else:
            acc[...] += contrib
        if f + 1 < N_FC:
            ab = ab_next


def _ffn_weight_scratch():
    return [
        pltpu.VMEM((N_FC, D_MODEL, 2 * FC), BF16),
        pltpu.VMEM((D_FF, D_MODEL), BF16),
        pltpu.VMEM((2, 2, D_MODEL, FC), F32),
        pltpu.VMEM((2, FC, D_MODEL), F32),
        pltpu.SemaphoreType.DMA((2, 3)),
    ]


def _ffn_kernel(h_ref, res_ref, wg_ref, wu_ref, wd_ref, mod_ref, o_ref,
                acc, w_gu, w_d, stg_gu, stg_d, sem, *, w_idx):
    first = pl.program_id(0) == 0

    @pl.when(first)
    def _():
        _swiglu_tile(h_ref[...], w_gu, w_d, acc,
                     (wg_ref.at[w_idx], wu_ref.at[w_idx], wd_ref.at[w_idx], stg_gu, stg_d, sem))

    @pl.when(jnp.logical_not(first))
    def _():
        _swiglu_tile(h_ref[...], w_gu, w_d, acc)

    o_ref[...] = res_ref[...] + mod_ref[0, 0, 5:6, :] * acc[...]


def _ffn(h2, res, wg, wu, wd, mods, layer, w_idx):
    return pl.pallas_call(
        functools.partial(_ffn_kernel, w_idx=w_idx),
        grid=(NTOK // TM_F,),
        in_specs=[
            pl.BlockSpec((TM_F, D_MODEL), lambda i: (i, 0)),
            pl.BlockSpec((TM_F, D_MODEL), lambda i: (i, 0)),
            pl.BlockSpec(memory_space=pl.ANY),
            pl.BlockSpec(memory_space=pl.ANY),
            pl.BlockSpec(memory_space=pl.ANY),
            pl.BlockSpec((1, 1, 6, D_MODEL), lambda i: (layer, _group_of_tile(i, TM_F), 0, 0)),
        ],
        out_specs=pl.BlockSpec((TM_F, D_MODEL), lambda i: (i, 0)),
        out_shape=jax.ShapeDtypeStruct((NTOK, D_MODEL), F32),
        scratch_shapes=[pltpu.VMEM((TM_F, D_MODEL), F32)] + _ffn_weight_scratch(),
        compiler_params=pltpu.CompilerParams(
            dimension_semantics=("arbitrary",), vmem_limit_bytes=VMEM_LIMIT),
        name="ffn",
    )(h2, res, wg, wu, wd, mods)


def _tile_plan(counts):
    nt = (counts + TR - 1) // TR
    cum = jnp.cumsum(nt)
    total = cum[-1]
    t = jnp.arange(MAX_TILES, dtype=jnp.int32)
    tt = jnp.minimum(t, total - 1)
    e = jnp.sum((cum[None, :] <= tt[:, None]).astype(jnp.int32), axis=1)
    k = tt - (cum - nt)[e]
    n = jnp.where(t < total, jnp.clip(counts[e] - k * TR, 0, TR), 0)
    return e.astype(jnp.int32), (e * REG_TILES + k).astype(jnp.int32), n.astype(jnp.int32)


def _row_copy(src, src_row, dst, dst_row, sem):
    return pltpu.make_async_copy(src.at[pl.ds(src_row, 1)], dst.at[pl.ds(dst_row, 1)], sem)


def _dispatch_kernel(slot_ref, h_ref, xs_ref, sem):
    n_blocks = NTOK // TD

    def issue_block(b, s):
        def issue(r, carry):
            tok = b * TD + r
            for k in range(TOP_K):
                _row_copy(h_ref, tok, xs_ref, slot_ref[TOP_K * tok + k], sem.at[s]).start()
            return carry

        lax.fori_loop(0, TD, issue, 0, unroll=8)

    def wait_block(s):
        for k in range(TOP_K):
            pltpu.make_async_copy(h_ref.at[pl.ds(0, TD)], xs_ref.at[pl.ds(0, TD)], sem.at[s]).wait()

    issue_block(0, 0)

    def pair(i, carry):
        issue_block(2 * i + 1, 1)
        wait_block(0)

        @pl.when(2 * i + 2 < n_blocks)
        def _():
            issue_block(2 * i + 2, 0)

        wait_block(1)
        return carry

    lax.fori_loop(0, n_blocks // 2, pair, 0)


def _dispatch(slots, h2f):
    assert (NTOK // TD) % 2 == 0
    return pl.pallas_call(
        _dispatch_kernel,
        grid_spec=pltpu.PrefetchScalarGridSpec(
            num_scalar_prefetch=1,
            grid=(1,),
            in_specs=[pl.BlockSpec(memory_space=pl.ANY)],
            out_specs=pl.BlockSpec(memory_space=pl.ANY),
            scratch_shapes=[pltpu.SemaphoreType.DMA((2,))],
        ),
        out_shape=jax.ShapeDtypeStruct((N_EXPERTS * REG, D_MODEL), F32),
        compiler_params=pltpu.CompilerParams(
            dimension_semantics=("arbitrary",), vmem_limit_bytes=VMEM_LIMIT),
        name="moe_dispatch",
    )(slots, h2f)


def _gffn_kernel(te_ref, tb_ref, tn_ref, x_ref, wg_ref, wu_ref, wd_ref, o_ref,
                 acc, w_gu, w_d, stg_gu, stg_d, sem, *, w_base):
    t = pl.program_id(0)
    n = tn_ref[t]

    @pl.when(n > 0)
    def _():
        row = lax.broadcasted_iota(jnp.int32, (TR, D_MODEL), 0)
        h = jnp.where(row < n, x_ref[...], 0.0).astype(BF16)
        first = tb_ref[t] % REG_TILES == 0

        @pl.when(first)
        def _():
            e = w_base + te_ref[t]
            _swiglu_tile(h, w_gu, w_d, acc, (wg_ref.at[e], wu_ref.at[e], wd_ref.at[e], stg_gu, stg_d, sem))

        @pl.when(jnp.logical_not(first))
        def _():
            _swiglu_tile(h, w_gu, w_d, acc)

        o_ref[...] = acc[...]


def _gffn(tile_e, tile_blk, tile_n, xs, wg, wu, wd, w_base):
    return pl.pallas_call(
        functools.partial(_gffn_kernel, w_base=w_base),
        grid_spec=pltpu.PrefetchScalarGridSpec(
            num_scalar_prefetch=3,
            grid=(MAX_TILES,),
            in_specs=[
                pl.BlockSpec((TR, D_MODEL), lambda t, te, tb, tn: (tb[t], 0)),
                pl.BlockSpec(memory_space=pl.ANY),
                pl.BlockSpec(memory_space=pl.ANY),
                pl.BlockSpec(memory_space=pl.ANY),
            ],
            out_specs=pl.BlockSpec((TR, D_MODEL), lambda t, te, tb, tn: (tb[t], 0)),
            scratch_shapes=[pltpu.VMEM((TR, D_MODEL), F32)] + _ffn_weight_scratch(),
        ),
        out_shape=jax.ShapeDtypeStruct((N_EXPERTS * REG, D_MODEL), F32),
        compiler_params=pltpu.CompilerParams(
            dimension_semantics=("arbitrary",), vmem_limit_bytes=VMEM_LIMIT),
        name="moe_ffn",
    )(tile_e, tile_blk, tile_n, xs, wg, wu, wd)


def _combine_kernel(slot_ref, x1_ref, rinfo_ref, mod_ref, ys_ref, *rest, final_norm):
    if final_norm:
        gf_ref, yp_ref, yl_ref, buf, sem = rest
    else:
        o_ref, buf, sem = rest
    i = pl.program_id(0)
    n_steps = pl.num_programs(0)

    def gather(step, s):
        base = step * (TOP_K * TD)

        def issue(r, carry):
            for k in range(TOP_K):
                _row_copy(ys_ref, slot_ref[base + TOP_K * r + k], buf.at[s, k], r, sem.at[s]).start()
            return carry

        lax.fori_loop(0, TD, issue, 0, unroll=8)

    def wait(s):
        for k in range(TOP_K):
            pltpu.make_async_copy(ys_ref.at[pl.ds(0, TD)], buf.at[s, k], sem.at[s]).wait()

    @pl.when(i == 0)
    def _():
        gather(0, 0)

    for s in range(2):
        @pl.when(i % 2 == s)
        def _():
            @pl.when(i + 1 < n_steps)
            def _():
                gather(i + 1, 1 - s)

            wait(s)

    b = buf.at[i % 2]
    y = rinfo_ref[:, R_W1:R_W1 + 1] * b[0] + rinfo_ref[:, R_W2:R_W2 + 1] * b[1]
    x = x1_ref[...] + mod_ref[0, 0, 5:6, :] * y
    if final_norm:
        out = _rmsnorm(x, gf_ref[...])
        is_lat = pl.program_id(0) >= N_CTX // TD

        @pl.when(jnp.logical_not(is_lat))
        def _():
            yp_ref[...] = out

        @pl.when(is_lat)
        def _():
            yl_ref[...] = out
    else:
        o_ref[...] = x


def _combine(slots, x1, rinfo, mods, ys, layer, norm_f=None):
    final_norm = norm_f is not None
    n_ctx_t = N_CTX // TD
    in_specs = [
        pl.BlockSpec((TD, D_MODEL), lambda i, s: (i, 0)),
        pl.BlockSpec((TD, LANES), lambda i, s: (i, 0)),
        pl.BlockSpec((1, 1, 6, D_MODEL), lambda i, s: (layer, _group_of_tile(i, TD), 0, 0)),
        pl.BlockSpec(memory_space=pl.ANY),
    ]
    args = [slots, x1, rinfo, mods, ys]
    if final_norm:
        in_specs.append(pl.BlockSpec((1, D_MODEL), lambda i, s: (0, 0)))
        args.append(norm_f)
        out_specs = [pl.BlockSpec((TD, D_MODEL), lambda i, s: (jnp.minimum(i, n_ctx_t - 1), 0)),
                     pl.BlockSpec((TD, D_MODEL), lambda i, s: (jnp.maximum(i - n_ctx_t, 0), 0))]
        out_shape = [jax.ShapeDtypeStruct((N_CTX, D_MODEL), F32), jax.ShapeDtypeStruct((N_LAT, D_MODEL), F32)]
    else:
        out_specs = pl.BlockSpec((TD, D_MODEL), lambda i, s: (i, 0))
        out_shape = jax.ShapeDtypeStruct((NTOK, D_MODEL), F32)
    return pl.pallas_call(
        functools.partial(_combine_kernel, final_norm=final_norm),
        grid_spec=pltpu.PrefetchScalarGridSpec(
            num_scalar_prefetch=1,
            grid=(NTOK // TD,),
            in_specs=in_specs,
            out_specs=out_specs,
            scratch_shapes=[pltpu.VMEM((2, TOP_K, TD, D_MODEL), F32), pltpu.SemaphoreType.DMA((2,))],
        ),
        out_shape=out_shape,
        compiler_params=pltpu.CompilerParams(
            dimension_semantics=("arbitrary",), vmem_limit_bytes=VMEM_LIMIT),
        name="moe_combine",
    )(*args)


def _final_kernel(x_ref, g_ref, o_ref):
    o_ref[...] = _rmsnorm(x_ref[...], g_ref[...])


def _final(x, g, row_off, rows):
    off = row_off // TM
    return pl.pallas_call(
        _final_kernel,
        grid=(rows // TM,),
        in_specs=[
            pl.BlockSpec((TM, D_MODEL), lambda i: (off + i, 0)),
            pl.BlockSpec((1, D_MODEL), lambda i: (0, 0)),
        ],
        out_specs=pl.BlockSpec((TM, D_MODEL), lambda i: (i, 0)),
        out_shape=jax.ShapeDtypeStruct((rows, D_MODEL), F32),
        compiler_params=pltpu.CompilerParams(
            dimension_semantics=("arbitrary",), vmem_limit_bytes=VMEM_LIMIT),
        name="final_norm",
    )(x, g)


def kernel(x_prompt, x_sample, state_mlstm_C, state_mlstm_n, state_mlstm_m, state_ret_S, c, c_ctx,
           norm1_g, norm2_g, norm_f_g, w_ada, b_ada, w_in, b_gates, ret_decay_logit,
           mlstm_norm_g, ret_norm_g, w_out, ffn_w_gate, ffn_w_up, ffn_w_down,
           moe_w_router, moe_w_gate, moe_w_up, moe_w_down):
    xs_in = (x_prompt.reshape(N_CTX, D_MODEL), x_sample.reshape(N_LAT, D_MODEL))
    cvec = jnp.concatenate(
        [c_ctx[None, :], c, jnp.zeros((N_GROUPS - 1 - DEC_BATCH, D_MODEL), F32)], 0)
    mods = _ada(cvec, w_ada, b_ada).reshape(DEPTH, N_GROUPS, 6, D_MODEL)

    n_m = 4 * W_M
    w_in_t = jnp.swapaxes(w_in, 1, 2)
    n_if = N_GATES // 2
    lane_pad = ((0, 0), (0, 0), (0, LANES - n_if))
    wg = jnp.concatenate([jnp.pad(w_in[:, :, n_m:n_m + n_if], lane_pad),
                          jnp.pad(w_in[:, :, n_m + n_if:n_m + N_GATES], lane_pad)], -1)
    bg = jnp.concatenate([jnp.pad(b_gates[:, None, :n_if], lane_pad),
                          jnp.pad(b_gates[:, None, n_if:], lane_pad)], -1)
    cos_np, sin_np = _rope_tables()
    cos_t, sin_t = jnp.asarray(cos_np), jnp.asarray(sin_np)
    dl = jnp.broadcast_to(ret_decay_logit.reshape(DEPTH, 2 * H_R, 1), (DEPTH, 2 * H_R, LANES))
    m0 = jnp.pad(state_mlstm_m.reshape(DEC_BATCH, DEPTH, 1, 2 * H_M),
                 ((0, 0), (0, 0), (0, 0), (0, LANES - 2 * H_M)))
    g1 = norm1_g.reshape(DEPTH, 1, D_MODEL)
    g2 = norm2_g.reshape(DEPTH, 1, D_MODEL)
    nm = mlstm_norm_g.reshape(DEPTH, 1, W_M)
    nr = ret_norm_g.reshape(DEPTH, 1, W_R)
    n_moe = moe_w_router.shape[0]
    wr_pad = jnp.pad(moe_w_router, ((0, 0), (0, 0), (0, LANES - N_EXPERTS)))
    moe_g = moe_w_gate.reshape(n_moe * N_EXPERTS, D_MODEL, D_FF)
    moe_u = moe_w_up.reshape(n_moe * N_EXPERTS, D_MODEL, D_FF)
    moe_d = moe_w_down.reshape(n_moe * N_EXPERTS, D_FF, D_MODEL)

    states = ()
    xs = xs_in
    for l in range(DEPTH):
        jl = l // 2
        p, gates = _proj(xs, g1, mods, w_in_t, wg, bg, cos_t, sin_t, l)
        y, *states = _scan_ctx(p, gates, dl, nm, nr, l, states)
        y = _scan_lat(p, gates, dl, nm, nr, state_mlstm_C, state_mlstm_n, m0, state_ret_S, y, l)
        if l % 2 == 0:
            x1, h2 = _out(y, xs, w_out, g2, mods, l)
            x = _ffn(h2, x1, ffn_w_gate, ffn_w_up, ffn_w_down, mods, l, jl)
        else:
            x1, h2f, rinfo, cnt = _out(y, xs, w_out, g2, mods, l, wr_pad, jl)
            slots = rinfo[:, R_S1:R_S2 + 1].astype(jnp.int32).reshape(TOP_K * NTOK)
            tile_e, tile_blk, tile_n = _tile_plan(cnt[0, :N_EXPERTS].astype(jnp.int32))
            xd = _dispatch(slots, h2f)
            yd = _gffn(tile_e, tile_blk, tile_n, xd, moe_g, moe_u, moe_d, jl * N_EXPERTS)
            if l == DEPTH - 1:
                y_ctx, y_lat = _combine(slots, x1, rinfo, mods, yd, l, norm_f_g.reshape(1, D_MODEL))
            else:
                x = _combine(slots, x1, rinfo, mods, yd, l)
        xs = (x,)

    if DEPTH % 2 == 1:
        y_ctx = _final(x, norm_f_g.reshape(1, D_MODEL), 0, N_CTX)
        y_lat = _final(x, norm_f_g.reshape(1, D_MODEL), N_CTX, N_LAT)
    y_prompt = y_ctx.reshape(BATCH, SEQ, D_MODEL)
    y_sample = y_lat.reshape(DEC_BATCH, DEC_SEQ, D_MODEL)
    new_C, new_n, new_m, new_S = states
    return (y_prompt, y_sample, new_C, new_n,
            new_m[:, :, 0, :2 * H_M].reshape(BATCH, DEPTH, 2, H_M), new_S)
```

```python
import functools

import numpy as np
import jax
import jax.numpy as jnp
from jax import lax
from jax.experimental import pallas as pl
from jax.experimental.pallas import tpu as pltpu

D_MODEL = 1024
BATCH = 32
SEQ = 256
DEPTH = 2
DEC_BATCH = 2
DEC_SEQ = 1024
GRID_W = 64
H_M = 4
DH = 128
H_R = 4
W_M = H_M * DH
W_R = H_R * DH
N_GATES = 4 * H_M
CHUNK = 128
D_FF = 2816
N_EXPERTS = 8
ROPE_BASE = 10000.0
EPS = 1e-6

N_CTX = BATCH * SEQ
N_LAT = DEC_BATCH * DEC_SEQ
NTOK = N_CTX + N_LAT
N_GROUPS = 8
K_SCALE = DH ** -0.5
P_COLS = 4 * W_M + 4 * W_R
LANES = 128
GATE_LANES = 2 * LANES
VMEM_LIMIT = 56 * 1024 * 1024

F32 = jnp.float32
BF16 = jnp.bfloat16
HIGHEST = lax.Precision.HIGHEST

TM = 1024
TN = 1024
FC = 256
TM_F = 512
TOP_K = 2
TR = 896
REG_TILES = -(-NTOK // TR)
REG = REG_TILES * TR
MAX_TILES = -(-TOP_K * NTOK // TR) + N_EXPERTS
TD = 512


def _group_of_tile(i, tm):
    return jnp.maximum(i * tm // DEC_SEQ - (N_CTX // DEC_SEQ - 1), 0)


def _silu(x):
    return x * jax.nn.sigmoid(x)


def _log_sigmoid(x):
    return jnp.minimum(x, 0.0) - jnp.log(1.0 + jnp.exp(-jnp.abs(x)))


def _rmsnorm(x, g):
    return x * lax.rsqrt(jnp.mean(x * x, -1, keepdims=True) + EPS) * g


def _ada_kernel(cv_ref, w_ref, b_ref, o_ref):
    s = _silu(cv_ref[...]).astype(BF16)
    o_ref[0] = jnp.dot(s, w_ref[0].astype(BF16), preferred_element_type=F32) + b_ref[0]


def _ada(cvec, w_ada, b_ada):
    tn = 1536
    n = 6 * D_MODEL
    return pl.pallas_call(
        _ada_kernel,
        grid=(DEPTH, n // tn),
        in_specs=[
            pl.BlockSpec((N_GROUPS, D_MODEL), lambda l, j: (0, 0)),
            pl.BlockSpec((1, D_MODEL, tn), lambda l, j: (l, 0, j)),
            pl.BlockSpec((1, 1, tn), lambda l, j: (l, 0, j)),
        ],
        out_specs=pl.BlockSpec((1, N_GROUPS, tn), lambda l, j: (l, 0, j)),
        out_shape=jax.ShapeDtypeStruct((DEPTH, N_GROUPS, n), F32),
        compiler_params=pltpu.CompilerParams(
            dimension_semantics=("arbitrary", "arbitrary"), vmem_limit_bytes=VMEM_LIMIT),
        name="ada",
    )(cvec, w_ada, b_ada.reshape(DEPTH, 1, n))


def _rope_tables():
    half = DH // 4
    freqs = ROPE_BASE ** (-np.arange(half, dtype=np.float64) / half)
    t = np.arange(DEC_SEQ)
    pos = np.stack([t // GRID_W, t % GRID_W], 1).astype(np.float64)
    d = np.arange(DH)
    ang = pos[:, d // (DH // 2)] * freqs[d % half][None, :]
    sign = np.where((d % (DH // 2)) < half, -1.0, 1.0)[None, :]
    return np.cos(ang).astype(np.float32), (sign * np.sin(ang)).astype(np.float32)


def _rope(a, cos, sin):
    lane = lax.broadcasted_iota(jnp.int32, a.shape, 1)
    first = (lane % (DH // 2)) < (DH // 4)
    partner = jnp.where(first, pltpu.roll(a, DH - DH // 4, 1), pltpu.roll(a, DH // 4, 1))
    return a * cos + partner * sin


def _proj_kernel(*refs, n_ctx_tiles, split_x, layer):
    if split_x:
        (xp_ref, xl_ref, g_ref, mod_ref, wt_ref, wg_ref, bg_ref, cos_ref, sin_ref,
         p_ref, gate_ref, h_scr, w_res, w_stg, w_sem) = refs
    else:
        (x_ref, g_ref, mod_ref, wt_ref, wg_ref, bg_ref, cos_ref, sin_ref,
         p_ref, gate_ref, h_scr, w_res, w_stg, w_sem) = refs
    p_ref = p_ref.at[0]
    i = pl.program_id(0)
    j = pl.program_id(1)
    is_lat = i >= n_ctx_tiles
    half = TN // 2

    def prologue(x):
        h = _rmsnorm(x, g_ref[0]) * (1.0 + mod_ref[0, 0, 1:2, :]) + mod_ref[0, 0, 0:1, :]
        h_scr[...] = h.astype(BF16)
        gate_ref[...] = _dot_f32x3(h, wg_ref[0]) + bg_ref[0]

    @pl.when(j == 0)
    def _():
        if split_x:
            pl.when(jnp.logical_not(is_lat))(lambda: prologue(xp_ref[...]))
            pl.when(is_lat)(lambda: prologue(xl_ref[...]))
        else:
            prologue(x_ref[...])

    n_col_tiles = P_COLS // TN

    def tile_copy(jj):
        row0 = jj * TN + (N_GATES if jj * TN >= 4 * W_M else 0)
        s = jj % 2
        return pltpu.make_async_copy(wt_ref.at[layer, pl.ds(row0, TN), :], w_stg.at[s], w_sem.at[s])

    def matmul(jj):
        @pl.when(i == 0)
        def _():
            if jj == 0:
                tile_copy(0).start()
            if jj + 1 < n_col_tiles:
                tile_copy(jj + 1).start()
            tile_copy(jj).wait()
            w_res[jj] = w_stg[jj % 2].astype(BF16)

        return _dot_nt(h_scr[...], w_res[jj])

    @pl.when(j == 0)
    def _():
        acc = matmul(0)
        p_ref[:, :half] = acc[:, :half].astype(BF16)
        p_ref[:, half:] = (acc[:, half:] * K_SCALE).astype(BF16)

    @pl.when(j == 1)
    def _():
        p_ref[...] = matmul(1).astype(BF16)

    @pl.when(j == 2)
    def _():
        acc = matmul(2)

        @pl.when(is_lat)
        def _():
            cos = cos_ref[...]
            sin = sin_ref[...]
            for hd in range(TN // DH):
                sl = slice(hd * DH, (hd + 1) * DH)
                r = _rope(acc[:, sl], cos, sin)
                p_ref[:, sl] = (r * K_SCALE if hd * DH >= half else r).astype(BF16)

        @pl.when(jnp.logical_not(is_lat))
        def _():
            p_ref[:, :half] = acc[:, :half].astype(BF16)
            p_ref[:, half:] = (acc[:, half:] * K_SCALE).astype(BF16)

    @pl.when(j == 3)
    def _():
        p_ref[...] = matmul(3).astype(BF16)


def _proj(xs, g1, mods, w_in_t, wg, bg, cos_t, sin_t, layer):
    n_ctx_tiles = N_CTX // TM
    tiles_per_seq = DEC_SEQ // TM
    split_x = len(xs) == 2
    if split_x:
        x_specs = [pl.BlockSpec((TM, D_MODEL), lambda i, j: (jnp.minimum(i, n_ctx_tiles - 1), 0)),
                   pl.BlockSpec((TM, D_MODEL), lambda i, j: (jnp.maximum(i - n_ctx_tiles, 0), 0))]
    else:
        x_specs = [pl.BlockSpec((TM, D_MODEL), lambda i, j: (i, 0))]
    return pl.pallas_call(
        functools.partial(_proj_kernel, n_ctx_tiles=n_ctx_tiles, split_x=split_x, layer=layer),
        grid=(NTOK // TM, P_COLS // TN),
        in_specs=x_specs + [
            pl.BlockSpec((1, 1, D_MODEL), lambda i, j: (layer, 0, 0)),
            pl.BlockSpec((1, 1, 6, D_MODEL), lambda i, j: (layer, _group_of_tile(i, TM), 0, 0)),
            pl.BlockSpec(memory_space=pl.ANY),
            pl.BlockSpec((1, D_MODEL, GATE_LANES), lambda i, j: (layer, 0, 0)),
            pl.BlockSpec((1, 1, GATE_LANES), lambda i, j: (layer, 0, 0)),
            pl.BlockSpec((TM, DH), lambda i, j: (i % tiles_per_seq, 0)),
            pl.BlockSpec((TM, DH), lambda i, j: (i % tiles_per_seq, 0)),
        ],
        out_specs=[
            pl.BlockSpec((1, TM, TN), lambda i, j: (j, i, 0)),
            pl.BlockSpec((TM, GATE_LANES), lambda i, j: (i, 0)),
        ],
        out_shape=[
            jax.ShapeDtypeStruct((P_COLS // TN, NTOK, TN), BF16),
            jax.ShapeDtypeStruct((NTOK, GATE_LANES), F32),
        ],
        scratch_shapes=[pltpu.VMEM((TM, D_MODEL), BF16),
                        pltpu.VMEM((P_COLS // TN, TN, D_MODEL), BF16),
                        pltpu.VMEM((2, TN, D_MODEL), F32),
                        pltpu.SemaphoreType.DMA((2,))],
        compiler_params=pltpu.CompilerParams(
            dimension_semantics=("arbitrary", "arbitrary"), vmem_limit_bytes=VMEM_LIMIT),
        name="proj",
    )(*xs, g1, mods, w_in_t, wg, bg, cos_t, sin_t)


def _split3(x):
    hi = x.astype(BF16)
    r1 = x - hi.astype(F32)
    mid = r1.astype(BF16)
    lo = (r1 - mid.astype(F32)).astype(BF16)
    return hi, mid, lo


def _dot(a, b):
    return jnp.dot(a, b, preferred_element_type=F32)


def _dot_nt(a, b):
    return lax.dot_general(a, b, (((1,), (1,)), ((), ())), preferred_element_type=F32)


def _tri_dot_left(tri, x):
    hi, mid, lo = _split3(x)
    return _dot(tri, hi) + _dot(tri, mid) + _dot(tri, lo)


def _tri_dot_right(x, tri):
    hi, mid, lo = _split3(x)
    return _dot(hi, tri) + _dot(mid, tri) + _dot(lo, tri)


def _run_max(x, reverse):
    n_tiles = x.shape[0] // 8
    sub = lax.broadcasted_iota(jnp.int32, (8, LANES), 0)
    out = [None] * n_tiles
    carry = None
    for t in (range(n_tiles - 1, -1, -1) if reverse else range(n_tiles)):
        v = x[8 * t:8 * t + 8, :]
        for s in (1, 2, 4):
            if reverse:
                v = jnp.maximum(v, jnp.where(sub < 8 - s, pltpu.roll(v, 8 - s, 0), -jnp.inf))
            else:
                v = jnp.maximum(v, jnp.where(sub >= s, pltpu.roll(v, s, 0), -jnp.inf))
        if carry is not None:
            v = jnp.maximum(v, carry)
        carry = jnp.broadcast_to(v[0:1, :] if reverse else v[7:8, :], (8, LANES))
        out[t] = v
    return jnp.concatenate(out, axis=0)


def _scan_kernel(*refs, T, has_state, n_prev=0):
    if has_state:
        (p_ref, g_ref, dl_ref, nm_ref, nr_ref, C0_ref, n0_ref, m0_ref, S0_ref, _yprev_ref,
         y_ref, CN_s, S_s, m_s, hf_s, hb_s, dm_s, dq_s, dk_s, dL_s, kT_s) = refs
    else:
        p_ref, g_ref, dl_ref, nm_ref, nr_ref = refs[:5]
        prev_refs = refs[5:9] if n_prev else ()
        (y_ref, C_out, n_out, m_out, S_out,
         CN_s, S_s, m_s, hf_s, hb_s, dm_s, dq_s, dk_s, dL_s, kT_s) = refs[5 + len(prev_refs):]
    L = CHUNK
    n_chunks = T // L
    row_i = lax.broadcasted_iota(jnp.int32, (L, L), 0)
    col_j = lax.broadcasted_iota(jnp.int32, (L, L), 1)
    lower = col_j <= row_i
    upper = col_j >= row_i
    tril = lower.astype(BF16)
    triu = upper.astype(BF16)
    ones = jnp.ones((L, DH), BF16)
    c_km = W_M
    c_vm = 2 * W_M
    c_om = 3 * W_M
    c_qr = 4 * W_M
    c_kr = c_qr + W_R
    c_vr = c_qr + 2 * W_R
    c_gr = c_qr + 3 * W_R

    def pcols(rows, col):
        return p_ref[col // TN, rows, col % TN:col % TN + DH]

    for d in range(2):
        for h in range(H_M):
            k = d * H_M + h
            if has_state:
                CN_s[k, :, :DH] = C0_ref[0, 0, d, h]
                CN_s[k, :, DH:] = jnp.broadcast_to(n0_ref[0, 0, d, h:h + 1, :], (DH, DH)).T
                S_s[k] = S0_ref[0, 0, d, h]
            else:
                CN_s[k] = jnp.zeros((DH, 2 * DH), F32)
                S_s[k] = jnp.zeros((DH, DH), F32)
    m_s[...] = m0_ref[0, 0] if has_state else jnp.zeros((1, LANES), F32)

    @pl.when(pl.program_id(0) == 0)
    def _():
        pos_i = row_i.astype(F32)
        pos_j = col_j.astype(F32)
        for d in range(2):
            for h in range(H_R):
                k = d * H_R + h
                lg_row = _log_sigmoid(dl_ref[0, k:k + 1, :])
                lg = jnp.broadcast_to(lg_row, (L, L))
                rel = (row_i - col_j if d == 0 else col_j - row_i).astype(F32)
                dm_s[k] = jnp.where(rel >= 0, jnp.exp(lg * jnp.maximum(rel, 0.0)), 0.0)
                dq_s[k] = jnp.exp(lg * (pos_i + 1.0 if d == 0 else L - pos_i))
                dk_s[k] = jnp.exp(lg * (L - 1.0 - pos_j if d == 0 else pos_j))
                dL_s[k] = jnp.exp(lg_row * float(L))

    def transpose_keys(c, carry):
        r0 = pl.multiple_of(c * L, L)
        for h in range(H_M):
            kT_s[h, c] = pcols(pl.ds(r0, L), c_km + h * DH).astype(F32).T
            kT_s[H_M + h, c] = pcols(pl.ds(r0, L), c_kr + h * DH).astype(F32).T
        return carry

    lax.fori_loop(0, n_chunks, transpose_keys, 0)

    def chunk_step(c, carry):
        m_prev = m_s[...]
        m_new = []
        prep = []
        for d in range(2):
            ci = c if d == 0 else n_chunks - 1 - c
            r0 = pl.multiple_of(ci * L, L)
            mask = lower if d == 0 else upper
            e_row = L - 1 if d == 0 else 0
            FL = _log_sigmoid(g_ref[pl.ds(r0, L), LANES:2 * LANES])
            Bc = _tri_dot_left(tril if d == 0 else triu, FL)
            Zc = g_ref[pl.ds(r0, L), 0:LANES] - Bc
            M = jnp.maximum(_run_max(Zc, reverse=(d == 1)), m_prev)
            m_row = Bc + M
            M_end = M[e_row:e_row + 1, :]
            m_new.append(Bc[e_row:e_row + 1, :] + M_end)
            decay = jnp.exp(m_prev - M_end)
            prep.append(dict(ci=ci, r0=r0, mask=mask, M=M, m_row=m_row, decay=decay,
                             ZT=Zc.T,
                             WT=jnp.exp(Zc - M_end).T))
        pairs = [(d, h) for d in range(2) for h in range(H_M)]

        def rows(d, col):
            return pcols(pl.ds(prep[d]["r0"], L), col)

        qk, qkr = {}, {}
        for d, h in pairs:
            qk[d, h] = _dot_nt(rows(d, h * DH), rows(d, c_km + h * DH))
            qkr[d, h] = _dot_nt(rows(d, c_qr + h * DH), rows(d, c_kr + h * DH))
        upd, updr = {}, {}
        for d, h in pairs:
            k = d * H_M + h
            ci = prep[d]["ci"]
            vo = jnp.concatenate([rows(d, c_vm + h * DH), ones], axis=1)
            wkT = (kT_s[h, ci] * jnp.broadcast_to(prep[d]["WT"][k:k + 1, :], (DH, L))).astype(BF16)
            upd[d, h] = _dot(wkT, vo)
            kdT = (kT_s[H_M + h, ci] * dk_s[k]).astype(BF16)
            updr[d, h] = _dot(kdT, rows(d, c_vr + h * DH))
        for d, h in pairs:
            k = d * H_M + h
            r0 = prep[d]["r0"]
            h_dst = hf_s if d == 0 else hb_s
            q = rows(d, h * DH)
            M_col = jnp.broadcast_to(prep[d]["M"][:, k:k + 1], (L, L))
            z_row = jnp.broadcast_to(prep[d]["ZT"][k:k + 1, :], (L, L))
            D = jnp.where(prep[d]["mask"], jnp.exp(z_row - M_col), 0.0)
            s = (qk[d, h] * D).astype(BF16)
            w_inter = jnp.exp(jnp.broadcast_to(m_prev[:, k:k + 1], (L, L)) - M_col)
            wq = (w_inter * q.astype(F32)).astype(BF16)
            vo = jnp.concatenate([rows(d, c_vm + h * DH), ones], axis=1)
            CN = CN_s[k]
            res = _dot(jnp.concatenate([s, wq], axis=1),
                       jnp.concatenate([vo, CN.astype(BF16)], axis=0))
            floor = jnp.exp(-jnp.broadcast_to(prep[d]["m_row"][:, k:k + 1], (L, L)))
            h_dst[pl.ds(r0, L), h * DH:(h + 1) * DH] = res[:, :DH] / jnp.maximum(jnp.abs(res[:, DH:]), floor)
            CN_s[k] = jnp.broadcast_to(prep[d]["decay"][:, k:k + 1], (DH, 2 * DH)) * CN + upd[d, h]
            qr = rows(d, c_qr + h * DH)
            S = S_s[k]
            sr = (qkr[d, h] * dm_s[k]).astype(BF16)
            qd = (qr.astype(F32) * dq_s[k]).astype(BF16)
            h_dst[pl.ds(r0, L), W_M + h * DH:W_M + (h + 1) * DH] = _dot(
                jnp.concatenate([sr, qd], axis=1),
                jnp.concatenate([rows(d, c_vr + h * DH), S.astype(BF16)], axis=0))
            S_s[k] = dL_s[k] * S + updr[d, h]
        lane = lax.broadcasted_iota(jnp.int32, (1, LANES), 1)
        m_s[...] = jnp.where(lane < H_M, m_new[0], m_new[1])
        return carry

    lax.fori_loop(0, n_chunks, chunk_step, 0)

    for h in range(H_M):
        sl = slice(h * DH, (h + 1) * DH)
        hs = hf_s[:, sl] + hb_s[:, sl]
        yn = _rmsnorm(hs, nm_ref[0, :, sl])
        om = pcols(slice(None), c_om + h * DH).astype(F32)
        y_ref[:, sl] = (jax.nn.sigmoid(om) * yn).astype(BF16)
        slr = slice(W_M + h * DH, W_M + (h + 1) * DH)
        hr = hf_s[:, slr] + hb_s[:, slr]
        ynr = _rmsnorm(hr, nr_ref[0, :, sl])
        gr = pcols(slice(None), c_gr + h * DH).astype(F32)
        y_ref[:, slr] = (_silu(gr) * ynr).astype(BF16)

    if not has_state:
        for prev, out in zip(prev_refs, (C_out, n_out, m_out, S_out)):
            out[0, :n_prev] = prev[0]
        for d in range(2):
            for h in range(H_M):
                k = d * H_M + h
                C_out[0, n_prev, d, h] = CN_s[k, :, :DH]
                n_out[0, n_prev, d, h:h + 1, :] = CN_s[k, :, DH:].T[0:1, :]
                S_out[0, n_prev, d, h] = S_s[k]
        m_out[0, n_prev] = m_s[...]


def _scan_scratch(T):
    return [
        pltpu.VMEM((2 * H_M, DH, 2 * DH), F32),
        pltpu.VMEM((2 * H_R, DH, DH), F32),
        pltpu.VMEM((1, LANES), F32),
        pltpu.VMEM((T, W_M + W_R), F32),
        pltpu.VMEM((T, W_M + W_R), F32),
        pltpu.VMEM((2 * H_R, CHUNK, CHUNK), F32),
        pltpu.VMEM((2 * H_R, CHUNK, CHUNK), F32),
        pltpu.VMEM((2 * H_R, CHUNK, CHUNK), F32),
        pltpu.VMEM((2 * H_R, 1, LANES), F32),
        pltpu.VMEM((H_M + H_R, T // CHUNK, DH, CHUNK), F32),
    ]


def _scan_ctx(p, gates, dl, nm, nr, layer, prev_states=()):
    T = SEQ
    n_lay = layer + 1
    state_tails = [(2, H_M, DH, DH), (2, H_M, DH), (1, LANES), (2, H_R, DH, DH)]

    def state_spec(n, tail):
        return pl.BlockSpec((1, n) + tail, lambda b: (b,) + (0,) * (1 + len(tail)))
    common = [
        pl.BlockSpec((P_COLS // TN, T, TN), lambda b: (0, b, 0)),
        pl.BlockSpec((T, GATE_LANES), lambda b: (b, 0)),
        pl.BlockSpec((1, 2 * H_R, LANES), lambda b: (layer, 0, 0)),
        pl.BlockSpec((1, 1, W_M), lambda b: (layer, 0, 0)),
        pl.BlockSpec((1, 1, W_R), lambda b: (layer, 0, 0)),
    ]
    return pl.pallas_call(
        functools.partial(_scan_kernel, T=T, has_state=False, n_prev=layer if prev_states else 0),
        grid=(BATCH,),
        in_specs=common + [state_spec(layer, tail) for tail in state_tails[:len(prev_states)]],
        out_specs=[pl.BlockSpec((T, D_MODEL), lambda b: (b, 0))] + [state_spec(n_lay, tail) for tail in state_tails],
        out_shape=[jax.ShapeDtypeStruct((NTOK, D_MODEL), BF16)] + [
            jax.ShapeDtypeStruct((BATCH, n_lay) + tail, F32) for tail in state_tails],
        scratch_shapes=_scan_scratch(T),
        compiler_params=pltpu.CompilerParams(
            dimension_semantics=("arbitrary",), vmem_limit_bytes=VMEM_LIMIT),
        name="scan_ctx",
    )(p, gates, dl, nm, nr, *prev_states)


def _scan_lat(p, gates, dl, nm, nr, C0, n0, m0, S0, y_prev, layer):
    T = DEC_SEQ
    off = N_CTX // T
    in_specs = [
        pl.BlockSpec((P_COLS // TN, T, TN), lambda b: (0, off + b, 0)),
        pl.BlockSpec((T, GATE_LANES), lambda b: (off + b, 0)),
        pl.BlockSpec((1, 2 * H_R, LANES), lambda b: (layer, 0, 0)),
        pl.BlockSpec((1, 1, W_M), lambda b: (layer, 0, 0)),
        pl.BlockSpec((1, 1, W_R), lambda b: (layer, 0, 0)),
        pl.BlockSpec((1, 1, 2, H_M, DH, DH), lambda b: (b, layer, 0, 0, 0, 0)),
        pl.BlockSpec((1, 1, 2, H_M, DH), lambda b: (b, layer, 0, 0, 0)),
        pl.BlockSpec((1, 1, 1, LANES), lambda b: (b, layer, 0, 0)),
        pl.BlockSpec((1, 1, 2, H_R, DH, DH), lambda b: (b, layer, 0, 0, 0, 0)),
        pl.BlockSpec(memory_space=pl.ANY),
    ]
    return pl.pallas_call(
        functools.partial(_scan_kernel, T=T, has_state=True),
        grid=(DEC_BATCH,),
        in_specs=in_specs,
        out_specs=pl.BlockSpec((T, D_MODEL), lambda b: (off + b, 0)),
        out_shape=jax.ShapeDtypeStruct((NTOK, D_MODEL), BF16),
        input_output_aliases={9: 0},
        scratch_shapes=_scan_scratch(T),
        compiler_params=pltpu.CompilerParams(
            dimension_semantics=("arbitrary",), vmem_limit_bytes=VMEM_LIMIT),
        name="scan_lat",
    )(p, gates, dl, nm, nr, C0, n0, m0, S0, y_prev)


def _top2(logits):
    lane = lax.broadcasted_iota(jnp.int32, logits.shape, 1)
    v1 = jnp.max(logits, -1, keepdims=True)
    i1 = jnp.min(jnp.where(logits == v1, lane, LANES), -1, keepdims=True)
    rest = jnp.where(lane == i1, -jnp.inf, logits)
    v2 = jnp.max(rest, -1, keepdims=True)
    i2 = jnp.min(jnp.where(rest == v2, lane, LANES), -1, keepdims=True)
    e2 = jnp.exp(v2 - v1)
    return i1, i2, 1.0 / (1.0 + e2), e2 / (1.0 + e2)


def _split2(x):
    hi = x.astype(BF16)
    return hi, (x - hi.astype(F32)).astype(BF16)


def _dot_f32x3(a, b):
    a_hi, a_lo = _split2(a)
    b_hi, b_lo = _split2(b)
    return _dot(a_hi, b_hi) + _dot(a_hi, b_lo) + _dot(a_lo, b_hi)


R_E1, R_E2, R_W1, R_W2, R_S1, R_S2 = range(6)


def _out_kernel(*refs, with_router, split_x):
    y_ref = refs[0]
    if split_x:
        xp_ref, xl_ref = refs[1:3]
        x_in = jnp.where(pl.program_id(0) >= N_CTX // TM, xl_ref[...], xp_ref[...])
    else:
        x_in = refs[1][...]
    refs = refs[3:] if split_x else refs[2:]
    if with_router:
        (w_ref, g_ref, mod_ref, wr_ref,
         x1_ref, h2_ref, rinfo_ref, cnt_ref, w_scr, tri_scr, cnt_scr) = refs
    else:
        w_ref, g_ref, mod_ref, x1_ref, h2_ref, w_scr = refs

    @pl.when(pl.program_id(0) == 0)
    def _():
        w_scr[...] = w_ref[0].astype(BF16)
        if with_router:
            r = lax.broadcasted_iota(jnp.int32, (LANES, LANES), 0)
            c = lax.broadcasted_iota(jnp.int32, (LANES, LANES), 1)
            tri_scr[...] = (c < r).astype(BF16)
            cnt_scr[...] = jnp.zeros_like(cnt_scr)

    o = jnp.dot(y_ref[...], w_scr[...], preferred_element_type=F32)
    x1 = x_in + mod_ref[0, 0, 2:3, :] * o
    x1_ref[...] = x1
    h2 = _rmsnorm(x1, g_ref[0]) * (1.0 + mod_ref[0, 0, 4:5, :]) + mod_ref[0, 0, 3:4, :]
    if not with_router:
        h2_ref[...] = h2.astype(BF16)
    else:
        h2_ref[...] = h2
        h_hi, h_lo = _split2(h2)
        w_hi, w_lo = _split2(wr_ref[0])
        t = _dot(h_hi, jnp.concatenate([w_hi, w_lo], axis=1))
        logits = t[:, :LANES] + t[:, LANES:] + _dot(h_lo, w_hi)
        lane = lax.broadcasted_iota(jnp.int32, logits.shape, 1)
        i1, i2, w1, w2 = _top2(jnp.where(lane < N_EXPERTS, logits, -jnp.inf))
        oh1 = lane == i1
        oh2 = lane == i2
        sel = jnp.where(oh1 | oh2, 1.0, 0.0)
        tri = tri_scr[...]
        run = cnt_scr[...]
        ranks = []
        for blk in range(TM // LANES):
            s_blk = sel[blk * LANES:(blk + 1) * LANES, :]
            ranks.append(_dot(tri, s_blk.astype(BF16)) + run)
            run = run + jnp.sum(s_blk, 0, keepdims=True)
        rank = jnp.concatenate(ranks, axis=0)
        r1 = jnp.sum(jnp.where(oh1, rank, 0.0), -1, keepdims=True)
        r2 = jnp.sum(jnp.where(oh2, rank, 0.0), -1, keepdims=True)
        s1 = i1.astype(F32) * float(REG) + r1
        s2 = i2.astype(F32) * float(REG) + r2
        info = jnp.zeros(logits.shape, F32)
        for col, val in ((R_E1, i1.astype(F32)), (R_E2, i2.astype(F32)), (R_W1, w1), (R_W2, w2),
                         (R_S1, s1), (R_S2, s2)):
            info = jnp.where(lane == col, val, info)
        rinfo_ref[...] = info
        cnt_scr[...] = run
        cnt_ref[...] = run


def _out(y, xs, w_out, g2, mods, layer, w_router_pad=None, router_idx=0):
    with_router = w_router_pad is not None
    split_x = len(xs) == 2
    n_ctx_tiles = N_CTX // TM
    if split_x:
        x_specs = [pl.BlockSpec((TM, D_MODEL), lambda i: (jnp.minimum(i, n_ctx_tiles - 1), 0)),
                   pl.BlockSpec((TM, D_MODEL), lambda i: (jnp.maximum(i - n_ctx_tiles, 0), 0))]
    else:
        x_specs = [pl.BlockSpec((TM, D_MODEL), lambda i: (i, 0))]
    in_specs = [pl.BlockSpec((TM, D_MODEL), lambda i: (i, 0))] + x_specs + [
        pl.BlockSpec((1, D_MODEL, D_MODEL), lambda i: (layer, 0, 0)),
        pl.BlockSpec((1, 1, D_MODEL), lambda i: (layer, 0, 0)),
        pl.BlockSpec((1, 1, 6, D_MODEL), lambda i: (layer, _group_of_tile(i, TM), 0, 0)),
    ]
    out_specs = [
        pl.BlockSpec((TM, D_MODEL), lambda i: (i, 0)),
        pl.BlockSpec((TM, D_MODEL), lambda i: (i, 0)),
    ]
    out_shape = [
        jax.ShapeDtypeStruct((NTOK, D_MODEL), F32),
        jax.ShapeDtypeStruct((NTOK, D_MODEL), F32 if with_router else BF16),
    ]
    args = [y, *xs, w_out, g2, mods]
    scratch = [pltpu.VMEM((D_MODEL, D_MODEL), BF16)]
    if with_router:
        in_specs.append(pl.BlockSpec((1, D_MODEL, LANES), lambda i: (router_idx, 0, 0)))
        out_specs += [pl.BlockSpec((TM, LANES), lambda i: (i, 0)),
                      pl.BlockSpec((1, LANES), lambda i: (0, 0))]
        out_shape += [jax.ShapeDtypeStruct((NTOK, LANES), F32),
                      jax.ShapeDtypeStruct((1, LANES), F32)]
        args.append(w_router_pad)
        scratch += [pltpu.VMEM((LANES, LANES), BF16), pltpu.VMEM((1, LANES), F32)]
    return pl.pallas_call(
        functools.partial(_out_kernel, with_router=with_router, split_x=split_x),
        grid=(NTOK // TM,),
        in_specs=in_specs,
        out_specs=out_specs,
        out_shape=out_shape,
        scratch_shapes=scratch,
        compiler_params=pltpu.CompilerParams(
            dimension_semantics=("arbitrary",), vmem_limit_bytes=VMEM_LIMIT),
        name="out_router" if with_router else "out",
    )(*args)


N_FC = D_FF // FC
def _swiglu_tile(h, w_gu, w_d, acc, fetch=None):
    if fetch is not None:
        wg_hbm, wu_hbm, wd_hbm, stg_gu, stg_d, sem = fetch

        def copies(f):
            s = f % 2
            cols = pl.ds(f * FC, FC)
            return (pltpu.make_async_copy(wg_hbm.at[:, cols], stg_gu.at[s, 0], sem.at[s, 0]),
                    pltpu.make_async_copy(wu_hbm.at[:, cols], stg_gu.at[s, 1], sem.at[s, 1]),
                    pltpu.make_async_copy(wd_hbm.at[cols, :], stg_d.at[s], sem.at[s, 2]))

        def start(f):
            for c in copies(f):
                c.start()

        def land(f):
            for c in copies(f):
                c.wait()
            s = f % 2
            w_gu[f, :, :FC] = stg_gu[s, 0].astype(BF16)
            w_gu[f, :, FC:] = stg_gu[s, 1].astype(BF16)
            w_d[f * FC:(f + 1) * FC, :] = stg_d[s].astype(BF16)
    else:
        start = land = lambda f: None

    def up(f):
        return jnp.dot(h, w_gu[f], preferred_element_type=F32)

    start(0)
    if N_FC > 1:
        start(1)
    land(0)
    ab = up(0)
    for f in range(N_FC):
        if f + 2 < N_FC:
            start(f + 2)
        if f + 1 < N_FC:
            land(f + 1)
            ab_next = up(f + 1)
        t = (_silu(ab[:, :FC]) * ab[:, FC:]).astype(BF16)
        contrib = jnp.dot(t, w_d[f * FC:(f + 1) * FC, :], preferred_element_type=F32)
        if f == 0:
            acc[...] = contrib
        else:
            acc[...] += contrib
        if f + 1 < N_FC:
            ab = ab_next


def _ffn_weight_scratch():
    return [
        pltpu.VMEM((N_FC, D_MODEL, 2 * FC), BF16),
        pltpu.VMEM((D_FF, D_MODEL), BF16),
        pltpu.VMEM((2, 2, D_MODEL, FC), F32),
        pltpu.VMEM((2, FC, D_MODEL), F32),
        pltpu.SemaphoreType.DMA((2, 3)),
    ]


def _ffn_kernel(h_ref, res_ref, wg_ref, wu_ref, wd_ref, mod_ref, o_ref,
                acc, w_gu, w_d, stg_gu, stg_d, sem, *, w_idx):
    first = pl.program_id(0) == 0

    @pl.when(first)
    def _():
        _swiglu_tile(h_ref[...], w_gu, w_d, acc,
                     (wg_ref.at[w_idx], wu_ref.at[w_idx], wd_ref.at[w_idx], stg_gu, stg_d, sem))

    @pl.when(jnp.logical_not(first))
    def _():
        _swiglu_tile(h_ref[...], w_gu, w_d, acc)

    o_ref[...] = res_ref[...] + mod_ref[0, 0, 5:6, :] * acc[...]


def _ffn(h2, res, wg, wu, wd, mods, layer, w_idx):
    return pl.pallas_call(
        functools.partial(_ffn_kernel, w_idx=w_idx),
        grid=(NTOK // TM_F,),
        in_specs=[
            pl.BlockSpec((TM_F, D_MODEL), lambda i: (i, 0)),
            pl.BlockSpec((TM_F, D_MODEL), lambda i: (i, 0)),
            pl.BlockSpec(memory_space=pl.ANY),
            pl.BlockSpec(memory_space=pl.ANY),
            pl.BlockSpec(memory_space=pl.ANY),
            pl.BlockSpec((1, 1, 6, D_MODEL), lambda i: (layer, _group_of_tile(i, TM_F), 0, 0)),
        ],
        out_specs=pl.BlockSpec((TM_F, D_MODEL), lambda i: (i, 0)),
        out_shape=jax.ShapeDtypeStruct((NTOK, D_MODEL), F32),
        scratch_shapes=[pltpu.VMEM((TM_F, D_MODEL), F32)] + _ffn_weight_scratch(),
        compiler_params=pltpu.CompilerParams(
            dimension_semantics=("arbitrary",), vmem_limit_bytes=VMEM_LIMIT),
        name="ffn",
    )(h2, res, wg, wu, wd, mods)


def _tile_plan(counts):
    nt = (counts + TR - 1) // TR
    cum = jnp.cumsum(nt)
    total = cum[-1]
    t = jnp.arange(MAX_TILES, dtype=jnp.int32)
    tt = jnp.minimum(t, total - 1)
    e = jnp.sum((cum[None, :] <= tt[:, None]).astype(jnp.int32), axis=1)
    k = tt - (cum - nt)[e]
    n = jnp.where(t < total, jnp.clip(counts[e] - k * TR, 0, TR), 0)
    return e.astype(jnp.int32), (e * REG_TILES + k).astype(jnp.int32), n.astype(jnp.int32)


def _row_copy(src, src_row, dst, dst_row, sem):
    return pltpu.make_async_copy(src.at[pl.ds(src_row, 1)], dst.at[pl.ds(dst_row, 1)], sem)


def _dispatch_kernel(slot_ref, h_ref, xs_ref, sem):
    base = pl.program_id(0) * (TOP_K * TD)

    def issue(r, carry):
        for k in range(TOP_K):
            _row_copy(h_ref, r, xs_ref, slot_ref[base + TOP_K * r + k], sem).start(priority=k % 2)
        return carry

    lax.fori_loop(0, TD, issue, 0, unroll=8)
    for k in range(TOP_K):
        pltpu.make_async_copy(h_ref, xs_ref.at[pl.ds(0, TD)], sem).wait()


def _dispatch(slots, h2f):
    return pl.pallas_call(
        _dispatch_kernel,
        grid_spec=pltpu.PrefetchScalarGridSpec(
            num_scalar_prefetch=1,
            grid=(NTOK // TD,),
            in_specs=[pl.BlockSpec((TD, D_MODEL), lambda i, s: (i, 0))],
            out_specs=pl.BlockSpec(memory_space=pl.ANY),
            scratch_shapes=[pltpu.SemaphoreType.DMA],
        ),
        out_shape=jax.ShapeDtypeStruct((N_EXPERTS * REG, D_MODEL), F32),
        compiler_params=pltpu.CompilerParams(
            dimension_semantics=("arbitrary",), vmem_limit_bytes=VMEM_LIMIT),
        name="moe_dispatch",
    )(slots, h2f)


def _gffn_kernel(te_ref, tb_ref, tn_ref, x_ref, wg_ref, wu_ref, wd_ref, o_ref,
                 acc, w_gu, w_d, stg_gu, stg_d, sem, *, w_base):
    t = pl.program_id(0)
    n = tn_ref[t]

    @pl.when(n > 0)
    def _():
        row = lax.broadcasted_iota(jnp.int32, (TR, D_MODEL), 0)
        h = jnp.where(row < n, x_ref[...], 0.0).astype(BF16)
        first = tb_ref[t] % REG_TILES == 0

        @pl.when(first)
        def _():
            e = w_base + te_ref[t]
            _swiglu_tile(h, w_gu, w_d, acc, (wg_ref.at[e], wu_ref.at[e], wd_ref.at[e], stg_gu, stg_d, sem))

        @pl.when(jnp.logical_not(first))
        def _():
            _swiglu_tile(h, w_gu, w_d, acc)

        o_ref[...] = acc[...]


def _gffn(tile_e, tile_blk, tile_n, xs, wg, wu, wd, w_base):
    return pl.pallas_call(
        functools.partial(_gffn_kernel, w_base=w_base),
        grid_spec=pltpu.PrefetchScalarGridSpec(
            num_scalar_prefetch=3,
            grid=(MAX_TILES,),
            in_specs=[
                pl.BlockSpec((TR, D_MODEL), lambda t, te, tb, tn: (tb[t], 0)),
                pl.BlockSpec(memory_space=pl.ANY),
                pl.BlockSpec(memory_space=pl.ANY),
                pl.BlockSpec(memory_space=pl.ANY),
            ],
            out_specs=pl.BlockSpec((TR, D_MODEL), lambda t, te, tb, tn: (tb[t], 0)),
            scratch_shapes=[pltpu.VMEM((TR, D_MODEL), F32)] + _ffn_weight_scratch(),
        ),
        out_shape=jax.ShapeDtypeStruct((N_EXPERTS * REG, D_MODEL), F32),
        compiler_params=pltpu.CompilerParams(
            dimension_semantics=("arbitrary",), vmem_limit_bytes=VMEM_LIMIT),
        name="moe_ffn",
    )(tile_e, tile_blk, tile_n, xs, wg, wu, wd)


def _combine_kernel(slot_ref, x1_ref, rinfo_ref, mod_ref, ys_ref, *rest, final_norm):
    if final_norm:
        gf_ref, yp_ref, yl_ref, buf, sem = rest
    else:
        o_ref, buf, sem = rest
    i = pl.program_id(0)
    n_steps = pl.num_programs(0)

    def gather(step, s):
        base = step * (TOP_K * TD)

        def issue(r, carry):
            for k in range(TOP_K):
                _row_copy(ys_ref, slot_ref[base + TOP_K * r + k], buf.at[s, k], r, sem.at[s]).start(priority=k % 2)
            return carry

        lax.fori_loop(0, TD, issue, 0, unroll=8)

    def wait(s):
        for k in range(TOP_K):
            pltpu.make_async_copy(ys_ref.at[pl.ds(0, TD)], buf.at[s, k], sem.at[s]).wait()

    @pl.when(i == 0)
    def _():
        gather(0, 0)

    for s in range(2):
        @pl.when(i % 2 == s)
        def _():
            @pl.when(i + 1 < n_steps)
            def _():
                gather(i + 1, 1 - s)

            wait(s)

    b = buf.at[i % 2]
    y = rinfo_ref[:, R_W1:R_W1 + 1] * b[0] + rinfo_ref[:, R_W2:R_W2 + 1] * b[1]
    x = x1_ref[...] + mod_ref[0, 0, 5:6, :] * y
    if final_norm:
        out = _rmsnorm(x, gf_ref[...])
        is_lat = pl.program_id(0) >= N_CTX // TD

        @pl.when(jnp.logical_not(is_lat))
        def _():
            yp_ref[...] = out

        @pl.when(is_lat)
        def _():
            yl_ref[...] = out
    else:
        o_ref[...] = x


def _combine(slots, x1, rinfo, mods, ys, layer, norm_f=None):
    final_norm = norm_f is not None
    n_ctx_t = N_CTX // TD
    in_specs = [
        pl.BlockSpec((TD, D_MODEL), lambda i, s: (i, 0)),
        pl.BlockSpec((TD, LANES), lambda i, s: (i, 0)),
        pl.BlockSpec((1, 1, 6, D_MODEL), lambda i, s: (layer, _group_of_tile(i, TD), 0, 0)),
        pl.BlockSpec(memory_space=pl.ANY),
    ]
    args = [slots, x1, rinfo, mods, ys]
    if final_norm:
        in_specs.append(pl.BlockSpec((1, D_MODEL), lambda i, s: (0, 0)))
        args.append(norm_f)
        out_specs = [pl.BlockSpec((TD, D_MODEL), lambda i, s: (jnp.minimum(i, n_ctx_t - 1), 0)),
                     pl.BlockSpec((TD, D_MODEL), lambda i, s: (jnp.maximum(i - n_ctx_t, 0), 0))]
        out_shape = [jax.ShapeDtypeStruct((N_CTX, D_MODEL), F32), jax.ShapeDtypeStruct((N_LAT, D_MODEL), F32)]
    else:
        out_specs = pl.BlockSpec((TD, D_MODEL), lambda i, s: (i, 0))
        out_shape = jax.ShapeDtypeStruct((NTOK, D_MODEL), F32)
    return pl.pallas_call(
        functools.partial(_combine_kernel, final_norm=final_norm),
        grid_spec=pltpu.PrefetchScalarGridSpec(
            num_scalar_prefetch=1,
            grid=(NTOK // TD,),
            in_specs=in_specs,
            out_specs=out_specs,
            scratch_shapes=[pltpu.VMEM((2, TOP_K, TD, D_MODEL), F32), pltpu.SemaphoreType.DMA((2,))],
        ),
        out_shape=out_shape,
        compiler_params=pltpu.CompilerParams(
            dimension_semantics=("arbitrary",), vmem_limit_bytes=VMEM_LIMIT),
        name="moe_combine",
    )(*args)


def _final_kernel(x_ref, g_ref, o_ref):
    o_ref[...] = _rmsnorm(x_ref[...], g_ref[...])


def _final(x, g, row_off, rows):
    off = row_off // TM
    return pl.pallas_call(
        _final_kernel,
        grid=(rows // TM,),
        in_specs=[
            pl.BlockSpec((TM, D_MODEL), lambda i: (off + i, 0)),
            pl.BlockSpec((1, D_MODEL), lambda i: (0, 0)),
        ],
        out_specs=pl.BlockSpec((TM, D_MODEL), lambda i: (i, 0)),
        out_shape=jax.ShapeDtypeStruct((rows, D_MODEL), F32),
        compiler_params=pltpu.CompilerParams(
            dimension_semantics=("arbitrary",), vmem_limit_bytes=VMEM_LIMIT),
        name="final_norm",
    )(x, g)


def kernel(x_prompt, x_sample, state_mlstm_C, state_mlstm_n, state_mlstm_m, state_ret_S, c, c_ctx,
           norm1_g, norm2_g, norm_f_g, w_ada, b_ada, w_in, b_gates, ret_decay_logit,
           mlstm_norm_g, ret_norm_g, w_out, ffn_w_gate, ffn_w_up, ffn_w_down,
           moe_w_router, moe_w_gate, moe_w_up, moe_w_down):
    xs_in = (x_prompt.reshape(N_CTX, D_MODEL), x_sample.reshape(N_LAT, D_MODEL))
    cvec = jnp.concatenate(
        [c_ctx[None, :], c, jnp.zeros((N_GROUPS - 1 - DEC_BATCH, D_MODEL), F32)], 0)
    mods = _ada(cvec, w_ada, b_ada).reshape(DEPTH, N_GROUPS, 6, D_MODEL)

    n_m = 4 * W_M
    w_in_t = jnp.swapaxes(w_in, 1, 2)
    n_if = N_GATES // 2
    lane_pad = ((0, 0), (0, 0), (0, LANES - n_if))
    wg = jnp.concatenate([jnp.pad(w_in[:, :, n_m:n_m + n_if], lane_pad),
                          jnp.pad(w_in[:, :, n_m + n_if:n_m + N_GATES], lane_pad)], -1)
    bg = jnp.concatenate([jnp.pad(b_gates[:, None, :n_if], lane_pad),
                          jnp.pad(b_gates[:, None, n_if:], lane_pad)], -1)
    cos_np, sin_np = _rope_tables()
    cos_t, sin_t = jnp.asarray(cos_np), jnp.asarray(sin_np)
    dl = jnp.broadcast_to(ret_decay_logit.reshape(DEPTH, 2 * H_R, 1), (DEPTH, 2 * H_R, LANES))
    m0 = jnp.pad(state_mlstm_m.reshape(DEC_BATCH, DEPTH, 1, 2 * H_M),
                 ((0, 0), (0, 0), (0, 0), (0, LANES - 2 * H_M)))
    g1 = norm1_g.reshape(DEPTH, 1, D_MODEL)
    g2 = norm2_g.reshape(DEPTH, 1, D_MODEL)
    nm = mlstm_norm_g.reshape(DEPTH, 1, W_M)
    nr = ret_norm_g.reshape(DEPTH, 1, W_R)
    n_moe = moe_w_router.shape[0]
    wr_pad = jnp.pad(moe_w_router, ((0, 0), (0, 0), (0, LANES - N_EXPERTS)))
    moe_g = moe_w_gate.reshape(n_moe * N_EXPERTS, D_MODEL, D_FF)
    moe_u = moe_w_up.reshape(n_moe * N_EXPERTS, D_MODEL, D_FF)
    moe_d = moe_w_down.reshape(n_moe * N_EXPERTS, D_FF, D_MODEL)

    states = ()
    xs = xs_in
    for l in range(DEPTH):
        jl = l // 2
        p, gates = _proj(xs, g1, mods, w_in_t, wg, bg, cos_t, sin_t, l)
        y, *states = _scan_ctx(p, gates, dl, nm, nr, l, states)
        y = _scan_lat(p, gates, dl, nm, nr, state_mlstm_C, state_mlstm_n, m0, state_ret_S, y, l)
        if l % 2 == 0:
            x1, h2 = _out(y, xs, w_out, g2, mods, l)
            x = _ffn(h2, x1, ffn_w_gate, ffn_w_up, ffn_w_down, mods, l, jl)
        else:
            x1, h2f, rinfo, cnt = _out(y, xs, w_out, g2, mods, l, wr_pad, jl)
            slots = rinfo[:, R_S1:R_S2 + 1].astype(jnp.int32).reshape(TOP_K * NTOK)
            tile_e, tile_blk, tile_n = _tile_plan(cnt[0, :N_EXPERTS].astype(jnp.int32))
            xd = _dispatch(slots, h2f)
            yd = _gffn(tile_e, tile_blk, tile_n, xd, moe_g, moe_u, moe_d, jl * N_EXPERTS)
            if l == DEPTH - 1:
                y_ctx, y_lat = _combine(slots, x1, rinfo, mods, yd, l, norm_f_g.reshape(1, D_MODEL))
            else:
                x = _combine(slots, x1, rinfo, mods, yd, l)
        xs = (x,)

    if DEPTH % 2 == 1:
        y_ctx = _final(x, norm_f_g.reshape(1, D_MODEL), 0, N_CTX)
        y_lat = _final(x, norm_f_g.reshape(1, D_MODEL), N_CTX, N_LAT)
    y_prompt = y_ctx.reshape(BATCH, SEQ, D_MODEL)
    y_sample = y_lat.reshape(DEC_BATCH, DEC_SEQ, D_MODEL)
    new_C, new_n, new_m, new_S = states
    return (y_prompt, y_sample, new_C, new_n,
            new_m[:, :, 0, :2 * H_M].reshape(BATCH, DEPTH, 2, H_M), new_S)
```

```python
import functools

import numpy as np
import jax
import jax.numpy as jnp
from jax import lax
from jax.experimental import pallas as pl
from jax.experimental.pallas import tpu as pltpu

D_MODEL = 1024
BATCH = 32
SEQ = 256
DEPTH = 2
DEC_BATCH = 2
DEC_SEQ = 1024
GRID_W = 64
H_M = 4
DH = 128
H_R = 4
W_M = H_M * DH
W_R = H_R * DH
N_GATES = 4 * H_M
CHUNK = 128
D_FF = 2816
N_EXPERTS = 8
ROPE_BASE = 10000.0
EPS = 1e-6

N_CTX = BATCH * SEQ
N_LAT = DEC_BATCH * DEC_SEQ
NTOK = N_CTX + N_LAT
N_GROUPS = 8
K_SCALE = DH ** -0.5
P_COLS = 4 * W_M + 4 * W_R
LANES = 128
GATE_LANES = 2 * LANES
VMEM_LIMIT = 56 * 1024 * 1024

F32 = jnp.float32
BF16 = jnp.bfloat16
HIGHEST = lax.Precision.HIGHEST

TM = 1024
TN = 1024
FC = 256
TM_F = 512
TOP_K = 2
TR = 896
REG_TILES = -(-NTOK // TR)
REG = REG_TILES * TR
MAX_TILES = -(-TOP_K * NTOK // TR) + N_EXPERTS
TD = 512


def _group_of_tile(i, tm):
    return jnp.maximum(i * tm // DEC_SEQ - (N_CTX // DEC_SEQ - 1), 0)


def _silu(x):
    return x * jax.nn.sigmoid(x)


def _log_sigmoid(x):
    return jnp.minimum(x, 0.0) - jnp.log(1.0 + jnp.exp(-jnp.abs(x)))


def _rmsnorm(x, g):
    return x * lax.rsqrt(jnp.mean(x * x, -1, keepdims=True) + EPS) * g


def _ada_kernel(cv_ref, w_ref, b_ref, o_ref):
    s = _silu(cv_ref[...]).astype(BF16)
    o_ref[0] = jnp.dot(s, w_ref[0].astype(BF16), preferred_element_type=F32) + b_ref[0]


def _ada(cvec, w_ada, b_ada):
    tn = 1536
    n = 6 * D_MODEL
    return pl.pallas_call(
        _ada_kernel,
        grid=(DEPTH, n // tn),
        in_specs=[
            pl.BlockSpec((N_GROUPS, D_MODEL), lambda l, j: (0, 0)),
            pl.BlockSpec((1, D_MODEL, tn), lambda l, j: (l, 0, j)),
            pl.BlockSpec((1, 1, tn), lambda l, j: (l, 0, j)),
        ],
        out_specs=pl.BlockSpec((1, N_GROUPS, tn), lambda l, j: (l, 0, j)),
        out_shape=jax.ShapeDtypeStruct((DEPTH, N_GROUPS, n), F32),
        compiler_params=pltpu.CompilerParams(
            dimension_semantics=("arbitrary", "arbitrary"), vmem_limit_bytes=VMEM_LIMIT),
        name="ada",
    )(cvec, w_ada, b_ada.reshape(DEPTH, 1, n))


def _rope_tables():
    half = DH // 4
    freqs = ROPE_BASE ** (-np.arange(half, dtype=np.float64) / half)
    t = np.arange(DEC_SEQ)
    pos = np.stack([t // GRID_W, t % GRID_W], 1).astype(np.float64)
    d = np.arange(DH)
    ang = pos[:, d // (DH // 2)] * freqs[d % half][None, :]
    sign = np.where((d % (DH // 2)) < half, -1.0, 1.0)[None, :]
    return np.cos(ang).astype(np.float32), (sign * np.sin(ang)).astype(np.float32)


def _rope(a, cos, sin):
    lane = lax.broadcasted_iota(jnp.int32, a.shape, 1)
    first = (lane % (DH // 2)) < (DH // 4)
    partner = jnp.where(first, pltpu.roll(a, DH - DH // 4, 1), pltpu.roll(a, DH // 4, 1))
    return a * cos + partner * sin


def _proj_kernel(*refs, n_ctx_tiles, split_x, layer):
    if split_x:
        (xp_ref, xl_ref, g_ref, mod_ref, wt_ref, wg_ref, bg_ref, cos_ref, sin_ref,
         p_ref, gate_ref, h_scr, w_res, w_stg, w_sem) = refs
    else:
        (x_ref, g_ref, mod_ref, wt_ref, wg_ref, bg_ref, cos_ref, sin_ref,
         p_ref, gate_ref, h_scr, w_res, w_stg, w_sem) = refs
    p_ref = p_ref.at[0]
    i = pl.program_id(0)
    j = pl.program_id(1)
    is_lat = i >= n_ctx_tiles
    half = TN // 2

    def prologue(x):
        h = _rmsnorm(x, g_ref[0]) * (1.0 + mod_ref[0, 0, 1:2, :]) + mod_ref[0, 0, 0:1, :]
        h_scr[...] = h.astype(BF16)
        gate_ref[...] = _dot_f32x3(h, wg_ref[0]) + bg_ref[0]

    @pl.when(j == 0)
    def _():
        if split_x:
            pl.when(jnp.logical_not(is_lat))(lambda: prologue(xp_ref[...]))
            pl.when(is_lat)(lambda: prologue(xl_ref[...]))
        else:
            prologue(x_ref[...])

    n_col_tiles = P_COLS // TN

    def tile_copy(jj):
        row0 = jj * TN + (N_GATES if jj * TN >= 4 * W_M else 0)
        s = jj % 2
        return pltpu.make_async_copy(wt_ref.at[layer, pl.ds(row0, TN), :], w_stg.at[s], w_sem.at[s])

    def matmul(jj):
        @pl.when(i == 0)
        def _():
            if jj == 0:
                tile_copy(0).start()
            if jj + 1 < n_col_tiles:
                tile_copy(jj + 1).start()
            tile_copy(jj).wait()
            w_res[jj] = w_stg[jj % 2].T.astype(BF16)

        return _dot(h_scr[...], w_res[jj])

    @pl.when(j == 0)
    def _():
        acc = matmul(0)
        p_ref[:, :half] = acc[:, :half].astype(BF16)
        p_ref[:, half:] = (acc[:, half:] * K_SCALE).astype(BF16)

    @pl.when(j == 1)
    def _():
        p_ref[...] = matmul(1).astype(BF16)

    @pl.when(j == 2)
    def _():
        acc = matmul(2)

        @pl.when(is_lat)
        def _():
            cos = cos_ref[...]
            sin = sin_ref[...]
            for hd in range(TN // DH):
                sl = slice(hd * DH, (hd + 1) * DH)
                r = _rope(acc[:, sl], cos, sin)
                p_ref[:, sl] = (r * K_SCALE if hd * DH >= half else r).astype(BF16)

        @pl.when(jnp.logical_not(is_lat))
        def _():
            p_ref[:, :half] = acc[:, :half].astype(BF16)
            p_ref[:, half:] = (acc[:, half:] * K_SCALE).astype(BF16)

    @pl.when(j == 3)
    def _():
        p_ref[...] = matmul(3).astype(BF16)


def _proj(xs, g1, mods, w_in_t, wg, bg, cos_t, sin_t, layer):
    n_ctx_tiles = N_CTX // TM
    tiles_per_seq = DEC_SEQ // TM
    split_x = len(xs) == 2
    if split_x:
        x_specs = [pl.BlockSpec((TM, D_MODEL), lambda i, j: (jnp.minimum(i, n_ctx_tiles - 1), 0)),
                   pl.BlockSpec((TM, D_MODEL), lambda i, j: (jnp.maximum(i - n_ctx_tiles, 0), 0))]
    else:
        x_specs = [pl.BlockSpec((TM, D_MODEL), lambda i, j: (i, 0))]
    return pl.pallas_call(
        functools.partial(_proj_kernel, n_ctx_tiles=n_ctx_tiles, split_x=split_x, layer=layer),
        grid=(NTOK // TM, P_COLS // TN),
        in_specs=x_specs + [
            pl.BlockSpec((1, 1, D_MODEL), lambda i, j: (layer, 0, 0)),
            pl.BlockSpec((1, 1, 6, D_MODEL), lambda i, j: (layer, _group_of_tile(i, TM), 0, 0)),
            pl.BlockSpec(memory_space=pl.ANY),
            pl.BlockSpec((1, D_MODEL, GATE_LANES), lambda i, j: (layer, 0, 0)),
            pl.BlockSpec((1, 1, GATE_LANES), lambda i, j: (layer, 0, 0)),
            pl.BlockSpec((TM, DH), lambda i, j: (i % tiles_per_seq, 0)),
            pl.BlockSpec((TM, DH), lambda i, j: (i % tiles_per_seq, 0)),
        ],
        out_specs=[
            pl.BlockSpec((1, TM, TN), lambda i, j: (j, i, 0)),
            pl.BlockSpec((TM, GATE_LANES), lambda i, j: (i, 0)),
        ],
        out_shape=[
            jax.ShapeDtypeStruct((P_COLS // TN, NTOK, TN), BF16),
            jax.ShapeDtypeStruct((NTOK, GATE_LANES), F32),
        ],
        scratch_shapes=[pltpu.VMEM((TM, D_MODEL), BF16),
                        pltpu.VMEM((P_COLS // TN, D_MODEL, TN), BF16),
                        pltpu.VMEM((2, TN, D_MODEL), F32),
                        pltpu.SemaphoreType.DMA((2,))],
        compiler_params=pltpu.CompilerParams(
            dimension_semantics=("arbitrary", "arbitrary"), vmem_limit_bytes=VMEM_LIMIT),
        name="proj",
    )(*xs, g1, mods, w_in_t, wg, bg, cos_t, sin_t)


def _split3(x):
    hi = x.astype(BF16)
    r1 = x - hi.astype(F32)
    mid = r1.astype(BF16)
    lo = (r1 - mid.astype(F32)).astype(BF16)
    return hi, mid, lo


def _dot(a, b):
    return jnp.dot(a, b, preferred_element_type=F32)


def _dot_nt(a, b):
    return lax.dot_general(a, b, (((1,), (1,)), ((), ())), preferred_element_type=F32)


def _tri_dot_left(tri, x):
    hi, mid, lo = _split3(x)
    return _dot(tri, hi) + _dot(tri, mid) + _dot(tri, lo)


def _tri_dot_right(x, tri):
    hi, mid, lo = _split3(x)
    return _dot(hi, tri) + _dot(mid, tri) + _dot(lo, tri)


def _run_max(x, reverse):
    n_tiles = x.shape[0] // 8
    sub = lax.broadcasted_iota(jnp.int32, (8, LANES), 0)
    out = [None] * n_tiles
    carry = None
    for t in (range(n_tiles - 1, -1, -1) if reverse else range(n_tiles)):
        v = x[8 * t:8 * t + 8, :]
        for s in (1, 2, 4):
            if reverse:
                v = jnp.maximum(v, jnp.where(sub < 8 - s, pltpu.roll(v, 8 - s, 0), -jnp.inf))
            else:
                v = jnp.maximum(v, jnp.where(sub >= s, pltpu.roll(v, s, 0), -jnp.inf))
        if carry is not None:
            v = jnp.maximum(v, carry)
        carry = jnp.broadcast_to(v[0:1, :] if reverse else v[7:8, :], (8, LANES))
        out[t] = v
    return jnp.concatenate(out, axis=0)


def _scan_kernel(*refs, T, has_state, n_prev=0):
    if has_state:
        (p_ref, g_ref, dl_ref, nm_ref, nr_ref, C0_ref, n0_ref, m0_ref, S0_ref, _yprev_ref,
         y_ref, CN_s, S_s, m_s, hf_s, hb_s, dm_s, dq_s, dk_s, dL_s, kT_s) = refs
    else:
        p_ref, g_ref, dl_ref, nm_ref, nr_ref = refs[:5]
        prev_refs = refs[5:9] if n_prev else ()
        (y_ref, C_out, n_out, m_out, S_out,
         CN_s, S_s, m_s, hf_s, hb_s, dm_s, dq_s, dk_s, dL_s, kT_s) = refs[5 + len(prev_refs):]
    L = CHUNK
    n_chunks = T // L
    row_i = lax.broadcasted_iota(jnp.int32, (L, L), 0)
    col_j = lax.broadcasted_iota(jnp.int32, (L, L), 1)
    lower = col_j <= row_i
    upper = col_j >= row_i
    tril = lower.astype(BF16)
    triu = upper.astype(BF16)
    ones = jnp.ones((L, DH), BF16)
    c_km = W_M
    c_vm = 2 * W_M
    c_om = 3 * W_M
    c_qr = 4 * W_M
    c_kr = c_qr + W_R
    c_vr = c_qr + 2 * W_R
    c_gr = c_qr + 3 * W_R

    def pcols(rows, col):
        return p_ref[col // TN, rows, col % TN:col % TN + DH]

    for d in range(2):
        for h in range(H_M):
            k = d * H_M + h
            if has_state:
                CN_s[k, :, :DH] = C0_ref[0, 0, d, h]
                CN_s[k, :, DH:] = jnp.broadcast_to(n0_ref[0, 0, d, h:h + 1, :], (DH, DH)).T
                S_s[k] = S0_ref[0, 0, d, h]
            else:
                CN_s[k] = jnp.zeros((DH, 2 * DH), F32)
                S_s[k] = jnp.zeros((DH, DH), F32)
    m_s[...] = m0_ref[0, 0] if has_state else jnp.zeros((1, LANES), F32)

    @pl.when(pl.program_id(0) == 0)
    def _():
        pos_i = row_i.astype(F32)
        pos_j = col_j.astype(F32)
        for d in range(2):
            for h in range(H_R):
                k = d * H_R + h
                lg_row = _log_sigmoid(dl_ref[0, k:k + 1, :])
                lg = jnp.broadcast_to(lg_row, (L, L))
                rel = (row_i - col_j if d == 0 else col_j - row_i).astype(F32)
                dm_s[k] = jnp.where(rel >= 0, jnp.exp(lg * jnp.maximum(rel, 0.0)), 0.0)
                dq_s[k] = jnp.exp(lg * (pos_i + 1.0 if d == 0 else L - pos_i))
                dk_s[k] = jnp.exp(lg * (L - 1.0 - pos_j if d == 0 else pos_j))
                dL_s[k] = jnp.exp(lg_row * float(L))

    def transpose_keys(c, carry):
        r0 = pl.multiple_of(c * L, L)
        for h in range(H_M):
            kT_s[h, c] = pcols(pl.ds(r0, L), c_km + h * DH).astype(F32).T
            kT_s[H_M + h, c] = pcols(pl.ds(r0, L), c_kr + h * DH).astype(F32).T
        return carry

    lax.fori_loop(0, n_chunks, transpose_keys, 0)

    def chunk_step(c, carry):
        m_prev = m_s[...]
        m_new = []
        prep = []
        for d in range(2):
            ci = c if d == 0 else n_chunks - 1 - c
            r0 = pl.multiple_of(ci * L, L)
            mask = lower if d == 0 else upper
            e_row = L - 1 if d == 0 else 0
            FL = _log_sigmoid(g_ref[pl.ds(r0, L), LANES:2 * LANES])
            Bc = _tri_dot_left(tril if d == 0 else triu, FL)
            Zc = g_ref[pl.ds(r0, L), 0:LANES] - Bc
            M = jnp.maximum(_run_max(Zc, reverse=(d == 1)), m_prev)
            m_row = Bc + M
            M_end = M[e_row:e_row + 1, :]
            m_new.append(Bc[e_row:e_row + 1, :] + M_end)
            decay = jnp.exp(m_prev - M_end)
            prep.append(dict(ci=ci, r0=r0, mask=mask, M=M, m_row=m_row, decay=decay,
                             ZT=Zc.T,
                             WT=jnp.exp(Zc - M_end).T))
        pairs = [(d, h) for d in range(2) for h in range(H_M)]

        def rows(d, col):
            return pcols(pl.ds(prep[d]["r0"], L), col)

        qk, qkr = {}, {}
        for d, h in pairs:
            qk[d, h] = _dot_nt(rows(d, h * DH), rows(d, c_km + h * DH))
            qkr[d, h] = _dot_nt(rows(d, c_qr + h * DH), rows(d, c_kr + h * DH))
        upd, updr = {}, {}
        for d, h in pairs:
            k = d * H_M + h
            ci = prep[d]["ci"]
            vo = jnp.concatenate([rows(d, c_vm + h * DH), ones], axis=1)
            wkT = (kT_s[h, ci] * jnp.broadcast_to(prep[d]["WT"][k:k + 1, :], (DH, L))).astype(BF16)
            upd[d, h] = _dot(wkT, vo)
            kdT = (kT_s[H_M + h, ci] * dk_s[k]).astype(BF16)
            updr[d, h] = _dot(kdT, rows(d, c_vr + h * DH))
        for d, h in pairs:
            k = d * H_M + h
            r0 = prep[d]["r0"]
            h_dst = hf_s if d == 0 else hb_s
            q = rows(d, h * DH)
            M_col = jnp.broadcast_to(prep[d]["M"][:, k:k + 1], (L, L))
            z_row = jnp.broadcast_to(prep[d]["ZT"][k:k + 1, :], (L, L))
            D = jnp.where(prep[d]["mask"], jnp.exp(z_row - M_col), 0.0)
            s = (qk[d, h] * D).astype(BF16)
            w_inter = jnp.exp(jnp.broadcast_to(m_prev[:, k:k + 1], (L, L)) - M_col)
            wq = (w_inter * q.astype(F32)).astype(BF16)
            vo = jnp.concatenate([rows(d, c_vm + h * DH), ones], axis=1)
            CN = CN_s[k]
            res = _dot(jnp.concatenate([s, wq], axis=1),
                       jnp.concatenate([vo, CN.astype(BF16)], axis=0))
            floor = jnp.exp(-jnp.broadcast_to(prep[d]["m_row"][:, k:k + 1], (L, L)))
            h_dst[pl.ds(r0, L), h * DH:(h + 1) * DH] = res[:, :DH] / jnp.maximum(jnp.abs(res[:, DH:]), floor)
            CN_s[k] = jnp.broadcast_to(prep[d]["decay"][:, k:k + 1], (DH, 2 * DH)) * CN + upd[d, h]
            qr = rows(d, c_qr + h * DH)
            S = S_s[k]
            sr = (qkr[d, h] * dm_s[k]).astype(BF16)
            qd = (qr.astype(F32) * dq_s[k]).astype(BF16)
            h_dst[pl.ds(r0, L), W_M + h * DH:W_M + (h + 1) * DH] = _dot(
                jnp.concatenate([sr, qd], axis=1),
                jnp.concatenate([rows(d, c_vr + h * DH), S.astype(BF16)], axis=0))
            S_s[k] = dL_s[k] * S + updr[d, h]
        lane = lax.broadcasted_iota(jnp.int32, (1, LANES), 1)
        m_s[...] = jnp.where(lane < H_M, m_new[0], m_new[1])
        return carry

    lax.fori_loop(0, n_chunks, chunk_step, 0)

    for h in range(H_M):
        sl = slice(h * DH, (h + 1) * DH)
        hs = hf_s[:, sl] + hb_s[:, sl]
        yn = _rmsnorm(hs, nm_ref[0, :, sl])
        om = pcols(slice(None), c_om + h * DH).astype(F32)
        y_ref[:, sl] = (jax.nn.sigmoid(om) * yn).astype(BF16)
        slr = slice(W_M + h * DH, W_M + (h + 1) * DH)
        hr = hf_s[:, slr] + hb_s[:, slr]
        ynr = _rmsnorm(hr, nr_ref[0, :, sl])
        gr = pcols(slice(None), c_gr + h * DH).astype(F32)
        y_ref[:, slr] = (_silu(gr) * ynr).astype(BF16)

    if not has_state:
        for prev, out in zip(prev_refs, (C_out, n_out, m_out, S_out)):
            out[0, :n_prev] = prev[0]
        for d in range(2):
            for h in range(H_M):
                k = d * H_M + h
                C_out[0, n_prev, d, h] = CN_s[k, :, :DH]
                n_out[0, n_prev, d, h:h + 1, :] = CN_s[k, :, DH:].T[0:1, :]
                S_out[0, n_prev, d, h] = S_s[k]
        m_out[0, n_prev] = m_s[...]


def _scan_scratch(T):
    return [
        pltpu.VMEM((2 * H_M, DH, 2 * DH), F32),
        pltpu.VMEM((2 * H_R, DH, DH), F32),
        pltpu.VMEM((1, LANES), F32),
        pltpu.VMEM((T, W_M + W_R), F32),
        pltpu.VMEM((T, W_M + W_R), F32),
        pltpu.VMEM((2 * H_R, CHUNK, CHUNK), F32),
        pltpu.VMEM((2 * H_R, CHUNK, CHUNK), F32),
        pltpu.VMEM((2 * H_R, CHUNK, CHUNK), F32),
        pltpu.VMEM((2 * H_R, 1, LANES), F32),
        pltpu.VMEM((H_M + H_R, T // CHUNK, DH, CHUNK), F32),
    ]


def _scan_ctx(p, gates, dl, nm, nr, layer, prev_states=()):
    T = SEQ
    n_lay = layer + 1
    state_tails = [(2, H_M, DH, DH), (2, H_M, DH), (1, LANES), (2, H_R, DH, DH)]

    def state_spec(n, tail):
        return pl.BlockSpec((1, n) + tail, lambda b: (b,) + (0,) * (1 + len(tail)))
    common = [
        pl.BlockSpec((P_COLS // TN, T, TN), lambda b: (0, b, 0)),
        pl.BlockSpec((T, GATE_LANES), lambda b: (b, 0)),
        pl.BlockSpec((1, 2 * H_R, LANES), lambda b: (layer, 0, 0)),
        pl.BlockSpec((1, 1, W_M), lambda b: (layer, 0, 0)),
        pl.BlockSpec((1, 1, W_R), lambda b: (layer, 0, 0)),
    ]
    return pl.pallas_call(
        functools.partial(_scan_kernel, T=T, has_state=False, n_prev=layer if prev_states else 0),
        grid=(BATCH,),
        in_specs=common + [state_spec(layer, tail) for tail in state_tails[:len(prev_states)]],
        out_specs=[pl.BlockSpec((T, D_MODEL), lambda b: (b, 0))] + [state_spec(n_lay, tail) for tail in state_tails],
        out_shape=[jax.ShapeDtypeStruct((NTOK, D_MODEL), BF16)] + [
            jax.ShapeDtypeStruct((BATCH, n_lay) + tail, F32) for tail in state_tails],
        scratch_shapes=_scan_scratch(T),
        compiler_params=pltpu.CompilerParams(
            dimension_semantics=("arbitrary",), vmem_limit_bytes=VMEM_LIMIT),
        name="scan_ctx",
    )(p, gates, dl, nm, nr, *prev_states)


def _scan_lat(p, gates, dl, nm, nr, C0, n0, m0, S0, y_prev, layer):
    T = DEC_SEQ
    off = N_CTX // T
    in_specs = [
        pl.BlockSpec((P_COLS // TN, T, TN), lambda b: (0, off + b, 0)),
        pl.BlockSpec((T, GATE_LANES), lambda b: (off + b, 0)),
        pl.BlockSpec((1, 2 * H_R, LANES), lambda b: (layer, 0, 0)),
        pl.BlockSpec((1, 1, W_M), lambda b: (layer, 0, 0)),
        pl.BlockSpec((1, 1, W_R), lambda b: (layer, 0, 0)),
        pl.BlockSpec((1, 1, 2, H_M, DH, DH), lambda b: (b, layer, 0, 0, 0, 0)),
        pl.BlockSpec((1, 1, 2, H_M, DH), lambda b: (b, layer, 0, 0, 0)),
        pl.BlockSpec((1, 1, 1, LANES), lambda b: (b, layer, 0, 0)),
        pl.BlockSpec((1, 1, 2, H_R, DH, DH), lambda b: (b, layer, 0, 0, 0, 0)),
        pl.BlockSpec(memory_space=pl.ANY),
    ]
    return pl.pallas_call(
        functools.partial(_scan_kernel, T=T, has_state=True),
        grid=(DEC_BATCH,),
        in_specs=in_specs,
        out_specs=pl.BlockSpec((T, D_MODEL), lambda b: (off + b, 0)),
        out_shape=jax.ShapeDtypeStruct((NTOK, D_MODEL), BF16),
        input_output_aliases={9: 0},
        scratch_shapes=_scan_scratch(T),
        compiler_params=pltpu.CompilerParams(
            dimension_semantics=("arbitrary",), vmem_limit_bytes=VMEM_LIMIT),
        name="scan_lat",
    )(p, gates, dl, nm, nr, C0, n0, m0, S0, y_prev)


def _top2(logits):
    lane = lax.broadcasted_iota(jnp.int32, logits.shape, 1)
    v1 = jnp.max(logits, -1, keepdims=True)
    i1 = jnp.min(jnp.where(logits == v1, lane, LANES), -1, keepdims=True)
    rest = jnp.where(lane == i1, -jnp.inf, logits)
    v2 = jnp.max(rest, -1, keepdims=True)
    i2 = jnp.min(jnp.where(rest == v2, lane, LANES), -1, keepdims=True)
    e2 = jnp.exp(v2 - v1)
    return i1, i2, 1.0 / (1.0 + e2), e2 / (1.0 + e2)


def _split2(x):
    hi = x.astype(BF16)
    return hi, (x - hi.astype(F32)).astype(BF16)


def _dot_f32x3(a, b):
    a_hi, a_lo = _split2(a)
    b_hi, b_lo = _split2(b)
    return _dot(a_hi, b_hi) + _dot(a_hi, b_lo) + _dot(a_lo, b_hi)


R_E1, R_E2, R_W1, R_W2, R_S1, R_S2 = range(6)


def _out_kernel(*refs, with_router, split_x):
    y_ref = refs[0]
    if split_x:
        xp_ref, xl_ref = refs[1:3]
        x_in = jnp.where(pl.program_id(0) >= N_CTX // TM, xl_ref[...], xp_ref[...])
    else:
        x_in = refs[1][...]
    refs = refs[3:] if split_x else refs[2:]
    if with_router:
        (w_ref, g_ref, mod_ref, wr_ref,
         x1_ref, h2_ref, rinfo_ref, cnt_ref, w_scr, tri_scr, cnt_scr) = refs
    else:
        w_ref, g_ref, mod_ref, x1_ref, h2_ref, w_scr = refs

    @pl.when(pl.program_id(0) == 0)
    def _():
        w_scr[...] = w_ref[0].astype(BF16)
        if with_router:
            r = lax.broadcasted_iota(jnp.int32, (LANES, LANES), 0)
            c = lax.broadcasted_iota(jnp.int32, (LANES, LANES), 1)
            tri_scr[...] = (c < r).astype(BF16)
            cnt_scr[...] = jnp.zeros_like(cnt_scr)

    o = jnp.dot(y_ref[...], w_scr[...], preferred_element_type=F32)
    x1 = x_in + mod_ref[0, 0, 2:3, :] * o
    x1_ref[...] = x1
    h2 = _rmsnorm(x1, g_ref[0]) * (1.0 + mod_ref[0, 0, 4:5, :]) + mod_ref[0, 0, 3:4, :]
    if not with_router:
        h2_ref[...] = h2.astype(BF16)
    else:
        h2_ref[...] = h2
        h_hi, h_lo = _split2(h2)
        w_hi, w_lo = _split2(wr_ref[0])
        t = _dot(h_hi, jnp.concatenate([w_hi, w_lo], axis=1))
        logits = t[:, :LANES] + t[:, LANES:] + _dot(h_lo, w_hi)
        lane = lax.broadcasted_iota(jnp.int32, logits.shape, 1)
        i1, i2, w1, w2 = _top2(jnp.where(lane < N_EXPERTS, logits, -jnp.inf))
        oh1 = lane == i1
        oh2 = lane == i2
        sel = jnp.where(oh1 | oh2, 1.0, 0.0)
        tri = tri_scr[...]
        run = cnt_scr[...]
        ranks = []
        for blk in range(TM // LANES):
            s_blk = sel[blk * LANES:(blk + 1) * LANES, :]
            ranks.append(_dot(tri, s_blk.astype(BF16)) + run)
            run = run + jnp.sum(s_blk, 0, keepdims=True)
        rank = jnp.concatenate(ranks, axis=0)
        r1 = jnp.sum(jnp.where(oh1, rank, 0.0), -1, keepdims=True)
        r2 = jnp.sum(jnp.where(oh2, rank, 0.0), -1, keepdims=True)
        s1 = i1.astype(F32) * float(REG) + r1
        s2 = i2.astype(F32) * float(REG) + r2
        info = jnp.zeros(logits.shape, F32)
        for col, val in ((R_E1, i1.astype(F32)), (R_E2, i2.astype(F32)), (R_W1, w1), (R_W2, w2),
                         (R_S1, s1), (R_S2, s2)):
            info = jnp.where(lane == col, val, info)
        rinfo_ref[...] = info
        cnt_scr[...] = run
        cnt_ref[...] = run


def _out(y, xs, w_out, g2, mods, layer, w_router_pad=None, router_idx=0):
    with_router = w_router_pad is not None
    split_x = len(xs) == 2
    n_ctx_tiles = N_CTX // TM
    if split_x:
        x_specs = [pl.BlockSpec((TM, D_MODEL), lambda i: (jnp.minimum(i, n_ctx_tiles - 1), 0)),
                   pl.BlockSpec((TM, D_MODEL), lambda i: (jnp.maximum(i - n_ctx_tiles, 0), 0))]
    else:
        x_specs = [pl.BlockSpec((TM, D_MODEL), lambda i: (i, 0))]
    in_specs = [pl.BlockSpec((TM, D_MODEL), lambda i: (i, 0))] + x_specs + [
        pl.BlockSpec((1, D_MODEL, D_MODEL), lambda i: (layer, 0, 0)),
        pl.BlockSpec((1, 1, D_MODEL), lambda i: (layer, 0, 0)),
        pl.BlockSpec((1, 1, 6, D_MODEL), lambda i: (layer, _group_of_tile(i, TM), 0, 0)),
    ]
    out_specs = [
        pl.BlockSpec((TM, D_MODEL), lambda i: (i, 0)),
        pl.BlockSpec((TM, D_MODEL), lambda i: (i, 0)),
    ]
    out_shape = [
        jax.ShapeDtypeStruct((NTOK, D_MODEL), F32),
        jax.ShapeDtypeStruct((NTOK, D_MODEL), F32 if with_router else BF16),
    ]
    args = [y, *xs, w_out, g2, mods]
    scratch = [pltpu.VMEM((D_MODEL, D_MODEL), BF16)]
    if with_router:
        in_specs.append(pl.BlockSpec((1, D_MODEL, LANES), lambda i: (router_idx, 0, 0)))
        out_specs += [pl.BlockSpec((TM, LANES), lambda i: (i, 0)),
                      pl.BlockSpec((1, LANES), lambda i: (0, 0))]
        out_shape += [jax.ShapeDtypeStruct((NTOK, LANES), F32),
                      jax.ShapeDtypeStruct((1, LANES), F32)]
        args.append(w_router_pad)
        scratch += [pltpu.VMEM((LANES, LANES), BF16), pltpu.VMEM((1, LANES), F32)]
    return pl.pallas_call(
        functools.partial(_out_kernel, with_router=with_router, split_x=split_x),
        grid=(NTOK // TM,),
        in_specs=in_specs,
        out_specs=out_specs,
        out_shape=out_shape,
        scratch_shapes=scratch,
        compiler_params=pltpu.CompilerParams(
            dimension_semantics=("arbitrary",), vmem_limit_bytes=VMEM_LIMIT),
        name="out_router" if with_router else "out",
    )(*args)


N_FC = D_FF // FC
def _chunk_copies(fetch, f):
    wg_hbm, wu_hbm, wd_hbm, stg_gu, stg_d, sem = fetch
    s = f % 2
    cols = pl.ds(f * FC, FC)
    return (pltpu.make_async_copy(wg_hbm.at[:, cols], stg_gu.at[s, 0], sem.at[s, 0]),
            pltpu.make_async_copy(wu_hbm.at[:, cols], stg_gu.at[s, 1], sem.at[s, 1]),
            pltpu.make_async_copy(wd_hbm.at[cols, :], stg_d.at[s], sem.at[s, 2]))


def _start_first_chunks(fetch):
    for f in range(min(2, N_FC)):
        for c in _chunk_copies(fetch, f):
            c.start()


def _swiglu_tile(h, w_gu, w_d, acc, fetch=None, first_chunks_started=None):
    if fetch is not None:
        stg_gu, stg_d = fetch[3], fetch[4]

        def copies(f):
            return _chunk_copies(fetch, f)

        def start(f):
            for c in copies(f):
                c.start()

        def land(f):
            for c in copies(f):
                c.wait()
            s = f % 2
            w_gu[f, :, :FC] = stg_gu[s, 0].astype(BF16)
            w_gu[f, :, FC:] = stg_gu[s, 1].astype(BF16)
            w_d[f * FC:(f + 1) * FC, :] = stg_d[s].astype(BF16)
    else:
        start = land = lambda f: None

    def up(f):
        return jnp.dot(h, w_gu[f], preferred_element_type=F32)

    if fetch is not None:
        if first_chunks_started is None:
            _start_first_chunks(fetch)
        else:
            pl.when(jnp.logical_not(first_chunks_started))(lambda: _start_first_chunks(fetch))
    land(0)
    ab = up(0)
    for f in range(N_FC):
        if f + 2 < N_FC:
            start(f + 2)
        if f + 1 < N_FC:
            land(f + 1)
            ab_next = up(f + 1)
        t = (_silu(ab[:, :FC]) * ab[:, FC:]).astype(BF16)
        contrib = jnp.dot(t, w_d[f * FC:(f + 1) * FC, :], preferred_element_type=F32)
        if f == 0:
            acc[...] = contrib
        else:
            acc[...] += contrib
        if f + 1 < N_FC:
            ab = ab_next


def _ffn_weight_scratch():
    return [
        pltpu.VMEM((N_FC, D_MODEL, 2 * FC), BF16),
        pltpu.VMEM((D_FF, D_MODEL), BF16),
        pltpu.VMEM((2, 2, D_MODEL, FC), F32),
        pltpu.VMEM((2, FC, D_MODEL), F32),
        pltpu.SemaphoreType.DMA((2, 3)),
    ]


def _ffn_kernel(h_ref, res_ref, wg_ref, wu_ref, wd_ref, mod_ref, o_ref,
                acc, w_gu, w_d, stg_gu, stg_d, sem, *, w_idx):
    first = pl.program_id(0) == 0

    @pl.when(first)
    def _():
        _swiglu_tile(h_ref[...], w_gu, w_d, acc,
                     (wg_ref.at[w_idx], wu_ref.at[w_idx], wd_ref.at[w_idx], stg_gu, stg_d, sem))

    @pl.when(jnp.logical_not(first))
    def _():
        _swiglu_tile(h_ref[...], w_gu, w_d, acc)

    o_ref[...] = res_ref[...] + mod_ref[0, 0, 5:6, :] * acc[...]


def _ffn(h2, res, wg, wu, wd, mods, layer, w_idx):
    return pl.pallas_call(
        functools.partial(_ffn_kernel, w_idx=w_idx),
        grid=(NTOK // TM_F,),
        in_specs=[
            pl.BlockSpec((TM_F, D_MODEL), lambda i: (i, 0)),
            pl.BlockSpec((TM_F, D_MODEL), lambda i: (i, 0)),
            pl.BlockSpec(memory_space=pl.ANY),
            pl.BlockSpec(memory_space=pl.ANY),
            pl.BlockSpec(memory_space=pl.ANY),
            pl.BlockSpec((1, 1, 6, D_MODEL), lambda i: (layer, _group_of_tile(i, TM_F), 0, 0)),
        ],
        out_specs=pl.BlockSpec((TM_F, D_MODEL), lambda i: (i, 0)),
        out_shape=jax.ShapeDtypeStruct((NTOK, D_MODEL), F32),
        scratch_shapes=[pltpu.VMEM((TM_F, D_MODEL), F32)] + _ffn_weight_scratch(),
        compiler_params=pltpu.CompilerParams(
            dimension_semantics=("arbitrary",), vmem_limit_bytes=VMEM_LIMIT),
        name="ffn",
    )(h2, res, wg, wu, wd, mods)


def _tile_plan(counts):
    nt = (counts + TR - 1) // TR
    cum = jnp.cumsum(nt)
    total = cum[-1]
    t = jnp.arange(MAX_TILES, dtype=jnp.int32)
    tt = jnp.minimum(t, total - 1)
    e = jnp.sum((cum[None, :] <= tt[:, None]).astype(jnp.int32), axis=1)
    k = tt - (cum - nt)[e]
    n = jnp.where(t < total, jnp.clip(counts[e] - k * TR, 0, TR), 0)
    return e.astype(jnp.int32), (e * REG_TILES + k).astype(jnp.int32), n.astype(jnp.int32)


def _row_copy(src, src_row, dst, dst_row, sem):
    return pltpu.make_async_copy(src.at[pl.ds(src_row, 1)], dst.at[pl.ds(dst_row, 1)], sem)


def _dispatch_kernel(slot_ref, h_ref, xs_ref, sem):
    base = pl.program_id(0) * (TOP_K * TD)

    def issue(r, carry):
        for k in range(TOP_K):
            _row_copy(h_ref, r, xs_ref, slot_ref[base + TOP_K * r + k], sem).start(priority=k % 2)
        return carry

    lax.fori_loop(0, TD, issue, 0, unroll=8)
    for k in range(TOP_K):
        pltpu.make_async_copy(h_ref, xs_ref.at[pl.ds(0, TD)], sem).wait()


def _dispatch(slots, h2f):
    return pl.pallas_call(
        _dispatch_kernel,
        grid_spec=pltpu.PrefetchScalarGridSpec(
            num_scalar_prefetch=1,
            grid=(NTOK // TD,),
            in_specs=[pl.BlockSpec((TD, D_MODEL), lambda i, s: (i, 0))],
            out_specs=pl.BlockSpec(memory_space=pl.ANY),
            scratch_shapes=[pltpu.SemaphoreType.DMA],
        ),
        out_shape=jax.ShapeDtypeStruct((N_EXPERTS * REG, D_MODEL), F32),
        compiler_params=pltpu.CompilerParams(
            dimension_semantics=("arbitrary",), vmem_limit_bytes=VMEM_LIMIT),
        name="moe_dispatch",
    )(slots, h2f)


def _gffn_kernel(te_ref, tb_ref, tn_ref, x_ref, wg_ref, wu_ref, wd_ref, o_ref,
                 acc, w_gu, w_d, stg_gu, stg_d, sem, *, w_base):
    t = pl.program_id(0)
    n = tn_ref[t]

    @pl.when(n > 0)
    def _():
        row = lax.broadcasted_iota(jnp.int32, (TR, D_MODEL), 0)
        h = jnp.where(row < n, x_ref[...], 0.0).astype(BF16)
        def is_first(tt):
            return tb_ref[tt] % REG_TILES == 0

        def fetch_of(tt):
            e = w_base + te_ref[tt]
            return (wg_ref.at[e], wu_ref.at[e], wd_ref.at[e], stg_gu, stg_d, sem)

        first = is_first(t)
        t_prev = jnp.maximum(t - 1, 0)
        t_next = jnp.minimum(t + 1, MAX_TILES - 1)

        @pl.when(first)
        def _():
            started = (t > 0) & jnp.logical_not(is_first(t_prev))
            _swiglu_tile(h, w_gu, w_d, acc, fetch_of(t), started)

        @pl.when(jnp.logical_not(first))
        def _():
            @pl.when((t + 1 < MAX_TILES) & (tn_ref[t_next] > 0) & is_first(t_next))
            def _():
                _start_first_chunks(fetch_of(t_next))

            _swiglu_tile(h, w_gu, w_d, acc)

        o_ref[...] = acc[...]


def _gffn(tile_e, tile_blk, tile_n, xs, wg, wu, wd, w_base):
    return pl.pallas_call(
        functools.partial(_gffn_kernel, w_base=w_base),
        grid_spec=pltpu.PrefetchScalarGridSpec(
            num_scalar_prefetch=3,
            grid=(MAX_TILES,),
            in_specs=[
                pl.BlockSpec((TR, D_MODEL), lambda t, te, tb, tn: (tb[t], 0)),
                pl.BlockSpec(memory_space=pl.ANY),
                pl.BlockSpec(memory_space=pl.ANY),
                pl.BlockSpec(memory_space=pl.ANY),
            ],
            out_specs=pl.BlockSpec((TR, D_MODEL), lambda t, te, tb, tn: (tb[t], 0)),
            scratch_shapes=[pltpu.VMEM((TR, D_MODEL), F32)] + _ffn_weight_scratch(),
        ),
        out_shape=jax.ShapeDtypeStruct((N_EXPERTS * REG, D_MODEL), F32),
        compiler_params=pltpu.CompilerParams(
            dimension_semantics=("arbitrary",), vmem_limit_bytes=VMEM_LIMIT),
        name="moe_ffn",
    )(tile_e, tile_blk, tile_n, xs, wg, wu, wd)


def _combine_kernel(slot_ref, x1_ref, rinfo_ref, mod_ref, ys_ref, *rest, final_norm):
    if final_norm:
        gf_ref, yp_ref, yl_ref, buf, sem = rest
    else:
        o_ref, buf, sem = rest
    i = pl.program_id(0)
    n_steps = pl.num_programs(0)

    def gather(step, s):
        base = step * (TOP_K * TD)

        def issue(r, carry):
            for k in range(TOP_K):
                _row_copy(ys_ref, slot_ref[base + TOP_K * r + k], buf.at[s, k], r, sem.at[s]).start(priority=k % 2)
            return carry

        lax.fori_loop(0, TD, issue, 0, unroll=8)

    def wait(s):
        for k in range(TOP_K):
            pltpu.make_async_copy(ys_ref.at[pl.ds(0, TD)], buf.at[s, k], sem.at[s]).wait()

    @pl.when(i == 0)
    def _():
        gather(0, 0)

    for s in range(2):
        @pl.when(i % 2 == s)
        def _():
            @pl.when(i + 1 < n_steps)
            def _():
                gather(i + 1, 1 - s)

            wait(s)

    b = buf.at[i % 2]
    y = rinfo_ref[:, R_W1:R_W1 + 1] * b[0] + rinfo_ref[:, R_W2:R_W2 + 1] * b[1]
    x = x1_ref[...] + mod_ref[0, 0, 5:6, :] * y
    if final_norm:
        out = _rmsnorm(x, gf_ref[...])
        is_lat = pl.program_id(0) >= N_CTX // TD

        @pl.when(jnp.logical_not(is_lat))
        def _():
            yp_ref[...] = out

        @pl.when(is_lat)
        def _():
            yl_ref[...] = out
    else:
        o_ref[...] = x


def _combine(slots, x1, rinfo, mods, ys, layer, norm_f=None):
    final_norm = norm_f is not None
    n_ctx_t = N_CTX // TD
    in_specs = [
        pl.BlockSpec((TD, D_MODEL), lambda i, s: (i, 0)),
        pl.BlockSpec((TD, LANES), lambda i, s: (i, 0)),
        pl.BlockSpec((1, 1, 6, D_MODEL), lambda i, s: (layer, _group_of_tile(i, TD), 0, 0)),
        pl.BlockSpec(memory_space=pl.ANY),
    ]
    args = [slots, x1, rinfo, mods, ys]
    if final_norm:
        in_specs.append(pl.BlockSpec((1, D_MODEL), lambda i, s: (0, 0)))
        args.append(norm_f)
        out_specs = [pl.BlockSpec((TD, D_MODEL), lambda i, s: (jnp.minimum(i, n_ctx_t - 1), 0)),
                     pl.BlockSpec((TD, D_MODEL), lambda i, s: (jnp.maximum(i - n_ctx_t, 0), 0))]
        out_shape = [jax.ShapeDtypeStruct((N_CTX, D_MODEL), F32), jax.ShapeDtypeStruct((N_LAT, D_MODEL), F32)]
    else:
        out_specs = pl.BlockSpec((TD, D_MODEL), lambda i, s: (i, 0))
        out_shape = jax.ShapeDtypeStruct((NTOK, D_MODEL), F32)
    return pl.pallas_call(
        functools.partial(_combine_kernel, final_norm=final_norm),
        grid_spec=pltpu.PrefetchScalarGridSpec(
            num_scalar_prefetch=1,
            grid=(NTOK // TD,),
            in_specs=in_specs,
            out_specs=out_specs,
            scratch_shapes=[pltpu.VMEM((2, TOP_K, TD, D_MODEL), F32), pltpu.SemaphoreType.DMA((2,))],
        ),
        out_shape=out_shape,
        compiler_params=pltpu.CompilerParams(
            dimension_semantics=("arbitrary",), vmem_limit_bytes=VMEM_LIMIT),
        name="moe_combine",
    )(*args)


def _final_kernel(x_ref, g_ref, o_ref):
    o_ref[...] = _rmsnorm(x_ref[...], g_ref[...])


def _final(x, g, row_off, rows):
    off = row_off // TM
    return pl.pallas_call(
        _final_kernel,
        grid=(rows // TM,),
        in_specs=[
            pl.BlockSpec((TM, D_MODEL), lambda i: (off + i, 0)),
            pl.BlockSpec((1, D_MODEL), lambda i: (0, 0)),
        ],
        out_specs=pl.BlockSpec((TM, D_MODEL), lambda i: (i, 0)),
        out_shape=jax.ShapeDtypeStruct((rows, D_MODEL), F32),
        compiler_params=pltpu.CompilerParams(
            dimension_semantics=("arbitrary",), vmem_limit_bytes=VMEM_LIMIT),
        name="final_norm",
    )(x, g)


def kernel(x_prompt, x_sample, state_mlstm_C, state_mlstm_n, state_mlstm_m, state_ret_S, c, c_ctx,
           norm1_g, norm2_g, norm_f_g, w_ada, b_ada, w_in, b_gates, ret_decay_logit,
           mlstm_norm_g, ret_norm_g, w_out, ffn_w_gate, ffn_w_up, ffn_w_down,
           moe_w_router, moe_w_gate, moe_w_up, moe_w_down):
    xs_in = (x_prompt.reshape(N_CTX, D_MODEL), x_sample.reshape(N_LAT, D_MODEL))
    cvec = jnp.concatenate(
        [c_ctx[None, :], c, jnp.zeros((N_GROUPS - 1 - DEC_BATCH, D_MODEL), F32)], 0)
    mods = _ada(cvec, w_ada, b_ada).reshape(DEPTH, N_GROUPS, 6, D_MODEL)

    n_m = 4 * W_M
    w_in_t = jnp.swapaxes(w_in, 1, 2)
    n_if = N_GATES // 2
    lane_pad = ((0, 0), (0, 0), (0, LANES - n_if))
    wg = jnp.concatenate([jnp.pad(w_in[:, :, n_m:n_m + n_if], lane_pad),
                          jnp.pad(w_in[:, :, n_m + n_if:n_m + N_GATES], lane_pad)], -1)
    bg = jnp.concatenate([jnp.pad(b_gates[:, None, :n_if], lane_pad),
                          jnp.pad(b_gates[:, None, n_if:], lane_pad)], -1)
    cos_np, sin_np = _rope_tables()
    cos_t, sin_t = jnp.asarray(cos_np), jnp.asarray(sin_np)
    dl = jnp.broadcast_to(ret_decay_logit.reshape(DEPTH, 2 * H_R, 1), (DEPTH, 2 * H_R, LANES))
    m0 = jnp.pad(state_mlstm_m.reshape(DEC_BATCH, DEPTH, 1, 2 * H_M),
                 ((0, 0), (0, 0), (0, 0), (0, LANES - 2 * H_M)))
    g1 = norm1_g.reshape(DEPTH, 1, D_MODEL)
    g2 = norm2_g.reshape(DEPTH, 1, D_MODEL)
    nm = mlstm_norm_g.reshape(DEPTH, 1, W_M)
    nr = ret_norm_g.reshape(DEPTH, 1, W_R)
    n_moe = moe_w_router.shape[0]
    wr_pad = jnp.pad(moe_w_router, ((0, 0), (0, 0), (0, LANES - N_EXPERTS)))
    moe_g = moe_w_gate.reshape(n_moe * N_EXPERTS, D_MODEL, D_FF)
    moe_u = moe_w_up.reshape(n_moe * N_EXPERTS, D_MODEL, D_FF)
    moe_d = moe_w_down.reshape(n_moe * N_EXPERTS, D_FF, D_MODEL)

    states = ()
    xs = xs_in
    for l in range(DEPTH):
        jl = l // 2
        p, gates = _proj(xs, g1, mods, w_in_t, wg, bg, cos_t, sin_t, l)
        y, *states = _scan_ctx(p, gates, dl, nm, nr, l, states)
        y = _scan_lat(p, gates, dl, nm, nr, state_mlstm_C, state_mlstm_n, m0, state_ret_S, y, l)
        if l % 2 == 0:
            x1, h2 = _out(y, xs, w_out, g2, mods, l)
            x = _ffn(h2, x1, ffn_w_gate, ffn_w_up, ffn_w_down, mods, l, jl)
        else:
            x1, h2f, rinfo, cnt = _out(y, xs, w_out, g2, mods, l, wr_pad, jl)
            slots = rinfo[:, R_S1:R_S2 + 1].astype(jnp.int32).reshape(TOP_K * NTOK)
            tile_e, tile_blk, tile_n = _tile_plan(cnt[0, :N_EXPERTS].astype(jnp.int32))
            xd = _dispatch(slots, h2f)
            yd = _gffn(tile_e, tile_blk, tile_n, xd, moe_g, moe_u, moe_d, jl * N_EXPERTS)
            if l == DEPTH - 1:
                y_ctx, y_lat = _combine(slots, x1, rinfo, mods, yd, l, norm_f_g.reshape(1, D_MODEL))
            else:
                x = _combine(slots, x1, rinfo, mods, yd, l)
        xs = (x,)

    if DEPTH % 2 == 1:
        y_ctx = _final(x, norm_f_g.reshape(1, D_MODEL), 0, N_CTX)
        y_lat = _final(x, norm_f_g.reshape(1, D_MODEL), N_CTX, N_LAT)
    y_prompt = y_ctx.reshape(BATCH, SEQ, D_MODEL)
    y_sample = y_lat.reshape(DEC_BATCH, DEC_SEQ, D_MODEL)
    new_C, new_n, new_m, new_S = states
    return (y_prompt, y_sample, new_C, new_n,
            new_m[:, :, 0, :2 * H_M].reshape(BATCH, DEPTH, 2, H_M), new_S)
```

```python
import functools

import numpy as np
import jax
import jax.numpy as jnp
from jax import lax
from jax.experimental import pallas as pl
from jax.experimental.pallas import tpu as pltpu

D_MODEL = 1024
BATCH = 32
SEQ = 256
DEPTH = 2
DEC_BATCH = 2
DEC_SEQ = 1024
GRID_W = 64
H_M = 4
DH = 128
H_R = 4
W_M = H_M * DH
W_R = H_R * DH
N_GATES = 4 * H_M
CHUNK = 128
D_FF = 2816
N_EXPERTS = 8
ROPE_BASE = 10000.0
EPS = 1e-6

N_CTX = BATCH * SEQ
N_LAT = DEC_BATCH * DEC_SEQ
NTOK = N_CTX + N_LAT
N_GROUPS = 8
K_SCALE = DH ** -0.5
P_COLS = 4 * W_M + 4 * W_R
LANES = 128
GATE_LANES = 2 * LANES
VMEM_LIMIT = 56 * 1024 * 1024

F32 = jnp.float32
BF16 = jnp.bfloat16
HIGHEST = lax.Precision.HIGHEST

TM = 1024
TN = 1024
FC = 256
TM_F = 512
TOP_K = 2
TR = 896
REG_TILES = -(-NTOK // TR)
REG = REG_TILES * TR
MAX_TILES = -(-TOP_K * NTOK // TR) + N_EXPERTS
TD = 512


def _group_of_tile(i, tm):
    return jnp.maximum(i * tm // DEC_SEQ - (N_CTX // DEC_SEQ - 1), 0)


def _silu(x):
    return x * jax.nn.sigmoid(x)


def _log_sigmoid(x):
    return jnp.minimum(x, 0.0) - jnp.log(1.0 + jnp.exp(-jnp.abs(x)))


def _rmsnorm(x, g):
    return x * lax.rsqrt(jnp.mean(x * x, -1, keepdims=True) + EPS) * g


def _ada_kernel(cv_ref, w_ref, b_ref, o_ref):
    s = _silu(cv_ref[...]).astype(BF16)
    o_ref[0] = jnp.dot(s, w_ref[0].astype(BF16), preferred_element_type=F32) + b_ref[0]


def _ada(cvec, w_ada, b_ada):
    tn = 1536
    n = 6 * D_MODEL
    return pl.pallas_call(
        _ada_kernel,
        grid=(DEPTH, n // tn),
        in_specs=[
            pl.BlockSpec((N_GROUPS, D_MODEL), lambda l, j: (0, 0)),
            pl.BlockSpec((1, D_MODEL, tn), lambda l, j: (l, 0, j)),
            pl.BlockSpec((1, 1, tn), lambda l, j: (l, 0, j)),
        ],
        out_specs=pl.BlockSpec((1, N_GROUPS, tn), lambda l, j: (l, 0, j)),
        out_shape=jax.ShapeDtypeStruct((DEPTH, N_GROUPS, n), F32),
        compiler_params=pltpu.CompilerParams(
            dimension_semantics=("arbitrary", "arbitrary"), vmem_limit_bytes=VMEM_LIMIT),
        name="ada",
    )(cvec, w_ada, b_ada.reshape(DEPTH, 1, n))


def _rope_tables():
    half = DH // 4
    freqs = ROPE_BASE ** (-np.arange(half, dtype=np.float64) / half)
    t = np.arange(DEC_SEQ)
    pos = np.stack([t // GRID_W, t % GRID_W], 1).astype(np.float64)
    d = np.arange(DH)
    ang = pos[:, d // (DH // 2)] * freqs[d % half][None, :]
    sign = np.where((d % (DH // 2)) < half, -1.0, 1.0)[None, :]
    return np.cos(ang).astype(np.float32), (sign * np.sin(ang)).astype(np.float32)


def _rope(a, cos, sin):
    lane = lax.broadcasted_iota(jnp.int32, a.shape, 1)
    first = (lane % (DH // 2)) < (DH // 4)
    partner = jnp.where(first, pltpu.roll(a, DH - DH // 4, 1), pltpu.roll(a, DH // 4, 1))
    return a * cos + partner * sin


def _proj_kernel(*refs, n_ctx_tiles, split_x, layer):
    if split_x:
        (xp_ref, xl_ref, g_ref, mod_ref, wt_ref, wg_ref, bg_ref, cos_ref, sin_ref,
         p_ref, gate_ref, h_scr, w_res, w_stg, w_sem) = refs
    else:
        (x_ref, g_ref, mod_ref, wt_ref, wg_ref, bg_ref, cos_ref, sin_ref,
         p_ref, gate_ref, h_scr, w_res, w_stg, w_sem) = refs
    p_ref = p_ref.at[0]
    i = pl.program_id(0)
    j = pl.program_id(1)
    is_lat = i >= n_ctx_tiles
    half = TN // 2

    def prologue(x):
        h = _rmsnorm(x, g_ref[0]) * (1.0 + mod_ref[0, 0, 1:2, :]) + mod_ref[0, 0, 0:1, :]
        h_scr[...] = h.astype(BF16)
        gate_ref[...] = _dot_f32x3(h, wg_ref[0]) + bg_ref[0]

    @pl.when(j == 0)
    def _():
        if split_x:
            pl.when(jnp.logical_not(is_lat))(lambda: prologue(xp_ref[...]))
            pl.when(is_lat)(lambda: prologue(xl_ref[...]))
        else:
            prologue(x_ref[...])

    n_col_tiles = P_COLS // TN

    def tile_copy(jj):
        row0 = jj * TN + (N_GATES if jj * TN >= 4 * W_M else 0)
        s = jj % 2
        return pltpu.make_async_copy(wt_ref.at[layer, pl.ds(row0, TN), :], w_stg.at[s], w_sem.at[s])

    def matmul(jj):
        @pl.when(i == 0)
        def _():
            if jj == 0:
                tile_copy(0).start()
            if jj + 1 < n_col_tiles:
                tile_copy(jj + 1).start()
            tile_copy(jj).wait()
            w_res[jj] = w_stg[jj % 2].T.astype(BF16)

        return _dot(h_scr[...], w_res[jj])

    @pl.when(j == 0)
    def _():
        acc = matmul(0)
        p_ref[:, :half] = acc[:, :half].astype(BF16)
        p_ref[:, half:] = (acc[:, half:] * K_SCALE).astype(BF16)

    @pl.when(j == 1)
    def _():
        p_ref[...] = matmul(1).astype(BF16)

    @pl.when(j == 2)
    def _():
        acc = matmul(2)

        @pl.when(is_lat)
        def _():
            cos = cos_ref[...]
            sin = sin_ref[...]
            for hd in range(TN // DH):
                sl = slice(hd * DH, (hd + 1) * DH)
                r = _rope(acc[:, sl], cos, sin)
                p_ref[:, sl] = (r * K_SCALE if hd * DH >= half else r).astype(BF16)

        @pl.when(jnp.logical_not(is_lat))
        def _():
            p_ref[:, :half] = acc[:, :half].astype(BF16)
            p_ref[:, half:] = (acc[:, half:] * K_SCALE).astype(BF16)

    @pl.when(j == 3)
    def _():
        p_ref[...] = matmul(3).astype(BF16)


def _proj(xs, g1, mods, w_in_t, wg, bg, cos_t, sin_t, layer):
    n_ctx_tiles = N_CTX // TM
    tiles_per_seq = DEC_SEQ // TM
    split_x = len(xs) == 2
    if split_x:
        x_specs = [pl.BlockSpec((TM, D_MODEL), lambda i, j: (jnp.minimum(i, n_ctx_tiles - 1), 0)),
                   pl.BlockSpec((TM, D_MODEL), lambda i, j: (jnp.maximum(i - n_ctx_tiles, 0), 0))]
    else:
        x_specs = [pl.BlockSpec((TM, D_MODEL), lambda i, j: (i, 0))]
    return pl.pallas_call(
        functools.partial(_proj_kernel, n_ctx_tiles=n_ctx_tiles, split_x=split_x, layer=layer),
        grid=(NTOK // TM, P_COLS // TN),
        in_specs=x_specs + [
            pl.BlockSpec((1, 1, D_MODEL), lambda i, j: (layer, 0, 0)),
            pl.BlockSpec((1, 1, 6, D_MODEL), lambda i, j: (layer, _group_of_tile(i, TM), 0, 0)),
            pl.BlockSpec(memory_space=pl.ANY),
            pl.BlockSpec((1, D_MODEL, GATE_LANES), lambda i, j: (layer, 0, 0)),
            pl.BlockSpec((1, 1, GATE_LANES), lambda i, j: (layer, 0, 0)),
            pl.BlockSpec((TM, DH), lambda i, j: (i % tiles_per_seq, 0)),
            pl.BlockSpec((TM, DH), lambda i, j: (i % tiles_per_seq, 0)),
        ],
        out_specs=[
            pl.BlockSpec((1, TM, TN), lambda i, j: (j, i, 0)),
            pl.BlockSpec((TM, GATE_LANES), lambda i, j: (i, 0)),
        ],
        out_shape=[
            jax.ShapeDtypeStruct((P_COLS // TN, NTOK, TN), BF16),
            jax.ShapeDtypeStruct((NTOK, GATE_LANES), F32),
        ],
        scratch_shapes=[pltpu.VMEM((TM, D_MODEL), BF16),
                        pltpu.VMEM((P_COLS // TN, D_MODEL, TN), BF16),
                        pltpu.VMEM((2, TN, D_MODEL), F32),
                        pltpu.SemaphoreType.DMA((2,))],
        compiler_params=pltpu.CompilerParams(
            dimension_semantics=("arbitrary", "arbitrary"), vmem_limit_bytes=VMEM_LIMIT),
        name="proj",
    )(*xs, g1, mods, w_in_t, wg, bg, cos_t, sin_t)


def _split3(x):
    hi = x.astype(BF16)
    r1 = x - hi.astype(F32)
    mid = r1.astype(BF16)
    lo = (r1 - mid.astype(F32)).astype(BF16)
    return hi, mid, lo


def _dot(a, b):
    return jnp.dot(a, b, preferred_element_type=F32)


def _dot_nt(a, b):
    return lax.dot_general(a, b, (((1,), (1,)), ((), ())), preferred_element_type=F32)


def _tri_dot_left(tri, x):
    hi, mid, lo = _split3(x)
    return _dot(tri, hi) + _dot(tri, mid) + _dot(tri, lo)


def _tri_dot_right(x, tri):
    hi, mid, lo = _split3(x)
    return _dot(hi, tri) + _dot(mid, tri) + _dot(lo, tri)


def _run_max(x, reverse):
    n_tiles = x.shape[0] // 8
    sub = lax.broadcasted_iota(jnp.int32, (8, LANES), 0)
    out = [None] * n_tiles
    carry = None
    for t in (range(n_tiles - 1, -1, -1) if reverse else range(n_tiles)):
        v = x[8 * t:8 * t + 8, :]
        for s in (1, 2, 4):
            if reverse:
                v = jnp.maximum(v, jnp.where(sub < 8 - s, pltpu.roll(v, 8 - s, 0), -jnp.inf))
            else:
                v = jnp.maximum(v, jnp.where(sub >= s, pltpu.roll(v, s, 0), -jnp.inf))
        if carry is not None:
            v = jnp.maximum(v, carry)
        carry = jnp.broadcast_to(v[0:1, :] if reverse else v[7:8, :], (8, LANES))
        out[t] = v
    return jnp.concatenate(out, axis=0)


def _scan_kernel(*refs, T, has_state, n_prev=0):
    if has_state:
        (p_ref, g_ref, dl_ref, nm_ref, nr_ref, C0_ref, n0_ref, m0_ref, S0_ref, _yprev_ref,
         y_ref, CN_s, S_s, m_s, hf_s, hb_s, dm_s, dq_s, dk_s, dL_s, kT_s) = refs
    else:
        p_ref, g_ref, dl_ref, nm_ref, nr_ref = refs[:5]
        prev_refs = refs[5:9] if n_prev else ()
        (y_ref, C_out, n_out, m_out, S_out,
         CN_s, S_s, m_s, hf_s, hb_s, dm_s, dq_s, dk_s, dL_s, kT_s) = refs[5 + len(prev_refs):]
    L = CHUNK
    n_chunks = T // L
    row_i = lax.broadcasted_iota(jnp.int32, (L, L), 0)
    col_j = lax.broadcasted_iota(jnp.int32, (L, L), 1)
    lower = col_j <= row_i
    upper = col_j >= row_i
    tril = lower.astype(BF16)
    triu = upper.astype(BF16)
    ones = jnp.ones((L, DH), BF16)
    c_km = W_M
    c_vm = 2 * W_M
    c_om = 3 * W_M
    c_qr = 4 * W_M
    c_kr = c_qr + W_R
    c_vr = c_qr + 2 * W_R
    c_gr = c_qr + 3 * W_R

    def pcols(rows, col):
        return p_ref[col // TN, rows, col % TN:col % TN + DH]

    for d in range(2):
        for h in range(H_M):
            k = d * H_M + h
            if has_state:
                CN_s[k, :, :DH] = C0_ref[0, 0, d, h]
                CN_s[k, :, DH:] = jnp.broadcast_to(n0_ref[0, 0, d, h:h + 1, :], (DH, DH)).T
                S_s[k] = S0_ref[0, 0, d, h]
            else:
                CN_s[k] = jnp.zeros((DH, 2 * DH), F32)
                S_s[k] = jnp.zeros((DH, DH), F32)
    m_s[...] = m0_ref[0, 0] if has_state else jnp.zeros((1, LANES), F32)

    @pl.when(pl.program_id(0) == 0)
    def _():
        pos_i = row_i.astype(F32)
        pos_j = col_j.astype(F32)
        for d in range(2):
            for h in range(H_R):
                k = d * H_R + h
                lg_row = _log_sigmoid(dl_ref[0, k:k + 1, :])
                lg = jnp.broadcast_to(lg_row, (L, L))
                rel = (row_i - col_j if d == 0 else col_j - row_i).astype(F32)
                dm_s[k] = jnp.where(rel >= 0, jnp.exp(lg * jnp.maximum(rel, 0.0)), 0.0)
                dq_s[k] = jnp.exp(lg * (pos_i + 1.0 if d == 0 else L - pos_i))
                dk_s[k] = jnp.exp(lg * (L - 1.0 - pos_j if d == 0 else pos_j))
                dL_s[k] = jnp.exp(lg_row * float(L))

    def transpose_keys(c, carry):
        r0 = pl.multiple_of(c * L, L)
        for h in range(H_M):
            kT_s[h, c] = pcols(pl.ds(r0, L), c_km + h * DH).astype(F32).T
            kT_s[H_M + h, c] = pcols(pl.ds(r0, L), c_kr + h * DH).astype(F32).T
        return carry

    lax.fori_loop(0, n_chunks, transpose_keys, 0)

    def chunk_step(c, carry):
        m_prev = m_s[...]
        m_new = []
        prep = []
        for d in range(2):
            ci = c if d == 0 else n_chunks - 1 - c
            r0 = pl.multiple_of(ci * L, L)
            mask = lower if d == 0 else upper
            e_row = L - 1 if d == 0 else 0
            FL = _log_sigmoid(g_ref[pl.ds(r0, L), LANES:2 * LANES])
            Bc = _tri_dot_left(tril if d == 0 else triu, FL)
            Zc = g_ref[pl.ds(r0, L), 0:LANES] - Bc
            M = jnp.maximum(_run_max(Zc, reverse=(d == 1)), m_prev)
            m_row = Bc + M
            M_end = M[e_row:e_row + 1, :]
            m_new.append(Bc[e_row:e_row + 1, :] + M_end)
            decay = jnp.exp(m_prev - M_end)
            prep.append(dict(ci=ci, r0=r0, mask=mask, M=M, m_row=m_row, decay=decay,
                             ZT=Zc.T,
                             WT=jnp.exp(Zc - M_end).T))
        pairs = [(d, h) for d in range(2) for h in range(H_M)]

        def rows(d, col):
            return pcols(pl.ds(prep[d]["r0"], L), col)

        qk, qkr = {}, {}
        for d, h in pairs:
            qk[d, h] = _dot_nt(rows(d, h * DH), rows(d, c_km + h * DH))
            qkr[d, h] = _dot_nt(rows(d, c_qr + h * DH), rows(d, c_kr + h * DH))
        upd, updr = {}, {}
        for d, h in pairs:
            k = d * H_M + h
            ci = prep[d]["ci"]
            vo = jnp.concatenate([rows(d, c_vm + h * DH), ones], axis=1)
            wkT = (kT_s[h, ci] * jnp.broadcast_to(prep[d]["WT"][k:k + 1, :], (DH, L))).astype(BF16)
            upd[d, h] = _dot(wkT, vo)
            kdT = (kT_s[H_M + h, ci] * dk_s[k]).astype(BF16)
            updr[d, h] = _dot(kdT, rows(d, c_vr + h * DH))
        for d, h in pairs:
            k = d * H_M + h
            r0 = prep[d]["r0"]
            h_dst = hf_s if d == 0 else hb_s
            q = rows(d, h * DH)
            M_col = jnp.broadcast_to(prep[d]["M"][:, k:k + 1], (L, L))
            z_row = jnp.broadcast_to(prep[d]["ZT"][k:k + 1, :], (L, L))
            D = jnp.where(prep[d]["mask"], jnp.exp(z_row - M_col), 0.0)
            s = (qk[d, h] * D).astype(BF16)
            w_inter = jnp.exp(jnp.broadcast_to(m_prev[:, k:k + 1], (L, L)) - M_col)
            wq = (w_inter * q.astype(F32)).astype(BF16)
            vo = jnp.concatenate([rows(d, c_vm + h * DH), ones], axis=1)
            CN = CN_s[k]
            res = _dot(jnp.concatenate([s, wq], axis=1),
                       jnp.concatenate([vo, CN.astype(BF16)], axis=0))
            floor = jnp.exp(-jnp.broadcast_to(prep[d]["m_row"][:, k:k + 1], (L, L)))
            h_dst[pl.ds(r0, L), h * DH:(h + 1) * DH] = res[:, :DH] / jnp.maximum(jnp.abs(res[:, DH:]), floor)
            CN_s[k] = jnp.broadcast_to(prep[d]["decay"][:, k:k + 1], (DH, 2 * DH)) * CN + upd[d, h]
            qr = rows(d, c_qr + h * DH)
            S = S_s[k]
            sr = (qkr[d, h] * dm_s[k]).astype(BF16)
            qd = (qr.astype(F32) * dq_s[k]).astype(BF16)
            h_dst[pl.ds(r0, L), W_M + h * DH:W_M + (h + 1) * DH] = _dot(
                jnp.concatenate([sr, qd], axis=1),
                jnp.concatenate([rows(d, c_vr + h * DH), S.astype(BF16)], axis=0))
            S_s[k] = dL_s[k] * S + updr[d, h]
        lane = lax.broadcasted_iota(jnp.int32, (1, LANES), 1)
        m_s[...] = jnp.where(lane < H_M, m_new[0], m_new[1])
        return carry

    lax.fori_loop(0, n_chunks, chunk_step, 0)

    for h in range(H_M):
        sl = slice(h * DH, (h + 1) * DH)
        hs = hf_s[:, sl] + hb_s[:, sl]
        yn = _rmsnorm(hs, nm_ref[0, :, sl])
        om = pcols(slice(None), c_om + h * DH).astype(F32)
        y_ref[:, sl] = (jax.nn.sigmoid(om) * yn).astype(BF16)
        slr = slice(W_M + h * DH, W_M + (h + 1) * DH)
        hr = hf_s[:, slr] + hb_s[:, slr]
        ynr = _rmsnorm(hr, nr_ref[0, :, sl])
        gr = pcols(slice(None), c_gr + h * DH).astype(F32)
        y_ref[:, slr] = (_silu(gr) * ynr).astype(BF16)

    if not has_state:
        for prev, out in zip(prev_refs, (C_out, n_out, m_out, S_out)):
            out[0, :n_prev] = prev[0]
        for d in range(2):
            for h in range(H_M):
                k = d * H_M + h
                C_out[0, n_prev, d, h] = CN_s[k, :, :DH]
                n_out[0, n_prev, d, h:h + 1, :] = CN_s[k, :, DH:].T[0:1, :]
                S_out[0, n_prev, d, h] = S_s[k]
        m_out[0, n_prev] = m_s[...]


def _scan_scratch(T):
    return [
        pltpu.VMEM((2 * H_M, DH, 2 * DH), F32),
        pltpu.VMEM((2 * H_R, DH, DH), F32),
        pltpu.VMEM((1, LANES), F32),
        pltpu.VMEM((T, W_M + W_R), F32),
        pltpu.VMEM((T, W_M + W_R), F32),
        pltpu.VMEM((2 * H_R, CHUNK, CHUNK), F32),
        pltpu.VMEM((2 * H_R, CHUNK, CHUNK), F32),
        pltpu.VMEM((2 * H_R, CHUNK, CHUNK), F32),
        pltpu.VMEM((2 * H_R, 1, LANES), F32),
        pltpu.VMEM((H_M + H_R, T // CHUNK, DH, CHUNK), F32),
    ]


def _scan_ctx(p, gates, dl, nm, nr, layer, prev_states=()):
    T = SEQ
    n_lay = layer + 1
    state_tails = [(2, H_M, DH, DH), (2, H_M, DH), (1, LANES), (2, H_R, DH, DH)]

    def state_spec(n, tail):
        return pl.BlockSpec((1, n) + tail, lambda b: (b,) + (0,) * (1 + len(tail)))
    common = [
        pl.BlockSpec((P_COLS // TN, T, TN), lambda b: (0, b, 0)),
        pl.BlockSpec((T, GATE_LANES), lambda b: (b, 0)),
        pl.BlockSpec((1, 2 * H_R, LANES), lambda b: (layer, 0, 0)),
        pl.BlockSpec((1, 1, W_M), lambda b: (layer, 0, 0)),
        pl.BlockSpec((1, 1, W_R), lambda b: (layer, 0, 0)),
    ]
    return pl.pallas_call(
        functools.partial(_scan_kernel, T=T, has_state=False, n_prev=layer if prev_states else 0),
        grid=(BATCH,),
        in_specs=common + [state_spec(layer, tail) for tail in state_tails[:len(prev_states)]],
        out_specs=[pl.BlockSpec((T, D_MODEL), lambda b: (b, 0))] + [state_spec(n_lay, tail) for tail in state_tails],
        out_shape=[jax.ShapeDtypeStruct((NTOK, D_MODEL), BF16)] + [
            jax.ShapeDtypeStruct((BATCH, n_lay) + tail, F32) for tail in state_tails],
        scratch_shapes=_scan_scratch(T),
        compiler_params=pltpu.CompilerParams(
            dimension_semantics=("arbitrary",), vmem_limit_bytes=VMEM_LIMIT),
        name="scan_ctx",
    )(p, gates, dl, nm, nr, *prev_states)


def _scan_lat(p, gates, dl, nm, nr, C0, n0, m0, S0, y_prev, layer):
    T = DEC_SEQ
    off = N_CTX // T
    in_specs = [
        pl.BlockSpec((P_COLS // TN, T, TN), lambda b: (0, off + b, 0)),
        pl.BlockSpec((T, GATE_LANES), lambda b: (off + b, 0)),
        pl.BlockSpec((1, 2 * H_R, LANES), lambda b: (layer, 0, 0)),
        pl.BlockSpec((1, 1, W_M), lambda b: (layer, 0, 0)),
        pl.BlockSpec((1, 1, W_R), lambda b: (layer, 0, 0)),
        pl.BlockSpec((1, 1, 2, H_M, DH, DH), lambda b: (b, layer, 0, 0, 0, 0)),
        pl.BlockSpec((1, 1, 2, H_M, DH), lambda b: (b, layer, 0, 0, 0)),
        pl.BlockSpec((1, 1, 1, LANES), lambda b: (b, layer, 0, 0)),
        pl.BlockSpec((1, 1, 2, H_R, DH, DH), lambda b: (b, layer, 0, 0, 0, 0)),
        pl.BlockSpec(memory_space=pl.ANY),
    ]
    return pl.pallas_call(
        functools.partial(_scan_kernel, T=T, has_state=True),
        grid=(DEC_BATCH,),
        in_specs=in_specs,
        out_specs=pl.BlockSpec((T, D_MODEL), lambda b: (off + b, 0)),
        out_shape=jax.ShapeDtypeStruct((NTOK, D_MODEL), BF16),
        input_output_aliases={9: 0},
        scratch_shapes=_scan_scratch(T),
        compiler_params=pltpu.CompilerParams(
            dimension_semantics=("arbitrary",), vmem_limit_bytes=VMEM_LIMIT),
        name="scan_lat",
    )(p, gates, dl, nm, nr, C0, n0, m0, S0, y_prev)


def _top2(logits):
    lane = lax.broadcasted_iota(jnp.int32, logits.shape, 1)
    v1 = jnp.max(logits, -1, keepdims=True)
    i1 = jnp.min(jnp.where(logits == v1, lane, LANES), -1, keepdims=True)
    rest = jnp.where(lane == i1, -jnp.inf, logits)
    v2 = jnp.max(rest, -1, keepdims=True)
    i2 = jnp.min(jnp.where(rest == v2, lane, LANES), -1, keepdims=True)
    e2 = jnp.exp(v2 - v1)
    return i1, i2, 1.0 / (1.0 + e2), e2 / (1.0 + e2)


def _split2(x):
    hi = x.astype(BF16)
    return hi, (x - hi.astype(F32)).astype(BF16)


def _dot_f32x3(a, b):
    a_hi, a_lo = _split2(a)
    b_hi, b_lo = _split2(b)
    return _dot(a_hi, b_hi) + _dot(a_hi, b_lo) + _dot(a_lo, b_hi)


R_E1, R_E2, R_W1, R_W2, R_S1, R_S2 = range(6)


def _out_kernel(*refs, with_router, split_x):
    y_ref = refs[0]
    if split_x:
        xp_ref, xl_ref = refs[1:3]
        x_in = jnp.where(pl.program_id(0) >= N_CTX // TM, xl_ref[...], xp_ref[...])
    else:
        x_in = refs[1][...]
    refs = refs[3:] if split_x else refs[2:]
    if with_router:
        (w_ref, g_ref, mod_ref, wr_ref,
         x1_ref, h2_ref, rinfo_ref, cnt_ref, w_scr, tri_scr, cnt_scr) = refs
    else:
        w_ref, g_ref, mod_ref, x1_ref, h2_ref, w_scr = refs

    @pl.when(pl.program_id(0) == 0)
    def _():
        w_scr[...] = w_ref[0].astype(BF16)
        if with_router:
            r = lax.broadcasted_iota(jnp.int32, (LANES, LANES), 0)
            c = lax.broadcasted_iota(jnp.int32, (LANES, LANES), 1)
            tri_scr[...] = (c < r).astype(BF16)
            cnt_scr[...] = jnp.zeros_like(cnt_scr)

    o = jnp.dot(y_ref[...], w_scr[...], preferred_element_type=F32)
    x1 = x_in + mod_ref[0, 0, 2:3, :] * o
    x1_ref[...] = x1
    h2 = _rmsnorm(x1, g_ref[0]) * (1.0 + mod_ref[0, 0, 4:5, :]) + mod_ref[0, 0, 3:4, :]
    if not with_router:
        h2_ref[...] = h2.astype(BF16)
    else:
        h2_ref[...] = h2
        h_hi, h_lo = _split2(h2)
        w_hi, w_lo = _split2(wr_ref[0])
        t = _dot(h_hi, jnp.concatenate([w_hi, w_lo], axis=1))
        logits = t[:, :LANES] + t[:, LANES:] + _dot(h_lo, w_hi)
        lane = lax.broadcasted_iota(jnp.int32, logits.shape, 1)
        i1, i2, w1, w2 = _top2(jnp.where(lane < N_EXPERTS, logits, -jnp.inf))
        oh1 = lane == i1
        oh2 = lane == i2
        sel = jnp.where(oh1 | oh2, 1.0, 0.0)
        tri = tri_scr[...]
        run = cnt_scr[...]
        ranks = []
        for blk in range(TM // LANES):
            s_blk = sel[blk * LANES:(blk + 1) * LANES, :]
            ranks.append(_dot(tri, s_blk.astype(BF16)) + run)
            run = run + jnp.sum(s_blk, 0, keepdims=True)
        rank = jnp.concatenate(ranks, axis=0)
        r1 = jnp.sum(jnp.where(oh1, rank, 0.0), -1, keepdims=True)
        r2 = jnp.sum(jnp.where(oh2, rank, 0.0), -1, keepdims=True)
        s1 = i1.astype(F32) * float(REG) + r1
        s2 = i2.astype(F32) * float(REG) + r2
        info = jnp.zeros(logits.shape, F32)
        for col, val in ((R_E1, i1.astype(F32)), (R_E2, i2.astype(F32)), (R_W1, w1), (R_W2, w2),
                         (R_S1, s1), (R_S2, s2)):
            info = jnp.where(lane == col, val, info)
        rinfo_ref[...] = info
        cnt_scr[...] = run
        cnt_ref[...] = run


def _out(y, xs, w_out, g2, mods, layer, w_router_pad=None, router_idx=0):
    with_router = w_router_pad is not None
    split_x = len(xs) == 2
    n_ctx_tiles = N_CTX // TM
    if split_x:
        x_specs = [pl.BlockSpec((TM, D_MODEL), lambda i: (jnp.minimum(i, n_ctx_tiles - 1), 0)),
                   pl.BlockSpec((TM, D_MODEL), lambda i: (jnp.maximum(i - n_ctx_tiles, 0), 0))]
    else:
        x_specs = [pl.BlockSpec((TM, D_MODEL), lambda i: (i, 0))]
    in_specs = [pl.BlockSpec((TM, D_MODEL), lambda i: (i, 0))] + x_specs + [
        pl.BlockSpec((1, D_MODEL, D_MODEL), lambda i: (layer, 0, 0)),
        pl.BlockSpec((1, 1, D_MODEL), lambda i: (layer, 0, 0)),
        pl.BlockSpec((1, 1, 6, D_MODEL), lambda i: (layer, _group_of_tile(i, TM), 0, 0)),
    ]
    out_specs = [
        pl.BlockSpec((TM, D_MODEL), lambda i: (i, 0)),
        pl.BlockSpec((TM, D_MODEL), lambda i: (i, 0)),
    ]
    out_shape = [
        jax.ShapeDtypeStruct((NTOK, D_MODEL), F32),
        jax.ShapeDtypeStruct((NTOK, D_MODEL), F32 if with_router else BF16),
    ]
    args = [y, *xs, w_out, g2, mods]
    scratch = [pltpu.VMEM((D_MODEL, D_MODEL), BF16)]
    if with_router:
        in_specs.append(pl.BlockSpec((1, D_MODEL, LANES), lambda i: (router_idx, 0, 0)))
        out_specs += [pl.BlockSpec((TM, LANES), lambda i: (i, 0)),
                      pl.BlockSpec((1, LANES), lambda i: (0, 0))]
        out_shape += [jax.ShapeDtypeStruct((NTOK, LANES), F32),
                      jax.ShapeDtypeStruct((1, LANES), F32)]
        args.append(w_router_pad)
        scratch += [pltpu.VMEM((LANES, LANES), BF16), pltpu.VMEM((1, LANES), F32)]
    return pl.pallas_call(
        functools.partial(_out_kernel, with_router=with_router, split_x=split_x),
        grid=(NTOK // TM,),
        in_specs=in_specs,
        out_specs=out_specs,
        out_shape=out_shape,
        scratch_shapes=scratch,
        compiler_params=pltpu.CompilerParams(
            dimension_semantics=("arbitrary",), vmem_limit_bytes=VMEM_LIMIT),
        name="out_router" if with_router else "out",
    )(*args)


N_FC = D_FF // FC
def _chunk_copies(fetch, f):
    wg_hbm, wu_hbm, wd_hbm, stg_gu, stg_d, sem = fetch
    s = f % 2
    cols = pl.ds(f * FC, FC)
    return (pltpu.make_async_copy(wg_hbm.at[:, cols], stg_gu.at[s, 0], sem.at[s, 0]),
            pltpu.make_async_copy(wu_hbm.at[:, cols], stg_gu.at[s, 1], sem.at[s, 1]),
            pltpu.make_async_copy(wd_hbm.at[cols, :], stg_d.at[s], sem.at[s, 2]))


def _start_first_chunks(fetch):
    for f in range(min(2, N_FC)):
        for c in _chunk_copies(fetch, f):
            c.start()


def _swiglu_tile(h, w_gu, w_d, acc, fetch=None, first_chunks_started=None):
    if fetch is not None:
        stg_gu, stg_d = fetch[3], fetch[4]

        def copies(f):
            return _chunk_copies(fetch, f)

        def start(f):
            for c in copies(f):
                c.start()

        def land(f):
            for c in copies(f):
                c.wait()
            s = f % 2
            w_gu[f, :, :FC] = stg_gu[s, 0].astype(BF16)
            w_gu[f, :, FC:] = stg_gu[s, 1].astype(BF16)
            w_d[f * FC:(f + 1) * FC, :] = stg_d[s].astype(BF16)
    else:
        start = land = lambda f: None

    def up(f):
        return jnp.dot(h, w_gu[f], preferred_element_type=F32)

    if fetch is not None:
        if first_chunks_started is None:
            _start_first_chunks(fetch)
        else:
            pl.when(jnp.logical_not(first_chunks_started))(lambda: _start_first_chunks(fetch))
    land(0)
    ab = up(0)
    for f in range(N_FC):
        if f + 2 < N_FC:
            start(f + 2)
        if f + 1 < N_FC:
            land(f + 1)
            ab_next = up(f + 1)
        t = (_silu(ab[:, :FC]) * ab[:, FC:]).astype(BF16)
        contrib = jnp.dot(t, w_d[f * FC:(f + 1) * FC, :], preferred_element_type=F32)
        if f == 0:
            acc[...] = contrib
        else:
            acc[...] += contrib
        if f + 1 < N_FC:
            ab = ab_next


def _ffn_weight_scratch():
    return [
        pltpu.VMEM((N_FC, D_MODEL, 2 * FC), BF16),
        pltpu.VMEM((D_FF, D_MODEL), BF16),
        pltpu.VMEM((2, 2, D_MODEL, FC), F32),
        pltpu.VMEM((2, FC, D_MODEL), F32),
        pltpu.SemaphoreType.DMA((2, 3)),
    ]


def _out_ffn_kernel(*refs, w_idx, split_x):
    y_ref = refs[0]
    if split_x:
        xp_ref, xl_ref = refs[1:3]
        x_in = jnp.where(pl.program_id(0) >= N_CTX // TM_F, xl_ref[...], xp_ref[...])
    else:
        x_in = refs[1][...]
    (wo_ref, g_ref, mod_ref, wg_ref, wu_ref, wd_ref, o_ref,
     wo_scr, h_scr, acc, w_gu, w_d, stg_gu, stg_d, sem) = refs[3:] if split_x else refs[2:]
    first = pl.program_id(0) == 0

    @pl.when(first)
    def _():
        wo_scr[...] = wo_ref[0].astype(BF16)

    x1 = x_in + mod_ref[0, 0, 2:3, :] * jnp.dot(y_ref[...], wo_scr[...], preferred_element_type=F32)
    h2 = _rmsnorm(x1, g_ref[0]) * (1.0 + mod_ref[0, 0, 4:5, :]) + mod_ref[0, 0, 3:4, :]
    h_scr[...] = h2.astype(BF16)
    o_ref[...] = x1

    @pl.when(first)
    def _():
        _swiglu_tile(h_scr[...], w_gu, w_d, acc,
                     (wg_ref.at[w_idx], wu_ref.at[w_idx], wd_ref.at[w_idx], stg_gu, stg_d, sem))

    @pl.when(jnp.logical_not(first))
    def _():
        _swiglu_tile(h_scr[...], w_gu, w_d, acc)

    o_ref[...] = o_ref[...] + mod_ref[0, 0, 5:6, :] * acc[...]


def _out_ffn(y, xs, w_out, g2, wg, wu, wd, mods, layer, w_idx):
    split_x = len(xs) == 2
    n_ctx_tiles = N_CTX // TM_F
    if split_x:
        x_specs = [pl.BlockSpec((TM_F, D_MODEL), lambda i: (jnp.minimum(i, n_ctx_tiles - 1), 0)),
                   pl.BlockSpec((TM_F, D_MODEL), lambda i: (jnp.maximum(i - n_ctx_tiles, 0), 0))]
    else:
        x_specs = [pl.BlockSpec((TM_F, D_MODEL), lambda i: (i, 0))]
    return pl.pallas_call(
        functools.partial(_out_ffn_kernel, w_idx=w_idx, split_x=split_x),
        grid=(NTOK // TM_F,),
        in_specs=[pl.BlockSpec((TM_F, D_MODEL), lambda i: (i, 0))] + x_specs + [
            pl.BlockSpec((1, D_MODEL, D_MODEL), lambda i: (layer, 0, 0), pipeline_mode=pl.Buffered(1)),
            pl.BlockSpec((1, 1, D_MODEL), lambda i: (layer, 0, 0)),
            pl.BlockSpec((1, 1, 6, D_MODEL), lambda i: (layer, _group_of_tile(i, TM_F), 0, 0)),
            pl.BlockSpec(memory_space=pl.ANY),
            pl.BlockSpec(memory_space=pl.ANY),
            pl.BlockSpec(memory_space=pl.ANY),
        ],
        out_specs=pl.BlockSpec((TM_F, D_MODEL), lambda i: (i, 0)),
        out_shape=jax.ShapeDtypeStruct((NTOK, D_MODEL), F32),
        scratch_shapes=[pltpu.VMEM((D_MODEL, D_MODEL), BF16),
                        pltpu.VMEM((TM_F, D_MODEL), BF16),
                        pltpu.VMEM((TM_F, D_MODEL), F32)] + _ffn_weight_scratch(),
        compiler_params=pltpu.CompilerParams(
            dimension_semantics=("arbitrary",), vmem_limit_bytes=VMEM_LIMIT),
        name="out_ffn",
    )(y, *xs, w_out, g2, mods, wg, wu, wd)


def _tile_plan(counts):
    nt = (counts + TR - 1) // TR
    cum = jnp.cumsum(nt)
    total = cum[-1]
    t = jnp.arange(MAX_TILES, dtype=jnp.int32)
    tt = jnp.minimum(t, total - 1)
    e = jnp.sum((cum[None, :] <= tt[:, None]).astype(jnp.int32), axis=1)
    k = tt - (cum - nt)[e]
    n = jnp.where(t < total, jnp.clip(counts[e] - k * TR, 0, TR), 0)
    return e.astype(jnp.int32), (e * REG_TILES + k).astype(jnp.int32), n.astype(jnp.int32)


def _row_copy(src, src_row, dst, dst_row, sem):
    return pltpu.make_async_copy(src.at[pl.ds(src_row, 1)], dst.at[pl.ds(dst_row, 1)], sem)


def _dispatch_kernel(slot_ref, h_ref, xs_ref, sem):
    base = pl.program_id(0) * (TOP_K * TD)

    def issue(r, carry):
        for k in range(TOP_K):
            _row_copy(h_ref, r, xs_ref, slot_ref[base + TOP_K * r + k], sem).start(priority=k % 2)
        return carry

    lax.fori_loop(0, TD, issue, 0, unroll=8)
    for k in range(TOP_K):
        pltpu.make_async_copy(h_ref, xs_ref.at[pl.ds(0, TD)], sem).wait()


def _dispatch(slots, h2f):
    return pl.pallas_call(
        _dispatch_kernel,
        grid_spec=pltpu.PrefetchScalarGridSpec(
            num_scalar_prefetch=1,
            grid=(NTOK // TD,),
            in_specs=[pl.BlockSpec((TD, D_MODEL), lambda i, s: (i, 0))],
            out_specs=pl.BlockSpec(memory_space=pl.ANY),
            scratch_shapes=[pltpu.SemaphoreType.DMA],
        ),
        out_shape=jax.ShapeDtypeStruct((N_EXPERTS * REG, D_MODEL), F32),
        compiler_params=pltpu.CompilerParams(
            dimension_semantics=("arbitrary",), vmem_limit_bytes=VMEM_LIMIT),
        name="moe_dispatch",
    )(slots, h2f)


def _gffn_kernel(te_ref, tb_ref, tn_ref, x_ref, wg_ref, wu_ref, wd_ref, o_ref,
                 acc, w_gu, w_d, stg_gu, stg_d, sem, *, w_base):
    t = pl.program_id(0)
    n = tn_ref[t]

    @pl.when(n > 0)
    def _():
        row = lax.broadcasted_iota(jnp.int32, (TR, D_MODEL), 0)
        h = jnp.where(row < n, x_ref[...], 0.0).astype(BF16)
        def is_first(tt):
            return tb_ref[tt] % REG_TILES == 0

        def fetch_of(tt):
            e = w_base + te_ref[tt]
            return (wg_ref.at[e], wu_ref.at[e], wd_ref.at[e], stg_gu, stg_d, sem)

        first = is_first(t)
        t_prev = jnp.maximum(t - 1, 0)
        t_next = jnp.minimum(t + 1, MAX_TILES - 1)

        @pl.when(first)
        def _():
            started = (t > 0) & jnp.logical_not(is_first(t_prev))
            _swiglu_tile(h, w_gu, w_d, acc, fetch_of(t), started)

        @pl.when(jnp.logical_not(first))
        def _():
            @pl.when((t + 1 < MAX_TILES) & (tn_ref[t_next] > 0) & is_first(t_next))
            def _():
                _start_first_chunks(fetch_of(t_next))

            _swiglu_tile(h, w_gu, w_d, acc)

        o_ref[...] = acc[...]


def _gffn(tile_e, tile_blk, tile_n, xs, wg, wu, wd, w_base):
    return pl.pallas_call(
        functools.partial(_gffn_kernel, w_base=w_base),
        grid_spec=pltpu.PrefetchScalarGridSpec(
            num_scalar_prefetch=3,
            grid=(MAX_TILES,),
            in_specs=[
                pl.BlockSpec((TR, D_MODEL), lambda t, te, tb, tn: (tb[t], 0)),
                pl.BlockSpec(memory_space=pl.ANY),
                pl.BlockSpec(memory_space=pl.ANY),
                pl.BlockSpec(memory_space=pl.ANY),
            ],
            out_specs=pl.BlockSpec((TR, D_MODEL), lambda t, te, tb, tn: (tb[t], 0)),
            scratch_shapes=[pltpu.VMEM((TR, D_MODEL), F32)] + _ffn_weight_scratch(),
        ),
        out_shape=jax.ShapeDtypeStruct((N_EXPERTS * REG, D_MODEL), F32),
        compiler_params=pltpu.CompilerParams(
            dimension_semantics=("arbitrary",), vmem_limit_bytes=VMEM_LIMIT),
        name="moe_ffn",
    )(tile_e, tile_blk, tile_n, xs, wg, wu, wd)


def _combine_kernel(slot_ref, x1_ref, rinfo_ref, mod_ref, ys_ref, *rest, final_norm):
    if final_norm:
        gf_ref, yp_ref, yl_ref, buf, sem = rest
    else:
        o_ref, buf, sem = rest
    i = pl.program_id(0)
    n_steps = pl.num_programs(0)

    def gather(step, s):
        base = step * (TOP_K * TD)

        def issue(r, carry):
            for k in range(TOP_K):
                _row_copy(ys_ref, slot_ref[base + TOP_K * r + k], buf.at[s, k], r, sem.at[s]).start(priority=k % 2)
            return carry

        lax.fori_loop(0, TD, issue, 0, unroll=8)

    def wait(s):
        for k in range(TOP_K):
            pltpu.make_async_copy(ys_ref.at[pl.ds(0, TD)], buf.at[s, k], sem.at[s]).wait()

    @pl.when(i == 0)
    def _():
        gather(0, 0)

    for s in range(2):
        @pl.when(i % 2 == s)
        def _():
            @pl.when(i + 1 < n_steps)
            def _():
                gather(i + 1, 1 - s)

            wait(s)

    b = buf.at[i % 2]
    y = rinfo_ref[:, R_W1:R_W1 + 1] * b[0] + rinfo_ref[:, R_W2:R_W2 + 1] * b[1]
    x = x1_ref[...] + mod_ref[0, 0, 5:6, :] * y
    if final_norm:
        out = _rmsnorm(x, gf_ref[...])
        is_lat = pl.program_id(0) >= N_CTX // TD

        @pl.when(jnp.logical_not(is_lat))
        def _():
            yp_ref[...] = out

        @pl.when(is_lat)
        def _():
            yl_ref[...] = out
    else:
        o_ref[...] = x


def _combine(slots, x1, rinfo, mods, ys, layer, norm_f=None):
    final_norm = norm_f is not None
    n_ctx_t = N_CTX // TD
    in_specs = [
        pl.BlockSpec((TD, D_MODEL), lambda i, s: (i, 0)),
        pl.BlockSpec((TD, LANES), lambda i, s: (i, 0)),
        pl.BlockSpec((1, 1, 6, D_MODEL), lambda i, s: (layer, _group_of_tile(i, TD), 0, 0)),
        pl.BlockSpec(memory_space=pl.ANY),
    ]
    args = [slots, x1, rinfo, mods, ys]
    if final_norm:
        in_specs.append(pl.BlockSpec((1, D_MODEL), lambda i, s: (0, 0)))
        args.append(norm_f)
        out_specs = [pl.BlockSpec((TD, D_MODEL), lambda i, s: (jnp.minimum(i, n_ctx_t - 1), 0)),
                     pl.BlockSpec((TD, D_MODEL), lambda i, s: (jnp.maximum(i - n_ctx_t, 0), 0))]
        out_shape = [jax.ShapeDtypeStruct((N_CTX, D_MODEL), F32), jax.ShapeDtypeStruct((N_LAT, D_MODEL), F32)]
    else:
        out_specs = pl.BlockSpec((TD, D_MODEL), lambda i, s: (i, 0))
        out_shape = jax.ShapeDtypeStruct((NTOK, D_MODEL), F32)
    return pl.pallas_call(
        functools.partial(_combine_kernel, final_norm=final_norm),
        grid_spec=pltpu.PrefetchScalarGridSpec(
            num_scalar_prefetch=1,
            grid=(NTOK // TD,),
            in_specs=in_specs,
            out_specs=out_specs,
            scratch_shapes=[pltpu.VMEM((2, TOP_K, TD, D_MODEL), F32), pltpu.SemaphoreType.DMA((2,))],
        ),
        out_shape=out_shape,
        compiler_params=pltpu.CompilerParams(
            dimension_semantics=("arbitrary",), vmem_limit_bytes=VMEM_LIMIT),
        name="moe_combine",
    )(*args)


def _final_kernel(x_ref, g_ref, o_ref):
    o_ref[...] = _rmsnorm(x_ref[...], g_ref[...])


def _final(x, g, row_off, rows):
    off = row_off // TM
    return pl.pallas_call(
        _final_kernel,
        grid=(rows // TM,),
        in_specs=[
            pl.BlockSpec((TM, D_MODEL), lambda i: (off + i, 0)),
            pl.BlockSpec((1, D_MODEL), lambda i: (0, 0)),
        ],
        out_specs=pl.BlockSpec((TM, D_MODEL), lambda i: (i, 0)),
        out_shape=jax.ShapeDtypeStruct((rows, D_MODEL), F32),
        compiler_params=pltpu.CompilerParams(
            dimension_semantics=("arbitrary",), vmem_limit_bytes=VMEM_LIMIT),
        name="final_norm",
    )(x, g)


def kernel(x_prompt, x_sample, state_mlstm_C, state_mlstm_n, state_mlstm_m, state_ret_S, c, c_ctx,
           norm1_g, norm2_g, norm_f_g, w_ada, b_ada, w_in, b_gates, ret_decay_logit,
           mlstm_norm_g, ret_norm_g, w_out, ffn_w_gate, ffn_w_up, ffn_w_down,
           moe_w_router, moe_w_gate, moe_w_up, moe_w_down):
    xs_in = (x_prompt.reshape(N_CTX, D_MODEL), x_sample.reshape(N_LAT, D_MODEL))
    cvec = jnp.concatenate(
        [c_ctx[None, :], c, jnp.zeros((N_GROUPS - 1 - DEC_BATCH, D_MODEL), F32)], 0)
    mods = _ada(cvec, w_ada, b_ada).reshape(DEPTH, N_GROUPS, 6, D_MODEL)

    n_m = 4 * W_M
    w_in_t = jnp.swapaxes(w_in, 1, 2)
    n_if = N_GATES // 2
    lane_pad = ((0, 0), (0, 0), (0, LANES - n_if))
    wg = jnp.concatenate([jnp.pad(w_in[:, :, n_m:n_m + n_if], lane_pad),
                          jnp.pad(w_in[:, :, n_m + n_if:n_m + N_GATES], lane_pad)], -1)
    bg = jnp.concatenate([jnp.pad(b_gates[:, None, :n_if], lane_pad),
                          jnp.pad(b_gates[:, None, n_if:], lane_pad)], -1)
    cos_np, sin_np = _rope_tables()
    cos_t, sin_t = jnp.asarray(cos_np), jnp.asarray(sin_np)
    dl = jnp.broadcast_to(ret_decay_logit.reshape(DEPTH, 2 * H_R, 1), (DEPTH, 2 * H_R, LANES))
    m0 = jnp.pad(state_mlstm_m.reshape(DEC_BATCH, DEPTH, 1, 2 * H_M),
                 ((0, 0), (0, 0), (0, 0), (0, LANES - 2 * H_M)))
    g1 = norm1_g.reshape(DEPTH, 1, D_MODEL)
    g2 = norm2_g.reshape(DEPTH, 1, D_MODEL)
    nm = mlstm_norm_g.reshape(DEPTH, 1, W_M)
    nr = ret_norm_g.reshape(DEPTH, 1, W_R)
    n_moe = moe_w_router.shape[0]
    wr_pad = jnp.pad(moe_w_router, ((0, 0), (0, 0), (0, LANES - N_EXPERTS)))
    moe_g = moe_w_gate.reshape(n_moe * N_EXPERTS, D_MODEL, D_FF)
    moe_u = moe_w_up.reshape(n_moe * N_EXPERTS, D_MODEL, D_FF)
    moe_d = moe_w_down.reshape(n_moe * N_EXPERTS, D_FF, D_MODEL)

    states = ()
    xs = xs_in
    for l in range(DEPTH):
        jl = l // 2
        p, gates = _proj(xs, g1, mods, w_in_t, wg, bg, cos_t, sin_t, l)
        y, *states = _scan_ctx(p, gates, dl, nm, nr, l, states)
        y = _scan_lat(p, gates, dl, nm, nr, state_mlstm_C, state_mlstm_n, m0, state_ret_S, y, l)
        if l % 2 == 0:
            x = _out_ffn(y, xs, w_out, g2, ffn_w_gate, ffn_w_up, ffn_w_down, mods, l, jl)
        else:
            x1, h2f, rinfo, cnt = _out(y, xs, w_out, g2, mods, l, wr_pad, jl)
            slots = rinfo[:, R_S1:R_S2 + 1].astype(jnp.int32).reshape(TOP_K * NTOK)
            tile_e, tile_blk, tile_n = _tile_plan(cnt[0, :N_EXPERTS].astype(jnp.int32))
            xd = _dispatch(slots, h2f)
            yd = _gffn(tile_e, tile_blk, tile_n, xd, moe_g, moe_u, moe_d, jl * N_EXPERTS)
            if l == DEPTH - 1:
                y_ctx, y_lat = _combine(slots, x1, rinfo, mods, yd, l, norm_f_g.reshape(1, D_MODEL))
            else:
                x = _combine(slots, x1, rinfo, mods, yd, l)
        xs = (x,)

    if DEPTH % 2 == 1:
        y_ctx = _final(x, norm_f_g.reshape(1, D_MODEL), 0, N_CTX)
        y_lat = _final(x, norm_f_g.reshape(1, D_MODEL), N_CTX, N_LAT)
    y_prompt = y_ctx.reshape(BATCH, SEQ, D_MODEL)
    y_sample = y_lat.reshape(DEC_BATCH, DEC_SEQ, D_MODEL)
    new_C, new_n, new_m, new_S = states
    return (y_prompt, y_sample, new_C, new_n,
            new_m[:, :, 0, :2 * H_M].reshape(BATCH, DEPTH, 2, H_M), new_S)
```

```python
import functools

import numpy as np
import jax
import jax.numpy as jnp
from jax import lax
from jax.experimental import pallas as pl
from jax.experimental.pallas import tpu as pltpu

D_MODEL = 1024
BATCH = 32
SEQ = 256
DEPTH = 2
DEC_BATCH = 2
DEC_SEQ = 1024
GRID_W = 64
H_M = 4
DH = 128
H_R = 4
W_M = H_M * DH
W_R = H_R * DH
N_GATES = 4 * H_M
CHUNK = 128
D_FF = 2816
N_EXPERTS = 8
ROPE_BASE = 10000.0
EPS = 1e-6

N_CTX = BATCH * SEQ
N_LAT = DEC_BATCH * DEC_SEQ
NTOK = N_CTX + N_LAT
N_GROUPS = 8
K_SCALE = DH ** -0.5
P_COLS = 4 * W_M + 4 * W_R
LANES = 128
GATE_LANES = 2 * LANES
VMEM_LIMIT = 56 * 1024 * 1024

F32 = jnp.float32
BF16 = jnp.bfloat16

TM = 1024
TN = 1024
FC = 256
TM_F = 512
TOP_K = 2
TR = 896
REG_TILES = -(-NTOK // TR)
REG = REG_TILES * TR
MAX_TILES = -(-TOP_K * NTOK // TR) + N_EXPERTS
TD = 512


def _group_of_tile(i, tm):
    return jnp.maximum(i * tm // DEC_SEQ - (N_CTX // DEC_SEQ - 1), 0)


def _silu(x):
    return x * jax.nn.sigmoid(x)


def _log_sigmoid(x):
    return jnp.minimum(x, 0.0) - jnp.log(1.0 + jnp.exp(-jnp.abs(x)))


def _rmsnorm(x, g):
    return x * lax.rsqrt(jnp.mean(x * x, -1, keepdims=True) + EPS) * g


def _ada_kernel(cv_ref, w_ref, b_ref, o_ref):
    s = _silu(cv_ref[...]).astype(BF16)
    o_ref[0] = jnp.dot(s, w_ref[0].astype(BF16), preferred_element_type=F32) + b_ref[0]


def _ada(cvec, w_ada, b_ada):
    tn = 1536
    n = 6 * D_MODEL
    return pl.pallas_call(
        _ada_kernel,
        grid=(DEPTH, n // tn),
        in_specs=[
            pl.BlockSpec((N_GROUPS, D_MODEL), lambda l, j: (0, 0)),
            pl.BlockSpec((1, D_MODEL, tn), lambda l, j: (l, 0, j)),
            pl.BlockSpec((1, 1, tn), lambda l, j: (l, 0, j)),
        ],
        out_specs=pl.BlockSpec((1, N_GROUPS, tn), lambda l, j: (l, 0, j)),
        out_shape=jax.ShapeDtypeStruct((DEPTH, N_GROUPS, n), F32),
        compiler_params=pltpu.CompilerParams(
            dimension_semantics=("arbitrary", "arbitrary"), vmem_limit_bytes=VMEM_LIMIT),
        name="ada",
    )(cvec, w_ada, b_ada.reshape(DEPTH, 1, n))


def _rope_tables():
    half = DH // 4
    freqs = ROPE_BASE ** (-np.arange(half, dtype=np.float64) / half)
    t = np.arange(DEC_SEQ)
    pos = np.stack([t // GRID_W, t % GRID_W], 1).astype(np.float64)
    d = np.arange(DH)
    ang = pos[:, d // (DH // 2)] * freqs[d % half][None, :]
    sign = np.where((d % (DH // 2)) < half, -1.0, 1.0)[None, :]
    return np.cos(ang).astype(np.float32), (sign * np.sin(ang)).astype(np.float32)


def _rope(a, cos, sin):
    lane = lax.broadcasted_iota(jnp.int32, a.shape, 1)
    first = (lane % (DH // 2)) < (DH // 4)
    partner = jnp.where(first, pltpu.roll(a, DH - DH // 4, 1), pltpu.roll(a, DH // 4, 1))
    return a * cos + partner * sin


def _proj_kernel(*refs, n_ctx_tiles, split_x, layer):
    if split_x:
        (xp_ref, xl_ref, g_ref, mod_ref, wt_ref, wg_ref, bg_ref, cos_ref, sin_ref,
         p_ref, gate_ref, h_scr, w_res, w_stg, w_sem) = refs
    else:
        (x_ref, g_ref, mod_ref, wt_ref, wg_ref, bg_ref, cos_ref, sin_ref,
         p_ref, gate_ref, h_scr, w_res, w_stg, w_sem) = refs
    p_ref = p_ref.at[0]
    i = pl.program_id(0)
    j = pl.program_id(1)
    is_lat = i >= n_ctx_tiles
    half = TN // 2

    def prologue(x):
        h = _rmsnorm(x, g_ref[0]) * (1.0 + mod_ref[0, 0, 1:2, :]) + mod_ref[0, 0, 0:1, :]
        h_scr[...] = h.astype(BF16)
        gate_ref[...] = _dot_f32x3(h, wg_ref[0]) + bg_ref[0]

    @pl.when(j == 0)
    def _():
        if split_x:
            pl.when(jnp.logical_not(is_lat))(lambda: prologue(xp_ref[...]))
            pl.when(is_lat)(lambda: prologue(xl_ref[...]))
        else:
            prologue(x_ref[...])

    n_col_tiles = P_COLS // TN

    def tile_copy(jj):
        row0 = jj * TN + (N_GATES if jj * TN >= 4 * W_M else 0)
        s = jj % 2
        return pltpu.make_async_copy(wt_ref.at[layer, pl.ds(row0, TN), :], w_stg.at[s], w_sem.at[s])

    def matmul(jj):
        @pl.when(i == 0)
        def _():
            if jj == 0:
                tile_copy(0).start()
            if jj + 1 < n_col_tiles:
                tile_copy(jj + 1).start()
            tile_copy(jj).wait()
            w_res[jj] = w_stg[jj % 2].T.astype(BF16)

        return _dot(h_scr[...], w_res[jj])

    @pl.when(j == 0)
    def _():
        acc = matmul(0)
        p_ref[:, :half] = acc[:, :half].astype(BF16)
        p_ref[:, half:] = (acc[:, half:] * K_SCALE).astype(BF16)

    @pl.when(j == 1)
    def _():
        p_ref[...] = matmul(1).astype(BF16)

    @pl.when(j == 2)
    def _():
        acc = matmul(2)

        @pl.when(is_lat)
        def _():
            cos = cos_ref[...]
            sin = sin_ref[...]
            for hd in range(TN // DH):
                sl = slice(hd * DH, (hd + 1) * DH)
                r = _rope(acc[:, sl], cos, sin)
                p_ref[:, sl] = (r * K_SCALE if hd * DH >= half else r).astype(BF16)

        @pl.when(jnp.logical_not(is_lat))
        def _():
            p_ref[:, :half] = acc[:, :half].astype(BF16)
            p_ref[:, half:] = (acc[:, half:] * K_SCALE).astype(BF16)

    @pl.when(j == 3)
    def _():
        p_ref[...] = matmul(3).astype(BF16)


def _proj(xs, g1, mods, w_in_t, wg, bg, cos_t, sin_t, layer):
    n_ctx_tiles = N_CTX // TM
    tiles_per_seq = DEC_SEQ // TM
    split_x = len(xs) == 2
    if split_x:
        x_specs = [pl.BlockSpec((TM, D_MODEL), lambda i, j: (jnp.minimum(i, n_ctx_tiles - 1), 0)),
                   pl.BlockSpec((TM, D_MODEL), lambda i, j: (jnp.maximum(i - n_ctx_tiles, 0), 0))]
    else:
        x_specs = [pl.BlockSpec((TM, D_MODEL), lambda i, j: (i, 0))]
    return pl.pallas_call(
        functools.partial(_proj_kernel, n_ctx_tiles=n_ctx_tiles, split_x=split_x, layer=layer),
        grid=(NTOK // TM, P_COLS // TN),
        in_specs=x_specs + [
            pl.BlockSpec((1, 1, D_MODEL), lambda i, j: (layer, 0, 0)),
            pl.BlockSpec((1, 1, 6, D_MODEL), lambda i, j: (layer, _group_of_tile(i, TM), 0, 0)),
            pl.BlockSpec(memory_space=pl.ANY),
            pl.BlockSpec((1, D_MODEL, GATE_LANES), lambda i, j: (layer, 0, 0)),
            pl.BlockSpec((1, 1, GATE_LANES), lambda i, j: (layer, 0, 0)),
            pl.BlockSpec((TM, DH), lambda i, j: (i % tiles_per_seq, 0)),
            pl.BlockSpec((TM, DH), lambda i, j: (i % tiles_per_seq, 0)),
        ],
        out_specs=[
            pl.BlockSpec((1, TM, TN), lambda i, j: (j, i, 0)),
            pl.BlockSpec((TM, GATE_LANES), lambda i, j: (i, 0)),
        ],
        out_shape=[
            jax.ShapeDtypeStruct((P_COLS // TN, NTOK, TN), BF16),
            jax.ShapeDtypeStruct((NTOK, GATE_LANES), F32),
        ],
        scratch_shapes=[pltpu.VMEM((TM, D_MODEL), BF16),
                        pltpu.VMEM((P_COLS // TN, D_MODEL, TN), BF16),
                        pltpu.VMEM((2, TN, D_MODEL), F32),
                        pltpu.SemaphoreType.DMA((2,))],
        compiler_params=pltpu.CompilerParams(
            dimension_semantics=("arbitrary", "arbitrary"), vmem_limit_bytes=VMEM_LIMIT),
        name="proj",
    )(*xs, g1, mods, w_in_t, wg, bg, cos_t, sin_t)


def _split3(x):
    hi = x.astype(BF16)
    r1 = x - hi.astype(F32)
    mid = r1.astype(BF16)
    lo = (r1 - mid.astype(F32)).astype(BF16)
    return hi, mid, lo


def _dot(a, b):
    return jnp.dot(a, b, preferred_element_type=F32)


def _dot_nt(a, b):
    return lax.dot_general(a, b, (((1,), (1,)), ((), ())), preferred_element_type=F32)


def _tri_dot_left(tri, x):
    hi, mid, lo = _split3(x)
    return _dot(tri, hi) + _dot(tri, mid) + _dot(tri, lo)


def _run_max(x, reverse):
    n_tiles = x.shape[0] // 8
    sub = lax.broadcasted_iota(jnp.int32, (8, LANES), 0)
    out = [None] * n_tiles
    carry = None
    for t in (range(n_tiles - 1, -1, -1) if reverse else range(n_tiles)):
        v = x[8 * t:8 * t + 8, :]
        for s in (1, 2, 4):
            if reverse:
                v = jnp.maximum(v, jnp.where(sub < 8 - s, pltpu.roll(v, 8 - s, 0), -jnp.inf))
            else:
                v = jnp.maximum(v, jnp.where(sub >= s, pltpu.roll(v, s, 0), -jnp.inf))
        if carry is not None:
            v = jnp.maximum(v, carry)
        carry = jnp.broadcast_to(v[0:1, :] if reverse else v[7:8, :], (8, LANES))
        out[t] = v
    return jnp.concatenate(out, axis=0)


def _scan_kernel(*refs, T, has_state, n_prev=0):
    if has_state:
        (p_ref, g_ref, dl_ref, nm_ref, nr_ref, C0_ref, n0_ref, m0_ref, S0_ref, _yprev_ref,
         y_ref, CN_s, S_s, m_s, hf_s, hb_s, dm_s, dq_s, dk_s, dL_s, kT_s) = refs
    else:
        p_ref, g_ref, dl_ref, nm_ref, nr_ref = refs[:5]
        prev_refs = refs[5:9] if n_prev else ()
        (y_ref, C_out, n_out, m_out, S_out,
         CN_s, S_s, m_s, hf_s, hb_s, dm_s, dq_s, dk_s, dL_s, kT_s) = refs[5 + len(prev_refs):]
    L = CHUNK
    n_chunks = T // L
    row_i = lax.broadcasted_iota(jnp.int32, (L, L), 0)
    col_j = lax.broadcasted_iota(jnp.int32, (L, L), 1)
    lower = col_j <= row_i
    upper = col_j >= row_i
    tril = lower.astype(BF16)
    triu = upper.astype(BF16)
    ones = jnp.ones((L, DH), BF16)
    c_km = W_M
    c_vm = 2 * W_M
    c_om = 3 * W_M
    c_qr = 4 * W_M
    c_kr = c_qr + W_R
    c_vr = c_qr + 2 * W_R
    c_gr = c_qr + 3 * W_R

    def pcols(rows, col):
        return p_ref[col // TN, rows, col % TN:col % TN + DH]

    for d in range(2):
        for h in range(H_M):
            k = d * H_M + h
            if has_state:
                CN_s[k, :, :DH] = C0_ref[0, 0, d, h]
                CN_s[k, :, DH:] = jnp.broadcast_to(n0_ref[0, 0, d, h:h + 1, :], (DH, DH)).T
                S_s[k] = S0_ref[0, 0, d, h]
            else:
                CN_s[k] = jnp.zeros((DH, 2 * DH), F32)
                S_s[k] = jnp.zeros((DH, DH), F32)
    m_s[...] = m0_ref[0, 0] if has_state else jnp.zeros((1, LANES), F32)

    @pl.when(pl.program_id(0) == 0)
    def _():
        pos_i = row_i.astype(F32)
        pos_j = col_j.astype(F32)
        for d in range(2):
            for h in range(H_R):
                k = d * H_R + h
                lg_row = _log_sigmoid(dl_ref[0, k:k + 1, :])
                lg = jnp.broadcast_to(lg_row, (L, L))
                rel = (row_i - col_j if d == 0 else col_j - row_i).astype(F32)
                dm_s[k] = jnp.where(rel >= 0, jnp.exp(lg * jnp.maximum(rel, 0.0)), 0.0)
                dq_s[k] = jnp.exp(lg * (pos_i + 1.0 if d == 0 else L - pos_i))
                dk_s[k] = jnp.exp(lg * (L - 1.0 - pos_j if d == 0 else pos_j))
                dL_s[k] = jnp.exp(lg_row * float(L))

    def transpose_keys(c, carry):
        r0 = pl.multiple_of(c * L, L)
        for h in range(H_M):
            kT_s[h, c] = pcols(pl.ds(r0, L), c_km + h * DH).astype(F32).T
            kT_s[H_M + h, c] = pcols(pl.ds(r0, L), c_kr + h * DH).astype(F32).T
        return carry

    lax.fori_loop(0, n_chunks, transpose_keys, 0)

    def chunk_step(c, carry):
        m_prev = m_s[...]
        m_new = []
        prep = []
        for d in range(2):
            ci = c if d == 0 else n_chunks - 1 - c
            r0 = pl.multiple_of(ci * L, L)
            mask = lower if d == 0 else upper
            e_row = L - 1 if d == 0 else 0
            FL = _log_sigmoid(g_ref[pl.ds(r0, L), LANES:2 * LANES])
            Bc = _tri_dot_left(tril if d == 0 else triu, FL)
            Zc = g_ref[pl.ds(r0, L), 0:LANES] - Bc
            M = jnp.maximum(_run_max(Zc, reverse=(d == 1)), m_prev)
            m_row = Bc + M
            M_end = M[e_row:e_row + 1, :]
            m_new.append(Bc[e_row:e_row + 1, :] + M_end)
            decay = jnp.exp(m_prev - M_end)
            prep.append(dict(ci=ci, r0=r0, mask=mask, M=M, m_row=m_row, decay=decay,
                             ZT=Zc.T,
                             WT=jnp.exp(Zc - M_end).T))
        pairs = [(d, h) for d in range(2) for h in range(H_M)]

        def rows(d, col):
            return pcols(pl.ds(prep[d]["r0"], L), col)

        qk, qkr = {}, {}
        for d, h in pairs:
            qk[d, h] = _dot_nt(rows(d, h * DH), rows(d, c_km + h * DH))
            qkr[d, h] = _dot_nt(rows(d, c_qr + h * DH), rows(d, c_kr + h * DH))
        upd, updr = {}, {}
        for d, h in pairs:
            k = d * H_M + h
            ci = prep[d]["ci"]
            vo = jnp.concatenate([rows(d, c_vm + h * DH), ones], axis=1)
            wkT = (kT_s[h, ci] * jnp.broadcast_to(prep[d]["WT"][k:k + 1, :], (DH, L))).astype(BF16)
            upd[d, h] = _dot(wkT, vo)
            kdT = (kT_s[H_M + h, ci] * dk_s[k]).astype(BF16)
            updr[d, h] = _dot(kdT, rows(d, c_vr + h * DH))
        for d, h in pairs:
            k = d * H_M + h
            r0 = prep[d]["r0"]
            h_dst = hf_s if d == 0 else hb_s
            q = rows(d, h * DH)
            M_col = jnp.broadcast_to(prep[d]["M"][:, k:k + 1], (L, L))
            z_row = jnp.broadcast_to(prep[d]["ZT"][k:k + 1, :], (L, L))
            D = jnp.where(prep[d]["mask"], jnp.exp(z_row - M_col), 0.0)
            s = (qk[d, h] * D).astype(BF16)
            w_inter = jnp.exp(jnp.broadcast_to(m_prev[:, k:k + 1], (L, L)) - M_col)
            wq = (w_inter * q.astype(F32)).astype(BF16)
            vo = jnp.concatenate([rows(d, c_vm + h * DH), ones], axis=1)
            CN = CN_s[k]
            res = _dot(jnp.concatenate([s, wq], axis=1),
                       jnp.concatenate([vo, CN.astype(BF16)], axis=0))
            floor = jnp.exp(-jnp.broadcast_to(prep[d]["m_row"][:, k:k + 1], (L, L)))
            h_dst[pl.ds(r0, L), h * DH:(h + 1) * DH] = res[:, :DH] / jnp.maximum(jnp.abs(res[:, DH:]), floor)
            CN_s[k] = jnp.broadcast_to(prep[d]["decay"][:, k:k + 1], (DH, 2 * DH)) * CN + upd[d, h]
            qr = rows(d, c_qr + h * DH)
            S = S_s[k]
            sr = (qkr[d, h] * dm_s[k]).astype(BF16)
            qd = (qr.astype(F32) * dq_s[k]).astype(BF16)
            h_dst[pl.ds(r0, L), W_M + h * DH:W_M + (h + 1) * DH] = _dot(
                jnp.concatenate([sr, qd], axis=1),
                jnp.concatenate([rows(d, c_vr + h * DH), S.astype(BF16)], axis=0))
            S_s[k] = dL_s[k] * S + updr[d, h]
        lane = lax.broadcasted_iota(jnp.int32, (1, LANES), 1)
        m_s[...] = jnp.where(lane < H_M, m_new[0], m_new[1])
        return carry

    lax.fori_loop(0, n_chunks, chunk_step, 0)

    for h in range(H_M):
        sl = slice(h * DH, (h + 1) * DH)
        hs = hf_s[:, sl] + hb_s[:, sl]
        yn = _rmsnorm(hs, nm_ref[0, :, sl])
        om = pcols(slice(None), c_om + h * DH).astype(F32)
        y_ref[:, sl] = (jax.nn.sigmoid(om) * yn).astype(BF16)
        slr = slice(W_M + h * DH, W_M + (h + 1) * DH)
        hr = hf_s[:, slr] + hb_s[:, slr]
        ynr = _rmsnorm(hr, nr_ref[0, :, sl])
        gr = pcols(slice(None), c_gr + h * DH).astype(F32)
        y_ref[:, slr] = (_silu(gr) * ynr).astype(BF16)

    if not has_state:
        for prev, out in zip(prev_refs, (C_out, n_out, m_out, S_out)):
            out[0, :n_prev] = prev[0]
        for d in range(2):
            for h in range(H_M):
                k = d * H_M + h
                C_out[0, n_prev, d, h] = CN_s[k, :, :DH]
                n_out[0, n_prev, d, h:h + 1, :] = CN_s[k, :, DH:].T[0:1, :]
                S_out[0, n_prev, d, h] = S_s[k]
        m_out[0, n_prev] = m_s[...]


def _scan_scratch(T):
    return [
        pltpu.VMEM((2 * H_M, DH, 2 * DH), F32),
        pltpu.VMEM((2 * H_R, DH, DH), F32),
        pltpu.VMEM((1, LANES), F32),
        pltpu.VMEM((T, W_M + W_R), F32),
        pltpu.VMEM((T, W_M + W_R), F32),
        pltpu.VMEM((2 * H_R, CHUNK, CHUNK), F32),
        pltpu.VMEM((2 * H_R, CHUNK, CHUNK), F32),
        pltpu.VMEM((2 * H_R, CHUNK, CHUNK), F32),
        pltpu.VMEM((2 * H_R, 1, LANES), F32),
        pltpu.VMEM((H_M + H_R, T // CHUNK, DH, CHUNK), F32),
    ]


def _scan_ctx(p, gates, dl, nm, nr, layer, prev_states=()):
    T = SEQ
    n_lay = layer + 1
    state_tails = [(2, H_M, DH, DH), (2, H_M, DH), (1, LANES), (2, H_R, DH, DH)]

    def state_spec(n, tail):
        return pl.BlockSpec((1, n) + tail, lambda b: (b,) + (0,) * (1 + len(tail)))
    common = [
        pl.BlockSpec((P_COLS // TN, T, TN), lambda b: (0, b, 0)),
        pl.BlockSpec((T, GATE_LANES), lambda b: (b, 0)),
        pl.BlockSpec((1, 2 * H_R, LANES), lambda b: (layer, 0, 0)),
        pl.BlockSpec((1, 1, W_M), lambda b: (layer, 0, 0)),
        pl.BlockSpec((1, 1, W_R), lambda b: (layer, 0, 0)),
    ]
    return pl.pallas_call(
        functools.partial(_scan_kernel, T=T, has_state=False, n_prev=layer if prev_states else 0),
        grid=(BATCH,),
        in_specs=common + [state_spec(layer, tail) for tail in state_tails[:len(prev_states)]],
        out_specs=[pl.BlockSpec((T, D_MODEL), lambda b: (b, 0))] + [state_spec(n_lay, tail) for tail in state_tails],
        out_shape=[jax.ShapeDtypeStruct((NTOK, D_MODEL), BF16)] + [
            jax.ShapeDtypeStruct((BATCH, n_lay) + tail, F32) for tail in state_tails],
        scratch_shapes=_scan_scratch(T),
        compiler_params=pltpu.CompilerParams(
            dimension_semantics=("arbitrary",), vmem_limit_bytes=VMEM_LIMIT),
        name="scan_ctx",
    )(p, gates, dl, nm, nr, *prev_states)


def _scan_lat(p, gates, dl, nm, nr, C0, n0, m0, S0, y_prev, layer):
    T = DEC_SEQ
    off = N_CTX // T
    in_specs = [
        pl.BlockSpec((P_COLS // TN, T, TN), lambda b: (0, off + b, 0)),
        pl.BlockSpec((T, GATE_LANES), lambda b: (off + b, 0)),
        pl.BlockSpec((1, 2 * H_R, LANES), lambda b: (layer, 0, 0)),
        pl.BlockSpec((1, 1, W_M), lambda b: (layer, 0, 0)),
        pl.BlockSpec((1, 1, W_R), lambda b: (layer, 0, 0)),
        pl.BlockSpec((1, 1, 2, H_M, DH, DH), lambda b: (b, layer, 0, 0, 0, 0)),
        pl.BlockSpec((1, 1, 2, H_M, DH), lambda b: (b, layer, 0, 0, 0)),
        pl.BlockSpec((1, 1, 1, LANES), lambda b: (b, layer, 0, 0)),
        pl.BlockSpec((1, 1, 2, H_R, DH, DH), lambda b: (b, layer, 0, 0, 0, 0)),
        pl.BlockSpec(memory_space=pl.ANY),
    ]
    return pl.pallas_call(
        functools.partial(_scan_kernel, T=T, has_state=True),
        grid=(DEC_BATCH,),
        in_specs=in_specs,
        out_specs=pl.BlockSpec((T, D_MODEL), lambda b: (off + b, 0)),
        out_shape=jax.ShapeDtypeStruct((NTOK, D_MODEL), BF16),
        input_output_aliases={9: 0},
        scratch_shapes=_scan_scratch(T),
        compiler_params=pltpu.CompilerParams(
            dimension_semantics=("arbitrary",), vmem_limit_bytes=VMEM_LIMIT),
        name="scan_lat",
    )(p, gates, dl, nm, nr, C0, n0, m0, S0, y_prev)


def _top2(logits):
    lane = lax.broadcasted_iota(jnp.int32, logits.shape, 1)
    v1 = jnp.max(logits, -1, keepdims=True)
    i1 = jnp.min(jnp.where(logits == v1, lane, LANES), -1, keepdims=True)
    rest = jnp.where(lane == i1, -jnp.inf, logits)
    v2 = jnp.max(rest, -1, keepdims=True)
    i2 = jnp.min(jnp.where(rest == v2, lane, LANES), -1, keepdims=True)
    e2 = jnp.exp(v2 - v1)
    return i1, i2, 1.0 / (1.0 + e2), e2 / (1.0 + e2)


def _split2(x):
    hi = x.astype(BF16)
    return hi, (x - hi.astype(F32)).astype(BF16)


def _dot_f32x3(a, b):
    a_hi, a_lo = _split2(a)
    b_hi, b_lo = _split2(b)
    return _dot(a_hi, b_hi) + _dot(a_hi, b_lo) + _dot(a_lo, b_hi)


R_E1, R_E2, R_W1, R_W2, R_S1, R_S2 = range(6)


def _out_kernel(*refs, split_x):
    y_ref = refs[0]
    if split_x:
        xp_ref, xl_ref = refs[1:3]
        x_in = jnp.where(pl.program_id(0) >= N_CTX // TM, xl_ref[...], xp_ref[...])
    else:
        x_in = refs[1][...]
    (w_ref, g_ref, mod_ref, wr_ref,
     x1_ref, h2_ref, rinfo_ref, cnt_ref, w_scr, tri_scr, cnt_scr) = refs[3:] if split_x else refs[2:]

    @pl.when(pl.program_id(0) == 0)
    def _():
        w_scr[...] = w_ref[0].astype(BF16)
        r = lax.broadcasted_iota(jnp.int32, (LANES, LANES), 0)
        c = lax.broadcasted_iota(jnp.int32, (LANES, LANES), 1)
        tri_scr[...] = (c < r).astype(BF16)
        cnt_scr[...] = jnp.zeros_like(cnt_scr)

    o = jnp.dot(y_ref[...], w_scr[...], preferred_element_type=F32)
    x1 = x_in + mod_ref[0, 0, 2:3, :] * o
    x1_ref[...] = x1
    h2 = _rmsnorm(x1, g_ref[0]) * (1.0 + mod_ref[0, 0, 4:5, :]) + mod_ref[0, 0, 3:4, :]
    h2_ref[...] = h2
    h_hi, h_lo = _split2(h2)
    w_hi, w_lo = _split2(wr_ref[0])
    t = _dot(h_hi, jnp.concatenate([w_hi, w_lo], axis=1))
    logits = t[:, :LANES] + t[:, LANES:] + _dot(h_lo, w_hi)
    lane = lax.broadcasted_iota(jnp.int32, logits.shape, 1)
    i1, i2, w1, w2 = _top2(jnp.where(lane < N_EXPERTS, logits, -jnp.inf))
    oh1 = lane == i1
    oh2 = lane == i2
    sel = jnp.where(oh1 | oh2, 1.0, 0.0)
    tri = tri_scr[...]
    run = cnt_scr[...]
    ranks = []
    for blk in range(TM // LANES):
        s_blk = sel[blk * LANES:(blk + 1) * LANES, :]
        ranks.append(_dot(tri, s_blk.astype(BF16)) + run)
        run = run + jnp.sum(s_blk, 0, keepdims=True)
    rank = jnp.concatenate(ranks, axis=0)
    r1 = jnp.sum(jnp.where(oh1, rank, 0.0), -1, keepdims=True)
    r2 = jnp.sum(jnp.where(oh2, rank, 0.0), -1, keepdims=True)
    s1 = i1.astype(F32) * float(REG) + r1
    s2 = i2.astype(F32) * float(REG) + r2
    info = jnp.zeros(logits.shape, F32)
    for col, val in ((R_E1, i1.astype(F32)), (R_E2, i2.astype(F32)), (R_W1, w1), (R_W2, w2),
                     (R_S1, s1), (R_S2, s2)):
        info = jnp.where(lane == col, val, info)
    rinfo_ref[...] = info
    cnt_scr[...] = run
    cnt_ref[...] = run


def _out(y, xs, w_out, g2, mods, layer, w_router_pad, router_idx):
    split_x = len(xs) == 2
    n_ctx_tiles = N_CTX // TM
    if split_x:
        x_specs = [pl.BlockSpec((TM, D_MODEL), lambda i: (jnp.minimum(i, n_ctx_tiles - 1), 0)),
                   pl.BlockSpec((TM, D_MODEL), lambda i: (jnp.maximum(i - n_ctx_tiles, 0), 0))]
    else:
        x_specs = [pl.BlockSpec((TM, D_MODEL), lambda i: (i, 0))]
    in_specs = [pl.BlockSpec((TM, D_MODEL), lambda i: (i, 0))] + x_specs + [
        pl.BlockSpec((1, D_MODEL, D_MODEL), lambda i: (layer, 0, 0)),
        pl.BlockSpec((1, 1, D_MODEL), lambda i: (layer, 0, 0)),
        pl.BlockSpec((1, 1, 6, D_MODEL), lambda i: (layer, _group_of_tile(i, TM), 0, 0)),
        pl.BlockSpec((1, D_MODEL, LANES), lambda i: (router_idx, 0, 0)),
    ]
    out_specs = [
        pl.BlockSpec((TM, D_MODEL), lambda i: (i, 0)),
        pl.BlockSpec((TM, D_MODEL), lambda i: (i, 0)),
        pl.BlockSpec((TM, LANES), lambda i: (i, 0)),
        pl.BlockSpec((1, LANES), lambda i: (0, 0)),
    ]
    out_shape = [
        jax.ShapeDtypeStruct((NTOK, D_MODEL), F32),
        jax.ShapeDtypeStruct((NTOK, D_MODEL), F32),
        jax.ShapeDtypeStruct((NTOK, LANES), F32),
        jax.ShapeDtypeStruct((1, LANES), F32),
    ]
    return pl.pallas_call(
        functools.partial(_out_kernel, split_x=split_x),
        grid=(NTOK // TM,),
        in_specs=in_specs,
        out_specs=out_specs,
        out_shape=out_shape,
        scratch_shapes=[pltpu.VMEM((D_MODEL, D_MODEL), BF16), pltpu.VMEM((LANES, LANES), BF16),
                        pltpu.VMEM((1, LANES), F32)],
        compiler_params=pltpu.CompilerParams(
            dimension_semantics=("arbitrary",), vmem_limit_bytes=VMEM_LIMIT),
        name="out_router",
    )(y, *xs, w_out, g2, mods, w_router_pad)


N_FC = D_FF // FC


def _n_slots(fetch):
    return fetch[3].shape[0]


def _chunk_copies(fetch, f):
    wg_hbm, wu_hbm, wd_hbm, stg_gu, stg_d, sem = fetch
    s = f % _n_slots(fetch)
    cols = pl.ds(f * FC, FC)
    return (pltpu.make_async_copy(wg_hbm.at[:, cols], stg_gu.at[s, 0], sem.at[s, 0]),
            pltpu.make_async_copy(wu_hbm.at[:, cols], stg_gu.at[s, 1], sem.at[s, 1]),
            pltpu.make_async_copy(wd_hbm.at[cols, :], stg_d.at[s], sem.at[s, 2]))


def _start_first_chunks(fetch):
    for f in range(min(_n_slots(fetch), N_FC)):
        for c in _chunk_copies(fetch, f):
            c.start()


def _swiglu_tile(h, w_gu, w_d, acc, fetch=None, first_chunks_started=None):
    n_slots = 0
    if fetch is not None:
        stg_gu, stg_d = fetch[3], fetch[4]
        n_slots = _n_slots(fetch)

        def copies(f):
            return _chunk_copies(fetch, f)

        def start(f):
            for c in copies(f):
                c.start()

        def land(f):
            for c in copies(f):
                c.wait()
            s = f % n_slots
            w_gu[f, :, :FC] = stg_gu[s, 0].astype(BF16)
            w_gu[f, :, FC:] = stg_gu[s, 1].astype(BF16)
            w_d[f * FC:(f + 1) * FC, :] = stg_d[s].astype(BF16)
    else:
        start = land = lambda f: None

    def up(f):
        return jnp.dot(h, w_gu[f], preferred_element_type=F32)

    if fetch is not None:
        if first_chunks_started is None:
            _start_first_chunks(fetch)
        else:
            pl.when(jnp.logical_not(first_chunks_started))(lambda: _start_first_chunks(fetch))
    land(0)
    ab = up(0)
    for f in range(N_FC):
        if fetch is not None and f + n_slots < N_FC:
            start(f + n_slots)
        if f + 1 < N_FC:
            land(f + 1)
            ab_next = up(f + 1)
        t = (_silu(ab[:, :FC]) * ab[:, FC:]).astype(BF16)
        contrib = jnp.dot(t, w_d[f * FC:(f + 1) * FC, :], preferred_element_type=F32)
        if f == 0:
            acc[...] = contrib
        else:
            acc[...] += contrib
        if f + 1 < N_FC:
            ab = ab_next


def _ffn_weight_scratch(n_slots=2):
    return [
        pltpu.VMEM((N_FC, D_MODEL, 2 * FC), BF16),
        pltpu.VMEM((D_FF, D_MODEL), BF16),
        pltpu.VMEM((n_slots, 2, D_MODEL, FC), F32),
        pltpu.VMEM((n_slots, FC, D_MODEL), F32),
        pltpu.SemaphoreType.DMA((n_slots, 3)),
    ]


def _out_ffn_kernel(*refs, w_idx, split_x):
    y_ref = refs[0]
    if split_x:
        xp_ref, xl_ref = refs[1:3]
        x_in = jnp.where(pl.program_id(0) >= N_CTX // TM_F, xl_ref[...], xp_ref[...])
    else:
        x_in = refs[1][...]
    (wo_ref, g_ref, mod_ref, wg_ref, wu_ref, wd_ref, o_ref,
     wo_scr, h_scr, acc, w_gu, w_d, stg_gu, stg_d, sem) = refs[3:] if split_x else refs[2:]
    first = pl.program_id(0) == 0

    @pl.when(first)
    def _():
        wo_scr[...] = wo_ref[0].astype(BF16)

    x1 = x_in + mod_ref[0, 0, 2:3, :] * jnp.dot(y_ref[...], wo_scr[...], preferred_element_type=F32)
    h2 = _rmsnorm(x1, g_ref[0]) * (1.0 + mod_ref[0, 0, 4:5, :]) + mod_ref[0, 0, 3:4, :]
    h_scr[...] = h2.astype(BF16)
    o_ref[...] = x1

    @pl.when(first)
    def _():
        _swiglu_tile(h_scr[...], w_gu, w_d, acc,
                     (wg_ref.at[w_idx], wu_ref.at[w_idx], wd_ref.at[w_idx], stg_gu, stg_d, sem))

    @pl.when(jnp.logical_not(first))
    def _():
        _swiglu_tile(h_scr[...], w_gu, w_d, acc)

    o_ref[...] = o_ref[...] + mod_ref[0, 0, 5:6, :] * acc[...]


def _out_ffn(y, xs, w_out, g2, wg, wu, wd, mods, layer, w_idx):
    split_x = len(xs) == 2
    n_ctx_tiles = N_CTX // TM_F
    if split_x:
        x_specs = [pl.BlockSpec((TM_F, D_MODEL), lambda i: (jnp.minimum(i, n_ctx_tiles - 1), 0)),
                   pl.BlockSpec((TM_F, D_MODEL), lambda i: (jnp.maximum(i - n_ctx_tiles, 0), 0))]
    else:
        x_specs = [pl.BlockSpec((TM_F, D_MODEL), lambda i: (i, 0))]
    return pl.pallas_call(
        functools.partial(_out_ffn_kernel, w_idx=w_idx, split_x=split_x),
        grid=(NTOK // TM_F,),
        in_specs=[pl.BlockSpec((TM_F, D_MODEL), lambda i: (i, 0))] + x_specs + [
            pl.BlockSpec((1, D_MODEL, D_MODEL), lambda i: (layer, 0, 0), pipeline_mode=pl.Buffered(1)),
            pl.BlockSpec((1, 1, D_MODEL), lambda i: (layer, 0, 0)),
            pl.BlockSpec((1, 1, 6, D_MODEL), lambda i: (layer, _group_of_tile(i, TM_F), 0, 0)),
            pl.BlockSpec(memory_space=pl.ANY),
            pl.BlockSpec(memory_space=pl.ANY),
            pl.BlockSpec(memory_space=pl.ANY),
        ],
        out_specs=pl.BlockSpec((TM_F, D_MODEL), lambda i: (i, 0)),
        out_shape=jax.ShapeDtypeStruct((NTOK, D_MODEL), F32),
        scratch_shapes=[pltpu.VMEM((D_MODEL, D_MODEL), BF16),
                        pltpu.VMEM((TM_F, D_MODEL), BF16),
                        pltpu.VMEM((TM_F, D_MODEL), F32)] + _ffn_weight_scratch(),
        compiler_params=pltpu.CompilerParams(
            dimension_semantics=("arbitrary",), vmem_limit_bytes=VMEM_LIMIT),
        name="out_ffn",
    )(y, *xs, w_out, g2, mods, wg, wu, wd)


def _tile_plan(counts):
    nt = (counts + TR - 1) // TR
    cum = jnp.cumsum(nt)
    total = cum[-1]
    t = jnp.arange(MAX_TILES, dtype=jnp.int32)
    tt = jnp.minimum(t, total - 1)
    e = jnp.sum((cum[None, :] <= tt[:, None]).astype(jnp.int32), axis=1)
    k = tt - (cum - nt)[e]
    n = jnp.where(t < total, jnp.clip(counts[e] - k * TR, 0, TR), 0)
    return e.astype(jnp.int32), (e * REG_TILES + k).astype(jnp.int32), n.astype(jnp.int32)


def _row_copy(src, src_row, dst, dst_row, sem):
    return pltpu.make_async_copy(src.at[pl.ds(src_row, 1)], dst.at[pl.ds(dst_row, 1)], sem)


def _dispatch_kernel(slot_ref, h_ref, xs_ref, sem):
    base = pl.program_id(0) * (TOP_K * TD)

    def issue(r, carry):
        for k in range(TOP_K):
            _row_copy(h_ref, r, xs_ref, slot_ref[base + TOP_K * r + k], sem).start(priority=k % 2)
        return carry

    lax.fori_loop(0, TD, issue, 0, unroll=8)
    for k in range(TOP_K):
        pltpu.make_async_copy(h_ref, xs_ref.at[pl.ds(0, TD)], sem).wait()


def _dispatch(slots, h2f):
    return pl.pallas_call(
        _dispatch_kernel,
        grid_spec=pltpu.PrefetchScalarGridSpec(
            num_scalar_prefetch=1,
            grid=(NTOK // TD,),
            in_specs=[pl.BlockSpec((TD, D_MODEL), lambda i, s: (i, 0))],
            out_specs=pl.BlockSpec(memory_space=pl.ANY),
            scratch_shapes=[pltpu.SemaphoreType.DMA],
        ),
        out_shape=jax.ShapeDtypeStruct((N_EXPERTS * REG, D_MODEL), F32),
        compiler_params=pltpu.CompilerParams(
            dimension_semantics=("arbitrary",), vmem_limit_bytes=VMEM_LIMIT),
        name="moe_dispatch",
    )(slots, h2f)


def _gffn_kernel(te_ref, tb_ref, tn_ref, x_ref, wg_ref, wu_ref, wd_ref, o_ref,
                 w_gu, w_d, stg_gu, stg_d, sem, *, w_base):
    t = pl.program_id(0)
    n = tn_ref[t]

    @pl.when(n > 0)
    def _():
        row = lax.broadcasted_iota(jnp.int32, (TR, D_MODEL), 0)
        h = jnp.where(row < n, x_ref[...], 0.0).astype(BF16)
        def is_first(tt):
            return tb_ref[tt] % REG_TILES == 0

        def fetch_of(tt):
            e = w_base + te_ref[tt]
            return (wg_ref.at[e], wu_ref.at[e], wd_ref.at[e], stg_gu, stg_d, sem)

        first = is_first(t)
        t_prev = jnp.maximum(t - 1, 0)
        t_next = jnp.minimum(t + 1, MAX_TILES - 1)

        @pl.when(first)
        def _():
            started = (t > 0) & jnp.logical_not(is_first(t_prev))
            _swiglu_tile(h, w_gu, w_d, o_ref, fetch_of(t), started)

        @pl.when(jnp.logical_not(first))
        def _():
            @pl.when((t + 1 < MAX_TILES) & (tn_ref[t_next] > 0) & is_first(t_next))
            def _():
                _start_first_chunks(fetch_of(t_next))

            _swiglu_tile(h, w_gu, w_d, o_ref)


def _gffn(tile_e, tile_blk, tile_n, xs, wg, wu, wd, w_base):
    return pl.pallas_call(
        functools.partial(_gffn_kernel, w_base=w_base),
        grid_spec=pltpu.PrefetchScalarGridSpec(
            num_scalar_prefetch=3,
            grid=(MAX_TILES,),
            in_specs=[
                pl.BlockSpec((TR, D_MODEL), lambda t, te, tb, tn: (tb[t], 0)),
                pl.BlockSpec(memory_space=pl.ANY),
                pl.BlockSpec(memory_space=pl.ANY),
                pl.BlockSpec(memory_space=pl.ANY),
            ],
            out_specs=pl.BlockSpec((TR, D_MODEL), lambda t, te, tb, tn: (tb[t], 0)),
            scratch_shapes=_ffn_weight_scratch(n_slots=4),
        ),
        out_shape=jax.ShapeDtypeStruct((N_EXPERTS * REG, D_MODEL), F32),
        compiler_params=pltpu.CompilerParams(
            dimension_semantics=("arbitrary",), vmem_limit_bytes=VMEM_LIMIT),
        name="moe_ffn",
    )(tile_e, tile_blk, tile_n, xs, wg, wu, wd)


def _combine_kernel(slot_ref, x1_ref, rinfo_ref, mod_ref, ys_ref, *rest, final_norm):
    if final_norm:
        gf_ref, yp_ref, yl_ref, buf, sem = rest
    else:
        o_ref, buf, sem = rest
    i = pl.program_id(0)
    n_steps = pl.num_programs(0)

    def gather(step, s):
        base = step * (TOP_K * TD)

        def issue(r, carry):
            for k in range(TOP_K):
                _row_copy(ys_ref, slot_ref[base + TOP_K * r + k], buf.at[s, k], r, sem.at[s]).start(priority=k % 2)
            return carry

        lax.fori_loop(0, TD, issue, 0, unroll=8)

    def wait(s):
        for k in range(TOP_K):
            pltpu.make_async_copy(ys_ref.at[pl.ds(0, TD)], buf.at[s, k], sem.at[s]).wait()

    @pl.when(i == 0)
    def _():
        gather(0, 0)

    for s in range(2):
        @pl.when(i % 2 == s)
        def _():
            @pl.when(i + 1 < n_steps)
            def _():
                gather(i + 1, 1 - s)

            wait(s)

    b = buf.at[i % 2]
    y = rinfo_ref[:, R_W1:R_W1 + 1] * b[0] + rinfo_ref[:, R_W2:R_W2 + 1] * b[1]
    x = x1_ref[...] + mod_ref[0, 0, 5:6, :] * y
    if final_norm:
        out = _rmsnorm(x, gf_ref[...])
        is_lat = pl.program_id(0) >= N_CTX // TD

        @pl.when(jnp.logical_not(is_lat))
        def _():
            yp_ref[...] = out

        @pl.when(is_lat)
        def _():
            yl_ref[...] = out
    else:
        o_ref[...] = x


def _combine(slots, x1, rinfo, mods, ys, layer, norm_f=None):
    final_norm = norm_f is not None
    n_ctx_t = N_CTX // TD
    in_specs = [
        pl.BlockSpec((TD, D_MODEL), lambda i, s: (i, 0)),
        pl.BlockSpec((TD, LANES), lambda i, s: (i, 0)),
        pl.BlockSpec((1, 1, 6, D_MODEL), lambda i, s: (layer, _group_of_tile(i, TD), 0, 0)),
        pl.BlockSpec(memory_space=pl.ANY),
    ]
    args = [slots, x1, rinfo, mods, ys]
    if final_norm:
        in_specs.append(pl.BlockSpec((1, D_MODEL), lambda i, s: (0, 0)))
        args.append(norm_f)
        out_specs = [pl.BlockSpec((TD, D_MODEL), lambda i, s: (jnp.minimum(i, n_ctx_t - 1), 0)),
                     pl.BlockSpec((TD, D_MODEL), lambda i, s: (jnp.maximum(i - n_ctx_t, 0), 0))]
        out_shape = [jax.ShapeDtypeStruct((N_CTX, D_MODEL), F32), jax.ShapeDtypeStruct((N_LAT, D_MODEL), F32)]
    else:
        out_specs = pl.BlockSpec((TD, D_MODEL), lambda i, s: (i, 0))
        out_shape = jax.ShapeDtypeStruct((NTOK, D_MODEL), F32)
    return pl.pallas_call(
        functools.partial(_combine_kernel, final_norm=final_norm),
        grid_spec=pltpu.PrefetchScalarGridSpec(
            num_scalar_prefetch=1,
            grid=(NTOK // TD,),
            in_specs=in_specs,
            out_specs=out_specs,
            scratch_shapes=[pltpu.VMEM((2, TOP_K, TD, D_MODEL), F32), pltpu.SemaphoreType.DMA((2,))],
        ),
        out_shape=out_shape,
        compiler_params=pltpu.CompilerParams(
            dimension_semantics=("arbitrary",), vmem_limit_bytes=VMEM_LIMIT),
        name="moe_combine",
    )(*args)


def _final_kernel(x_ref, g_ref, o_ref):
    o_ref[...] = _rmsnorm(x_ref[...], g_ref[...])


def _final(x, g, row_off, rows):
    off = row_off // TM
    return pl.pallas_call(
        _final_kernel,
        grid=(rows // TM,),
        in_specs=[
            pl.BlockSpec((TM, D_MODEL), lambda i: (off + i, 0)),
            pl.BlockSpec((1, D_MODEL), lambda i: (0, 0)),
        ],
        out_specs=pl.BlockSpec((TM, D_MODEL), lambda i: (i, 0)),
        out_shape=jax.ShapeDtypeStruct((rows, D_MODEL), F32),
        compiler_params=pltpu.CompilerParams(
            dimension_semantics=("arbitrary",), vmem_limit_bytes=VMEM_LIMIT),
        name="final_norm",
    )(x, g)


def kernel(x_prompt, x_sample, state_mlstm_C, state_mlstm_n, state_mlstm_m, state_ret_S, c, c_ctx,
           norm1_g, norm2_g, norm_f_g, w_ada, b_ada, w_in, b_gates, ret_decay_logit,
           mlstm_norm_g, ret_norm_g, w_out, ffn_w_gate, ffn_w_up, ffn_w_down,
           moe_w_router, moe_w_gate, moe_w_up, moe_w_down):
    xs_in = (x_prompt.reshape(N_CTX, D_MODEL), x_sample.reshape(N_LAT, D_MODEL))
    cvec = jnp.concatenate(
        [c_ctx[None, :], c, jnp.zeros((N_GROUPS - 1 - DEC_BATCH, D_MODEL), F32)], 0)
    mods = _ada(cvec, w_ada, b_ada).reshape(DEPTH, N_GROUPS, 6, D_MODEL)

    n_m = 4 * W_M
    w_in_t = jnp.swapaxes(w_in, 1, 2)
    n_if = N_GATES // 2
    lane_pad = ((0, 0), (0, 0), (0, LANES - n_if))
    wg = jnp.concatenate([jnp.pad(w_in[:, :, n_m:n_m + n_if], lane_pad),
                          jnp.pad(w_in[:, :, n_m + n_if:n_m + N_GATES], lane_pad)], -1)
    bg = jnp.concatenate([jnp.pad(b_gates[:, None, :n_if], lane_pad),
                          jnp.pad(b_gates[:, None, n_if:], lane_pad)], -1)
    cos_np, sin_np = _rope_tables()
    cos_t, sin_t = jnp.asarray(cos_np), jnp.asarray(sin_np)
    dl = jnp.broadcast_to(ret_decay_logit.reshape(DEPTH, 2 * H_R, 1), (DEPTH, 2 * H_R, LANES))
    m0 = jnp.pad(state_mlstm_m.reshape(DEC_BATCH, DEPTH, 1, 2 * H_M),
                 ((0, 0), (0, 0), (0, 0), (0, LANES - 2 * H_M)))
    g1 = norm1_g.reshape(DEPTH, 1, D_MODEL)
    g2 = norm2_g.reshape(DEPTH, 1, D_MODEL)
    nm = mlstm_norm_g.reshape(DEPTH, 1, W_M)
    nr = ret_norm_g.reshape(DEPTH, 1, W_R)
    n_moe = moe_w_router.shape[0]
    wr_pad = jnp.pad(moe_w_router, ((0, 0), (0, 0), (0, LANES - N_EXPERTS)))
    moe_g = moe_w_gate.reshape(n_moe * N_EXPERTS, D_MODEL, D_FF)
    moe_u = moe_w_up.reshape(n_moe * N_EXPERTS, D_MODEL, D_FF)
    moe_d = moe_w_down.reshape(n_moe * N_EXPERTS, D_FF, D_MODEL)

    states = ()
    xs = xs_in
    for l in range(DEPTH):
        jl = l // 2
        p, gates = _proj(xs, g1, mods, w_in_t, wg, bg, cos_t, sin_t, l)
        y, *states = _scan_ctx(p, gates, dl, nm, nr, l, states)
        y = _scan_lat(p, gates, dl, nm, nr, state_mlstm_C, state_mlstm_n, m0, state_ret_S, y, l)
        if l % 2 == 0:
            x = _out_ffn(y, xs, w_out, g2, ffn_w_gate, ffn_w_up, ffn_w_down, mods, l, jl)
        else:
            x1, h2f, rinfo, cnt = _out(y, xs, w_out, g2, mods, l, wr_pad, jl)
            slots = rinfo[:, R_S1:R_S2 + 1].astype(jnp.int32).reshape(TOP_K * NTOK)
            tile_e, tile_blk, tile_n = _tile_plan(cnt[0, :N_EXPERTS].astype(jnp.int32))
            xd = _dispatch(slots, h2f)
            yd = _gffn(tile_e, tile_blk, tile_n, xd, moe_g, moe_u, moe_d, jl * N_EXPERTS)
            if l == DEPTH - 1:
                y_ctx, y_lat = _combine(slots, x1, rinfo, mods, yd, l, norm_f_g.reshape(1, D_MODEL))
            else:
                x = _combine(slots, x1, rinfo, mods, yd, l)
        xs = (x,)

    if DEPTH % 2 == 1:
        y_ctx = _final(x, norm_f_g.reshape(1, D_MODEL), 0, N_CTX)
        y_lat = _final(x, norm_f_g.reshape(1, D_MODEL), N_CTX, N_LAT)
    y_prompt = y_ctx.reshape(BATCH, SEQ, D_MODEL)
    y_sample = y_lat.reshape(DEC_BATCH, DEC_SEQ, D_MODEL)
    new_C, new_n, new_m, new_S = states
    return (y_prompt, y_sample, new_C, new_n,
            new_m[:, :, 0, :2 * H_M].reshape(BATCH, DEPTH, 2, H_M), new_S)
```

```python
import functools

import numpy as np
import jax
import jax.numpy as jnp
from jax import lax
from jax.experimental import pallas as pl
from jax.experimental.pallas import tpu as pltpu

D_MODEL = 1024
BATCH = 32
SEQ = 256
DEPTH = 2
DEC_BATCH = 2
DEC_SEQ = 1024
GRID_W = 64
H_M = 4
DH = 128
H_R = 4
W_M = H_M * DH
W_R = H_R * DH
N_GATES = 4 * H_M
CHUNK = 128
D_FF = 2816
N_EXPERTS = 8
ROPE_BASE = 10000.0
EPS = 1e-6

N_CTX = BATCH * SEQ
N_LAT = DEC_BATCH * DEC_SEQ
NTOK = N_CTX + N_LAT
N_GROUPS = 8
K_SCALE = DH ** -0.5
P_COLS = 4 * W_M + 4 * W_R
LANES = 128
SUBLANES = 8
GATE_LANES = 2 * LANES
VMEM_LIMIT = 56 * 1024 * 1024

F32 = jnp.float32
BF16 = jnp.bfloat16

TM = 1024
TN = 1024
FC = 256
TM_F = 512
TOP_K = 2
TR = 896
REG_TILES = -(-NTOK // TR)
REG = REG_TILES * TR
MAX_TILES = -(-TOP_K * NTOK // TR) + N_EXPERTS
TD = 512
SCORE_AHEAD = 3


def _group_of_tile(i, tm):
    return jnp.maximum(i * tm // DEC_SEQ - (N_CTX // DEC_SEQ - 1), 0)


def _silu(x):
    return x * jax.nn.sigmoid(x)


def _log_sigmoid(x):
    return jnp.minimum(x, 0.0) - jnp.log(1.0 + jnp.exp(-jnp.abs(x)))


def _rmsnorm(x, g):
    return x * lax.rsqrt(jnp.mean(x * x, -1, keepdims=True) + EPS) * g


def _ada_kernel(cv_ref, w_ref, b_ref, o_ref):
    s = _silu(cv_ref[...]).astype(BF16)
    o_ref[0] = jnp.dot(s, w_ref[0].astype(BF16), preferred_element_type=F32) + b_ref[0]


def _ada(cvec, w_ada, b_ada):
    tn = 1536
    n = 6 * D_MODEL
    return pl.pallas_call(
        _ada_kernel,
        grid=(DEPTH, n // tn),
        in_specs=[
            pl.BlockSpec((N_GROUPS, D_MODEL), lambda l, j: (0, 0)),
            pl.BlockSpec((1, D_MODEL, tn), lambda l, j: (l, 0, j)),
            pl.BlockSpec((1, 1, tn), lambda l, j: (l, 0, j)),
        ],
        out_specs=pl.BlockSpec((1, N_GROUPS, tn), lambda l, j: (l, 0, j)),
        out_shape=jax.ShapeDtypeStruct((DEPTH, N_GROUPS, n), F32),
        compiler_params=pltpu.CompilerParams(
            dimension_semantics=("arbitrary", "arbitrary"), vmem_limit_bytes=VMEM_LIMIT),
        name="ada",
    )(cvec, w_ada, b_ada.reshape(DEPTH, 1, n))


def _rope_tables():
    half = DH // 4
    freqs = ROPE_BASE ** (-np.arange(half, dtype=np.float64) / half)
    t = np.arange(DEC_SEQ)
    pos = np.stack([t // GRID_W, t % GRID_W], 1).astype(np.float64)
    d = np.arange(DH)
    ang = pos[:, d // (DH // 2)] * freqs[d % half][None, :]
    sign = np.where((d % (DH // 2)) < half, -1.0, 1.0)[None, :]
    return np.cos(ang).astype(np.float32), (sign * np.sin(ang)).astype(np.float32)


def _rope(a, cos, sin):
    lane = lax.broadcasted_iota(jnp.int32, a.shape, 1)
    first = (lane % (DH // 2)) < (DH // 4)
    partner = jnp.where(first, pltpu.roll(a, DH - DH // 4, 1), pltpu.roll(a, DH // 4, 1))
    return a * cos + partner * sin


def _proj_kernel(*refs, n_ctx_tiles, split_x, layer):
    if split_x:
        (xp_ref, xl_ref, g_ref, mod_ref, wt_ref, wg_ref, bg_ref, cos_ref, sin_ref,
         p_ref, gate_ref, h_scr, w_res, w_stg, w_sem) = refs
    else:
        (x_ref, g_ref, mod_ref, wt_ref, wg_ref, bg_ref, cos_ref, sin_ref,
         p_ref, gate_ref, h_scr, w_res, w_stg, w_sem) = refs
    p_ref = p_ref.at[0]
    i = pl.program_id(0)
    j = pl.program_id(1)
    is_lat = i >= n_ctx_tiles
    half = TN // 2

    def prologue(x):
        h = _rmsnorm(x, g_ref[0]) * (1.0 + mod_ref[0, 0, 1:2, :]) + mod_ref[0, 0, 0:1, :]
        h_scr[...] = h.astype(BF16)
        gate_ref[...] = _dot_f32x3(h, wg_ref[0]) + bg_ref[0]

    @pl.when(j == 0)
    def _():
        if split_x:
            pl.when(jnp.logical_not(is_lat))(lambda: prologue(xp_ref[...]))
            pl.when(is_lat)(lambda: prologue(xl_ref[...]))
        else:
            prologue(x_ref[...])

    n_col_tiles = P_COLS // TN

    def tile_copy(jj):
        row0 = jj * TN + (N_GATES if jj * TN >= 4 * W_M else 0)
        s = jj % 2
        return pltpu.make_async_copy(wt_ref.at[layer, pl.ds(row0, TN), :], w_stg.at[s], w_sem.at[s])

    def matmul(jj):
        @pl.when(i == 0)
        def _():
            if jj == 0:
                tile_copy(0).start()
            if jj + 1 < n_col_tiles:
                tile_copy(jj + 1).start()
            tile_copy(jj).wait()
            w_res[jj] = w_stg[jj % 2].T.astype(BF16)

        return _dot(h_scr[...], w_res[jj])

    @pl.when(j == 0)
    def _():
        acc = matmul(0)
        p_ref[:, :half] = acc[:, :half].astype(BF16)
        p_ref[:, half:] = (acc[:, half:] * K_SCALE).astype(BF16)

    @pl.when(j == 1)
    def _():
        p_ref[...] = matmul(1).astype(BF16)

    @pl.when(j == 2)
    def _():
        acc = matmul(2)

        @pl.when(is_lat)
        def _():
            cos = cos_ref[...]
            sin = sin_ref[...]
            for hd in range(TN // DH):
                sl = slice(hd * DH, (hd + 1) * DH)
                r = _rope(acc[:, sl], cos, sin)
                p_ref[:, sl] = (r * K_SCALE if hd * DH >= half else r).astype(BF16)

        @pl.when(jnp.logical_not(is_lat))
        def _():
            p_ref[:, :half] = acc[:, :half].astype(BF16)
            p_ref[:, half:] = (acc[:, half:] * K_SCALE).astype(BF16)

    @pl.when(j == 3)
    def _():
        p_ref[...] = matmul(3).astype(BF16)


def _proj(xs, g1, mods, w_in_t, wg, bg, cos_t, sin_t, layer):
    n_ctx_tiles = N_CTX // TM
    tiles_per_seq = DEC_SEQ // TM
    split_x = len(xs) == 2
    if split_x:
        x_specs = [pl.BlockSpec((TM, D_MODEL), lambda i, j: (jnp.minimum(i, n_ctx_tiles - 1), 0)),
                   pl.BlockSpec((TM, D_MODEL), lambda i, j: (jnp.maximum(i - n_ctx_tiles, 0), 0))]
    else:
        x_specs = [pl.BlockSpec((TM, D_MODEL), lambda i, j: (i, 0))]
    return pl.pallas_call(
        functools.partial(_proj_kernel, n_ctx_tiles=n_ctx_tiles, split_x=split_x, layer=layer),
        grid=(NTOK // TM, P_COLS // TN),
        in_specs=x_specs + [
            pl.BlockSpec((1, 1, D_MODEL), lambda i, j: (layer, 0, 0)),
            pl.BlockSpec((1, 1, 6, D_MODEL), lambda i, j: (layer, _group_of_tile(i, TM), 0, 0)),
            pl.BlockSpec(memory_space=pl.ANY),
            pl.BlockSpec((1, D_MODEL, GATE_LANES), lambda i, j: (layer, 0, 0)),
            pl.BlockSpec((1, 1, GATE_LANES), lambda i, j: (layer, 0, 0)),
            pl.BlockSpec((TM, DH), lambda i, j: (i % tiles_per_seq, 0)),
            pl.BlockSpec((TM, DH), lambda i, j: (i % tiles_per_seq, 0)),
        ],
        out_specs=[
            pl.BlockSpec((1, TM, TN), lambda i, j: (j, i, 0)),
            pl.BlockSpec((TM, GATE_LANES), lambda i, j: (i, 0)),
        ],
        out_shape=[
            jax.ShapeDtypeStruct((P_COLS // TN, NTOK, TN), BF16),
            jax.ShapeDtypeStruct((NTOK, GATE_LANES), F32),
        ],
        scratch_shapes=[pltpu.VMEM((TM, D_MODEL), BF16),
                        pltpu.VMEM((P_COLS // TN, D_MODEL, TN), BF16),
                        pltpu.VMEM((2, TN, D_MODEL), F32),
                        pltpu.SemaphoreType.DMA((2,))],
        compiler_params=pltpu.CompilerParams(
            dimension_semantics=("arbitrary", "arbitrary"), vmem_limit_bytes=VMEM_LIMIT),
        name="proj",
    )(*xs, g1, mods, w_in_t, wg, bg, cos_t, sin_t)


def _split3(x):
    hi = x.astype(BF16)
    r1 = x - hi.astype(F32)
    mid = r1.astype(BF16)
    lo = (r1 - mid.astype(F32)).astype(BF16)
    return hi, mid, lo


def _dot(a, b):
    return jnp.dot(a, b, preferred_element_type=F32)


def _dot_nt(a, b):
    return lax.dot_general(a, b, (((1,), (1,)), ((), ())), preferred_element_type=F32)


def _tri_dot_left(tri, x):
    hi, mid, lo = _split3(x)
    return _dot(tri, hi) + _dot(tri, mid) + _dot(tri, lo)


def _run_max(x, reverse):
    n_tiles = x.shape[0] // SUBLANES
    sub = lax.broadcasted_iota(jnp.int32, (SUBLANES, LANES), 0)
    out = [None] * n_tiles
    carry = None
    for t in (range(n_tiles - 1, -1, -1) if reverse else range(n_tiles)):
        v = x[SUBLANES * t:SUBLANES * (t + 1), :]
        s = 1
        while s < SUBLANES:
            if reverse:
                v = jnp.maximum(v, jnp.where(sub < SUBLANES - s, pltpu.roll(v, SUBLANES - s, 0), -jnp.inf))
            else:
                v = jnp.maximum(v, jnp.where(sub >= s, pltpu.roll(v, s, 0), -jnp.inf))
            s *= 2
        if carry is not None:
            v = jnp.maximum(v, carry)
        carry = jnp.broadcast_to(v[0:1, :] if reverse else v[SUBLANES - 1:SUBLANES, :], (SUBLANES, LANES))
        out[t] = v
    return jnp.concatenate(out, axis=0)


def _scan_kernel(*refs, T, has_state, n_prev=0):
    if has_state:
        (p_ref, g_ref, dl_ref, nm_ref, nr_ref, C0_ref, n0_ref, m0_ref, S0_ref, _yprev_ref,
         y_ref, CN_s, S_s, m_s, hf_s, hb_s, dm_s, dq_s, dk_s, dL_s, kT_s) = refs
    else:
        p_ref, g_ref, dl_ref, nm_ref, nr_ref = refs[:5]
        prev_refs = refs[5:9] if n_prev else ()
        (y_ref, C_out, n_out, m_out, S_out,
         CN_s, S_s, m_s, hf_s, hb_s, dm_s, dq_s, dk_s, dL_s, kT_s) = refs[5 + len(prev_refs):]
    L = CHUNK
    n_chunks = T // L
    row_i = lax.broadcasted_iota(jnp.int32, (L, L), 0)
    col_j = lax.broadcasted_iota(jnp.int32, (L, L), 1)
    lower = col_j <= row_i
    upper = col_j >= row_i
    tril = lower.astype(BF16)
    triu = upper.astype(BF16)
    ones = jnp.ones((L, DH), BF16)
    c_km = W_M
    c_vm = 2 * W_M
    c_om = 3 * W_M
    c_qr = 4 * W_M
    c_kr = c_qr + W_R
    c_vr = c_qr + 2 * W_R
    c_gr = c_qr + 3 * W_R

    def pcols(rows, col):
        return p_ref[col // TN, rows, col % TN:col % TN + DH]

    for d in range(2):
        for h in range(H_M):
            k = d * H_M + h
            if has_state:
                CN_s[k, :, :DH] = C0_ref[0, 0, d, h]
                CN_s[k, :, DH:] = jnp.broadcast_to(n0_ref[0, 0, d, h:h + 1, :], (DH, DH)).T
                S_s[k] = S0_ref[0, 0, d, h]
            else:
                CN_s[k] = jnp.zeros((DH, 2 * DH), F32)
                S_s[k] = jnp.zeros((DH, DH), F32)
    m_s[...] = m0_ref[0, 0] if has_state else jnp.zeros((1, LANES), F32)

    @pl.when(pl.program_id(0) == 0)
    def _():
        pos_i = row_i.astype(F32)
        pos_j = col_j.astype(F32)
        for d in range(2):
            for h in range(H_R):
                k = d * H_R + h
                lg_row = _log_sigmoid(dl_ref[0, k:k + 1, :])
                lg = jnp.broadcast_to(lg_row, (L, L))
                rel = (row_i - col_j if d == 0 else col_j - row_i).astype(F32)
                dm_s[k] = jnp.where(rel >= 0, jnp.exp(lg * jnp.maximum(rel, 0.0)), 0.0)
                dq_s[k] = jnp.exp(lg * (pos_i + 1.0 if d == 0 else L - pos_i))
                dk_s[k] = jnp.exp(lg * (L - 1.0 - pos_j if d == 0 else pos_j))
                dL_s[k] = jnp.exp(lg_row * float(L))

    def transpose_keys(c, carry):
        r0 = pl.multiple_of(c * L, L)
        for h in range(H_M):
            kT_s[h, c] = pcols(pl.ds(r0, L), c_km + h * DH).astype(F32).T
            kT_s[H_M + h, c] = pcols(pl.ds(r0, L), c_kr + h * DH).astype(F32).T
        return carry

    lax.fori_loop(0, n_chunks, transpose_keys, 0)

    def chunk_step(c, carry):
        m_prev = m_s[...]
        m_new = []
        prep = []
        for d in range(2):
            ci = c if d == 0 else n_chunks - 1 - c
            r0 = pl.multiple_of(ci * L, L)
            mask = lower if d == 0 else upper
            e_row = L - 1 if d == 0 else 0
            FL = _log_sigmoid(g_ref[pl.ds(r0, L), LANES:2 * LANES])
            Bc = _tri_dot_left(tril if d == 0 else triu, FL)
            Zc = g_ref[pl.ds(r0, L), 0:LANES] - Bc
            M = jnp.maximum(_run_max(Zc, reverse=(d == 1)), m_prev)
            m_row = Bc + M
            M_end = M[e_row:e_row + 1, :]
            m_new.append(Bc[e_row:e_row + 1, :] + M_end)
            decay = jnp.exp(m_prev - M_end)
            prep.append(dict(ci=ci, r0=r0, mask=mask, M=M, m_row=m_row, decay=decay,
                             ZT=Zc.T,
                             WT=jnp.exp(Zc - M_end).T))
        pairs = [(d, h) for d in range(2) for h in range(H_M)]

        def rows(d, col):
            return pcols(pl.ds(prep[d]["r0"], L), col)

        qk, qkr = {}, {}

        def scores(idx):
            if idx < len(pairs):
                d, h = pairs[idx]
                qk[d, h] = _dot_nt(rows(d, h * DH), rows(d, c_km + h * DH))
                qkr[d, h] = _dot_nt(rows(d, c_qr + h * DH), rows(d, c_kr + h * DH))

        for idx in range(SCORE_AHEAD):
            scores(idx)
        for idx, (d, h) in enumerate(pairs):
            scores(idx + SCORE_AHEAD)
            k = d * H_M + h
            ci = prep[d]["ci"]
            r0 = prep[d]["r0"]
            h_dst = hf_s if d == 0 else hb_s
            vo = jnp.concatenate([rows(d, c_vm + h * DH), ones], axis=1)
            wkT = (kT_s[h, ci] * jnp.broadcast_to(prep[d]["WT"][k:k + 1, :], (DH, L))).astype(BF16)
            upd = _dot(wkT, vo)
            kdT = (kT_s[H_M + h, ci] * dk_s[k]).astype(BF16)
            updr = _dot(kdT, rows(d, c_vr + h * DH))
            q = rows(d, h * DH)
            M_col = jnp.broadcast_to(prep[d]["M"][:, k:k + 1], (L, L))
            z_row = jnp.broadcast_to(prep[d]["ZT"][k:k + 1, :], (L, L))
            D = jnp.where(prep[d]["mask"], jnp.exp(z_row - M_col), 0.0)
            s = (qk[d, h] * D).astype(BF16)
            w_inter = jnp.exp(jnp.broadcast_to(m_prev[:, k:k + 1], (L, L)) - M_col)
            wq = (w_inter * q.astype(F32)).astype(BF16)
            CN = CN_s[k]
            res = _dot(jnp.concatenate([s, wq], axis=1),
                       jnp.concatenate([vo, CN.astype(BF16)], axis=0))
            floor = jnp.exp(-jnp.broadcast_to(prep[d]["m_row"][:, k:k + 1], (L, L)))
            h_dst[pl.ds(r0, L), h * DH:(h + 1) * DH] = res[:, :DH] / jnp.maximum(jnp.abs(res[:, DH:]), floor)
            CN_s[k] = jnp.broadcast_to(prep[d]["decay"][:, k:k + 1], (DH, 2 * DH)) * CN + upd
            qr = rows(d, c_qr + h * DH)
            S = S_s[k]
            sr = (qkr[d, h] * dm_s[k]).astype(BF16)
            qd = (qr.astype(F32) * dq_s[k]).astype(BF16)
            h_dst[pl.ds(r0, L), W_M + h * DH:W_M + (h + 1) * DH] = _dot(
                jnp.concatenate([sr, qd], axis=1),
                jnp.concatenate([rows(d, c_vr + h * DH), S.astype(BF16)], axis=0))
            S_s[k] = dL_s[k] * S + updr
        lane = lax.broadcasted_iota(jnp.int32, (1, LANES), 1)
        m_s[...] = jnp.where(lane < H_M, m_new[0], m_new[1])
        return carry

    lax.fori_loop(0, n_chunks, chunk_step, 0)

    for h in range(H_M):
        sl = slice(h * DH, (h + 1) * DH)
        hs = hf_s[:, sl] + hb_s[:, sl]
        yn = _rmsnorm(hs, nm_ref[0, :, sl])
        om = pcols(slice(None), c_om + h * DH).astype(F32)
        y_ref[:, sl] = (jax.nn.sigmoid(om) * yn).astype(BF16)
        slr = slice(W_M + h * DH, W_M + (h + 1) * DH)
        hr = hf_s[:, slr] + hb_s[:, slr]
        ynr = _rmsnorm(hr, nr_ref[0, :, sl])
        gr = pcols(slice(None), c_gr + h * DH).astype(F32)
        y_ref[:, slr] = (_silu(gr) * ynr).astype(BF16)

    if not has_state:
        for prev, out in zip(prev_refs, (C_out, n_out, m_out, S_out)):
            out[0, :n_prev] = prev[0]
        for d in range(2):
            for h in range(H_M):
                k = d * H_M + h
                C_out[0, n_prev, d, h] = CN_s[k, :, :DH]
                n_out[0, n_prev, d, h:h + 1, :] = CN_s[k, :, DH:].T[0:1, :]
                S_out[0, n_prev, d, h] = S_s[k]
        m_out[0, n_prev] = m_s[...]


def _scan_scratch(T):
    return [
        pltpu.VMEM((2 * H_M, DH, 2 * DH), F32),
        pltpu.VMEM((2 * H_R, DH, DH), F32),
        pltpu.VMEM((1, LANES), F32),
        pltpu.VMEM((T, W_M + W_R), F32),
        pltpu.VMEM((T, W_M + W_R), F32),
        pltpu.VMEM((2 * H_R, CHUNK, CHUNK), F32),
        pltpu.VMEM((2 * H_R, CHUNK, CHUNK), F32),
        pltpu.VMEM((2 * H_R, CHUNK, CHUNK), F32),
        pltpu.VMEM((2 * H_R, 1, LANES), F32),
        pltpu.VMEM((H_M + H_R, T // CHUNK, DH, CHUNK), F32),
    ]


def _scan_ctx(p, gates, dl, nm, nr, layer, prev_states=()):
    T = SEQ
    n_lay = layer + 1
    state_tails = [(2, H_M, DH, DH), (2, H_M, DH), (1, LANES), (2, H_R, DH, DH)]

    def state_spec(n, tail):
        return pl.BlockSpec((1, n) + tail, lambda b: (b,) + (0,) * (1 + len(tail)))
    common = [
        pl.BlockSpec((P_COLS // TN, T, TN), lambda b: (0, b, 0)),
        pl.BlockSpec((T, GATE_LANES), lambda b: (b, 0)),
        pl.BlockSpec((1, 2 * H_R, LANES), lambda b: (layer, 0, 0)),
        pl.BlockSpec((1, 1, W_M), lambda b: (layer, 0, 0)),
        pl.BlockSpec((1, 1, W_R), lambda b: (layer, 0, 0)),
    ]
    return pl.pallas_call(
        functools.partial(_scan_kernel, T=T, has_state=False, n_prev=layer if prev_states else 0),
        grid=(BATCH,),
        in_specs=common + [state_spec(layer, tail) for tail in state_tails[:len(prev_states)]],
        out_specs=[pl.BlockSpec((T, D_MODEL), lambda b: (b, 0))] + [state_spec(n_lay, tail) for tail in state_tails],
        out_shape=[jax.ShapeDtypeStruct((NTOK, D_MODEL), BF16)] + [
            jax.ShapeDtypeStruct((BATCH, n_lay) + tail, F32) for tail in state_tails],
        scratch_shapes=_scan_scratch(T),
        compiler_params=pltpu.CompilerParams(
            dimension_semantics=("arbitrary",), vmem_limit_bytes=VMEM_LIMIT),
        name="scan_ctx",
    )(p, gates, dl, nm, nr, *prev_states)


def _scan_lat(p, gates, dl, nm, nr, C0, n0, m0, S0, y_prev, layer):
    T = DEC_SEQ
    off = N_CTX // T
    in_specs = [
        pl.BlockSpec((P_COLS // TN, T, TN), lambda b: (0, off + b, 0)),
        pl.BlockSpec((T, GATE_LANES), lambda b: (off + b, 0)),
        pl.BlockSpec((1, 2 * H_R, LANES), lambda b: (layer, 0, 0)),
        pl.BlockSpec((1, 1, W_M), lambda b: (layer, 0, 0)),
        pl.BlockSpec((1, 1, W_R), lambda b: (layer, 0, 0)),
        pl.BlockSpec((1, 1, 2, H_M, DH, DH), lambda b: (b, layer, 0, 0, 0, 0)),
        pl.BlockSpec((1, 1, 2, H_M, DH), lambda b: (b, layer, 0, 0, 0)),
        pl.BlockSpec((1, 1, 1, LANES), lambda b: (b, layer, 0, 0)),
        pl.BlockSpec((1, 1, 2, H_R, DH, DH), lambda b: (b, layer, 0, 0, 0, 0)),
        pl.BlockSpec(memory_space=pl.ANY),
    ]
    return pl.pallas_call(
        functools.partial(_scan_kernel, T=T, has_state=True),
        grid=(DEC_BATCH,),
        in_specs=in_specs,
        out_specs=pl.BlockSpec((T, D_MODEL), lambda b: (off + b, 0)),
        out_shape=jax.ShapeDtypeStruct((NTOK, D_MODEL), BF16),
        input_output_aliases={9: 0},
        scratch_shapes=_scan_scratch(T),
        compiler_params=pltpu.CompilerParams(
            dimension_semantics=("arbitrary",), vmem_limit_bytes=VMEM_LIMIT),
        name="scan_lat",
    )(p, gates, dl, nm, nr, C0, n0, m0, S0, y_prev)


def _top2(logits):
    lane = lax.broadcasted_iota(jnp.int32, logits.shape, 1)
    v1 = jnp.max(logits, -1, keepdims=True)
    i1 = jnp.min(jnp.where(logits == v1, lane, LANES), -1, keepdims=True)
    rest = jnp.where(lane == i1, -jnp.inf, logits)
    v2 = jnp.max(rest, -1, keepdims=True)
    i2 = jnp.min(jnp.where(rest == v2, lane, LANES), -1, keepdims=True)
    e2 = jnp.exp(v2 - v1)
    return i1, i2, 1.0 / (1.0 + e2), e2 / (1.0 + e2)


def _split2(x):
    hi = x.astype(BF16)
    return hi, (x - hi.astype(F32)).astype(BF16)


def _dot_f32x3(a, b):
    a_hi, a_lo = _split2(a)
    b_hi, b_lo = _split2(b)
    return _dot(a_hi, b_hi) + _dot(a_hi, b_lo) + _dot(a_lo, b_hi)


R_E1, R_E2, R_W1, R_W2, R_S1, R_S2 = range(6)


def _out_kernel(*refs, split_x):
    y_ref = refs[0]
    if split_x:
        xp_ref, xl_ref = refs[1:3]
        x_in = jnp.where(pl.program_id(0) >= N_CTX // TM, xl_ref[...], xp_ref[...])
    else:
        x_in = refs[1][...]
    (w_ref, g_ref, mod_ref, wr_ref,
     x1_ref, h2_ref, rinfo_ref, cnt_ref, w_scr, tri_scr, cnt_scr) = refs[3:] if split_x else refs[2:]

    @pl.when(pl.program_id(0) == 0)
    def _():
        w_scr[...] = w_ref[0].astype(BF16)
        r = lax.broadcasted_iota(jnp.int32, (LANES, LANES), 0)
        c = lax.broadcasted_iota(jnp.int32, (LANES, LANES), 1)
        tri_scr[...] = (c < r).astype(BF16)
        cnt_scr[...] = jnp.zeros_like(cnt_scr)

    o = jnp.dot(y_ref[...], w_scr[...], preferred_element_type=F32)
    x1 = x_in + mod_ref[0, 0, 2:3, :] * o
    x1_ref[...] = x1
    h2 = _rmsnorm(x1, g_ref[0]) * (1.0 + mod_ref[0, 0, 4:5, :]) + mod_ref[0, 0, 3:4, :]
    h2_ref[...] = h2
    h_hi, h_lo = _split2(h2)
    w_hi, w_lo = _split2(wr_ref[0])
    t = _dot(h_hi, jnp.concatenate([w_hi, w_lo], axis=1))
    logits = t[:, :LANES] + t[:, LANES:] + _dot(h_lo, w_hi)
    lane = lax.broadcasted_iota(jnp.int32, logits.shape, 1)
    i1, i2, w1, w2 = _top2(jnp.where(lane < N_EXPERTS, logits, -jnp.inf))
    oh1 = lane == i1
    oh2 = lane == i2
    sel = jnp.where(oh1 | oh2, 1.0, 0.0)
    tri = tri_scr[...]
    run = cnt_scr[...]
    ranks = []
    for blk in range(TM // LANES):
        s_blk = sel[blk * LANES:(blk + 1) * LANES, :]
        ranks.append(_dot(tri, s_blk.astype(BF16)) + run)
        run = run + jnp.sum(s_blk, 0, keepdims=True)
    rank = jnp.concatenate(ranks, axis=0)
    r1 = jnp.sum(jnp.where(oh1, rank, 0.0), -1, keepdims=True)
    r2 = jnp.sum(jnp.where(oh2, rank, 0.0), -1, keepdims=True)
    s1 = i1.astype(F32) * float(REG) + r1
    s2 = i2.astype(F32) * float(REG) + r2
    info = jnp.zeros(logits.shape, F32)
    for col, val in ((R_E1, i1.astype(F32)), (R_E2, i2.astype(F32)), (R_W1, w1), (R_W2, w2),
                     (R_S1, s1), (R_S2, s2)):
        info = jnp.where(lane == col, val, info)
    rinfo_ref[...] = info
    cnt_scr[...] = run
    cnt_ref[...] = run


def _out(y, xs, w_out, g2, mods, layer, w_router_pad, router_idx):
    split_x = len(xs) == 2
    n_ctx_tiles = N_CTX // TM
    if split_x:
        x_specs = [pl.BlockSpec((TM, D_MODEL), lambda i: (jnp.minimum(i, n_ctx_tiles - 1), 0)),
                   pl.BlockSpec((TM, D_MODEL), lambda i: (jnp.maximum(i - n_ctx_tiles, 0), 0))]
    else:
        x_specs = [pl.BlockSpec((TM, D_MODEL), lambda i: (i, 0))]
    in_specs = [pl.BlockSpec((TM, D_MODEL), lambda i: (i, 0))] + x_specs + [
        pl.BlockSpec((1, D_MODEL, D_MODEL), lambda i: (layer, 0, 0)),
        pl.BlockSpec((1, 1, D_MODEL), lambda i: (layer, 0, 0)),
        pl.BlockSpec((1, 1, 6, D_MODEL), lambda i: (layer, _group_of_tile(i, TM), 0, 0)),
        pl.BlockSpec((1, D_MODEL, LANES), lambda i: (router_idx, 0, 0)),
    ]
    out_specs = [
        pl.BlockSpec((TM, D_MODEL), lambda i: (i, 0)),
        pl.BlockSpec((TM, D_MODEL), lambda i: (i, 0)),
        pl.BlockSpec((TM, LANES), lambda i: (i, 0)),
        pl.BlockSpec((1, LANES), lambda i: (0, 0)),
    ]
    out_shape = [
        jax.ShapeDtypeStruct((NTOK, D_MODEL), F32),
        jax.ShapeDtypeStruct((NTOK, D_MODEL), F32),
        jax.ShapeDtypeStruct((NTOK, LANES), F32),
        jax.ShapeDtypeStruct((1, LANES), F32),
    ]
    return pl.pallas_call(
        functools.partial(_out_kernel, split_x=split_x),
        grid=(NTOK // TM,),
        in_specs=in_specs,
        out_specs=out_specs,
        out_shape=out_shape,
        scratch_shapes=[pltpu.VMEM((D_MODEL, D_MODEL), BF16), pltpu.VMEM((LANES, LANES), BF16),
                        pltpu.VMEM((1, LANES), F32)],
        compiler_params=pltpu.CompilerParams(
            dimension_semantics=("arbitrary",), vmem_limit_bytes=VMEM_LIMIT),
        name="out_router",
    )(y, *xs, w_out, g2, mods, w_router_pad)


N_FC = D_FF // FC


def _n_slots(fetch):
    return fetch[3].shape[0]


def _chunk_copies(fetch, f):
    wg_hbm, wu_hbm, wd_hbm, stg_gu, stg_d, sem = fetch
    s = f % _n_slots(fetch)
    cols = pl.ds(f * FC, FC)
    return (pltpu.make_async_copy(wg_hbm.at[:, cols], stg_gu.at[s, 0], sem.at[s, 0]),
            pltpu.make_async_copy(wu_hbm.at[:, cols], stg_gu.at[s, 1], sem.at[s, 1]),
            pltpu.make_async_copy(wd_hbm.at[cols, :], stg_d.at[s], sem.at[s, 2]))


def _start_first_chunks(fetch):
    for f in range(min(_n_slots(fetch), N_FC)):
        for c in _chunk_copies(fetch, f):
            c.start()


def _swiglu_tile(h, w_gu, w_d, acc, fetch=None, first_chunks_started=None):
    n_slots = 0
    if fetch is not None:
        stg_gu, stg_d = fetch[3], fetch[4]
        n_slots = _n_slots(fetch)

        def copies(f):
            return _chunk_copies(fetch, f)

        def start(f):
            for c in copies(f):
                c.start()

        def land(f):
            for c in copies(f):
                c.wait()
            s = f % n_slots
            w_gu[f, :, :FC] = stg_gu[s, 0].astype(BF16)
            w_gu[f, :, FC:] = stg_gu[s, 1].astype(BF16)
            w_d[f * FC:(f + 1) * FC, :] = stg_d[s].astype(BF16)
    else:
        start = land = lambda f: None

    def up(f):
        return jnp.dot(h, w_gu[f], preferred_element_type=F32)

    if fetch is not None:
        if first_chunks_started is None:
            _start_first_chunks(fetch)
        else:
            pl.when(jnp.logical_not(first_chunks_started))(lambda: _start_first_chunks(fetch))
    land(0)
    ab = up(0)
    for f in range(N_FC):
        if fetch is not None and f + n_slots < N_FC:
            start(f + n_slots)
        if f + 1 < N_FC:
            land(f + 1)
            ab_next = up(f + 1)
        t = (_silu(ab[:, :FC]) * ab[:, FC:]).astype(BF16)
        contrib = jnp.dot(t, w_d[f * FC:(f + 1) * FC, :], preferred_element_type=F32)
        if f == 0:
            acc[...] = contrib
        else:
            acc[...] += contrib
        if f + 1 < N_FC:
            ab = ab_next


def _ffn_weight_scratch(n_slots=2):
    return [
        pltpu.VMEM((N_FC, D_MODEL, 2 * FC), BF16),
        pltpu.VMEM((D_FF, D_MODEL), BF16),
        pltpu.VMEM((n_slots, 2, D_MODEL, FC), F32),
        pltpu.VMEM((n_slots, FC, D_MODEL), F32),
        pltpu.SemaphoreType.DMA((n_slots, 3)),
    ]


def _out_ffn_kernel(*refs, w_idx, split_x):
    y_ref = refs[0]
    if split_x:
        xp_ref, xl_ref = refs[1:3]
        x_in = jnp.where(pl.program_id(0) >= N_CTX // TM_F, xl_ref[...], xp_ref[...])
    else:
        x_in = refs[1][...]
    (wo_ref, g_ref, mod_ref, wg_ref, wu_ref, wd_ref, o_ref,
     wo_scr, h_scr, acc, w_gu, w_d, stg_gu, stg_d, sem) = refs[3:] if split_x else refs[2:]
    first = pl.program_id(0) == 0

    @pl.when(first)
    def _():
        wo_scr[...] = wo_ref[0].astype(BF16)

    x1 = x_in + mod_ref[0, 0, 2:3, :] * jnp.dot(y_ref[...], wo_scr[...], preferred_element_type=F32)
    h2 = _rmsnorm(x1, g_ref[0]) * (1.0 + mod_ref[0, 0, 4:5, :]) + mod_ref[0, 0, 3:4, :]
    h_scr[...] = h2.astype(BF16)
    o_ref[...] = x1

    @pl.when(first)
    def _():
        _swiglu_tile(h_scr[...], w_gu, w_d, acc,
                     (wg_ref.at[w_idx], wu_ref.at[w_idx], wd_ref.at[w_idx], stg_gu, stg_d, sem))

    @pl.when(jnp.logical_not(first))
    def _():
        _swiglu_tile(h_scr[...], w_gu, w_d, acc)

    o_ref[...] = o_ref[...] + mod_ref[0, 0, 5:6, :] * acc[...]


def _out_ffn(y, xs, w_out, g2, wg, wu, wd, mods, layer, w_idx):
    split_x = len(xs) == 2
    n_ctx_tiles = N_CTX // TM_F
    if split_x:
        x_specs = [pl.BlockSpec((TM_F, D_MODEL), lambda i: (jnp.minimum(i, n_ctx_tiles - 1), 0)),
                   pl.BlockSpec((TM_F, D_MODEL), lambda i: (jnp.maximum(i - n_ctx_tiles, 0), 0))]
    else:
        x_specs = [pl.BlockSpec((TM_F, D_MODEL), lambda i: (i, 0))]
    return pl.pallas_call(
        functools.partial(_out_ffn_kernel, w_idx=w_idx, split_x=split_x),
        grid=(NTOK // TM_F,),
        in_specs=[pl.BlockSpec((TM_F, D_MODEL), lambda i: (i, 0))] + x_specs + [
            pl.BlockSpec((1, D_MODEL, D_MODEL), lambda i: (layer, 0, 0), pipeline_mode=pl.Buffered(1)),
            pl.BlockSpec((1, 1, D_MODEL), lambda i: (layer, 0, 0)),
            pl.BlockSpec((1, 1, 6, D_MODEL), lambda i: (layer, _group_of_tile(i, TM_F), 0, 0)),
            pl.BlockSpec(memory_space=pl.ANY),
            pl.BlockSpec(memory_space=pl.ANY),
            pl.BlockSpec(memory_space=pl.ANY),
        ],
        out_specs=pl.BlockSpec((TM_F, D_MODEL), lambda i: (i, 0)),
        out_shape=jax.ShapeDtypeStruct((NTOK, D_MODEL), F32),
        scratch_shapes=[pltpu.VMEM((D_MODEL, D_MODEL), BF16),
                        pltpu.VMEM((TM_F, D_MODEL), BF16),
                        pltpu.VMEM((TM_F, D_MODEL), F32)] + _ffn_weight_scratch(),
        compiler_params=pltpu.CompilerParams(
            dimension_semantics=("arbitrary",), vmem_limit_bytes=VMEM_LIMIT),
        name="out_ffn",
    )(y, *xs, w_out, g2, mods, wg, wu, wd)


def _tile_plan(counts):
    nt = (counts + TR - 1) // TR
    cum = jnp.cumsum(nt)
    total = cum[-1]
    t = jnp.arange(MAX_TILES, dtype=jnp.int32)
    tt = jnp.minimum(t, total - 1)
    e = jnp.sum((cum[None, :] <= tt[:, None]).astype(jnp.int32), axis=1)
    k = tt - (cum - nt)[e]
    n = jnp.where(t < total, jnp.clip(counts[e] - k * TR, 0, TR), 0)
    return e.astype(jnp.int32), (e * REG_TILES + k).astype(jnp.int32), n.astype(jnp.int32)


def _row_copy(src, src_row, dst, dst_row, sem):
    return pltpu.make_async_copy(src.at[pl.ds(src_row, 1)], dst.at[pl.ds(dst_row, 1)], sem)


def _dispatch_kernel(slot_ref, h_ref, xs_ref, sem):
    base = pl.program_id(0) * (TOP_K * TD)

    def issue(r, carry):
        for k in range(TOP_K):
            _row_copy(h_ref, r, xs_ref, slot_ref[base + TOP_K * r + k], sem).start()
        return carry

    lax.fori_loop(0, TD, issue, 0, unroll=8)
    for k in range(TOP_K):
        pltpu.make_async_copy(h_ref, xs_ref.at[pl.ds(0, TD)], sem).wait()


def _dispatch(slots, h2f):
    return pl.pallas_call(
        _dispatch_kernel,
        grid_spec=pltpu.PrefetchScalarGridSpec(
            num_scalar_prefetch=1,
            grid=(NTOK // TD,),
            in_specs=[pl.BlockSpec((TD, D_MODEL), lambda i, s: (i, 0))],
            out_specs=pl.BlockSpec(memory_space=pl.ANY),
            scratch_shapes=[pltpu.SemaphoreType.DMA],
        ),
        out_shape=jax.ShapeDtypeStruct((N_EXPERTS * REG, D_MODEL), F32),
        compiler_params=pltpu.CompilerParams(
            dimension_semantics=("arbitrary",), vmem_limit_bytes=VMEM_LIMIT),
        name="moe_dispatch",
    )(slots, h2f)


def _gffn_kernel(te_ref, tb_ref, tn_ref, x_ref, wg_ref, wu_ref, wd_ref, o_ref,
                 w_gu, w_d, stg_gu, stg_d, sem, *, w_base):
    t = pl.program_id(0)
    n = tn_ref[t]

    @pl.when(n > 0)
    def _():
        row = lax.broadcasted_iota(jnp.int32, (TR, D_MODEL), 0)
        h = jnp.where(row < n, x_ref[...], 0.0).astype(BF16)
        def is_first(tt):
            return tb_ref[tt] % REG_TILES == 0

        def fetch_of(tt):
            e = w_base + te_ref[tt]
            return (wg_ref.at[e], wu_ref.at[e], wd_ref.at[e], stg_gu, stg_d, sem)

        first = is_first(t)
        t_prev = jnp.maximum(t - 1, 0)
        t_next = jnp.minimum(t + 1, MAX_TILES - 1)

        @pl.when(first)
        def _():
            started = (t > 0) & jnp.logical_not(is_first(t_prev))
            _swiglu_tile(h, w_gu, w_d, o_ref, fetch_of(t), started)

        @pl.when(jnp.logical_not(first))
        def _():
            @pl.when((t + 1 < MAX_TILES) & (tn_ref[t_next] > 0) & is_first(t_next))
            def _():
                _start_first_chunks(fetch_of(t_next))

            _swiglu_tile(h, w_gu, w_d, o_ref)


def _gffn(tile_e, tile_blk, tile_n, xs, wg, wu, wd, w_base):
    return pl.pallas_call(
        functools.partial(_gffn_kernel, w_base=w_base),
        grid_spec=pltpu.PrefetchScalarGridSpec(
            num_scalar_prefetch=3,
            grid=(MAX_TILES,),
            in_specs=[
                pl.BlockSpec((TR, D_MODEL), lambda t, te, tb, tn: (tb[t], 0)),
                pl.BlockSpec(memory_space=pl.ANY),
                pl.BlockSpec(memory_space=pl.ANY),
                pl.BlockSpec(memory_space=pl.ANY),
            ],
            out_specs=pl.BlockSpec((TR, D_MODEL), lambda t, te, tb, tn: (tb[t], 0)),
            scratch_shapes=_ffn_weight_scratch(n_slots=4),
        ),
        out_shape=jax.ShapeDtypeStruct((N_EXPERTS * REG, D_MODEL), F32),
        compiler_params=pltpu.CompilerParams(
            dimension_semantics=("arbitrary",), vmem_limit_bytes=VMEM_LIMIT),
        name="moe_ffn",
    )(tile_e, tile_blk, tile_n, xs, wg, wu, wd)


def _combine_kernel(slot_ref, x1_ref, rinfo_ref, mod_ref, ys_ref, *rest, final_norm):
    if final_norm:
        gf_ref, yp_ref, yl_ref, buf, sem = rest
    else:
        o_ref, buf, sem = rest
    i = pl.program_id(0)
    n_steps = pl.num_programs(0)

    def gather(step, s):
        base = step * (TOP_K * TD)

        def issue(r, carry):
            for k in range(TOP_K):
                _row_copy(ys_ref, slot_ref[base + TOP_K * r + k], buf.at[s, k], r, sem.at[s]).start()
            return carry

        lax.fori_loop(0, TD, issue, 0, unroll=8)

    def wait(s):
        for k in range(TOP_K):
            pltpu.make_async_copy(ys_ref.at[pl.ds(0, TD)], buf.at[s, k], sem.at[s]).wait()

    @pl.when(i == 0)
    def _():
        gather(0, 0)

    for s in range(2):
        @pl.when(i % 2 == s)
        def _():
            @pl.when(i + 1 < n_steps)
            def _():
                gather(i + 1, 1 - s)

            wait(s)

    b = buf.at[i % 2]
    y = rinfo_ref[:, R_W1:R_W1 + 1] * b[0] + rinfo_ref[:, R_W2:R_W2 + 1] * b[1]
    x = x1_ref[...] + mod_ref[0, 0, 5:6, :] * y
    if final_norm:
        out = _rmsnorm(x, gf_ref[...])
        is_lat = pl.program_id(0) >= N_CTX // TD

        @pl.when(jnp.logical_not(is_lat))
        def _():
            yp_ref[...] = out

        @pl.when(is_lat)
        def _():
            yl_ref[...] = out
    else:
        o_ref[...] = x


def _combine(slots, x1, rinfo, mods, ys, layer, norm_f=None):
    final_norm = norm_f is not None
    n_ctx_t = N_CTX // TD
    in_specs = [
        pl.BlockSpec((TD, D_MODEL), lambda i, s: (i, 0)),
        pl.BlockSpec((TD, LANES), lambda i, s: (i, 0)),
        pl.BlockSpec((1, 1, 6, D_MODEL), lambda i, s: (layer, _group_of_tile(i, TD), 0, 0)),
        pl.BlockSpec(memory_space=pl.ANY),
    ]
    args = [slots, x1, rinfo, mods, ys]
    if final_norm:
        in_specs.append(pl.BlockSpec((1, D_MODEL), lambda i, s: (0, 0)))
        args.append(norm_f)
        out_specs = [pl.BlockSpec((TD, D_MODEL), lambda i, s: (jnp.minimum(i, n_ctx_t - 1), 0)),
                     pl.BlockSpec((TD, D_MODEL), lambda i, s: (jnp.maximum(i - n_ctx_t, 0), 0))]
        out_shape = [jax.ShapeDtypeStruct((N_CTX, D_MODEL), F32), jax.ShapeDtypeStruct((N_LAT, D_MODEL), F32)]
    else:
        out_specs = pl.BlockSpec((TD, D_MODEL), lambda i, s: (i, 0))
        out_shape = jax.ShapeDtypeStruct((NTOK, D_MODEL), F32)
    return pl.pallas_call(
        functools.partial(_combine_kernel, final_norm=final_norm),
        grid_spec=pltpu.PrefetchScalarGridSpec(
            num_scalar_prefetch=1,
            grid=(NTOK // TD,),
            in_specs=in_specs,
            out_specs=out_specs,
            scratch_shapes=[pltpu.VMEM((2, TOP_K, TD, D_MODEL), F32), pltpu.SemaphoreType.DMA((2,))],
        ),
        out_shape=out_shape,
        compiler_params=pltpu.CompilerParams(
            dimension_semantics=("arbitrary",), vmem_limit_bytes=VMEM_LIMIT),
        name="moe_combine",
    )(*args)


def _final_kernel(x_ref, g_ref, o_ref):
    o_ref[...] = _rmsnorm(x_ref[...], g_ref[...])


def _final(x, g, row_off, rows):
    off = row_off // TM
    return pl.pallas_call(
        _final_kernel,
        grid=(rows // TM,),
        in_specs=[
            pl.BlockSpec((TM, D_MODEL), lambda i: (off + i, 0)),
            pl.BlockSpec((1, D_MODEL), lambda i: (0, 0)),
        ],
        out_specs=pl.BlockSpec((TM, D_MODEL), lambda i: (i, 0)),
        out_shape=jax.ShapeDtypeStruct((rows, D_MODEL), F32),
        compiler_params=pltpu.CompilerParams(
            dimension_semantics=("arbitrary",), vmem_limit_bytes=VMEM_LIMIT),
        name="final_norm",
    )(x, g)


def kernel(x_prompt, x_sample, state_mlstm_C, state_mlstm_n, state_mlstm_m, state_ret_S, c, c_ctx,
           norm1_g, norm2_g, norm_f_g, w_ada, b_ada, w_in, b_gates, ret_decay_logit,
           mlstm_norm_g, ret_norm_g, w_out, ffn_w_gate, ffn_w_up, ffn_w_down,
           moe_w_router, moe_w_gate, moe_w_up, moe_w_down):
    xs_in = (x_prompt.reshape(N_CTX, D_MODEL), x_sample.reshape(N_LAT, D_MODEL))
    cvec = jnp.concatenate(
        [c_ctx[None, :], c, jnp.zeros((N_GROUPS - 1 - DEC_BATCH, D_MODEL), F32)], 0)
    mods = _ada(cvec, w_ada, b_ada).reshape(DEPTH, N_GROUPS, 6, D_MODEL)

    n_m = 4 * W_M
    w_in_t = jnp.swapaxes(w_in, 1, 2)
    n_if = N_GATES // 2
    lane_pad = ((0, 0), (0, 0), (0, LANES - n_if))
    wg = jnp.concatenate([jnp.pad(w_in[:, :, n_m:n_m + n_if], lane_pad),
                          jnp.pad(w_in[:, :, n_m + n_if:n_m + N_GATES], lane_pad)], -1)
    bg = jnp.concatenate([jnp.pad(b_gates[:, None, :n_if], lane_pad),
                          jnp.pad(b_gates[:, None, n_if:], lane_pad)], -1)
    cos_np, sin_np = _rope_tables()
    cos_t, sin_t = jnp.asarray(cos_np), jnp.asarray(sin_np)
    dl = jnp.broadcast_to(ret_decay_logit.reshape(DEPTH, 2 * H_R, 1), (DEPTH, 2 * H_R, LANES))
    m0 = jnp.pad(state_mlstm_m.reshape(DEC_BATCH, DEPTH, 1, 2 * H_M),
                 ((0, 0), (0, 0), (0, 0), (0, LANES - 2 * H_M)))
    g1 = norm1_g.reshape(DEPTH, 1, D_MODEL)
    g2 = norm2_g.reshape(DEPTH, 1, D_MODEL)
    nm = mlstm_norm_g.reshape(DEPTH, 1, W_M)
    nr = ret_norm_g.reshape(DEPTH, 1, W_R)
    n_moe = moe_w_router.shape[0]
    wr_pad = jnp.pad(moe_w_router, ((0, 0), (0, 0), (0, LANES - N_EXPERTS)))
    moe_g = moe_w_gate.reshape(n_moe * N_EXPERTS, D_MODEL, D_FF)
    moe_u = moe_w_up.reshape(n_moe * N_EXPERTS, D_MODEL, D_FF)
    moe_d = moe_w_down.reshape(n_moe * N_EXPERTS, D_FF, D_MODEL)

    states = ()
    xs = xs_in
    for l in range(DEPTH):
        jl = l // 2
        p, gates = _proj(xs, g1, mods, w_in_t, wg, bg, cos_t, sin_t, l)
        y, *states = _scan_ctx(p, gates, dl, nm, nr, l, states)
        y = _scan_lat(p, gates, dl, nm, nr, state_mlstm_C, state_mlstm_n, m0, state_ret_S, y, l)
        if l % 2 == 0:
            x = _out_ffn(y, xs, w_out, g2, ffn_w_gate, ffn_w_up, ffn_w_down, mods, l, jl)
        else:
            x1, h2f, rinfo, cnt = _out(y, xs, w_out, g2, mods, l, wr_pad, jl)
            slots = rinfo[:, R_S1:R_S2 + 1].astype(jnp.int32).reshape(TOP_K * NTOK)
            tile_e, tile_blk, tile_n = _tile_plan(cnt[0, :N_EXPERTS].astype(jnp.int32))
            xd = _dispatch(slots, h2f)
            yd = _gffn(tile_e, tile_blk, tile_n, xd, moe_g, moe_u, moe_d, jl * N_EXPERTS)
            if l == DEPTH - 1:
                y_ctx, y_lat = _combine(slots, x1, rinfo, mods, yd, l, norm_f_g.reshape(1, D_MODEL))
            else:
                x = _combine(slots, x1, rinfo, mods, yd, l)
        xs = (x,)

    if DEPTH % 2 == 1:
        y_ctx = _final(x, norm_f_g.reshape(1, D_MODEL), 0, N_CTX)
        y_lat = _final(x, norm_f_g.reshape(1, D_MODEL), N_CTX, N_LAT)
    y_prompt = y_ctx.reshape(BATCH, SEQ, D_MODEL)
    y_sample = y_lat.reshape(DEC_BATCH, DEC_SEQ, D_MODEL)
    new_C, new_n, new_m, new_S = states
    return (y_prompt, y_sample, new_C, new_n,
            new_m[:, :, 0, :2 * H_M].reshape(BATCH, DEPTH, 2, H_M), new_S)
```

```python
import functools

import numpy as np
import jax
import jax.numpy as jnp
from jax import lax
from jax.experimental import pallas as pl
from jax.experimental.pallas import tpu as pltpu

D_MODEL = 1024
BATCH = 32
SEQ = 256
DEPTH = 2
DEC_BATCH = 2
DEC_SEQ = 1024
GRID_W = 64
H_M = 4
DH = 128
H_R = 4
W_M = H_M * DH
W_R = H_R * DH
N_GATES = 4 * H_M
CHUNK = 128
D_FF = 2816
N_EXPERTS = 8
ROPE_BASE = 10000.0
EPS = 1e-6

N_CTX = BATCH * SEQ
N_LAT = DEC_BATCH * DEC_SEQ
NTOK = N_CTX + N_LAT
N_GROUPS = 8
K_SCALE = DH ** -0.5
P_COLS = 4 * W_M + 4 * W_R
LANES = 128
SUBLANES = 8
GATE_LANES = 2 * LANES
VMEM_LIMIT = 56 * 1024 * 1024

F32 = jnp.float32
BF16 = jnp.bfloat16

TM = 1024
TN = 1024
FC = 256
TM_F = 512
TOP_K = 2
TR = 896
REG_TILES = -(-NTOK // TR)
REG = REG_TILES * TR
MAX_TILES = -(-TOP_K * NTOK // TR) + N_EXPERTS
TD = 1024
SCORE_AHEAD = 3


def _group_of_tile(i, tm):
    return jnp.maximum(i * tm // DEC_SEQ - (N_CTX // DEC_SEQ - 1), 0)


def _silu(x):
    return x * jax.nn.sigmoid(x)


def _log_sigmoid(x):
    return jnp.minimum(x, 0.0) - jnp.log(1.0 + jnp.exp(-jnp.abs(x)))


def _rmsnorm(x, g):
    return x * lax.rsqrt(jnp.mean(x * x, -1, keepdims=True) + EPS) * g


def _ada_kernel(cv_ref, w_ref, b_ref, o_ref):
    s = _silu(cv_ref[...]).astype(BF16)
    o_ref[0] = jnp.dot(s, w_ref[0].astype(BF16), preferred_element_type=F32) + b_ref[0]


def _ada(cvec, w_ada, b_ada):
    tn = 1536
    n = 6 * D_MODEL
    return pl.pallas_call(
        _ada_kernel,
        grid=(DEPTH, n // tn),
        in_specs=[
            pl.BlockSpec((N_GROUPS, D_MODEL), lambda l, j: (0, 0)),
            pl.BlockSpec((1, D_MODEL, tn), lambda l, j: (l, 0, j)),
            pl.BlockSpec((1, 1, tn), lambda l, j: (l, 0, j)),
        ],
        out_specs=pl.BlockSpec((1, N_GROUPS, tn), lambda l, j: (l, 0, j)),
        out_shape=jax.ShapeDtypeStruct((DEPTH, N_GROUPS, n), F32),
        compiler_params=pltpu.CompilerParams(
            dimension_semantics=("arbitrary", "arbitrary"), vmem_limit_bytes=VMEM_LIMIT),
        name="ada",
    )(cvec, w_ada, b_ada.reshape(DEPTH, 1, n))


def _rope_tables():
    half = DH // 4
    freqs = ROPE_BASE ** (-np.arange(half, dtype=np.float64) / half)
    t = np.arange(DEC_SEQ)
    pos = np.stack([t // GRID_W, t % GRID_W], 1).astype(np.float64)
    d = np.arange(DH)
    ang = pos[:, d // (DH // 2)] * freqs[d % half][None, :]
    sign = np.where((d % (DH // 2)) < half, -1.0, 1.0)[None, :]
    return np.cos(ang).astype(np.float32), (sign * np.sin(ang)).astype(np.float32)


def _rope(a, cos, sin):
    lane = lax.broadcasted_iota(jnp.int32, a.shape, 1)
    first = (lane % (DH // 2)) < (DH // 4)
    partner = jnp.where(first, pltpu.roll(a, DH - DH // 4, 1), pltpu.roll(a, DH // 4, 1))
    return a * cos + partner * sin


def _proj_kernel(*refs, n_ctx_tiles, split_x, layer):
    if split_x:
        (xp_ref, xl_ref, g_ref, mod_ref, wt_ref, wg_ref, bg_ref, cos_ref, sin_ref,
         p_ref, gate_ref, h_scr, w_res, w_stg, w_sem) = refs
    else:
        (x_ref, g_ref, mod_ref, wt_ref, wg_ref, bg_ref, cos_ref, sin_ref,
         p_ref, gate_ref, h_scr, w_res, w_stg, w_sem) = refs
    p_ref = p_ref.at[0]
    i = pl.program_id(0)
    j = pl.program_id(1)
    is_lat = i >= n_ctx_tiles
    half = TN // 2

    def prologue(x):
        h = _rmsnorm(x, g_ref[0]) * (1.0 + mod_ref[0, 0, 1:2, :]) + mod_ref[0, 0, 0:1, :]
        h_scr[...] = h.astype(BF16)
        gate_ref[...] = _dot_f32x3(h, wg_ref[0]) + bg_ref[0]

    @pl.when(j == 0)
    def _():
        if split_x:
            pl.when(jnp.logical_not(is_lat))(lambda: prologue(xp_ref[...]))
            pl.when(is_lat)(lambda: prologue(xl_ref[...]))
        else:
            prologue(x_ref[...])

    n_col_tiles = P_COLS // TN

    def tile_copy(jj):
        row0 = jj * TN + (N_GATES if jj * TN >= 4 * W_M else 0)
        s = jj % 2
        return pltpu.make_async_copy(wt_ref.at[layer, pl.ds(row0, TN), :], w_stg.at[s], w_sem.at[s])

    def matmul(jj):
        @pl.when(i == 0)
        def _():
            if jj == 0:
                tile_copy(0).start()
            if jj + 1 < n_col_tiles:
                tile_copy(jj + 1).start()
            tile_copy(jj).wait()
            w_res[jj] = w_stg[jj % 2].T.astype(BF16)

        return _dot(h_scr[...], w_res[jj])

    @pl.when(j == 0)
    def _():
        acc = matmul(0)
        p_ref[:, :half] = acc[:, :half].astype(BF16)
        p_ref[:, half:] = (acc[:, half:] * K_SCALE).astype(BF16)

    @pl.when(j == 1)
    def _():
        p_ref[...] = matmul(1).astype(BF16)

    @pl.when(j == 2)
    def _():
        acc = matmul(2)

        @pl.when(is_lat)
        def _():
            cos = cos_ref[...]
            sin = sin_ref[...]
            for hd in range(TN // DH):
                sl = slice(hd * DH, (hd + 1) * DH)
                r = _rope(acc[:, sl], cos, sin)
                p_ref[:, sl] = (r * K_SCALE if hd * DH >= half else r).astype(BF16)

        @pl.when(jnp.logical_not(is_lat))
        def _():
            p_ref[:, :half] = acc[:, :half].astype(BF16)
            p_ref[:, half:] = (acc[:, half:] * K_SCALE).astype(BF16)

    @pl.when(j == 3)
    def _():
        p_ref[...] = matmul(3).astype(BF16)


def _proj(xs, g1, mods, w_in_t, wg, bg, cos_t, sin_t, layer):
    n_ctx_tiles = N_CTX // TM
    tiles_per_seq = DEC_SEQ // TM
    split_x = len(xs) == 2
    if split_x:
        x_specs = [pl.BlockSpec((TM, D_MODEL), lambda i, j: (jnp.minimum(i, n_ctx_tiles - 1), 0)),
                   pl.BlockSpec((TM, D_MODEL), lambda i, j: (jnp.maximum(i - n_ctx_tiles, 0), 0))]
    else:
        x_specs = [pl.BlockSpec((TM, D_MODEL), lambda i, j: (i, 0))]
    return pl.pallas_call(
        functools.partial(_proj_kernel, n_ctx_tiles=n_ctx_tiles, split_x=split_x, layer=layer),
        grid=(NTOK // TM, P_COLS // TN),
        in_specs=x_specs + [
            pl.BlockSpec((1, 1, D_MODEL), lambda i, j: (layer, 0, 0)),
            pl.BlockSpec((1, 1, 6, D_MODEL), lambda i, j: (layer, _group_of_tile(i, TM), 0, 0)),
            pl.BlockSpec(memory_space=pl.ANY),
            pl.BlockSpec((1, D_MODEL, GATE_LANES), lambda i, j: (layer, 0, 0)),
            pl.BlockSpec((1, 1, GATE_LANES), lambda i, j: (layer, 0, 0)),
            pl.BlockSpec((TM, DH), lambda i, j: (i % tiles_per_seq, 0)),
            pl.BlockSpec((TM, DH), lambda i, j: (i % tiles_per_seq, 0)),
        ],
        out_specs=[
            pl.BlockSpec((1, TM, TN), lambda i, j: (j, i, 0)),
            pl.BlockSpec((TM, GATE_LANES), lambda i, j: (i, 0)),
        ],
        out_shape=[
            jax.ShapeDtypeStruct((P_COLS // TN, NTOK, TN), BF16),
            jax.ShapeDtypeStruct((NTOK, GATE_LANES), F32),
        ],
        scratch_shapes=[pltpu.VMEM((TM, D_MODEL), BF16),
                        pltpu.VMEM((P_COLS // TN, D_MODEL, TN), BF16),
                        pltpu.VMEM((2, TN, D_MODEL), F32),
                        pltpu.SemaphoreType.DMA((2,))],
        compiler_params=pltpu.CompilerParams(
            dimension_semantics=("arbitrary", "arbitrary"), vmem_limit_bytes=VMEM_LIMIT),
        name="proj",
    )(*xs, g1, mods, w_in_t, wg, bg, cos_t, sin_t)


def _split3(x):
    hi = x.astype(BF16)
    r1 = x - hi.astype(F32)
    mid = r1.astype(BF16)
    lo = (r1 - mid.astype(F32)).astype(BF16)
    return hi, mid, lo


def _dot(a, b):
    return jnp.dot(a, b, preferred_element_type=F32)


def _dot_nt(a, b):
    return lax.dot_general(a, b, (((1,), (1,)), ((), ())), preferred_element_type=F32)


def _tri_dot_left(tri, x):
    hi, mid, lo = _split3(x)
    return _dot(tri, hi) + _dot(tri, mid) + _dot(tri, lo)


def _run_max(x, reverse):
    n_tiles = x.shape[0] // SUBLANES
    sub = lax.broadcasted_iota(jnp.int32, (SUBLANES, LANES), 0)
    out = [None] * n_tiles
    carry = None
    for t in (range(n_tiles - 1, -1, -1) if reverse else range(n_tiles)):
        v = x[SUBLANES * t:SUBLANES * (t + 1), :]
        s = 1
        while s < SUBLANES:
            if reverse:
                v = jnp.maximum(v, jnp.where(sub < SUBLANES - s, pltpu.roll(v, SUBLANES - s, 0), -jnp.inf))
            else:
                v = jnp.maximum(v, jnp.where(sub >= s, pltpu.roll(v, s, 0), -jnp.inf))
            s *= 2
        if carry is not None:
            v = jnp.maximum(v, carry)
        carry = jnp.broadcast_to(v[0:1, :] if reverse else v[SUBLANES - 1:SUBLANES, :], (SUBLANES, LANES))
        out[t] = v
    return jnp.concatenate(out, axis=0)


def _scan_kernel(*refs, T, has_state, n_prev=0):
    if has_state:
        (p_ref, g_ref, dl_ref, nm_ref, nr_ref, C0_ref, n0_ref, m0_ref, S0_ref, _yprev_ref,
         y_ref, CN_s, S_s, m_s, hf_s, hb_s, dm_s, dq_s, dk_s, dL_s, kT_s) = refs
    else:
        p_ref, g_ref, dl_ref, nm_ref, nr_ref = refs[:5]
        prev_refs = refs[5:9] if n_prev else ()
        (y_ref, C_out, n_out, m_out, S_out,
         CN_s, S_s, m_s, hf_s, hb_s, dm_s, dq_s, dk_s, dL_s, kT_s) = refs[5 + len(prev_refs):]
    L = CHUNK
    n_chunks = T // L
    row_i = lax.broadcasted_iota(jnp.int32, (L, L), 0)
    col_j = lax.broadcasted_iota(jnp.int32, (L, L), 1)
    lower = col_j <= row_i
    upper = col_j >= row_i
    tril = lower.astype(BF16)
    triu = upper.astype(BF16)
    ones = jnp.ones((L, DH), BF16)
    c_km = W_M
    c_vm = 2 * W_M
    c_om = 3 * W_M
    c_qr = 4 * W_M
    c_kr = c_qr + W_R
    c_vr = c_qr + 2 * W_R
    c_gr = c_qr + 3 * W_R

    def pcols(rows, col):
        return p_ref[col // TN, rows, col % TN:col % TN + DH]

    for d in range(2):
        for h in range(H_M):
            k = d * H_M + h
            if has_state:
                CN_s[k, :, :DH] = C0_ref[0, 0, d, h]
                CN_s[k, :, DH:] = jnp.broadcast_to(n0_ref[0, 0, d, h:h + 1, :], (DH, DH)).T
                S_s[k] = S0_ref[0, 0, d, h]
            else:
                CN_s[k] = jnp.zeros((DH, 2 * DH), F32)
                S_s[k] = jnp.zeros((DH, DH), F32)
    m_s[...] = m0_ref[0, 0] if has_state else jnp.zeros((1, LANES), F32)

    @pl.when(pl.program_id(0) == 0)
    def _():
        pos_i = row_i.astype(F32)
        pos_j = col_j.astype(F32)
        for d in range(2):
            for h in range(H_R):
                k = d * H_R + h
                lg_row = _log_sigmoid(dl_ref[0, k:k + 1, :])
                lg = jnp.broadcast_to(lg_row, (L, L))
                rel = (row_i - col_j if d == 0 else col_j - row_i).astype(F32)
                dm_s[k] = jnp.where(rel >= 0, jnp.exp(lg * jnp.maximum(rel, 0.0)), 0.0)
                dq_s[k] = jnp.exp(lg * (pos_i + 1.0 if d == 0 else L - pos_i))
                dk_s[k] = jnp.exp(lg * (L - 1.0 - pos_j if d == 0 else pos_j))
                dL_s[k] = jnp.exp(lg_row * float(L))

    def transpose_keys(c, carry):
        r0 = pl.multiple_of(c * L, L)
        for h in range(H_M):
            kT_s[h, c] = pcols(pl.ds(r0, L), c_km + h * DH).astype(F32).T
            kT_s[H_M + h, c] = pcols(pl.ds(r0, L), c_kr + h * DH).astype(F32).T
        return carry

    lax.fori_loop(0, n_chunks, transpose_keys, 0)

    def chunk_step(c, carry):
        m_prev = m_s[...]
        m_new = []
        prep = []
        for d in range(2):
            ci = c if d == 0 else n_chunks - 1 - c
            r0 = pl.multiple_of(ci * L, L)
            mask = lower if d == 0 else upper
            e_row = L - 1 if d == 0 else 0
            FL = _log_sigmoid(g_ref[pl.ds(r0, L), LANES:2 * LANES])
            Bc = _tri_dot_left(tril if d == 0 else triu, FL)
            Zc = g_ref[pl.ds(r0, L), 0:LANES] - Bc
            M = jnp.maximum(_run_max(Zc, reverse=(d == 1)), m_prev)
            m_row = Bc + M
            M_end = M[e_row:e_row + 1, :]
            m_new.append(Bc[e_row:e_row + 1, :] + M_end)
            decay = jnp.exp(m_prev - M_end)
            prep.append(dict(ci=ci, r0=r0, mask=mask, M=M, m_row=m_row, decay=decay,
                             ZT=Zc.T,
                             WT=jnp.exp(Zc - M_end).T))
        pairs = [(d, h) for d in range(2) for h in range(H_M)]

        def rows(d, col):
            return pcols(pl.ds(prep[d]["r0"], L), col)

        qk, qkr = {}, {}

        def scores(idx):
            if idx < len(pairs):
                d, h = pairs[idx]
                qk[d, h] = _dot_nt(rows(d, h * DH), rows(d, c_km + h * DH))
                qkr[d, h] = _dot_nt(rows(d, c_qr + h * DH), rows(d, c_kr + h * DH))

        for idx in range(SCORE_AHEAD):
            scores(idx)
        for idx, (d, h) in enumerate(pairs):
            scores(idx + SCORE_AHEAD)
            k = d * H_M + h
            ci = prep[d]["ci"]
            r0 = prep[d]["r0"]
            h_dst = hf_s if d == 0 else hb_s
            vo = jnp.concatenate([rows(d, c_vm + h * DH), ones], axis=1)
            wkT = (kT_s[h, ci] * jnp.broadcast_to(prep[d]["WT"][k:k + 1, :], (DH, L))).astype(BF16)
            upd = _dot(wkT, vo)
            kdT = (kT_s[H_M + h, ci] * dk_s[k]).astype(BF16)
            updr = _dot(kdT, rows(d, c_vr + h * DH))
            q = rows(d, h * DH)
            M_col = jnp.broadcast_to(prep[d]["M"][:, k:k + 1], (L, L))
            z_row = jnp.broadcast_to(prep[d]["ZT"][k:k + 1, :], (L, L))
            D = jnp.where(prep[d]["mask"], jnp.exp(z_row - M_col), 0.0)
            s = (qk[d, h] * D).astype(BF16)
            w_inter = jnp.exp(jnp.broadcast_to(m_prev[:, k:k + 1], (L, L)) - M_col)
            wq = (w_inter * q.astype(F32)).astype(BF16)
            CN = CN_s[k]
            res = _dot(jnp.concatenate([s, wq], axis=1),
                       jnp.concatenate([vo, CN.astype(BF16)], axis=0))
            floor = jnp.exp(-jnp.broadcast_to(prep[d]["m_row"][:, k:k + 1], (L, L)))
            h_dst[pl.ds(r0, L), h * DH:(h + 1) * DH] = res[:, :DH] / jnp.maximum(jnp.abs(res[:, DH:]), floor)
            CN_s[k] = jnp.broadcast_to(prep[d]["decay"][:, k:k + 1], (DH, 2 * DH)) * CN + upd
            qr = rows(d, c_qr + h * DH)
            S = S_s[k]
            sr = (qkr[d, h] * dm_s[k]).astype(BF16)
            qd = (qr.astype(F32) * dq_s[k]).astype(BF16)
            h_dst[pl.ds(r0, L), W_M + h * DH:W_M + (h + 1) * DH] = _dot(
                jnp.concatenate([sr, qd], axis=1),
                jnp.concatenate([rows(d, c_vr + h * DH), S.astype(BF16)], axis=0))
            S_s[k] = dL_s[k] * S + updr
        lane = lax.broadcasted_iota(jnp.int32, (1, LANES), 1)
        m_s[...] = jnp.where(lane < H_M, m_new[0], m_new[1])
        return carry

    lax.fori_loop(0, n_chunks, chunk_step, 0)

    for h in range(H_M):
        sl = slice(h * DH, (h + 1) * DH)
        hs = hf_s[:, sl] + hb_s[:, sl]
        yn = _rmsnorm(hs, nm_ref[0, :, sl])
        om = pcols(slice(None), c_om + h * DH).astype(F32)
        y_ref[:, sl] = (jax.nn.sigmoid(om) * yn).astype(BF16)
        slr = slice(W_M + h * DH, W_M + (h + 1) * DH)
        hr = hf_s[:, slr] + hb_s[:, slr]
        ynr = _rmsnorm(hr, nr_ref[0, :, sl])
        gr = pcols(slice(None), c_gr + h * DH).astype(F32)
        y_ref[:, slr] = (_silu(gr) * ynr).astype(BF16)

    if not has_state:
        for prev, out in zip(prev_refs, (C_out, n_out, m_out, S_out)):
            out[0, :n_prev] = prev[0]
        for d in range(2):
            for h in range(H_M):
                k = d * H_M + h
                C_out[0, n_prev, d, h] = CN_s[k, :, :DH]
                n_out[0, n_prev, d, h:h + 1, :] = CN_s[k, :, DH:].T[0:1, :]
                S_out[0, n_prev, d, h] = S_s[k]
        m_out[0, n_prev] = m_s[...]


def _scan_scratch(T):
    return [
        pltpu.VMEM((2 * H_M, DH, 2 * DH), F32),
        pltpu.VMEM((2 * H_R, DH, DH), F32),
        pltpu.VMEM((1, LANES), F32),
        pltpu.VMEM((T, W_M + W_R), F32),
        pltpu.VMEM((T, W_M + W_R), F32),
        pltpu.VMEM((2 * H_R, CHUNK, CHUNK), F32),
        pltpu.VMEM((2 * H_R, CHUNK, CHUNK), F32),
        pltpu.VMEM((2 * H_R, CHUNK, CHUNK), F32),
        pltpu.VMEM((2 * H_R, 1, LANES), F32),
        pltpu.VMEM((H_M + H_R, T // CHUNK, DH, CHUNK), F32),
    ]


def _scan_ctx(p, gates, dl, nm, nr, layer, prev_states=()):
    T = SEQ
    n_lay = layer + 1
    state_tails = [(2, H_M, DH, DH), (2, H_M, DH), (1, LANES), (2, H_R, DH, DH)]

    def state_spec(n, tail):
        return pl.BlockSpec((1, n) + tail, lambda b: (b,) + (0,) * (1 + len(tail)))
    common = [
        pl.BlockSpec((P_COLS // TN, T, TN), lambda b: (0, b, 0)),
        pl.BlockSpec((T, GATE_LANES), lambda b: (b, 0)),
        pl.BlockSpec((1, 2 * H_R, LANES), lambda b: (layer, 0, 0)),
        pl.BlockSpec((1, 1, W_M), lambda b: (layer, 0, 0)),
        pl.BlockSpec((1, 1, W_R), lambda b: (layer, 0, 0)),
    ]
    return pl.pallas_call(
        functools.partial(_scan_kernel, T=T, has_state=False, n_prev=layer if prev_states else 0),
        grid=(BATCH,),
        in_specs=common + [state_spec(layer, tail) for tail in state_tails[:len(prev_states)]],
        out_specs=[pl.BlockSpec((T, D_MODEL), lambda b: (b, 0))] + [state_spec(n_lay, tail) for tail in state_tails],
        out_shape=[jax.ShapeDtypeStruct((NTOK, D_MODEL), BF16)] + [
            jax.ShapeDtypeStruct((BATCH, n_lay) + tail, F32) for tail in state_tails],
        scratch_shapes=_scan_scratch(T),
        compiler_params=pltpu.CompilerParams(
            dimension_semantics=("arbitrary",), vmem_limit_bytes=VMEM_LIMIT),
        name="scan_ctx",
    )(p, gates, dl, nm, nr, *prev_states)


def _scan_lat(p, gates, dl, nm, nr, C0, n0, m0, S0, y_prev, layer):
    T = DEC_SEQ
    off = N_CTX // T
    in_specs = [
        pl.BlockSpec((P_COLS // TN, T, TN), lambda b: (0, off + b, 0)),
        pl.BlockSpec((T, GATE_LANES), lambda b: (off + b, 0)),
        pl.BlockSpec((1, 2 * H_R, LANES), lambda b: (layer, 0, 0)),
        pl.BlockSpec((1, 1, W_M), lambda b: (layer, 0, 0)),
        pl.BlockSpec((1, 1, W_R), lambda b: (layer, 0, 0)),
        pl.BlockSpec((1, 1, 2, H_M, DH, DH), lambda b: (b, layer, 0, 0, 0, 0)),
        pl.BlockSpec((1, 1, 2, H_M, DH), lambda b: (b, layer, 0, 0, 0)),
        pl.BlockSpec((1, 1, 1, LANES), lambda b: (b, layer, 0, 0)),
        pl.BlockSpec((1, 1, 2, H_R, DH, DH), lambda b: (b, layer, 0, 0, 0, 0)),
        pl.BlockSpec(memory_space=pl.ANY),
    ]
    return pl.pallas_call(
        functools.partial(_scan_kernel, T=T, has_state=True),
        grid=(DEC_BATCH,),
        in_specs=in_specs,
        out_specs=pl.BlockSpec((T, D_MODEL), lambda b: (off + b, 0)),
        out_shape=jax.ShapeDtypeStruct((NTOK, D_MODEL), BF16),
        input_output_aliases={9: 0},
        scratch_shapes=_scan_scratch(T),
        compiler_params=pltpu.CompilerParams(
            dimension_semantics=("arbitrary",), vmem_limit_bytes=VMEM_LIMIT),
        name="scan_lat",
    )(p, gates, dl, nm, nr, C0, n0, m0, S0, y_prev)


def _top2(logits):
    lane = lax.broadcasted_iota(jnp.int32, logits.shape, 1)
    v1 = jnp.max(logits, -1, keepdims=True)
    i1 = jnp.min(jnp.where(logits == v1, lane, LANES), -1, keepdims=True)
    rest = jnp.where(lane == i1, -jnp.inf, logits)
    v2 = jnp.max(rest, -1, keepdims=True)
    i2 = jnp.min(jnp.where(rest == v2, lane, LANES), -1, keepdims=True)
    e2 = jnp.exp(v2 - v1)
    return i1, i2, 1.0 / (1.0 + e2), e2 / (1.0 + e2)


def _split2(x):
    hi = x.astype(BF16)
    return hi, (x - hi.astype(F32)).astype(BF16)


def _dot_f32x3(a, b):
    a_hi, a_lo = _split2(a)
    b_hi, b_lo = _split2(b)
    return _dot(a_hi, b_hi) + _dot(a_hi, b_lo) + _dot(a_lo, b_hi)


R_E1, R_E2, R_W1, R_W2, R_S1, R_S2 = range(6)


def _out_kernel(*refs, split_x):
    y_ref = refs[0]
    if split_x:
        xp_ref, xl_ref = refs[1:3]
        x_in = jnp.where(pl.program_id(0) >= N_CTX // TM, xl_ref[...], xp_ref[...])
    else:
        x_in = refs[1][...]
    (w_ref, g_ref, mod_ref, wr_ref,
     x1_ref, h2_ref, rinfo_ref, cnt_ref, w_scr, tri_scr, cnt_scr) = refs[3:] if split_x else refs[2:]

    @pl.when(pl.program_id(0) == 0)
    def _():
        w_scr[...] = w_ref[0].astype(BF16)
        r = lax.broadcasted_iota(jnp.int32, (LANES, LANES), 0)
        c = lax.broadcasted_iota(jnp.int32, (LANES, LANES), 1)
        tri_scr[...] = (c < r).astype(BF16)
        cnt_scr[...] = jnp.zeros_like(cnt_scr)

    o = jnp.dot(y_ref[...], w_scr[...], preferred_element_type=F32)
    x1 = x_in + mod_ref[0, 0, 2:3, :] * o
    x1_ref[...] = x1
    h2 = _rmsnorm(x1, g_ref[0]) * (1.0 + mod_ref[0, 0, 4:5, :]) + mod_ref[0, 0, 3:4, :]
    h2_ref[...] = h2
    h_hi, h_lo = _split2(h2)
    w_hi, w_lo = _split2(wr_ref[0])
    t = _dot(h_hi, jnp.concatenate([w_hi, w_lo], axis=1))
    logits = t[:, :LANES] + t[:, LANES:] + _dot(h_lo, w_hi)
    lane = lax.broadcasted_iota(jnp.int32, logits.shape, 1)
    i1, i2, w1, w2 = _top2(jnp.where(lane < N_EXPERTS, logits, -jnp.inf))
    oh1 = lane == i1
    oh2 = lane == i2
    sel = jnp.where(oh1 | oh2, 1.0, 0.0)
    tri = tri_scr[...]
    run = cnt_scr[...]
    ranks = []
    for blk in range(TM // LANES):
        s_blk = sel[blk * LANES:(blk + 1) * LANES, :]
        ranks.append(_dot(tri, s_blk.astype(BF16)) + run)
        run = run + jnp.sum(s_blk, 0, keepdims=True)
    rank = jnp.concatenate(ranks, axis=0)
    r1 = jnp.sum(jnp.where(oh1, rank, 0.0), -1, keepdims=True)
    r2 = jnp.sum(jnp.where(oh2, rank, 0.0), -1, keepdims=True)
    s1 = i1.astype(F32) * float(REG) + r1
    s2 = i2.astype(F32) * float(REG) + r2
    info = jnp.zeros(logits.shape, F32)
    for col, val in ((R_E1, i1.astype(F32)), (R_E2, i2.astype(F32)), (R_W1, w1), (R_W2, w2),
                     (R_S1, s1), (R_S2, s2)):
        info = jnp.where(lane == col, val, info)
    rinfo_ref[...] = info
    cnt_scr[...] = run
    cnt_ref[...] = run


def _out(y, xs, w_out, g2, mods, layer, w_router_pad, router_idx):
    split_x = len(xs) == 2
    n_ctx_tiles = N_CTX // TM
    if split_x:
        x_specs = [pl.BlockSpec((TM, D_MODEL), lambda i: (jnp.minimum(i, n_ctx_tiles - 1), 0)),
                   pl.BlockSpec((TM, D_MODEL), lambda i: (jnp.maximum(i - n_ctx_tiles, 0), 0))]
    else:
        x_specs = [pl.BlockSpec((TM, D_MODEL), lambda i: (i, 0))]
    in_specs = [pl.BlockSpec((TM, D_MODEL), lambda i: (i, 0))] + x_specs + [
        pl.BlockSpec((1, D_MODEL, D_MODEL), lambda i: (layer, 0, 0)),
        pl.BlockSpec((1, 1, D_MODEL), lambda i: (layer, 0, 0)),
        pl.BlockSpec((1, 1, 6, D_MODEL), lambda i: (layer, _group_of_tile(i, TM), 0, 0)),
        pl.BlockSpec((1, D_MODEL, LANES), lambda i: (router_idx, 0, 0)),
    ]
    out_specs = [
        pl.BlockSpec((TM, D_MODEL), lambda i: (i, 0)),
        pl.BlockSpec((TM, D_MODEL), lambda i: (i, 0)),
        pl.BlockSpec((TM, LANES), lambda i: (i, 0)),
        pl.BlockSpec((1, LANES), lambda i: (0, 0)),
    ]
    out_shape = [
        jax.ShapeDtypeStruct((NTOK, D_MODEL), F32),
        jax.ShapeDtypeStruct((NTOK, D_MODEL), F32),
        jax.ShapeDtypeStruct((NTOK, LANES), F32),
        jax.ShapeDtypeStruct((1, LANES), F32),
    ]
    return pl.pallas_call(
        functools.partial(_out_kernel, split_x=split_x),
        grid=(NTOK // TM,),
        in_specs=in_specs,
        out_specs=out_specs,
        out_shape=out_shape,
        scratch_shapes=[pltpu.VMEM((D_MODEL, D_MODEL), BF16), pltpu.VMEM((LANES, LANES), BF16),
                        pltpu.VMEM((1, LANES), F32)],
        compiler_params=pltpu.CompilerParams(
            dimension_semantics=("arbitrary",), vmem_limit_bytes=VMEM_LIMIT),
        name="out_router",
    )(y, *xs, w_out, g2, mods, w_router_pad)


N_FC = D_FF // FC


def _n_slots(fetch):
    return fetch[3].shape[0]


def _chunk_copies(fetch, f):
    wg_hbm, wu_hbm, wd_hbm, stg_gu, stg_d, sem = fetch
    s = f % _n_slots(fetch)
    cols = pl.ds(f * FC, FC)
    return (pltpu.make_async_copy(wg_hbm.at[:, cols], stg_gu.at[s, 0], sem.at[s, 0]),
            pltpu.make_async_copy(wu_hbm.at[:, cols], stg_gu.at[s, 1], sem.at[s, 1]),
            pltpu.make_async_copy(wd_hbm.at[cols, :], stg_d.at[s], sem.at[s, 2]))


def _start_first_chunks(fetch):
    for f in range(min(_n_slots(fetch), N_FC)):
        for c in _chunk_copies(fetch, f):
            c.start()


def _swiglu_tile(h, w_gu, w_d, acc, fetch=None, first_chunks_started=None):
    n_slots = 0
    if fetch is not None:
        stg_gu, stg_d = fetch[3], fetch[4]
        n_slots = _n_slots(fetch)

        def copies(f):
            return _chunk_copies(fetch, f)

        def start(f):
            for c in copies(f):
                c.start()

        def land(f):
            for c in copies(f):
                c.wait()
            s = f % n_slots
            w_gu[f, :, :FC] = stg_gu[s, 0].astype(BF16)
            w_gu[f, :, FC:] = stg_gu[s, 1].astype(BF16)
            w_d[f * FC:(f + 1) * FC, :] = stg_d[s].astype(BF16)
    else:
        start = land = lambda f: None

    def up(f):
        return jnp.dot(h, w_gu[f], preferred_element_type=F32)

    if fetch is not None:
        if first_chunks_started is None:
            _start_first_chunks(fetch)
        else:
            pl.when(jnp.logical_not(first_chunks_started))(lambda: _start_first_chunks(fetch))
    land(0)
    ab = up(0)
    for f in range(N_FC):
        if fetch is not None and f + n_slots < N_FC:
            start(f + n_slots)
        if f + 1 < N_FC:
            land(f + 1)
            ab_next = up(f + 1)
        t = (_silu(ab[:, :FC]) * ab[:, FC:]).astype(BF16)
        contrib = jnp.dot(t, w_d[f * FC:(f + 1) * FC, :], preferred_element_type=F32)
        if f == 0:
            acc[...] = contrib
        else:
            acc[...] += contrib
        if f + 1 < N_FC:
            ab = ab_next


def _ffn_weight_scratch(n_slots=2):
    return [
        pltpu.VMEM((N_FC, D_MODEL, 2 * FC), BF16),
        pltpu.VMEM((D_FF, D_MODEL), BF16),
        pltpu.VMEM((n_slots, 2, D_MODEL, FC), F32),
        pltpu.VMEM((n_slots, FC, D_MODEL), F32),
        pltpu.SemaphoreType.DMA((n_slots, 3)),
    ]


def _out_ffn_kernel(*refs, w_idx, split_x):
    y_ref = refs[0]
    if split_x:
        xp_ref, xl_ref = refs[1:3]
        x_in = jnp.where(pl.program_id(0) >= N_CTX // TM_F, xl_ref[...], xp_ref[...])
    else:
        x_in = refs[1][...]
    (wo_ref, g_ref, mod_ref, wg_ref, wu_ref, wd_ref, o_ref,
     wo_scr, h_scr, acc, w_gu, w_d, stg_gu, stg_d, sem) = refs[3:] if split_x else refs[2:]
    first = pl.program_id(0) == 0

    @pl.when(first)
    def _():
        wo_scr[...] = wo_ref[0].astype(BF16)

    x1 = x_in + mod_ref[0, 0, 2:3, :] * jnp.dot(y_ref[...], wo_scr[...], preferred_element_type=F32)
    h2 = _rmsnorm(x1, g_ref[0]) * (1.0 + mod_ref[0, 0, 4:5, :]) + mod_ref[0, 0, 3:4, :]
    h_scr[...] = h2.astype(BF16)
    o_ref[...] = x1

    @pl.when(first)
    def _():
        _swiglu_tile(h_scr[...], w_gu, w_d, acc,
                     (wg_ref.at[w_idx], wu_ref.at[w_idx], wd_ref.at[w_idx], stg_gu, stg_d, sem))

    @pl.when(jnp.logical_not(first))
    def _():
        _swiglu_tile(h_scr[...], w_gu, w_d, acc)

    o_ref[...] = o_ref[...] + mod_ref[0, 0, 5:6, :] * acc[...]


def _out_ffn(y, xs, w_out, g2, wg, wu, wd, mods, layer, w_idx):
    split_x = len(xs) == 2
    n_ctx_tiles = N_CTX // TM_F
    if split_x:
        x_specs = [pl.BlockSpec((TM_F, D_MODEL), lambda i: (jnp.minimum(i, n_ctx_tiles - 1), 0)),
                   pl.BlockSpec((TM_F, D_MODEL), lambda i: (jnp.maximum(i - n_ctx_tiles, 0), 0))]
    else:
        x_specs = [pl.BlockSpec((TM_F, D_MODEL), lambda i: (i, 0))]
    return pl.pallas_call(
        functools.partial(_out_ffn_kernel, w_idx=w_idx, split_x=split_x),
        grid=(NTOK // TM_F,),
        in_specs=[pl.BlockSpec((TM_F, D_MODEL), lambda i: (i, 0))] + x_specs + [
            pl.BlockSpec((1, D_MODEL, D_MODEL), lambda i: (layer, 0, 0), pipeline_mode=pl.Buffered(1)),
            pl.BlockSpec((1, 1, D_MODEL), lambda i: (layer, 0, 0)),
            pl.BlockSpec((1, 1, 6, D_MODEL), lambda i: (layer, _group_of_tile(i, TM_F), 0, 0)),
            pl.BlockSpec(memory_space=pl.ANY),
            pl.BlockSpec(memory_space=pl.ANY),
            pl.BlockSpec(memory_space=pl.ANY),
        ],
        out_specs=pl.BlockSpec((TM_F, D_MODEL), lambda i: (i, 0)),
        out_shape=jax.ShapeDtypeStruct((NTOK, D_MODEL), F32),
        scratch_shapes=[pltpu.VMEM((D_MODEL, D_MODEL), BF16),
                        pltpu.VMEM((TM_F, D_MODEL), BF16),
                        pltpu.VMEM((TM_F, D_MODEL), F32)] + _ffn_weight_scratch(),
        compiler_params=pltpu.CompilerParams(
            dimension_semantics=("arbitrary",), vmem_limit_bytes=VMEM_LIMIT),
        name="out_ffn",
    )(y, *xs, w_out, g2, mods, wg, wu, wd)


def _tile_plan(counts):
    nt = (counts + TR - 1) // TR
    cum = jnp.cumsum(nt)
    total = cum[-1]
    t = jnp.arange(MAX_TILES, dtype=jnp.int32)
    tt = jnp.minimum(t, total - 1)
    e = jnp.sum((cum[None, :] <= tt[:, None]).astype(jnp.int32), axis=1)
    k = tt - (cum - nt)[e]
    n = jnp.where(t < total, jnp.clip(counts[e] - k * TR, 0, TR), 0)
    return e.astype(jnp.int32), (e * REG_TILES + k).astype(jnp.int32), n.astype(jnp.int32)


def _row_copy(src, src_row, dst, dst_row, sem):
    return pltpu.make_async_copy(src.at[pl.ds(src_row, 1)], dst.at[pl.ds(dst_row, 1)], sem)


def _dispatch_kernel(slot_ref, h_ref, xs_ref, sem):
    base = pl.program_id(0) * (TOP_K * TD)

    def issue(r, carry):
        for k in range(TOP_K):
            _row_copy(h_ref, r, xs_ref, slot_ref[base + TOP_K * r + k], sem).start()
        return carry

    lax.fori_loop(0, TD, issue, 0, unroll=8)
    for k in range(TOP_K):
        pltpu.make_async_copy(h_ref, xs_ref.at[pl.ds(0, TD)], sem).wait()


def _dispatch(slots, h2f):
    return pl.pallas_call(
        _dispatch_kernel,
        grid_spec=pltpu.PrefetchScalarGridSpec(
            num_scalar_prefetch=1,
            grid=(NTOK // TD,),
            in_specs=[pl.BlockSpec((TD, D_MODEL), lambda i, s: (i, 0))],
            out_specs=pl.BlockSpec(memory_space=pl.ANY),
            scratch_shapes=[pltpu.SemaphoreType.DMA],
        ),
        out_shape=jax.ShapeDtypeStruct((N_EXPERTS * REG, D_MODEL), F32),
        compiler_params=pltpu.CompilerParams(
            dimension_semantics=("arbitrary",), vmem_limit_bytes=VMEM_LIMIT),
        name="moe_dispatch",
    )(slots, h2f)


def _gffn_kernel(te_ref, tb_ref, tn_ref, x_ref, wg_ref, wu_ref, wd_ref, o_ref,
                 w_gu, w_d, stg_gu, stg_d, sem, *, w_base):
    t = pl.program_id(0)
    n = tn_ref[t]

    @pl.when(n > 0)
    def _():
        row = lax.broadcasted_iota(jnp.int32, (TR, D_MODEL), 0)
        h = jnp.where(row < n, x_ref[...], 0.0).astype(BF16)
        def is_first(tt):
            return tb_ref[tt] % REG_TILES == 0

        def fetch_of(tt):
            e = w_base + te_ref[tt]
            return (wg_ref.at[e], wu_ref.at[e], wd_ref.at[e], stg_gu, stg_d, sem)

        first = is_first(t)
        t_prev = jnp.maximum(t - 1, 0)
        t_next = jnp.minimum(t + 1, MAX_TILES - 1)

        @pl.when(first)
        def _():
            started = (t > 0) & jnp.logical_not(is_first(t_prev))
            _swiglu_tile(h, w_gu, w_d, o_ref, fetch_of(t), started)

        @pl.when(jnp.logical_not(first))
        def _():
            @pl.when((t + 1 < MAX_TILES) & (tn_ref[t_next] > 0) & is_first(t_next))
            def _():
                _start_first_chunks(fetch_of(t_next))

            _swiglu_tile(h, w_gu, w_d, o_ref)


def _gffn(tile_e, tile_blk, tile_n, xs, wg, wu, wd, w_base):
    return pl.pallas_call(
        functools.partial(_gffn_kernel, w_base=w_base),
        grid_spec=pltpu.PrefetchScalarGridSpec(
            num_scalar_prefetch=3,
            grid=(MAX_TILES,),
            in_specs=[
                pl.BlockSpec((TR, D_MODEL), lambda t, te, tb, tn: (tb[t], 0)),
                pl.BlockSpec(memory_space=pl.ANY),
                pl.BlockSpec(memory_space=pl.ANY),
                pl.BlockSpec(memory_space=pl.ANY),
            ],
            out_specs=pl.BlockSpec((TR, D_MODEL), lambda t, te, tb, tn: (tb[t], 0)),
            scratch_shapes=_ffn_weight_scratch(n_slots=4),
        ),
        out_shape=jax.ShapeDtypeStruct((N_EXPERTS * REG, D_MODEL), F32),
        compiler_params=pltpu.CompilerParams(
            dimension_semantics=("arbitrary",), vmem_limit_bytes=VMEM_LIMIT),
        name="moe_ffn",
    )(tile_e, tile_blk, tile_n, xs, wg, wu, wd)


def _combine_kernel(slot_ref, x1_ref, rinfo_ref, mod_ref, ys_ref, *rest, final_norm):
    if final_norm:
        gf_ref, yp_ref, yl_ref, buf, sem = rest
    else:
        o_ref, buf, sem = rest
    i = pl.program_id(0)
    n_steps = pl.num_programs(0)

    def gather(step, s):
        base = step * (TOP_K * TD)

        def issue(r, carry):
            for k in range(TOP_K):
                _row_copy(ys_ref, slot_ref[base + TOP_K * r + k], buf.at[s, k], r, sem.at[s]).start()
            return carry

        lax.fori_loop(0, TD, issue, 0, unroll=8)

    def wait(s):
        for k in range(TOP_K):
            pltpu.make_async_copy(ys_ref.at[pl.ds(0, TD)], buf.at[s, k], sem.at[s]).wait()

    @pl.when(i == 0)
    def _():
        gather(0, 0)

    for s in range(2):
        @pl.when(i % 2 == s)
        def _():
            @pl.when(i + 1 < n_steps)
            def _():
                gather(i + 1, 1 - s)

            wait(s)

    b = buf.at[i % 2]
    y = rinfo_ref[:, R_W1:R_W1 + 1] * b[0] + rinfo_ref[:, R_W2:R_W2 + 1] * b[1]
    x = x1_ref[...] + mod_ref[0, 0, 5:6, :] * y
    if final_norm:
        out = _rmsnorm(x, gf_ref[...])
        is_lat = pl.program_id(0) >= N_CTX // TD

        @pl.when(jnp.logical_not(is_lat))
        def _():
            yp_ref[...] = out

        @pl.when(is_lat)
        def _():
            yl_ref[...] = out
    else:
        o_ref[...] = x


def _combine(slots, x1, rinfo, mods, ys, layer, norm_f=None):
    final_norm = norm_f is not None
    n_ctx_t = N_CTX // TD
    in_specs = [
        pl.BlockSpec((TD, D_MODEL), lambda i, s: (i, 0)),
        pl.BlockSpec((TD, LANES), lambda i, s: (i, 0)),
        pl.BlockSpec((1, 1, 6, D_MODEL), lambda i, s: (layer, _group_of_tile(i, TD), 0, 0)),
        pl.BlockSpec(memory_space=pl.ANY),
    ]
    args = [slots, x1, rinfo, mods, ys]
    if final_norm:
        in_specs.append(pl.BlockSpec((1, D_MODEL), lambda i, s: (0, 0)))
        args.append(norm_f)
        out_specs = [pl.BlockSpec((TD, D_MODEL), lambda i, s: (jnp.minimum(i, n_ctx_t - 1), 0)),
                     pl.BlockSpec((TD, D_MODEL), lambda i, s: (jnp.maximum(i - n_ctx_t, 0), 0))]
        out_shape = [jax.ShapeDtypeStruct((N_CTX, D_MODEL), F32), jax.ShapeDtypeStruct((N_LAT, D_MODEL), F32)]
    else:
        out_specs = pl.BlockSpec((TD, D_MODEL), lambda i, s: (i, 0))
        out_shape = jax.ShapeDtypeStruct((NTOK, D_MODEL), F32)
    return pl.pallas_call(
        functools.partial(_combine_kernel, final_norm=final_norm),
        grid_spec=pltpu.PrefetchScalarGridSpec(
            num_scalar_prefetch=1,
            grid=(NTOK // TD,),
            in_specs=in_specs,
            out_specs=out_specs,
            scratch_shapes=[pltpu.VMEM((2, TOP_K, TD, D_MODEL), F32), pltpu.SemaphoreType.DMA((2,))],
        ),
        out_shape=out_shape,
        compiler_params=pltpu.CompilerParams(
            dimension_semantics=("arbitrary",), vmem_limit_bytes=VMEM_LIMIT),
        name="moe_combine",
    )(*args)


def _final_kernel(x_ref, g_ref, o_ref):
    o_ref[...] = _rmsnorm(x_ref[...], g_ref[...])


def _final(x, g, row_off, rows):
    off = row_off // TM
    return pl.pallas_call(
        _final_kernel,
        grid=(rows // TM,),
        in_specs=[
            pl.BlockSpec((TM, D_MODEL), lambda i: (off + i, 0)),
            pl.BlockSpec((1, D_MODEL), lambda i: (0, 0)),
        ],
        out_specs=pl.BlockSpec((TM, D_MODEL), lambda i: (i, 0)),
        out_shape=jax.ShapeDtypeStruct((rows, D_MODEL), F32),
        compiler_params=pltpu.CompilerParams(
            dimension_semantics=("arbitrary",), vmem_limit_bytes=VMEM_LIMIT),
        name="final_norm",
    )(x, g)


def kernel(x_prompt, x_sample, state_mlstm_C, state_mlstm_n, state_mlstm_m, state_ret_S, c, c_ctx,
           norm1_g, norm2_g, norm_f_g, w_ada, b_ada, w_in, b_gates, ret_decay_logit,
           mlstm_norm_g, ret_norm_g, w_out, ffn_w_gate, ffn_w_up, ffn_w_down,
           moe_w_router, moe_w_gate, moe_w_up, moe_w_down):
    xs_in = (x_prompt.reshape(N_CTX, D_MODEL), x_sample.reshape(N_LAT, D_MODEL))
    cvec = jnp.concatenate(
        [c_ctx[None, :], c, jnp.zeros((N_GROUPS - 1 - DEC_BATCH, D_MODEL), F32)], 0)
    mods = _ada(cvec, w_ada, b_ada).reshape(DEPTH, N_GROUPS, 6, D_MODEL)

    n_m = 4 * W_M
    w_in_t = jnp.swapaxes(w_in, 1, 2)
    n_if = N_GATES // 2
    lane_pad = ((0, 0), (0, 0), (0, LANES - n_if))
    wg = jnp.concatenate([jnp.pad(w_in[:, :, n_m:n_m + n_if], lane_pad),
                          jnp.pad(w_in[:, :, n_m + n_if:n_m + N_GATES], lane_pad)], -1)
    bg = jnp.concatenate([jnp.pad(b_gates[:, None, :n_if], lane_pad),
                          jnp.pad(b_gates[:, None, n_if:], lane_pad)], -1)
    cos_np, sin_np = _rope_tables()
    cos_t, sin_t = jnp.asarray(cos_np), jnp.asarray(sin_np)
    dl = jnp.broadcast_to(ret_decay_logit.reshape(DEPTH, 2 * H_R, 1), (DEPTH, 2 * H_R, LANES))
    m0 = jnp.pad(state_mlstm_m.reshape(DEC_BATCH, DEPTH, 1, 2 * H_M),
                 ((0, 0), (0, 0), (0, 0), (0, LANES - 2 * H_M)))
    g1 = norm1_g.reshape(DEPTH, 1, D_MODEL)
    g2 = norm2_g.reshape(DEPTH, 1, D_MODEL)
    nm = mlstm_norm_g.reshape(DEPTH, 1, W_M)
    nr = ret_norm_g.reshape(DEPTH, 1, W_R)
    n_moe = moe_w_router.shape[0]
    wr_pad = jnp.pad(moe_w_router, ((0, 0), (0, 0), (0, LANES - N_EXPERTS)))
    moe_g = moe_w_gate.reshape(n_moe * N_EXPERTS, D_MODEL, D_FF)
    moe_u = moe_w_up.reshape(n_moe * N_EXPERTS, D_MODEL, D_FF)
    moe_d = moe_w_down.reshape(n_moe * N_EXPERTS, D_FF, D_MODEL)

    states = ()
    xs = xs_in
    for l in range(DEPTH):
        jl = l // 2
        p, gates = _proj(xs, g1, mods, w_in_t, wg, bg, cos_t, sin_t, l)
        y, *states = _scan_ctx(p, gates, dl, nm, nr, l, states)
        y = _scan_lat(p, gates, dl, nm, nr, state_mlstm_C, state_mlstm_n, m0, state_ret_S, y, l)
        if l % 2 == 0:
            x = _out_ffn(y, xs, w_out, g2, ffn_w_gate, ffn_w_up, ffn_w_down, mods, l, jl)
        else:
            x1, h2f, rinfo, cnt = _out(y, xs, w_out, g2, mods, l, wr_pad, jl)
            slots = rinfo[:, R_S1:R_S2 + 1].astype(jnp.int32).reshape(TOP_K * NTOK)
            tile_e, tile_blk, tile_n = _tile_plan(cnt[0, :N_EXPERTS].astype(jnp.int32))
            xd = _dispatch(slots, h2f)
            yd = _gffn(tile_e, tile_blk, tile_n, xd, moe_g, moe_u, moe_d, jl * N_EXPERTS)
            if l == DEPTH - 1:
                y_ctx, y_lat = _combine(slots, x1, rinfo, mods, yd, l, norm_f_g.reshape(1, D_MODEL))
            else:
                x = _combine(slots, x1, rinfo, mods, yd, l)
        xs = (x,)

    if DEPTH % 2 == 1:
        y_ctx = _final(x, norm_f_g.reshape(1, D_MODEL), 0, N_CTX)
        y_lat = _final(x, norm_f_g.reshape(1, D_MODEL), N_CTX, N_LAT)
    y_prompt = y_ctx.reshape(BATCH, SEQ, D_MODEL)
    y_sample = y_lat.reshape(DEC_BATCH, DEC_SEQ, D_MODEL)
    new_C, new_n, new_m, new_S = states
    return (y_prompt, y_sample, new_C, new_n,
            new_m[:, :, 0, :2 * H_M].reshape(BATCH, DEPTH, 2, H_M), new_S)
```
